```python
import math
import jax, jax.numpy as jnp
from jax import lax
import numpy as np

D_MODEL = 1024
BATCH = 4
SEQ = 4096
DEPTH = 1
DEC_BATCH = 128
DEC_SEQ = 1
PAST_LEN = 8192
PAGE_SIZE = 128

HEAD_DIM = 64
ATT_WIDTH = D_MODEL // 2
RWKV_WIDTH = D_MODEL - ATT_WIDTH
N_Q_HEADS = ATT_WIDTH // HEAD_DIM
N_KV_HEADS = 2
KV_WIDTH = N_KV_HEADS * HEAD_DIM
N_IDX_HEADS = 8
IDX_DIM = 64
TOPK_MAX = 256
ROPE_THETA = 500000.0
ROPE_DIMS = HEAD_DIM // 4
N_RWKV_HEADS = RWKV_WIDTH // HEAD_DIM
DECAY_LORA = 64
ICLR_LORA = 64
Q_BLOCK = 128
NORM_EPS = 1e-6
GN_EPS = 64e-5

COL_SIZES = (ATT_WIDTH, KV_WIDTH, KV_WIDTH, N_IDX_HEADS * IDX_DIM, N_IDX_HEADS, IDX_DIM, ATT_WIDTH,
             RWKV_WIDTH, RWKV_WIDTH, RWKV_WIDTH, DECAY_LORA, ICLR_LORA, RWKV_WIDTH)
D_IN = sum(COL_SIZES)
SHIFT_W = 3 * RWKV_WIDTH + DECAY_LORA + ICLR_LORA

kernel_name = 'hymba_rwkv7_dsa_decode_step'


def rms_norm(x, w, eps=NORM_EPS):
    xf = x.astype(jnp.float32)
    y = xf * lax.rsqrt(jnp.mean(xf * xf, axis=-1, keepdims=True) + eps)
    return (y * w.astype(jnp.float32)).astype(x.dtype)


def partial_rope(x, pos):
    half = ROPE_DIMS // 2
    inv = jnp.power(ROPE_THETA, -jnp.arange(half, dtype=jnp.float32) / half)
    ang = pos.astype(jnp.float32)[:, None] * inv[None, :]
    cos = jnp.cos(ang)[None, :, None, :]
    sin = jnp.sin(ang)[None, :, None, :]
    xr = x[..., :ROPE_DIMS].astype(jnp.float32)
    x1, x2 = xr[..., :half], xr[..., half:]
    rot = jnp.concatenate([x1 * cos - x2 * sin, x2 * cos + x1 * sin], axis=-1)
    return jnp.concatenate([rot.astype(x.dtype), x[..., ROPE_DIMS:]], axis=-1)


def split_cols(p):
    offs = [int(o) for o in np.cumsum(COL_SIZES)[:-1]]
    return jnp.split(p, offs, axis=-1)


def adaln_in(x, c, norm_w, w_ada, b_ada):
    mod = jax.nn.silu(c) @ w_ada + b_ada
    shift, scale, gate = jnp.split(mod, 3, axis=-1)
    h = rms_norm(x, norm_w) * (1.0 + scale[:, None, :]) + shift[:, None, :]
    return h, gate


def attn_inputs(cols, pos, q_norm_w, k_norm_w):
    q, k, v, qi, wi, ki = cols
    B, T = q.shape[:2]
    q = partial_rope(rms_norm(q.reshape(B, T, N_Q_HEADS, HEAD_DIM), q_norm_w), pos)
    k = partial_rope(rms_norm(k.reshape(B, T, N_KV_HEADS, HEAD_DIM), k_norm_w), pos)
    v = v.reshape(B, T, N_KV_HEADS, HEAD_DIM)
    qi = partial_rope(qi.reshape(B, T, N_IDX_HEADS, IDX_DIM), pos)
    ki = partial_rope(ki[:, :, None, :], pos)[:, :, 0, :]
    wi = wi * (N_IDX_HEADS ** -0.5 * IDX_DIM ** -0.5)
    return q, k, v, qi, wi, ki


def sparse_attend(q, qi, wi, qpos, kidx, gather_kv, topk):
    B, Q = q.shape[:2]
    L = kidx.shape[1]
    s = jax.nn.relu(jnp.einsum('bqhd,bsd->bqhs', qi, kidx))
    score = jnp.einsum('bqhs,bqh->bqs', s, wi).astype(jnp.float32)
    admissible = jnp.arange(L)[None, :] <= qpos[:, None]
    score = jnp.where(admissible[None], score, -jnp.inf)
    _, sel = lax.top_k(score, topk)
    valid = sel <= qpos[None, :, None]
    k_sel, v_sel = gather_kv(sel)
    qg = q.reshape(B, Q, N_KV_HEADS, N_Q_HEADS // N_KV_HEADS, HEAD_DIM)
    logits = jnp.einsum('bqgrd,bqkgd->bqgrk', qg, k_sel).astype(jnp.float32) * (HEAD_DIM ** -0.5)
    logits = jnp.where(valid[:, :, None, None, :], logits, -jnp.inf)
    p = jax.nn.softmax(logits, axis=-1).astype(v_sel.dtype)
    o = jnp.einsum('bqgrk,bqkgd->bqgrd', p, v_sel)
    return o.reshape(B, Q, N_Q_HEADS * HEAD_DIM)


def wkv_scan(S0, r, w, k, v, a, b):
    def step(S, inp):
        r_t, w_t, k_t, v_t, a_t, b_t = inp
        sa = jnp.einsum('bhij,bhj->bhi', S, a_t)
        S = S * w_t[:, :, None, :] + sa[..., None] * b_t[:, :, None, :] + v_t[..., None] * k_t[:, :, None, :]
        y = jnp.einsum('bhij,bhj->bhi', S, r_t)
        return S, y
    xs = tuple(jnp.moveaxis(t.astype(jnp.float32), 1, 0) for t in (r, w, k, v, a, b))
    S, ys = lax.scan(step, S0.astype(jnp.float32), xs)
    return S, jnp.moveaxis(ys, 0, 1)


def rwkv_branch(xs, prev_row, S0, gate, mu_shift, w0, w_up, a0, a_up, k_k, k_a, r_k, ln_x_w, ln_x_b):
    f32 = jnp.float32
    B, T = xs.shape[:2]
    prev = jnp.concatenate([prev_row.astype(xs.dtype), xs[:, :-1]], axis=1)
    xm = xs + mu_shift * (prev - xs)
    r, k, v, wd, ad = jnp.split(xm, [RWKV_WIDTH, 2 * RWKV_WIDTH, 3 * RWKV_WIDTH, 3 * RWKV_WIDTH + DECAY_LORA], axis=-1)
    heads = lambda t: t.reshape(B, T, N_RWKV_HEADS, HEAD_DIM)
    wlog = -jax.nn.softplus(-(w0 + jnp.tanh(wd) @ w_up).astype(f32)) - 0.5
    decay = jnp.exp(-jnp.exp(wlog))
    a = jax.nn.sigmoid((a0 + ad @ a_up).astype(f32))
    kf = k.astype(f32)
    kk = heads(kf * k_k.astype(f32))
    kk = kk / jnp.maximum(jnp.sqrt(jnp.sum(kk * kk, axis=-1, keepdims=True)), 1e-12)
    k_mod = heads(kf * (1.0 + (a - 1.0) * k_a.astype(f32)))
    a_h = heads(a)
    rf, vf = heads(r.astype(f32)), heads(v.astype(f32))
    S_new, y = wkv_scan(S0, rf, heads(decay), k_mod, vf, -kk, kk * a_h)
    mean = jnp.mean(y, axis=-1, keepdims=True)
    var = jnp.mean(jnp.square(y - mean), axis=-1, keepdims=True)
    yn = ((y - mean) * lax.rsqrt(var + GN_EPS)).reshape(B, T, RWKV_WIDTH)
    yn = yn * ln_x_w.astype(f32) + ln_x_b.astype(f32)
    bonus = jnp.sum(rf * k_mod * r_k.astype(f32), axis=-1, keepdims=True) * vf
    out = (yn + bonus.reshape(B, T, RWKV_WIDTH)) * jax.nn.silu(gate.astype(f32))
    return out.astype(xs.dtype), S_new, xs[:, -1:]


def setup_inputs(seed: int = 0) -> dict:
    key = jax.random.key(seed)
    ks = jax.random.split(key, 32)
    nrm = jax.random.normal
    n_pages = PAST_LEN // PAGE_SIZE
    n_used = DEC_BATCH * n_pages
    n_phys = (n_used * 5 + 3) // 4
    page_table = jax.random.permutation(ks[0], n_phys)[:n_used].reshape(DEC_BATCH, n_pages).astype(jnp.int32)
    return {
        'x_prompt': nrm(ks[1], (BATCH, SEQ, D_MODEL), jnp.float32),
        'x_sample': nrm(ks[2], (DEC_BATCH, DEC_SEQ, D_MODEL), jnp.float32),
        'cache_k': nrm(ks[3], (n_phys, PAGE_SIZE, N_KV_HEADS, HEAD_DIM), jnp.float32),
        'cache_v': nrm(ks[4], (n_phys, PAGE_SIZE, N_KV_HEADS, HEAD_DIM), jnp.float32),
        'cache_kidx': nrm(ks[5], (n_phys, PAGE_SIZE, IDX_DIM), jnp.float32),
        'state_wkv': 0.5 * nrm(ks[6], (DEC_BATCH, N_RWKV_HEADS, HEAD_DIM, HEAD_DIM), jnp.float32),
        'state_shift': nrm(ks[7], (DEC_BATCH, 1, SHIFT_W), jnp.float32),
        'page_table': page_table,
        'c_prompt': nrm(ks[8], (BATCH, D_MODEL), jnp.float32),
        'c_sample': nrm(ks[9], (DEC_BATCH, D_MODEL), jnp.float32),
        'norm_w': 1.0 + 0.1 * nrm(ks[10], (D_MODEL,), jnp.float32),
        'w_ada': 0.2 * D_MODEL ** -0.5 * nrm(ks[11], (D_MODEL, 3 * D_MODEL), jnp.float32),
        'b_ada': 0.02 * nrm(ks[12], (3 * D_MODEL,), jnp.float32),
        'w_in': D_MODEL ** -0.5 * nrm(ks[13], (D_MODEL, D_IN), jnp.float32),
        'q_norm_w': 1.0 + 0.1 * nrm(ks[14], (HEAD_DIM,), jnp.float32),
        'k_norm_w': 1.0 + 0.1 * nrm(ks[15], (HEAD_DIM,), jnp.float32),
        'mu_shift': jax.random.uniform(ks[16], (SHIFT_W,), jnp.float32),
        'w0': 0.5 + 0.5 * nrm(ks[17], (RWKV_WIDTH,), jnp.float32),
        'w_up': 0.1 * DECAY_LORA ** -0.5 * nrm(ks[18], (DECAY_LORA, RWKV_WIDTH), jnp.float32),
        'a0': 0.1 * nrm(ks[19], (RWKV_WIDTH,), jnp.float32),
        'a_up': 0.1 * ICLR_LORA ** -0.5 * nrm(ks[20], (ICLR_LORA, RWKV_WIDTH), jnp.float32),
        'k_k': 0.85 + 0.05 * nrm(ks[21], (RWKV_WIDTH,), jnp.float32),
        'k_a': 1.0 + 0.05 * nrm(ks[22], (RWKV_WIDTH,), jnp.float32),
        'r_k': 0.1 * nrm(ks[23], (N_RWKV_HEADS, HEAD_DIM), jnp.float32),
        'ln_x_w': 1.0 + 0.1 * nrm(ks[24], (RWKV_WIDTH,), jnp.float32),
        'ln_x_b': 0.02 * nrm(ks[25], (RWKV_WIDTH,), jnp.float32),
        'w_out': D_MODEL ** -0.5 * nrm(ks[26], (D_MODEL, D_MODEL), jnp.float32),
    }


def reference(x_prompt, x_sample, cache_k, cache_v, cache_kidx, state_wkv, state_shift, page_table,
              c_prompt, c_sample, norm_w, w_ada, b_ada, w_in, q_norm_w, k_norm_w, mu_shift, w0, w_up,
              a0, a_up, k_k, k_a, r_k, ln_x_w, ln_x_b, w_out):
    rw = (mu_shift, w0, w_up, a0, a_up, k_k, k_a, r_k, ln_x_w, ln_x_b)
    sdt = state_wkv.dtype

    x = x_prompt
    for _ in range(DEPTH):
        B, T, _ = x.shape
        h, gate_c = adaln_in(x, c_prompt, norm_w, w_ada, b_ada)
        cols = split_cols(h @ w_in)
        pos = jnp.arange(T)
        q, k_p, v_p, qi, wi, ki_p = attn_inputs(cols[:6], pos, q_norm_w, k_norm_w)
        bidx = jnp.arange(B)[:, None, None]
        topk = min(TOPK_MAX, T // 4)

        def prompt_block(i):
            start = i * Q_BLOCK
            sl = lambda t: lax.dynamic_slice_in_dim(t, start, Q_BLOCK, axis=1)
            return sparse_attend(sl(q), sl(qi), sl(wi), start + jnp.arange(Q_BLOCK), ki_p,
                                 lambda sel: (k_p[bidx, sel], v_p[bidx, sel]), topk)

        att = lax.map(prompt_block, jnp.arange(T // Q_BLOCK))
        att = jnp.moveaxis(att, 0, 1).reshape(B, T, ATT_WIDTH) * jax.nn.silu(cols[6])
        rw_out, S_p, shift_p = rwkv_branch(jnp.concatenate(cols[7:12], axis=-1),
                                           jnp.zeros((B, 1, SHIFT_W), x.dtype),
                                           jnp.zeros((B, N_RWKV_HEADS, HEAD_DIM, HEAD_DIM), jnp.float32),
                                           cols[12], *rw)
        mix = jnp.concatenate([att, rw_out], axis=-1) @ w_out
        x = x + gate_c[:, None, :] * mix
    y_prompt = x

    x = x_sample
    for _ in range(DEPTH):
        Bd, Qn, _ = x.shape
        past_len = page_table.shape[1] * PAGE_SIZE
        h, gate_c = adaln_in(x, c_sample, norm_w, w_ada, b_ada)
        cols = split_cols(h @ w_in)
        qpos = past_len + jnp.arange(Qn)
        q, k_s, v_s, qi, wi, ki_s = attn_inputs(cols[:6], qpos, q_norm_w, k_norm_w)
        bidx = jnp.arange(Bd)[:, None, None]
        kidx_all = jnp.concatenate([cache_kidx[page_table].reshape(Bd, past_len, IDX_DIM),
                                    ki_s.astype(cache_kidx.dtype)], axis=1)
        k_flat = cache_k.reshape(-1, N_KV_HEADS, HEAD_DIM)
        v_flat = cache_v.reshape(-1, N_KV_HEADS, HEAD_DIM)

        def gather_sample(sel):
            in_past = (sel < past_len)[..., None, None]
            sp = jnp.minimum(sel, past_len - 1)
            rows = page_table[bidx, sp // PAGE_SIZE] * PAGE_SIZE + sp % PAGE_SIZE
            j = jnp.clip(sel - past_len, 0, Qn - 1)
            k_sel = jnp.where(in_past, k_flat[rows], k_s[bidx, j].astype(k_flat.dtype))
            v_sel = jnp.where(in_past, v_flat[rows], v_s[bidx, j].astype(v_flat.dtype))
            return k_sel, v_sel

        topk = min(TOPK_MAX, (past_len + Qn) // 4)
        att = sparse_attend(q, qi, wi, qpos, kidx_all, gather_sample, topk)
        att = att.astype(x.dtype) * jax.nn.silu(cols[6])
        rw_out, S_s, shift_s = rwkv_branch(jnp.concatenate(cols[7:12], axis=-1), state_shift, state_wkv,
                                           cols[12], *rw)
        mix = jnp.concatenate([att, rw_out.astype(att.dtype)], axis=-1) @ w_out
        x = x + gate_c[:, None, :] * mix
    y_sample = x

    return (y_prompt, y_sample, k_p, v_p, ki_p, S_p.astype(sdt), shift_p.astype(state_shift.dtype),
            k_s, v_s, ki_s, S_s.astype(sdt), shift_s.astype(state_shift.dtype))
```

```python
import functools

import jax
import jax.numpy as jnp
from jax import lax
from jax.experimental import pallas as pl
from jax.experimental.pallas import tpu as pltpu

F32, BF16 = jnp.float32, jnp.bfloat16
HI = lax.Precision.HIGHEST

HEAD_DIM = 64
N_Q_HEADS = 8
N_KV_HEADS = 2
N_IDX_HEADS = 8
IDX_DIM = 64
N_RWKV_HEADS = 8
ATT_WIDTH = N_Q_HEADS * HEAD_DIM
RWKV_WIDTH = N_RWKV_HEADS * HEAD_DIM
KV_WIDTH = N_KV_HEADS * HEAD_DIM
LORA = 64
TOPK_MAX = 256
ROPE_THETA = 500000.0
ROPE_DIMS = HEAD_DIM // 4
NORM_EPS = 1e-6
GN_EPS = 64e-5
PAGE_SIZE = 128
LANES = 128
VMEM_LIMIT = 56 * 1024 * 1024

Q_BLOCK = 128
KEY_CHUNK = 512
N_BISECT = 16
RW_CHUNK = 64
RW_BLOCK = 512
PAGES_PER_STEP = 16


def _dot(a, b):
    return jnp.dot(a, b, preferred_element_type=F32)


def _dot_nt(a, b):
    return lax.dot_general(a, b, (((1,), (1,)), ((), ())), preferred_element_type=F32)


def _dot_hi(a, b):
    return jnp.dot(a, b, precision=HI, preferred_element_type=F32)


def _dot_tn(a, b):
    return lax.dot_general(a, b, (((0,), (0,)), ((), ())), preferred_element_type=F32)


def _iota(shape, dim):
    return lax.broadcasted_iota(jnp.int32, shape, dim)


def _head_block_ones(n):
    return jnp.where(_iota((n, n), 0) // HEAD_DIM == _iota((n, n), 1) // HEAD_DIM, 1.0, 0.0).astype(F32)


def _silu(x):
    return x * jax.nn.sigmoid(x)


def _cparams(sem):
    return pltpu.CompilerParams(dimension_semantics=sem, vmem_limit_bytes=VMEM_LIMIT)


def _mod_kernel(c_ref, w_ref, b_ref, o_ref):
    s = _silu(c_ref[...])
    o_ref[...] = _dot(s.astype(BF16), w_ref[...].astype(BF16)) + b_ref[...]


def _adaln_mod(c, w_ada, b_ada):
    rows, d = c.shape
    n = w_ada.shape[1]
    tn = 1024
    return pl.pallas_call(
        _mod_kernel,
        grid=(n // tn,),
        in_specs=[pl.BlockSpec((rows, d), lambda j: (0, 0)),
                  pl.BlockSpec((d, tn), lambda j: (0, j)),
                  pl.BlockSpec((1, tn), lambda j: (0, j))],
        out_specs=pl.BlockSpec((rows, tn), lambda j: (0, j)),
        out_shape=jax.ShapeDtypeStruct((rows, n), F32),
        compiler_params=_cparams(("arbitrary",)),
        name="adaln_mod",
    )(c, w_ada, b_ada.reshape(1, n))


_W_Q, _W_KV, _W_QI, _W_GA, _W_RKV, _W_GR, _W_TAIL, _W_END = 0, 512, 768, 1280, 1792, 3328, 3840, 4096


def _inproj_kernel(x_ref, shift_ref, scale_ref, nw_ref, w_ref, wwi_ref, wvt_ref, cos_ref, s1_ref, s2_ref,
                   qnw_ref, knw_ref,
                   q_o, k_o, kbf_o, v_o, vt_o, qi_o, ki_o, kibf_o, wit_o, sg_o, rkv_o, lora_o, gr_o):
    x = x_ref[0]
    ms = jnp.mean(x * x, axis=-1, keepdims=True)
    h = x * lax.rsqrt(ms + NORM_EPS) * nw_ref[...]
    h = h * (1.0 + scale_ref[0]) + shift_ref[0]
    hb = h.astype(BF16)
    cos, s1, s2 = cos_ref[...], s1_ref[...], s2_ref[...]
    bsum = _head_block_ones(LANES) * (1.0 / HEAD_DIM)

    def seg(a, b):
        return _dot(hb, w_ref[:, a:b])

    def rope(t):
        return t * cos + pltpu.roll(t, LANES - ROPE_DIMS // 2, 1) * s1 + pltpu.roll(t, ROPE_DIMS // 2, 1) * s2

    def head_norm(t, w):
        return t * lax.rsqrt(_dot_hi(t * t, bsum) + NORM_EPS) * w

    yq = seg(_W_Q, _W_KV)
    for s in range(ATT_WIDTH // LANES):
        t = rope(head_norm(yq[:, s * LANES:(s + 1) * LANES], qnw_ref[...]))
        q_o[0, :, s * LANES:(s + 1) * LANES] = (t * (HEAD_DIM ** -0.5)).astype(BF16)

    ykv = seg(_W_KV, _W_QI)
    k = rope(head_norm(ykv[:, :KV_WIDTH], knw_ref[...]))
    k_o[0] = k
    kbf_o[0] = k.astype(BF16)
    v_o[0] = ykv[:, KV_WIDTH:]
    vt_o[0] = _dot_nt(wvt_ref[...], hb).astype(BF16)

    yqi = seg(_W_QI, _W_GA)
    for s in range(N_IDX_HEADS * IDX_DIM // LANES):
        qi_o[0, :, s * LANES:(s + 1) * LANES] = rope(yqi[:, s * LANES:(s + 1) * LANES]).astype(BF16)

    sg_o[0] = _silu(seg(_W_GA, _W_RKV))
    for s in range(3):
        rkv_o[0, :, s * RWKV_WIDTH:(s + 1) * RWKV_WIDTH] = seg(_W_RKV + s * RWKV_WIDTH, _W_RKV + (s + 1) * RWKV_WIDTH)
    gr_o[0] = seg(_W_GR, _W_TAIL)

    yt = seg(_W_TAIL, _W_END)
    lora_o[0] = yt[:, :2 * LORA]
    ki = rope(yt[:, 2 * LORA:])[:, :IDX_DIM]
    ki_o[0] = ki
    kibf_o[0] = ki.astype(BF16)
    wit_o[0] = _dot_nt(wwi_ref[...], hb) * (N_IDX_HEADS ** -0.5 * IDX_DIM ** -0.5)


def _in_proj(x3, shift3, scale3, tabs, wts, tm):
    nb, t, d = x3.shape
    mod_rows = shift3.shape[1]
    tab_rows = tabs[0].shape[0]
    grid = (nb, t // tm)
    if mod_rows == 1:
        mod_spec = pl.BlockSpec((1, 1, d), lambda b, i: (b, 0, 0))
    else:
        mod_spec = pl.BlockSpec((1, tm, d), lambda b, i: (b, i, 0))
    if tab_rows == 1:
        tab_spec = pl.BlockSpec((1, LANES), lambda b, i: (0, 0))
    else:
        tab_spec = pl.BlockSpec((tm, LANES), lambda b, i: (i, 0))
    const = lambda shape: pl.BlockSpec(shape, lambda b, i: tuple(0 for _ in shape))
    row = lambda width: pl.BlockSpec((1, tm, width), lambda b, i: (b, i, 0))
    col = lambda rows: pl.BlockSpec((1, rows, tm), lambda b, i: (b, 0, i))
    out_defs = [
        ("q", (t, ATT_WIDTH), BF16, row(ATT_WIDTH)),
        ("k", (t, KV_WIDTH), F32, row(KV_WIDTH)),
        ("kbf", (t, KV_WIDTH), BF16, row(KV_WIDTH)),
        ("v", (t, KV_WIDTH), F32, row(KV_WIDTH)),
        ("vt", (KV_WIDTH, t), BF16, col(KV_WIDTH)),
        ("qi", (t, N_IDX_HEADS * IDX_DIM), BF16, row(N_IDX_HEADS * IDX_DIM)),
        ("ki", (t, IDX_DIM), F32, row(IDX_DIM)),
        ("kibf", (t, IDX_DIM), BF16, row(IDX_DIM)),
        ("wit", (N_IDX_HEADS, t), F32, col(N_IDX_HEADS)),
        ("sg", (t, ATT_WIDTH), F32, row(ATT_WIDTH)),
        ("rkv", (t, 3 * RWKV_WIDTH), F32, row(3 * RWKV_WIDTH)),
        ("lora", (t, 2 * LORA), F32, row(2 * LORA)),
        ("gr", (t, RWKV_WIDTH), F32, row(RWKV_WIDTH)),
    ]
    outs = pl.pallas_call(
        _inproj_kernel,
        grid=grid,
        in_specs=[pl.BlockSpec((1, tm, d), lambda b, i: (b, i, 0)), mod_spec, mod_spec, const((1, d)),
                  const(wts["w_main"].shape), const(wts["w_wit"].shape), const(wts["w_vt"].shape),
                  tab_spec, tab_spec, tab_spec, const((1, LANES)), const((1, LANES))],
        out_specs=[o[3] for o in out_defs],
        out_shape=[jax.ShapeDtypeStruct((nb,) + o[1], o[2]) for o in out_defs],
        compiler_params=_cparams(("arbitrary", "arbitrary")),
        name="in_proj",
    )(x3, shift3, scale3, wts["norm_w"], wts["w_main"], wts["w_wit"], wts["w_vt"], tabs[0], tabs[1], tabs[2],
      wts["qnw"], wts["knw"])
    return {o[0]: arr for o, arr in zip(out_defs, outs)}


def _rope_tables(pos):
    half = ROPE_DIMS // 2
    inv = jnp.power(ROPE_THETA, -jnp.arange(half, dtype=F32) / half)
    ang = pos.astype(F32)[:, None] * inv[None, :]
    cos, sin = jnp.cos(ang), jnp.sin(ang)
    n = pos.shape[0]
    pad = jnp.zeros((n, HEAD_DIM - ROPE_DIMS), F32)
    zero = jnp.zeros((n, half), F32)
    c_head = jnp.concatenate([cos, cos, pad + 1.0], axis=1)
    s1_head = jnp.concatenate([-sin, zero, pad], axis=1)
    s2_head = jnp.concatenate([zero, sin, pad], axis=1)
    two = lambda a: jnp.concatenate([a, a], axis=1)
    return two(c_head), two(s1_head), two(s2_head)


def _head_sum(t):
    ones = _head_block_ones(LANES)
    parts = [_dot_hi(t[:, s * LANES:(s + 1) * LANES], ones) for s in range(t.shape[1] // LANES)]
    return parts[0] if len(parts) == 1 else jnp.concatenate(parts, axis=1)


def _softplus(z):
    return jnp.maximum(z, 0.0) + jnp.log(1.0 + jnp.exp(-jnp.abs(z)))


def _rwkv_prep(xs, xl, prev, prevl, p):
    xm = xs + p["mu_rkv"] * (prev - xs)
    xml = xl + p["mu_lora"] * (prevl - xl)
    w = RWKV_WIDTH
    r, k, v = xm[:, :w], xm[:, w:2 * w], xm[:, 2 * w:]
    lane = _iota(xml.shape, 1)
    lhs = jnp.where(lane < LORA, jnp.tanh(xml), xml)
    lo = _dot(lhs.astype(BF16), p["w_lora"])
    wlog = -_softplus(-(p["w0"] + lo[:, :w])) - 0.5
    logw = -jnp.exp(wlog)
    a = jax.nn.sigmoid(p["a0"] + lo[:, w:])
    kkr = k * p["k_k"]
    kk = kkr / jnp.maximum(jnp.sqrt(_head_sum(kkr * kkr)), 1e-12)
    k_mod = k * (1.0 + (a - 1.0) * p["k_a"])
    bonus = _head_sum(r * k_mod * p["r_k"]) * v
    return r, logw, k_mod, v, -kk, kk * a, bonus


def _rwkv_epilogue(y, bonus, gate, p):
    mean = _head_sum(y) * (1.0 / HEAD_DIM)
    d = y - mean
    var = _head_sum(d * d) * (1.0 / HEAD_DIM)
    yn = d * lax.rsqrt(var + GN_EPS) * p["ln_w"] + p["ln_b"]
    return (yn + bonus) * _silu(gate)


_RW_PARAM_NAMES = ("mu_rkv", "mu_lora", "w0", "a0", "w_lora", "k_k", "k_a", "r_k", "ln_w", "ln_b")


def _load_params(refs):
    return {n: r[...] for n, r in zip(_RW_PARAM_NAMES, refs)}


def _rwkv_prompt_kernel(rkv_ref, lora_ref, gr_ref, *rest):
    nparam = len(_RW_PARAM_NAMES)
    p = _load_params(rest[:nparam])
    out_ref, sfin_ref = rest[nparam:nparam + 2]
    st_s, c_rkv, c_lora, r_s, lw_s, k_s, v_s, al_s, be_s, y_s = rest[nparam + 2:]
    i = pl.program_id(1)
    C = RW_CHUNK
    W = 2 * C

    @pl.when(i == 0)
    def _():
        st_s[...] = jnp.zeros_like(st_s)
        c_rkv[...] = jnp.zeros_like(c_rkv)
        c_lora[...] = jnp.zeros_like(c_lora)

    xs, xl = rkv_ref[0], lora_ref[0]
    tb = xs.shape[0]
    row = _iota((tb, 1), 0)
    prev = jnp.where(row == 0, c_rkv[...], pltpu.roll(xs, 1, 0))
    prevl = jnp.where(row == 0, c_lora[...], pltpu.roll(xl, 1, 0))
    c_rkv[...] = xs[tb - 1:tb, :]
    c_lora[...] = xl[tb - 1:tb, :]
    r, logw, k_mod, v, al, be, bonus = _rwkv_prep(xs, xl, prev, prevl, p)
    r_s[...], lw_s[...], k_s[...], v_s[...], al_s[...], be_s[...] = r, logw, k_mod, v, al, be

    rm, cn = _iota((W, W), 0), _iota((W, W), 1)
    tt, ss = rm % C, cn % C
    top, left = rm < C, cn < C
    strict, incl = ss < tt, ss <= tt
    mk_l_lo, mk_l_hi = top & left & strict, (~top) & (~left) & strict
    mk_z_lo, mk_z_hi = top & (~left) & strict, (~top) & left & strict
    mk_w = (_iota((C, W), 1) % C) <= _iota((C, W), 0)
    bd = (rm // HEAD_DIM) == (cn // HEAD_DIM)
    eye = jnp.where(rm == cn, 1.0, 0.0).astype(F32)
    tril = jnp.where(_iota((C, C), 1) <= _iota((C, C), 0), 1.0, 0.0).astype(F32)
    lane_lo = _iota((C, LANES), 1) < HEAD_DIM
    m_lo = jnp.where(lane_lo, 1.0, 0.0).astype(F32)
    m_hi = 1.0 - m_lo

    def chunk(c, carry):
        c0 = pl.multiple_of(c * C, C)
        rows = pl.ds(c0, C)
        cum_all = _dot_hi(tril, lw_s[rows, :])
        bf = lambda a: a.astype(BF16)
        for pr in range(N_RWKV_HEADS // 2):
            sl = slice(pr * LANES, (pr + 1) * LANES)
            cum = cum_all[:, sl]
            lw = lw_s[rows, sl]
            cum_c = cum[C - 1:C, :]
            e_neg, e_pos, e_prev, e_rel = jnp.exp(-cum), jnp.exp(cum), jnp.exp(cum - lw), jnp.exp(cum_c - cum)
            rr, kk, vv, aa, bb = r_s[rows, sl], k_s[rows, sl], v_s[rows, sl], al_s[rows, sl], be_s[rows, sl]
            a_t, r_t = aa * e_prev, rr * e_pos
            b_t, k_t = bb * e_neg, kk * e_neg
            xk = bf(jnp.concatenate([b_t, k_t], axis=0))
            kx = bf(jnp.concatenate([k_t, b_t], axis=0))
            g_lo = _dot_nt(bf(jnp.concatenate([a_t * m_lo, r_t * m_lo], axis=0)), xk)
            g_hi = _dot_nt(bf(jnp.concatenate([r_t * m_hi, a_t * m_hi], axis=0)), kx)
            l_mat = jnp.where(mk_l_lo, g_lo, 0.0) + jnp.where(mk_l_hi, g_hi, 0.0)
            z_mat = jnp.where(mk_z_lo, g_lo, 0.0) + jnp.where(mk_z_hi, g_hi, 0.0)
            w_lo = jnp.where(mk_w, g_lo[C:, :], 0.0)
            w_hi = jnp.where(mk_w, g_hi[:C, :], 0.0)
            inv = eye + l_mat
            pw = l_mat
            for _ in range(5):
                pw_b = bf(pw)
                pw = _dot(pw_b, pw_b)
                inv = inv + _dot(bf(inv), bf(pw))
            st = st_s[pr]
            st_b = bf(st)
            a_s = _dot_nt(bf(a_t), st_b)
            vv_b = bf(vv)
            rhs = _dot(bf(z_mat), jnp.concatenate([vv_b, vv_b], axis=0)) + jnp.concatenate([a_s, a_s], axis=0)
            u_st = _dot(bf(inv), bf(rhs))
            u = jnp.where(lane_lo, u_st[:C], u_st[C:])
            uv = jnp.concatenate([u, vv], axis=0)
            uv_b = bf(uv)
            vu_b = jnp.concatenate([vv_b, bf(u)], axis=0)
            y_in = jnp.where(lane_lo, _dot(bf(w_lo), uv_b), _dot(bf(w_hi), vu_b))
            y_s[rows, sl] = _dot_nt(bf(r_t), st_b) + y_in
            upd = _dot_tn(uv, jnp.concatenate([bb * e_rel, kk * e_rel], axis=0))
            st_s[pr] = st * jnp.exp(cum_c) + jnp.where(bd, upd, 0.0)
        return carry

    lax.fori_loop(0, tb // C, chunk, 0)
    out_ref[0] = _rwkv_epilogue(y_s[...], bonus, gr_ref[0], p)

    @pl.when(i == pl.num_programs(1) - 1)
    def _():
        for hd in range(N_RWKV_HEADS):
            o = (hd % 2) * HEAD_DIM
            sfin_ref[0, hd] = st_s[hd // 2][o:o + HEAD_DIM, o:o + HEAD_DIM]


def _rwkv_prompt(rkv, lora, gr, params):
    nb, t, _ = rkv.shape
    tb = min(RW_BLOCK, t)
    w = RWKV_WIDTH
    blk = lambda width: pl.BlockSpec((1, tb, width), lambda b, i: (b, i, 0))
    pspecs = [pl.BlockSpec(params[n].shape, lambda b, i: (0, 0)) for n in _RW_PARAM_NAMES]
    vec = lambda: pltpu.VMEM((tb, w), F32)
    return pl.pallas_call(
        _rwkv_prompt_kernel,
        grid=(nb, t // tb),
        in_specs=[blk(3 * w), blk(2 * LORA), blk(w)] + pspecs,
        out_specs=[blk(w), pl.BlockSpec((1, N_RWKV_HEADS, HEAD_DIM, HEAD_DIM), lambda b, i: (b, 0, 0, 0))],
        out_shape=[jax.ShapeDtypeStruct((nb, t, w), F32),
                   jax.ShapeDtypeStruct((nb, N_RWKV_HEADS, HEAD_DIM, HEAD_DIM), F32)],
        scratch_shapes=[pltpu.VMEM((N_RWKV_HEADS // 2, LANES, LANES), F32),
                        pltpu.VMEM((1, 3 * w), F32), pltpu.VMEM((1, 2 * LORA), F32),
                        vec(), vec(), vec(), vec(), vec(), vec(), vec()],
        compiler_params=_cparams(("arbitrary", "arbitrary")),
        name="rwkv_prompt",
    )(rkv, lora, gr, *[params[n] for n in _RW_PARAM_NAMES])


def _rwkv_dec_prep_kernel(rkv_ref, lora_ref, prev_ref, prevl_ref, *rest):
    nparam = len(_RW_PARAM_NAMES)
    p = _load_params(rest[:nparam])
    outs = rest[nparam:]
    r, logw, k_mod, v, al, be, bonus = _rwkv_prep(rkv_ref[...], lora_ref[...], prev_ref[...], prevl_ref[...], p)
    for o, val in zip(outs, (r, jnp.exp(logw), k_mod, v, al, be, bonus)):
        o[...] = val


def _rwkv_dec_prep(rkv, lora, prev, prevl, params):
    n = rkv.shape[0]
    full = lambda a: pl.BlockSpec(a.shape, lambda: tuple(0 for _ in a.shape))
    args = [rkv, lora, prev, prevl] + [params[k] for k in _RW_PARAM_NAMES]
    shp = jax.ShapeDtypeStruct((n, RWKV_WIDTH), F32)
    return pl.pallas_call(
        _rwkv_dec_prep_kernel,
        in_specs=[full(a) for a in args],
        out_specs=[pl.BlockSpec((n, RWKV_WIDTH), lambda: (0, 0))] * 7,
        out_shape=[shp] * 7,
        compiler_params=pltpu.CompilerParams(vmem_limit_bytes=VMEM_LIMIT),
        name="rwkv_dec_prep",
    )(*args)


def _rwkv_dec_state_kernel(r_ref, w_ref, k_ref, v_ref, a_ref, b_ref, s_ref, so_ref, y_ref):
    nseq = s_ref.shape[0]
    n = HEAD_DIM
    eye = jnp.where(_iota((n, n), 0) == _iota((n, n), 1), 1.0, 0.0).astype(F32)

    def one(idx, carry):
        s, hd = idx // N_RWKV_HEADS, idx % N_RWKV_HEADS
        row = lambda ref: ref[s, pl.ds(hd, 1), :]
        st = s_ref[s, hd]
        sa = jnp.sum(st * row(a_ref), axis=1, keepdims=True)
        v_col = jnp.sum(eye * row(v_ref), axis=1, keepdims=True)
        sn = st * row(w_ref) + sa * row(b_ref) + v_col * row(k_ref)
        y_col = jnp.sum(sn * row(r_ref), axis=1, keepdims=True)
        so_ref[s, hd] = sn
        y_ref[s, pl.ds(hd, 1), :] = jnp.sum(eye * y_col, axis=0, keepdims=True)
        return carry

    lax.fori_loop(0, nseq * N_RWKV_HEADS, one, 0)


def _rwkv_dec_state(vecs, state):
    nb = state.shape[0]
    sb = 8
    vspec = pl.BlockSpec((sb, N_RWKV_HEADS, HEAD_DIM), lambda i: (i, 0, 0))
    sspec = pl.BlockSpec((sb, N_RWKV_HEADS, HEAD_DIM, HEAD_DIM), lambda i: (i, 0, 0, 0))
    return pl.pallas_call(
        _rwkv_dec_state_kernel,
        grid=(nb // sb,),
        in_specs=[vspec] * 6 + [sspec],
        out_specs=[sspec, vspec],
        out_shape=[jax.ShapeDtypeStruct(state.shape, F32),
                   jax.ShapeDtypeStruct((nb, N_RWKV_HEADS, HEAD_DIM), F32)],
        compiler_params=_cparams(("arbitrary",)),
        name="rwkv_dec_state",
    )(*[a.reshape(nb, N_RWKV_HEADS, HEAD_DIM) for a in vecs], state)


def _rwkv_dec_epi_kernel(y_ref, bonus_ref, gr_ref, lnw_ref, lnb_ref, o_ref):
    o_ref[...] = _rwkv_epilogue(y_ref[...], bonus_ref[...], gr_ref[...], {"ln_w": lnw_ref[...], "ln_b": lnb_ref[...]})


def _rwkv_dec_epi(y, bonus, gr, params):
    n = y.shape[0]
    args = [y, bonus, gr, params["ln_w"], params["ln_b"]]
    return pl.pallas_call(
        _rwkv_dec_epi_kernel,
        in_specs=[pl.BlockSpec(a.shape, lambda: (0, 0)) for a in args],
        out_specs=pl.BlockSpec((n, RWKV_WIDTH), lambda: (0, 0)),
        out_shape=jax.ShapeDtypeStruct((n, RWKV_WIDTH), F32),
        name="rwkv_dec_epi",
    )(*args)


def _attn_prompt_kernel(q_ref, qi_ref, wit_ref, k_ref, vt_ref, kidx_ref, sg_ref, tri_ref, o_ref,
                        s_scr, m_scr, acc_scr, *, topk):
    j = pl.program_id(1)
    QB, KC = Q_BLOCK, KEY_CHUNK
    nch = (j * QB) // KC + 1
    inf = jnp.inf
    qpos_i = j * QB + _iota((1, QB), 1)
    k_q = jnp.minimum(float(topk), qpos_i.astype(F32) + 1.0)
    wit = wit_ref[0]
    qi = qi_ref[0]
    qi_stack = jnp.concatenate([qi[:, h * IDX_DIM:(h + 1) * IDX_DIM] for h in range(N_IDX_HEADS)], axis=0)

    def chunk_rows(c):
        return pl.ds(pl.multiple_of(c * KC, KC), KC)

    def score_chunk(c, carry):
        rows = chunk_rows(c)
        st = _dot_nt(kidx_ref[0, rows, :], qi_stack)
        acc = jnp.zeros((KC, QB), F32)
        for h in range(N_IDX_HEADS):
            acc = acc + jnp.maximum(st[:, h * QB:(h + 1) * QB], 0.0) * wit[h:h + 1, :]
        key = c * KC + _iota((KC, QB), 0)
        s_scr[rows, :] = jnp.where(key <= qpos_i, acc, -inf)
        return carry

    lax.fori_loop(0, nch, score_chunk, 0)

    def sweep(fn, init):
        return lax.fori_loop(0, nch, lambda c, car: fn(car, s_scr[chunk_rows(c), :]), init)

    full = lambda val: jnp.full((1, QB), val, F32)
    cmin = lambda a: jnp.min(a, axis=0, keepdims=True)
    csum = lambda a: jnp.sum(a, axis=0, keepdims=True)

    lo = sweep(lambda c, s: jnp.minimum(c, cmin(jnp.where(s > -inf, s, inf))), full(inf))
    hi = sweep(lambda c, s: jnp.maximum(c, jnp.max(s, axis=0, keepdims=True)), full(-inf))

    def bisect(_, lh):
        lo, hi = lh
        mid = 0.5 * lo + 0.5 * hi
        cnt = sweep(lambda c, s: c + csum(jnp.where(s >= mid, 1.0, 0.0)), full(0.0))
        ge = cnt >= k_q
        return jnp.where(ge, mid, lo), jnp.where(ge, hi, mid)

    lo, hi = lax.fori_loop(0, N_BISECT, bisect, (lo, hi))
    v0 = sweep(lambda c, s: jnp.minimum(c, cmin(jnp.where(s >= lo, s, inf))), full(inf))

    def refine(state):
        v, _, _ = state

        def f(car, s):
            g = s > v
            return car[0] + csum(jnp.where(g, 1.0, 0.0)), jnp.minimum(car[1], cmin(jnp.where(g, s, inf)))

        cnt, v2 = sweep(f, (full(0.0), full(inf)))
        ok = cnt < k_q
        return jnp.where(ok, v, v2), cnt, jnp.sum(jnp.where(ok, 0.0, 1.0))

    thr, c_gt, _ = lax.while_loop(lambda st: st[2] > 0.0, refine, (v0, full(0.0), jnp.float32(1.0)))
    need = k_q - c_gt

    q = q_ref[0]
    G, R = N_KV_HEADS, N_Q_HEADS // N_KV_HEADS
    q_stack = [jnp.concatenate([q[:, (g * R + r) * HEAD_DIM:(g * R + r + 1) * HEAD_DIM] for r in range(R)], axis=0)
               for g in range(G)]
    ones_rows = jnp.where(_iota((HEAD_DIM, KC), 0) == 0, 1.0, 0.0).astype(BF16)
    m_scr[...] = jnp.full(m_scr.shape, -inf, F32)
    acc_scr[...] = jnp.zeros_like(acc_scr)

    def attend(c, tie_carry):
        rows = chunk_rows(c)
        s = s_scr[rows, :]
        eq = s == thr
        rank = tie_carry + _dot(tri_ref[...], jnp.where(eq, 1.0, 0.0).astype(BF16))
        sel = (s > thr) | (eq & (rank < need))
        kc = k_ref[0, rows, :]
        for g in range(G):
            lg = _dot_nt(kc[:, g * HEAD_DIM:(g + 1) * HEAD_DIM], q_stack[g])
            lm = jnp.concatenate([jnp.where(sel, lg[:, r * QB:(r + 1) * QB], -inf) for r in range(R)], axis=1)
            m_old = m_scr[g]
            m_new = jnp.maximum(m_old, jnp.max(lm, axis=0, keepdims=True))
            m_safe = jnp.where(m_new == -inf, 0.0, m_new)
            alpha = jnp.exp(m_old - m_safe)
            p = jnp.exp(lm - m_safe).astype(BF16)
            vt_ext = jnp.concatenate([vt_ref[0, g * HEAD_DIM:(g + 1) * HEAD_DIM, rows], ones_rows], axis=0)
            acc_scr[g] = acc_scr[g] * alpha + _dot(vt_ext, p)
            m_scr[g] = m_new
        return tie_carry + csum(jnp.where(eq, 1.0, 0.0))

    lax.fori_loop(0, nch, attend, full(0.0))

    out_t = jnp.concatenate([acc_scr[g][:HEAD_DIM] / acc_scr[g][HEAD_DIM:HEAD_DIM + 1] for g in range(G)], axis=0)
    per_r = [out_t[:, r * QB:(r + 1) * QB].T for r in range(R)]
    att = jnp.concatenate([per_r[r][:, g * HEAD_DIM:(g + 1) * HEAD_DIM] for g in range(G) for r in range(R)], axis=1)
    o_ref[0] = att * sg_ref[0]


def _attn_prompt(q, qi, wit, kbf, vt, kibf, sg):
    nb, t, _ = q.shape
    topk = min(TOPK_MAX, t // 4)
    kc = KEY_CHUNK
    tri = jnp.where(jnp.arange(kc)[None, :] < jnp.arange(kc)[:, None], 1.0, 0.0).astype(BF16)
    qblk = lambda width: pl.BlockSpec((1, Q_BLOCK, width), lambda b, j: (b, j, 0))
    whole = lambda r, c: pl.BlockSpec((1, r, c), lambda b, j: (b, 0, 0))
    return pl.pallas_call(
        functools.partial(_attn_prompt_kernel, topk=topk),
        grid=(nb, t // Q_BLOCK),
        in_specs=[qblk(ATT_WIDTH), qblk(N_IDX_HEADS * IDX_DIM),
                  pl.BlockSpec((1, N_IDX_HEADS, Q_BLOCK), lambda b, j: (b, 0, j)),
                  whole(t, KV_WIDTH), whole(KV_WIDTH, t), whole(t, IDX_DIM), qblk(ATT_WIDTH),
                  pl.BlockSpec((kc, kc), lambda b, j: (0, 0))],
        out_specs=qblk(ATT_WIDTH),
        out_shape=jax.ShapeDtypeStruct((nb, t, ATT_WIDTH), F32),
        scratch_shapes=[pltpu.VMEM((t, Q_BLOCK), F32),
                        pltpu.VMEM((N_KV_HEADS, 1, N_Q_HEADS // N_KV_HEADS * Q_BLOCK), F32),
                        pltpu.VMEM((N_KV_HEADS, 2 * HEAD_DIM, N_Q_HEADS // N_KV_HEADS * Q_BLOCK), F32)],
        compiler_params=_cparams(("arbitrary", "arbitrary")),
        name="attn_prompt",
    )(q, qi, wit, kbf, vt, kibf, sg, tri)


def _outproj_kernel(att_ref, rw_ref, w_ref, x_ref, g_ref, o_ref):
    mix = _dot(att_ref[0].astype(BF16), w_ref[:ATT_WIDTH, :]) + _dot(rw_ref[0].astype(BF16), w_ref[ATT_WIDTH:, :])
    o_ref[0] = x_ref[0] + g_ref[0] * mix


def _out_proj(att, rw, w_out_bf, x3, gate3, tm):
    nb, t, d = x3.shape
    if gate3.shape[1] == 1:
        gspec = pl.BlockSpec((1, 1, d), lambda b, i: (b, 0, 0))
    else:
        gspec = pl.BlockSpec((1, tm, d), lambda b, i: (b, i, 0))
    blk = lambda width: pl.BlockSpec((1, tm, width), lambda b, i: (b, i, 0))
    return pl.pallas_call(
        _outproj_kernel,
        grid=(nb, t // tm),
        in_specs=[blk(ATT_WIDTH), blk(RWKV_WIDTH), pl.BlockSpec(w_out_bf.shape, lambda b, i: (0, 0)), blk(d), gspec],
        out_specs=blk(d),
        out_shape=jax.ShapeDtypeStruct((nb, t, d), F32),
        compiler_params=_cparams(("arbitrary", "arbitrary")),
        name="out_proj",
    )(att, rw, w_out_bf, x3, gate3)


def _dec_score_kernel(pt_ref, qi_ref, wi_ref, *refs):
    npg = len(refs) - 1
    o_ref = refs[npg]
    qi = qi_ref[0]
    qi16 = jnp.concatenate([qi, jnp.zeros_like(qi)], axis=0).astype(BF16)
    wi = wi_ref[0]
    for i in range(npg):
        s = _dot_nt(qi16, refs[i][0].astype(BF16))[:N_IDX_HEADS]
        o_ref[0, :, i * PAGE_SIZE:(i + 1) * PAGE_SIZE] = jnp.sum(jnp.maximum(s, 0.0) * wi, axis=0, keepdims=True)


def _page_specs(n, block, npg):
    zeros = (0,) * (len(block) - 1)
    return [pl.BlockSpec(block, functools.partial(lambda b, g, pt, i: (pt[b, g * npg + i],) + zeros, i=i))
            for i in range(n)]


def _dec_scores(page_table, qi3, wi3, cache_kidx):
    nb, n_pages = page_table.shape
    npg = min(PAGES_PER_STEP, n_pages)
    grid_spec = pltpu.PrefetchScalarGridSpec(
        num_scalar_prefetch=1,
        grid=(nb, n_pages // npg),
        in_specs=[pl.BlockSpec((1, N_IDX_HEADS, IDX_DIM), lambda b, g, pt: (b, 0, 0)),
                  pl.BlockSpec((1, N_IDX_HEADS, 1), lambda b, g, pt: (b, 0, 0))]
                 + _page_specs(npg, (1, PAGE_SIZE, IDX_DIM), npg),
        out_specs=pl.BlockSpec((1, 1, npg * PAGE_SIZE), lambda b, g, pt: (b, 0, g)),
    )
    return pl.pallas_call(
        _dec_score_kernel,
        grid_spec=grid_spec,
        out_shape=jax.ShapeDtypeStruct((nb, 1, n_pages * PAGE_SIZE), F32),
        compiler_params=_cparams(("arbitrary", "arbitrary")),
        name="dec_scores",
    )(page_table, qi3, wi3, *([cache_kidx] * npg))


def _dec_select_kernel(s_ref, qi_ref, ki_ref, wi_ref, sel_ref, sx, *, topk):
    nb, past = s_ref.shape
    inf = jnp.inf
    qi = qi_ref[...].astype(F32)
    ki = ki_ref[...].astype(F32)
    prod = qi * jnp.concatenate([ki] * N_IDX_HEADS, axis=1)
    nq = N_IDX_HEADS * IDX_DIM
    hsel = jnp.where(_iota((nq, LANES), 0) // IDX_DIM == _iota((nq, LANES), 1), 1.0, 0.0).astype(F32)
    sh = _dot_hi(prod, hsel)[:, :N_IDX_HEADS]
    s_new = jnp.sum(jnp.maximum(sh, 0.0) * wi_ref[...], axis=1, keepdims=True)
    sx[:, :past] = s_ref[...]
    sx[:, past:] = jnp.where(_iota((nb, LANES), 1) == 0, s_new, -inf)

    rsum = lambda a: jnp.sum(a, axis=1, keepdims=True)
    k_q = float(topk)
    s = sx[...]
    lo = jnp.min(jnp.where(s > -inf, s, inf), axis=1, keepdims=True)
    hi = jnp.max(s, axis=1, keepdims=True)

    def bisect(_, lh):
        lo, hi = lh
        mid = 0.5 * lo + 0.5 * hi
        ge = rsum(jnp.where(sx[...] >= mid, 1.0, 0.0)) >= k_q
        return jnp.where(ge, mid, lo), jnp.where(ge, hi, mid)

    lo, hi = lax.fori_loop(0, N_BISECT, bisect, (lo, hi))
    s = sx[...]
    v0 = jnp.min(jnp.where(s >= lo, s, inf), axis=1, keepdims=True)

    def refine(state):
        v, _, _ = state
        s = sx[...]
        g = s > v
        cnt = rsum(jnp.where(g, 1.0, 0.0))
        v2 = jnp.min(jnp.where(g, s, inf), axis=1, keepdims=True)
        ok = cnt < k_q
        return jnp.where(ok, v, v2), cnt, jnp.sum(jnp.where(ok, 0.0, 1.0))

    thr, c_gt, _ = lax.while_loop(lambda st: st[2] > 0.0, refine,
                                  (v0, jnp.zeros((nb, 1), F32), jnp.float32(1.0)))
    need = k_q - c_gt
    upper = jnp.where(_iota((LANES, LANES), 0) < _iota((LANES, LANES), 1), 1.0, 0.0).astype(BF16)
    carry = jnp.zeros((nb, 1), F32)
    for c in range((past + LANES) // LANES):
        sc = sx[:, c * LANES:(c + 1) * LANES]
        eq = sc == thr
        eqf = jnp.where(eq, 1.0, 0.0)
        rank = carry + _dot(eqf.astype(BF16), upper)
        sel_ref[:, c * LANES:(c + 1) * LANES] = jnp.where((sc > thr) | (eq & (rank < need)), 1.0, 0.0)
        carry = carry + rsum(eqf)


def _dec_select(scores, qi, ki, wi, topk):
    nb, past = scores.shape
    args = [scores, qi, ki, wi]
    return pl.pallas_call(
        functools.partial(_dec_select_kernel, topk=topk),
        in_specs=[pl.BlockSpec(a.shape, lambda: (0, 0)) for a in args],
        out_specs=pl.BlockSpec((nb, past + LANES), lambda: (0, 0)),
        out_shape=jax.ShapeDtypeStruct((nb, past + LANES), F32),
        scratch_shapes=[pltpu.VMEM((nb, past + LANES), F32)],
        compiler_params=pltpu.CompilerParams(vmem_limit_bytes=VMEM_LIMIT),
        name="dec_select",
    )(*args)


def _dec_attn_kernel(pt_ref, q_ref, sel_ref, selnew_ref, knew_ref, vnew_ref, sg_ref, *refs, npg):
    k_pages, v_pages = refs[:npg], refs[npg:2 * npg]
    o_ref, m_s, l_s, acc_s = refs[2 * npg:]
    g = pl.program_id(1)
    inf = jnp.inf
    nh = N_Q_HEADS
    R = N_Q_HEADS // N_KV_HEADS

    @pl.when(g == 0)
    def _():
        m_s[...] = jnp.full(m_s.shape, -inf, F32)
        l_s[...] = jnp.zeros_like(l_s)
        acc_s[...] = jnp.zeros_like(acc_s)

    q8 = q_ref[0].astype(F32)
    own = (_iota((nh, KV_WIDTH), 0) // R) == (_iota((nh, KV_WIDTH), 1) // HEAD_DIM)
    q_ext = jnp.where(own, jnp.concatenate([q8, q8], axis=1), 0.0)
    q16 = jnp.concatenate([q_ext, jnp.zeros_like(q_ext)], axis=0).astype(BF16)
    m, l, acc = m_s[...], l_s[...], acc_s[...]

    def fold(m, l, acc, lm, pv_fn):
        m_new = jnp.maximum(m, jnp.max(lm, axis=1, keepdims=True))
        m_safe = jnp.where(m_new == -inf, 0.0, m_new)
        alpha = jnp.exp(m - m_safe)
        p = jnp.exp(lm - m_safe)
        return m_new, alpha * l + jnp.sum(p, axis=1, keepdims=True), alpha * acc + pv_fn(p)

    for i in range(npg):
        kp = k_pages[i][0].astype(BF16)
        vp = v_pages[i][0].astype(BF16)
        lg = _dot_nt(q16, kp)[:nh]
        msk = sel_ref[0, :, i * PAGE_SIZE:(i + 1) * PAGE_SIZE] > 0.5
        lm = jnp.where(msk, lg, -inf)

        def pv(p, vp=vp):
            p16 = jnp.concatenate([p, jnp.zeros_like(p)], axis=0).astype(BF16)
            return _dot(p16, vp)[:nh]

        m, l, acc = fold(m, l, acc, lm, pv)

    m_s[...], l_s[...], acc_s[...] = m, l, acc

    @pl.when(g == pl.num_programs(1) - 1)
    def _():
        k_new = knew_ref[0].astype(BF16).astype(F32)
        v_new = vnew_ref[0].astype(BF16).astype(F32)
        lg = jnp.sum(q16[:nh].astype(F32) * k_new, axis=1, keepdims=True)
        lm = jnp.where(selnew_ref[0, :, 0:1] > 0.5, lg, -inf)
        m2, l2, acc2 = fold(m, l, acc, lm, lambda p: p.astype(BF16).astype(F32) * v_new)
        out = acc2 / l2
        res = jnp.where(_iota((nh, HEAD_DIM), 0) < R, out[:, :HEAD_DIM], out[:, HEAD_DIM:])
        o_ref[0] = res * sg_ref[0]


def _dec_attn(page_table, q3, sel3, k_new3, v_new3, sg3, cache_k2, cache_v2):
    nb, n_pages = page_table.shape
    npg = min(PAGES_PER_STEP, n_pages)
    past = n_pages * PAGE_SIZE
    per_seq = lambda r, c: pl.BlockSpec((1, r, c), lambda b, g, pt: (b, 0, 0))
    grid_spec = pltpu.PrefetchScalarGridSpec(
        num_scalar_prefetch=1,
        grid=(nb, n_pages // npg),
        in_specs=[per_seq(N_Q_HEADS, HEAD_DIM),
                  pl.BlockSpec((1, 1, npg * PAGE_SIZE), lambda b, g, pt: (b, 0, g)),
                  pl.BlockSpec((1, 1, LANES), lambda b, g, pt: (b, 0, past // LANES)),
                  per_seq(1, KV_WIDTH), per_seq(1, KV_WIDTH), per_seq(N_Q_HEADS, HEAD_DIM)]
                 + _page_specs(npg, (1, PAGE_SIZE, KV_WIDTH), npg) + _page_specs(npg, (1, PAGE_SIZE, KV_WIDTH), npg),
        out_specs=per_seq(N_Q_HEADS, HEAD_DIM),
        scratch_shapes=[pltpu.VMEM((N_Q_HEADS, 1), F32), pltpu.VMEM((N_Q_HEADS, 1), F32),
                        pltpu.VMEM((N_Q_HEADS, KV_WIDTH), F32)],
    )
    return pl.pallas_call(
        functools.partial(_dec_attn_kernel, npg=npg),
        grid_spec=grid_spec,
        out_shape=jax.ShapeDtypeStruct((nb, N_Q_HEADS, HEAD_DIM), F32),
        compiler_params=_cparams(("arbitrary", "arbitrary")),
        name="dec_attn",
    )(page_table, q3, sel3, sel3, k_new3, v_new3, sg3, *([cache_k2] * npg), *([cache_v2] * npg))


def kernel(x_prompt, x_sample, cache_k, cache_v, cache_kidx, state_wkv, state_shift, page_table,
           c_prompt, c_sample, norm_w, w_ada, b_ada, w_in, q_norm_w, k_norm_w, mu_shift, w0, w_up,
           a0, a_up, k_k, k_a, r_k, ln_x_w, ln_x_b, w_out):
    nb, t, d = x_prompt.shape
    nd = x_sample.shape[0]
    assert x_sample.shape[1] == 1
    n_pages = page_table.shape[1]
    past = n_pages * PAGE_SIZE
    w = RWKV_WIDTH

    offs = [0]
    for sz in (ATT_WIDTH, KV_WIDTH, KV_WIDTH, N_IDX_HEADS * IDX_DIM, N_IDX_HEADS, IDX_DIM, ATT_WIDTH,
               w, w, w, LORA, LORA, w):
        offs.append(offs[-1] + sz)
    col = lambda i: w_in[:, offs[i]:offs[i + 1]]
    w_main = jnp.concatenate([col(0), col(1), col(2), col(3), col(6), col(7), col(8), col(9), col(12),
                              col(10), col(11), col(5), jnp.zeros((d, LANES - IDX_DIM), w_in.dtype)], axis=1).astype(BF16)
    zl = jnp.zeros((LORA, w), F32)
    wts = {
        "norm_w": norm_w.reshape(1, d), "w_main": w_main,
        "w_wit": col(4).T.astype(BF16), "w_vt": col(2).T.astype(BF16),
        "qnw": jnp.concatenate([q_norm_w, q_norm_w]).reshape(1, LANES),
        "knw": jnp.concatenate([k_norm_w, k_norm_w]).reshape(1, LANES),
    }
    rw_params = {
        "mu_rkv": mu_shift[:3 * w].reshape(1, 3 * w), "mu_lora": mu_shift[3 * w:].reshape(1, 2 * LORA),
        "w0": w0.reshape(1, w), "a0": a0.reshape(1, w),
        "w_lora": jnp.concatenate([jnp.concatenate([w_up, zl], axis=1),
                                   jnp.concatenate([zl, a_up], axis=1)], axis=0).astype(BF16),
        "k_k": k_k.reshape(1, w), "k_a": k_a.reshape(1, w), "r_k": r_k.reshape(1, w),
        "ln_w": ln_x_w.reshape(1, w), "ln_b": ln_x_b.reshape(1, w),
    }
    w_out_bf = w_out.astype(BF16)

    n_c = nb + nd
    pad = (-n_c) % 8
    c_all = jnp.concatenate([c_prompt, c_sample, jnp.zeros((pad, d), F32)], axis=0)
    mod = _adaln_mod(c_all, w_ada, b_ada)
    shift_p, scale_p, gate_p = (mod[:nb, s * d:(s + 1) * d].reshape(nb, 1, d) for s in range(3))
    shift_s, scale_s, gate_s = (mod[nb:n_c, s * d:(s + 1) * d].reshape(1, nd, d) for s in range(3))

    tm = min(512, t)
    pr = _in_proj(x_prompt, shift_p, scale_p, _rope_tables(jnp.arange(t)), wts, tm)
    att_p = _attn_prompt(pr["q"], pr["qi"], pr["wit"], pr["kbf"], pr["vt"], pr["kibf"], pr["sg"])
    rw_p, s_p = _rwkv_prompt(pr["rkv"], pr["lora"], pr["gr"], rw_params)
    y_prompt = _out_proj(att_p, rw_p, w_out_bf, x_prompt, gate_p, tm)
    shift_prompt = jnp.concatenate([pr["rkv"][:, t - 1:], pr["lora"][:, t - 1:]], axis=-1)

    xs3 = x_sample.reshape(1, nd, d)
    sm = _in_proj(xs3, shift_s, scale_s, _rope_tables(jnp.full((1,), past)), wts, nd)
    qi_s, ki_s = sm["qi"][0], sm["ki"][0]
    wi_s = sm["wit"][0].T
    scores = _dec_scores(page_table, qi_s.reshape(nd, N_IDX_HEADS, IDX_DIM), wi_s.reshape(nd, N_IDX_HEADS, 1),
                         cache_kidx)
    topk = min(TOPK_MAX, (past + 1) // 4)
    sel = _dec_select(scores.reshape(nd, past), qi_s, sm["kibf"][0], wi_s, topk)
    att_s = _dec_attn(page_table, sm["q"][0].reshape(nd, N_Q_HEADS, HEAD_DIM), sel.reshape(nd, 1, past + LANES),
                      sm["k"][0].reshape(nd, 1, KV_WIDTH), sm["v"][0].reshape(nd, 1, KV_WIDTH),
                      sm["sg"][0].reshape(nd, N_Q_HEADS, HEAD_DIM),
                      cache_k.reshape(-1, PAGE_SIZE, KV_WIDTH), cache_v.reshape(-1, PAGE_SIZE, KV_WIDTH))
    prev = state_shift.reshape(nd, -1)
    vecs = _rwkv_dec_prep(sm["rkv"][0], sm["lora"][0], prev[:, :3 * w], prev[:, 3 * w:], rw_params)
    s_s, y_s = _rwkv_dec_state(vecs[:6], state_wkv)
    rw_s = _rwkv_dec_epi(y_s.reshape(nd, w), vecs[6], sm["gr"][0], rw_params)
    y_sample = _out_proj(att_s.reshape(1, nd, ATT_WIDTH), rw_s.reshape(1, nd, w), w_out_bf, xs3, gate_s, nd)
    shift_sample = jnp.concatenate([sm["rkv"][0], sm["lora"][0]], axis=-1).reshape(nd, 1, 3 * w + 2 * LORA)

    return (y_prompt, y_sample.reshape(nd, 1, d),
            pr["k"].reshape(nb, t, N_KV_HEADS, HEAD_DIM), pr["v"].reshape(nb, t, N_KV_HEADS, HEAD_DIM), pr["ki"],
            s_p, shift_prompt,
            sm["k"][0].reshape(nd, 1, N_KV_HEADS, HEAD_DIM), sm["v"][0].reshape(nd, 1, N_KV_HEADS, HEAD_DIM),
            ki_s.reshape(nd, 1, IDX_DIM), s_s, shift_sample)
```

```python
import functools

import jax
import jax.numpy as jnp
from jax import lax
from jax.experimental import pallas as pl
from jax.experimental.pallas import tpu as pltpu

F32, BF16 = jnp.float32, jnp.bfloat16
HI = lax.Precision.HIGHEST

HEAD_DIM = 64
N_Q_HEADS = 8
N_KV_HEADS = 2
N_IDX_HEADS = 8
IDX_DIM = 64
N_RWKV_HEADS = 8
ATT_WIDTH = N_Q_HEADS * HEAD_DIM
RWKV_WIDTH = N_RWKV_HEADS * HEAD_DIM
KV_WIDTH = N_KV_HEADS * HEAD_DIM
LORA = 64
TOPK_MAX = 256
ROPE_THETA = 500000.0
ROPE_DIMS = HEAD_DIM // 4
NORM_EPS = 1e-6
GN_EPS = 64e-5
PAGE_SIZE = 128
LANES = 128
Q_SCALE = HEAD_DIM ** -0.5 * 1.4426950408889634
VMEM_LIMIT = 56 * 1024 * 1024

Q_BLOCK = 128
KEY_CHUNK = 512
N_BISECT = 16
RW_CHUNK = 64
RW_BLOCK = 512
SCORE_PAGES_PER_STEP = 64
ATTN_PAGES_PER_STEP = 32


def _dot(a, b):
    return jnp.dot(a, b, preferred_element_type=F32)


def _dot_nt(a, b):
    return lax.dot_general(a, b, (((1,), (1,)), ((), ())), preferred_element_type=F32)


def _dot_hi(a, b):
    return jnp.dot(a, b, precision=HI, preferred_element_type=F32)


def _dot_tn(a, b):
    return lax.dot_general(a, b, (((0,), (0,)), ((), ())), preferred_element_type=F32)


def _iota(shape, dim):
    return lax.broadcasted_iota(jnp.int32, shape, dim)


def _head_block_ones(n):
    return jnp.where(_iota((n, n), 0) // HEAD_DIM == _iota((n, n), 1) // HEAD_DIM, 1.0, 0.0).astype(F32)


def _silu(x):
    return x * jax.nn.sigmoid(x)


def _cparams(sem):
    return pltpu.CompilerParams(dimension_semantics=sem, vmem_limit_bytes=VMEM_LIMIT)


def _mod_kernel(c_ref, w_ref, b_ref, o_ref):
    s = _silu(c_ref[...])
    o_ref[...] = _dot(s.astype(BF16), w_ref[...].astype(BF16)) + b_ref[...]


def _adaln_mod(c, w_ada, b_ada):
    rows, d = c.shape
    n = w_ada.shape[1]
    tn = 1024
    return pl.pallas_call(
        _mod_kernel,
        grid=(n // tn,),
        in_specs=[pl.BlockSpec((rows, d), lambda j: (0, 0)),
                  pl.BlockSpec((d, tn), lambda j: (0, j)),
                  pl.BlockSpec((1, tn), lambda j: (0, j))],
        out_specs=pl.BlockSpec((rows, tn), lambda j: (0, j)),
        out_shape=jax.ShapeDtypeStruct((rows, n), F32),
        compiler_params=_cparams(("arbitrary",)),
        name="adaln_mod",
    )(c, w_ada, b_ada.reshape(1, n))


_W_Q, _W_KV, _W_QI, _W_GA, _W_RKV, _W_GR, _W_TAIL, _W_END = 0, 512, 768, 1280, 1792, 3328, 3840, 4096


def _inproj_kernel(x_ref, shift_ref, scale_ref, nw_ref, w_ref, wwi_ref, wvt_ref, cos_ref, s1_ref, s2_ref,
                   qnw_ref, knw_ref,
                   q_o, k_o, kbf_o, v_o, vt_o, qi_o, ki_o, kibf_o, wit_o, sg_o, rkv_o, lora_o, gr_o):
    x = x_ref[0]
    ms = jnp.mean(x * x, axis=-1, keepdims=True)
    h = x * lax.rsqrt(ms + NORM_EPS) * nw_ref[...]
    h = h * (1.0 + scale_ref[0]) + shift_ref[0]
    hb = h.astype(BF16)
    cos, s1, s2 = cos_ref[...], s1_ref[...], s2_ref[...]
    bsum = _head_block_ones(LANES) * (1.0 / HEAD_DIM)

    def seg(a, b):
        return _dot(hb, w_ref[:, a:b])

    def rope(t):
        return t * cos + pltpu.roll(t, LANES - ROPE_DIMS // 2, 1) * s1 + pltpu.roll(t, ROPE_DIMS // 2, 1) * s2

    def head_norm(t, w):
        return t * lax.rsqrt(_dot_hi(t * t, bsum) + NORM_EPS) * w

    yq = seg(_W_Q, _W_KV)
    for s in range(ATT_WIDTH // LANES):
        t = rope(head_norm(yq[:, s * LANES:(s + 1) * LANES], qnw_ref[...]))
        q_o[0, :, s * LANES:(s + 1) * LANES] = (t * Q_SCALE).astype(BF16)

    ykv = seg(_W_KV, _W_QI)
    k = rope(head_norm(ykv[:, :KV_WIDTH], knw_ref[...]))
    k_o[0] = k
    kbf_o[0] = k.astype(BF16)
    v_o[0] = ykv[:, KV_WIDTH:]
    vt_o[0] = _dot_nt(wvt_ref[...], hb).astype(BF16)

    yqi = seg(_W_QI, _W_GA)
    for s in range(N_IDX_HEADS * IDX_DIM // LANES):
        qi_o[0, :, s * LANES:(s + 1) * LANES] = rope(yqi[:, s * LANES:(s + 1) * LANES]).astype(BF16)

    sg_o[0] = _silu(seg(_W_GA, _W_RKV))
    for s in range(3):
        rkv_o[0, :, s * RWKV_WIDTH:(s + 1) * RWKV_WIDTH] = seg(_W_RKV + s * RWKV_WIDTH, _W_RKV + (s + 1) * RWKV_WIDTH)
    gr_o[0] = seg(_W_GR, _W_TAIL)

    yt = seg(_W_TAIL, _W_END)
    lora_o[0] = yt[:, :2 * LORA]
    ki = rope(yt[:, 2 * LORA:])[:, :IDX_DIM]
    ki_o[0] = ki
    kibf_o[0] = ki.astype(BF16)
    wit_o[0] = _dot_nt(wwi_ref[...], hb) * (N_IDX_HEADS ** -0.5 * IDX_DIM ** -0.5)


def _in_proj(x3, shift3, scale3, tabs, wts, tm):
    nb, t, d = x3.shape
    mod_rows = shift3.shape[1]
    tab_rows = tabs[0].shape[0]
    grid = (nb, t // tm)
    if mod_rows == 1:
        mod_spec = pl.BlockSpec((1, 1, d), lambda b, i: (b, 0, 0))
    else:
        mod_spec = pl.BlockSpec((1, tm, d), lambda b, i: (b, i, 0))
    if tab_rows == 1:
        tab_spec = pl.BlockSpec((1, LANES), lambda b, i: (0, 0))
    else:
        tab_spec = pl.BlockSpec((tm, LANES), lambda b, i: (i, 0))
    const = lambda shape: pl.BlockSpec(shape, lambda b, i: tuple(0 for _ in shape))
    row = lambda width: pl.BlockSpec((1, tm, width), lambda b, i: (b, i, 0))
    col = lambda rows: pl.BlockSpec((1, rows, tm), lambda b, i: (b, 0, i))
    out_defs = [
        ("q", (t, ATT_WIDTH), BF16, row(ATT_WIDTH)),
        ("k", (t, KV_WIDTH), F32, row(KV_WIDTH)),
        ("kbf", (t, KV_WIDTH), BF16, row(KV_WIDTH)),
        ("v", (t, KV_WIDTH), F32, row(KV_WIDTH)),
        ("vt", (KV_WIDTH, t), BF16, col(KV_WIDTH)),
        ("qi", (t, N_IDX_HEADS * IDX_DIM), BF16, row(N_IDX_HEADS * IDX_DIM)),
        ("ki", (t, IDX_DIM), F32, row(IDX_DIM)),
        ("kibf", (t, IDX_DIM), BF16, row(IDX_DIM)),
        ("wit", (N_IDX_HEADS, t), F32, col(N_IDX_HEADS)),
        ("sg", (t, ATT_WIDTH), F32, row(ATT_WIDTH)),
        ("rkv", (t, 3 * RWKV_WIDTH), F32, row(3 * RWKV_WIDTH)),
        ("lora", (t, 2 * LORA), F32, row(2 * LORA)),
        ("gr", (t, RWKV_WIDTH), F32, row(RWKV_WIDTH)),
    ]
    outs = pl.pallas_call(
        _inproj_kernel,
        grid=grid,
        in_specs=[pl.BlockSpec((1, tm, d), lambda b, i: (b, i, 0)), mod_spec, mod_spec, const((1, d)),
                  const(wts["w_main"].shape), const(wts["w_wit"].shape), const(wts["w_vt"].shape),
                  tab_spec, tab_spec, tab_spec, const((1, LANES)), const((1, LANES))],
        out_specs=[o[3] for o in out_defs],
        out_shape=[jax.ShapeDtypeStruct((nb,) + o[1], o[2]) for o in out_defs],
        compiler_params=_cparams(("arbitrary", "arbitrary")),
        name="in_proj",
    )(x3, shift3, scale3, wts["norm_w"], wts["w_main"], wts["w_wit"], wts["w_vt"], tabs[0], tabs[1], tabs[2],
      wts["qnw"], wts["knw"])
    return {o[0]: arr for o, arr in zip(out_defs, outs)}


def _rope_tables(pos):
    half = ROPE_DIMS // 2
    inv = jnp.power(ROPE_THETA, -jnp.arange(half, dtype=F32) / half)
    ang = pos.astype(F32)[:, None] * inv[None, :]
    cos, sin = jnp.cos(ang), jnp.sin(ang)
    n = pos.shape[0]
    pad = jnp.zeros((n, HEAD_DIM - ROPE_DIMS), F32)
    zero = jnp.zeros((n, half), F32)
    c_head = jnp.concatenate([cos, cos, pad + 1.0], axis=1)
    s1_head = jnp.concatenate([-sin, zero, pad], axis=1)
    s2_head = jnp.concatenate([zero, sin, pad], axis=1)
    two = lambda a: jnp.concatenate([a, a], axis=1)
    return two(c_head), two(s1_head), two(s2_head)


def _head_sum(t):
    ones = _head_block_ones(LANES)
    parts = [_dot_hi(t[:, s * LANES:(s + 1) * LANES], ones) for s in range(t.shape[1] // LANES)]
    return parts[0] if len(parts) == 1 else jnp.concatenate(parts, axis=1)


def _softplus(z):
    return jnp.maximum(z, 0.0) + jnp.log(1.0 + jnp.exp(-jnp.abs(z)))


def _rwkv_prep(xs, xl, prev, prevl, p):
    xm = xs + p["mu_rkv"] * (prev - xs)
    xml = xl + p["mu_lora"] * (prevl - xl)
    w = RWKV_WIDTH
    r, k, v = xm[:, :w], xm[:, w:2 * w], xm[:, 2 * w:]
    lane = _iota(xml.shape, 1)
    lhs = jnp.where(lane < LORA, jnp.tanh(xml), xml)
    lo = _dot(lhs.astype(BF16), p["w_lora"])
    wlog = -_softplus(-(p["w0"] + lo[:, :w])) - 0.5
    logw = -jnp.exp(wlog)
    a = jax.nn.sigmoid(p["a0"] + lo[:, w:])
    kkr = k * p["k_k"]
    kk = kkr / jnp.maximum(jnp.sqrt(_head_sum(kkr * kkr)), 1e-12)
    k_mod = k * (1.0 + (a - 1.0) * p["k_a"])
    bonus = _head_sum(r * k_mod * p["r_k"]) * v
    return r, logw, k_mod, v, -kk, kk * a, bonus


def _rwkv_epilogue(y, bonus, gate, p):
    mean = _head_sum(y) * (1.0 / HEAD_DIM)
    d = y - mean
    var = _head_sum(d * d) * (1.0 / HEAD_DIM)
    yn = d * lax.rsqrt(var + GN_EPS) * p["ln_w"] + p["ln_b"]
    return (yn + bonus) * _silu(gate)


_RW_PARAM_NAMES = ("mu_rkv", "mu_lora", "w0", "a0", "w_lora", "k_k", "k_a", "r_k", "ln_w", "ln_b")


def _load_params(refs):
    return {n: r[...] for n, r in zip(_RW_PARAM_NAMES, refs)}


def _rwkv_prompt_kernel(rkv_ref, lora_ref, gr_ref, *rest):
    nparam = len(_RW_PARAM_NAMES)
    p = _load_params(rest[:nparam])
    out_ref, sfin_ref = rest[nparam:nparam + 2]
    (st_s, c_rkv, c_lora, r_s, lw_s, k_s, v_s, al_s, be_s, y_s,
     at_s, rt_s, bk_s, ec_s, wl_s, wh_s, zv_s, inv_s) = rest[nparam + 2:]
    i = pl.program_id(1)
    C = RW_CHUNK
    W = 2 * C

    @pl.when(i == 0)
    def _():
        st_s[...] = jnp.zeros_like(st_s)
        c_rkv[...] = jnp.zeros_like(c_rkv)
        c_lora[...] = jnp.zeros_like(c_lora)

    xs, xl = rkv_ref[0], lora_ref[0]
    tb = xs.shape[0]
    row = _iota((tb, 1), 0)
    prev = jnp.where(row == 0, c_rkv[...], pltpu.roll(xs, 1, 0))
    prevl = jnp.where(row == 0, c_lora[...], pltpu.roll(xl, 1, 0))
    c_rkv[...] = xs[tb - 1:tb, :]
    c_lora[...] = xl[tb - 1:tb, :]
    r, logw, k_mod, v, al, be, bonus = _rwkv_prep(xs, xl, prev, prevl, p)
    r_s[...], lw_s[...], k_s[...], v_s[...], al_s[...], be_s[...] = r, logw, k_mod, v, al, be

    rm, cn = _iota((W, W), 0), _iota((W, W), 1)
    tt, ss = rm % C, cn % C
    top, left = rm < C, cn < C
    strict, incl = ss < tt, ss <= tt
    mk_l_lo, mk_l_hi = top & left & strict, (~top) & (~left) & strict
    mk_z_lo, mk_z_hi = top & (~left) & strict, (~top) & left & strict
    mk_w = (_iota((C, W), 1) % C) <= _iota((C, W), 0)
    bd = (rm // HEAD_DIM) == (cn // HEAD_DIM)
    eye = jnp.where(rm == cn, 1.0, 0.0).astype(F32)
    tril = jnp.where(_iota((C, C), 1) <= _iota((C, C), 0), 1.0, 0.0).astype(F32)
    lane_lo = _iota((C, LANES), 1) < HEAD_DIM
    m_lo = jnp.where(lane_lo, 1.0, 0.0).astype(F32)
    m_hi = 1.0 - m_lo

    bf = lambda a: a.astype(BF16)
    pairs = range(N_RWKV_HEADS // 2)
    lanes_of = lambda pr: slice(pr * LANES, (pr + 1) * LANES)
    rows_of = lambda c: pl.ds(pl.multiple_of(c * C, C), C)
    unroll = 2

    def phase_a(it, carry):
        items = [(it * unroll + u, pr) for u in range(unroll) for pr in pairs]
        cums = []
        for u in range(unroll):
            cum_all = _dot_hi(tril, lw_s[rows_of(it * unroll + u), :])
            cums += [cum_all[:, lanes_of(pr)] for pr in pairs]
        g_lo, g_hi = [], []
        for n, (c, pr) in enumerate(items):
            rows, sl = rows_of(c), lanes_of(pr)
            cum = cums[n]
            cum_c = cum[C - 1:C, :]
            e_neg, e_rel = jnp.exp(-cum), jnp.exp(cum_c - cum)
            rr, kk, aa, bb = r_s[rows, sl], k_s[rows, sl], al_s[rows, sl], be_s[rows, sl]
            a_t, r_t = aa * jnp.exp(cum - lw_s[rows, sl]), rr * jnp.exp(cum)
            b_t, k_t = bb * e_neg, kk * e_neg
            at_s[c, pr], rt_s[c, pr] = bf(a_t), bf(r_t)
            bk_s[c, pr] = jnp.concatenate([bb * e_rel, kk * e_rel], axis=0)
            ec_s[c, pr] = jnp.exp(cum_c)
            xk = bf(jnp.concatenate([b_t, k_t], axis=0))
            kx = bf(jnp.concatenate([k_t, b_t], axis=0))
            g_lo.append(_dot_nt(bf(jnp.concatenate([a_t * m_lo, r_t * m_lo], axis=0)), xk))
            g_hi.append(_dot_nt(bf(jnp.concatenate([r_t * m_hi, a_t * m_hi], axis=0)), kx))
        l_mat = []
        for n, (c, pr) in enumerate(items):
            l_mat.append(jnp.where(mk_l_lo, g_lo[n], 0.0) + jnp.where(mk_l_hi, g_hi[n], 0.0))
            z_mat = jnp.where(mk_z_lo, g_lo[n], 0.0) + jnp.where(mk_z_hi, g_hi[n], 0.0)
            wl_s[c, pr] = bf(jnp.where(mk_w, g_lo[n][C:, :], 0.0))
            wh_s[c, pr] = bf(jnp.where(mk_w, g_hi[n][:C, :], 0.0))
            vv_b = bf(v_s[rows_of(c), lanes_of(pr)])
            zv_s[c, pr] = _dot(bf(z_mat), jnp.concatenate([vv_b, vv_b], axis=0))
        inv = [eye + m for m in l_mat]
        pw = l_mat
        for _ in range(5):
            pw = [_dot(bf(m), bf(m)) for m in pw]
            inv = [x + _dot(bf(x), bf(m)) for x, m in zip(inv, pw)]
        for n, (c, pr) in enumerate(items):
            inv_s[c, pr] = bf(inv[n])
        return carry

    lax.fori_loop(0, tb // (C * unroll), phase_a, 0)

    def phase_b(c, carry):
        rows = rows_of(c)
        st = [st_s[pr] for pr in pairs]
        st_b = [bf(s) for s in st]
        vv = [v_s[rows, lanes_of(pr)] for pr in pairs]
        a_s = [_dot_nt(at_s[c, pr], st_b[pr]) for pr in pairs]
        u_st = [_dot(inv_s[c, pr], bf(zv_s[c, pr] + jnp.concatenate([a_s[pr], a_s[pr]], axis=0))) for pr in pairs]
        u = [jnp.where(lane_lo, x[:C], x[C:]) for x in u_st]
        uv = [jnp.concatenate([u[pr], vv[pr]], axis=0) for pr in pairs]
        upd = [_dot_tn(uv[pr], bk_s[c, pr]) for pr in pairs]
        for pr in pairs:
            st_s[pr] = st[pr] * ec_s[c, pr] + jnp.where(bd, upd[pr], 0.0)
        for pr in pairs:
            uv_b = bf(uv[pr])
            vu_b = jnp.concatenate([uv_b[C:], uv_b[:C]], axis=0)
            y_in = jnp.where(lane_lo, _dot(wl_s[c, pr], uv_b), _dot(wh_s[c, pr], vu_b))
            y_s[rows, lanes_of(pr)] = _dot_nt(rt_s[c, pr], st_b[pr]) + y_in
        return carry

    lax.fori_loop(0, tb // C, phase_b, 0)
    out_ref[0] = _rwkv_epilogue(y_s[...], bonus, gr_ref[0], p)

    @pl.when(i == pl.num_programs(1) - 1)
    def _():
        for hd in range(N_RWKV_HEADS):
            o = (hd % 2) * HEAD_DIM
            sfin_ref[0, hd] = st_s[hd // 2][o:o + HEAD_DIM, o:o + HEAD_DIM]


def _rwkv_prompt(rkv, lora, gr, params):
    nb, t, _ = rkv.shape
    tb = min(RW_BLOCK, t)
    w = RWKV_WIDTH
    blk = lambda width: pl.BlockSpec((1, tb, width), lambda b, i: (b, i, 0))
    pspecs = [pl.BlockSpec(params[n].shape, lambda b, i: (0, 0)) for n in _RW_PARAM_NAMES]
    vec = lambda: pltpu.VMEM((tb, w), F32)
    nch, npair, c = tb // RW_CHUNK, N_RWKV_HEADS // 2, RW_CHUNK
    per_chunk = lambda rows, dt: pltpu.VMEM((nch, npair, rows, LANES), dt)
    return pl.pallas_call(
        _rwkv_prompt_kernel,
        grid=(nb, t // tb),
        in_specs=[blk(3 * w), blk(2 * LORA), blk(w)] + pspecs,
        out_specs=[blk(w), pl.BlockSpec((1, N_RWKV_HEADS, HEAD_DIM, HEAD_DIM), lambda b, i: (b, 0, 0, 0))],
        out_shape=[jax.ShapeDtypeStruct((nb, t, w), F32),
                   jax.ShapeDtypeStruct((nb, N_RWKV_HEADS, HEAD_DIM, HEAD_DIM), F32)],
        scratch_shapes=[pltpu.VMEM((N_RWKV_HEADS // 2, LANES, LANES), F32),
                        pltpu.VMEM((1, 3 * w), F32), pltpu.VMEM((1, 2 * LORA), F32),
                        vec(), vec(), vec(), vec(), vec(), vec(), vec(),
                        per_chunk(c, BF16), per_chunk(c, BF16), per_chunk(2 * c, F32), per_chunk(1, F32),
                        per_chunk(c, BF16), per_chunk(c, BF16), per_chunk(2 * c, F32), per_chunk(2 * c, BF16)],
        compiler_params=_cparams(("arbitrary", "arbitrary")),
        name="rwkv_prompt",
    )(rkv, lora, gr, *[params[n] for n in _RW_PARAM_NAMES])


def _rwkv_dec_prep_kernel(rkv_ref, lora_ref, prev_ref, prevl_ref, *rest):
    nparam = len(_RW_PARAM_NAMES)
    p = _load_params(rest[:nparam])
    outs = rest[nparam:]
    r, logw, k_mod, v, al, be, bonus = _rwkv_prep(rkv_ref[...], lora_ref[...], prev_ref[...], prevl_ref[...], p)
    for o, val in zip(outs[:6], (r, jnp.exp(logw), k_mod, v, al, be)):
        o[...] = val.T
    outs[6][...] = bonus


def _rwkv_dec_prep(rkv, lora, prev, prevl, params):
    n = rkv.shape[0]
    full = lambda a: pl.BlockSpec(a.shape, lambda: tuple(0 for _ in a.shape))
    args = [rkv, lora, prev, prevl] + [params[k] for k in _RW_PARAM_NAMES]
    t_shape, shape = (RWKV_WIDTH, n), (n, RWKV_WIDTH)
    return pl.pallas_call(
        _rwkv_dec_prep_kernel,
        in_specs=[full(a) for a in args],
        out_specs=[pl.BlockSpec(t_shape, lambda: (0, 0))] * 6 + [pl.BlockSpec(shape, lambda: (0, 0))],
        out_shape=[jax.ShapeDtypeStruct(t_shape, F32)] * 6 + [jax.ShapeDtypeStruct(shape, F32)],
        compiler_params=pltpu.CompilerParams(vmem_limit_bytes=VMEM_LIMIT),
        name="rwkv_dec_prep",
    )(*args)


def _rwkv_dec_state_kernel(r_ref, w_ref, k_ref, v_ref, a_ref, b_ref, s_ref, so_ref, y_ref):
    r, w, k, a, b = r_ref[0], w_ref[0], k_ref[0], a_ref[0], b_ref[0]

    def one(i, carry):
        st = s_ref[0, i]
        sa = jnp.sum(st * a, axis=0, keepdims=True)
        sn = st * w + sa * b + v_ref[0, pl.ds(i, 1), :] * k
        so_ref[0, i] = sn
        y_ref[0, pl.ds(i, 1), :] = jnp.sum(sn * r, axis=0, keepdims=True)
        return carry

    lax.fori_loop(0, HEAD_DIM, one, 0)


def _rwkv_dec_state(vecs_t, state_t):
    nh, n, _, nb = state_t.shape
    vspec = pl.BlockSpec((1, n, nb), lambda h: (h, 0, 0))
    sspec = pl.BlockSpec((1, n, n, nb), lambda h: (h, 0, 0, 0))
    return pl.pallas_call(
        _rwkv_dec_state_kernel,
        grid=(nh,),
        in_specs=[vspec] * 6 + [sspec],
        out_specs=[sspec, vspec],
        out_shape=[jax.ShapeDtypeStruct(state_t.shape, F32), jax.ShapeDtypeStruct((nh, n, nb), F32)],
        compiler_params=_cparams(("arbitrary",)),
        name="rwkv_dec_state",
    )(*[a.reshape(nh, n, nb) for a in vecs_t], state_t)


def _rwkv_dec_epi_kernel(yt_ref, bonus_ref, gr_ref, lnw_ref, lnb_ref, o_ref):
    o_ref[...] = _rwkv_epilogue(yt_ref[...].T, bonus_ref[...], gr_ref[...],
                                {"ln_w": lnw_ref[...], "ln_b": lnb_ref[...]})


def _rwkv_dec_epi(y_t, bonus, gr, params):
    n = bonus.shape[0]
    args = [y_t, bonus, gr, params["ln_w"], params["ln_b"]]
    return pl.pallas_call(
        _rwkv_dec_epi_kernel,
        in_specs=[pl.BlockSpec(a.shape, lambda: (0, 0)) for a in args],
        out_specs=pl.BlockSpec((n, RWKV_WIDTH), lambda: (0, 0)),
        out_shape=jax.ShapeDtypeStruct((n, RWKV_WIDTH), F32),
        name="rwkv_dec_epi",
    )(*args)


def _attn_prompt_kernel(q_ref, qi_ref, wit_ref, k_ref, vt_ref, kidx_ref, sg_ref, tri_ref, o_ref,
                        s_scr, m_scr, acc_scr, *, topk):
    j = pl.program_id(1)
    QB, KC = Q_BLOCK, KEY_CHUNK
    nch = (j * QB) // KC + 1
    inf = jnp.inf
    qpos_i = j * QB + _iota((1, QB), 1)
    k_q = jnp.minimum(float(topk), qpos_i.astype(F32) + 1.0)
    wit = wit_ref[0]
    qi = qi_ref[0]
    qi_stack = jnp.concatenate([qi[:, h * IDX_DIM:(h + 1) * IDX_DIM] for h in range(N_IDX_HEADS)], axis=0)

    def chunk_rows(c):
        return pl.ds(pl.multiple_of(c * KC, KC), KC)

    SUB = 32
    part = lambda a: a.reshape(KC // SUB, SUB, QB)
    psum = lambda a: jnp.sum(part(a), axis=0)
    pmin = lambda a: jnp.min(part(a), axis=0)
    pmax = lambda a: jnp.max(part(a), axis=0)
    full8 = lambda val: jnp.full((SUB, QB), val, F32)
    fold_sum = lambda a: jnp.sum(a, axis=0, keepdims=True)
    fold_min = lambda a: jnp.min(a, axis=0, keepdims=True)
    fold_max = lambda a: jnp.max(a, axis=0, keepdims=True)

    def score_chunk(c, lohi):
        rows = chunk_rows(c)
        st = _dot_nt(kidx_ref[0, rows, :], qi_stack)
        acc = jnp.zeros((KC, QB), F32)
        for h in range(N_IDX_HEADS):
            acc = acc + jnp.maximum(st[:, h * QB:(h + 1) * QB], 0.0) * wit[h:h + 1, :]
        ok = (c * KC + _iota((KC, QB), 0)) <= qpos_i
        s_scr[rows, :] = jnp.where(ok, acc, -inf)
        return (jnp.minimum(lohi[0], pmin(jnp.where(ok, acc, inf))),
                jnp.maximum(lohi[1], pmax(jnp.where(ok, acc, -inf))))

    lo8, hi8 = lax.fori_loop(0, nch, score_chunk, (full8(inf), full8(-inf)))
    lo, hi = fold_min(lo8), fold_max(hi8)

    def sweep(fn, init):
        return lax.fori_loop(0, nch, lambda c, car: fn(car, s_scr[chunk_rows(c), :]), init)

    def bisect(_, lh):
        lo, hi = lh
        mid = 0.5 * lo + 0.5 * hi
        cnt = fold_sum(sweep(lambda c, s: c + psum(jnp.where(s >= mid, 1.0, 0.0)), full8(0.0)))
        ge = cnt >= k_q
        return jnp.where(ge, mid, lo), jnp.where(ge, hi, mid)

    lo, hi = lax.fori_loop(0, N_BISECT, bisect, (lo, hi))
    v0 = fold_min(sweep(lambda c, s: jnp.minimum(c, pmin(jnp.where(s >= lo, s, inf))), full8(inf)))

    def refine(state):
        v, _, _ = state

        def f(car, s):
            g = s > v
            return car[0] + psum(jnp.where(g, 1.0, 0.0)), jnp.minimum(car[1], pmin(jnp.where(g, s, inf)))

        cnt8, v28 = sweep(f, (full8(0.0), full8(inf)))
        cnt, v2 = fold_sum(cnt8), fold_min(v28)
        ok = cnt < k_q
        return jnp.where(ok, v, v2), cnt, jnp.sum(jnp.where(ok, 0.0, 1.0))

    full = lambda val: jnp.full((1, QB), val, F32)
    thr, c_gt, _ = lax.while_loop(lambda st: st[2] > 0.0, refine, (v0, full(0.0), jnp.float32(1.0)))
    need = k_q - c_gt

    q = q_ref[0]
    G, R = N_KV_HEADS, N_Q_HEADS // N_KV_HEADS
    q_stack = [jnp.concatenate([q[:, (g * R + r) * HEAD_DIM:(g * R + r + 1) * HEAD_DIM] for r in range(R)], axis=0)
               for g in range(G)]
    ones_rows = jnp.where(_iota((HEAD_DIM, KC), 0) == 0, 1.0, 0.0).astype(BF16)
    m_scr[...] = jnp.full(m_scr.shape, -inf, F32)
    acc_scr[...] = jnp.zeros_like(acc_scr)

    def attend(c, tie_carry):
        rows = chunk_rows(c)
        s = s_scr[rows, :]
        eq = s == thr
        rank = tie_carry + _dot(tri_ref[...], jnp.where(eq, 1.0, 0.0).astype(BF16))
        sel = (s > thr) | (eq & (rank < need))
        kc = k_ref[0, rows, :]
        for g in range(G):
            lg = _dot_nt(kc[:, g * HEAD_DIM:(g + 1) * HEAD_DIM], q_stack[g])
            lm = jnp.concatenate([jnp.where(sel, lg[:, r * QB:(r + 1) * QB], -inf) for r in range(R)], axis=1)
            m_old = m_scr[g]
            m_new = jnp.maximum(m_old, jnp.max(lm, axis=0, keepdims=True))
            m_safe = jnp.where(m_new == -inf, 0.0, m_new)
            alpha = jnp.exp2(m_old - m_safe)
            p = jnp.exp2(lm - m_safe).astype(BF16)
            vt_ext = jnp.concatenate([vt_ref[0, g * HEAD_DIM:(g + 1) * HEAD_DIM, rows], ones_rows], axis=0)
            acc_scr[g] = acc_scr[g] * alpha + _dot(vt_ext, p)
            m_scr[g] = m_new
        return tie_carry + fold_sum(psum(jnp.where(eq, 1.0, 0.0)))

    lax.fori_loop(0, nch, attend, full(0.0))

    out_t = jnp.concatenate([acc_scr[g][:HEAD_DIM] / acc_scr[g][HEAD_DIM:HEAD_DIM + 1] for g in range(G)], axis=0)
    per_r = [out_t[:, r * QB:(r + 1) * QB].T for r in range(R)]
    att = jnp.concatenate([per_r[r][:, g * HEAD_DIM:(g + 1) * HEAD_DIM] for g in range(G) for r in range(R)], axis=1)
    o_ref[0] = att * sg_ref[0]


def _attn_prompt(q, qi, wit, kbf, vt, kibf, sg):
    nb, t, _ = q.shape
    topk = min(TOPK_MAX, t // 4)
    kc = KEY_CHUNK
    tri = jnp.where(jnp.arange(kc)[None, :] < jnp.arange(kc)[:, None], 1.0, 0.0).astype(BF16)
    qblk = lambda width: pl.BlockSpec((1, Q_BLOCK, width), lambda b, j: (b, j, 0))
    whole = lambda r, c: pl.BlockSpec((1, r, c), lambda b, j: (b, 0, 0))
    return pl.pallas_call(
        functools.partial(_attn_prompt_kernel, topk=topk),
        grid=(nb, t // Q_BLOCK),
        in_specs=[qblk(ATT_WIDTH), qblk(N_IDX_HEADS * IDX_DIM),
                  pl.BlockSpec((1, N_IDX_HEADS, Q_BLOCK), lambda b, j: (b, 0, j)),
                  whole(t, KV_WIDTH), whole(KV_WIDTH, t), whole(t, IDX_DIM), qblk(ATT_WIDTH),
                  pl.BlockSpec((kc, kc), lambda b, j: (0, 0))],
        out_specs=qblk(ATT_WIDTH),
        out_shape=jax.ShapeDtypeStruct((nb, t, ATT_WIDTH), F32),
        scratch_shapes=[pltpu.VMEM((t, Q_BLOCK), F32),
                        pltpu.VMEM((N_KV_HEADS, 1, N_Q_HEADS // N_KV_HEADS * Q_BLOCK), F32),
                        pltpu.VMEM((N_KV_HEADS, 2 * HEAD_DIM, N_Q_HEADS // N_KV_HEADS * Q_BLOCK), F32)],
        compiler_params=_cparams(("arbitrary", "arbitrary")),
        name="attn_prompt",
    )(q, qi, wit, kbf, vt, kibf, sg, tri)


def _outproj_kernel(att_ref, rw_ref, w_ref, x_ref, g_ref, o_ref):
    mix = _dot(att_ref[0].astype(BF16), w_ref[:ATT_WIDTH, :]) + _dot(rw_ref[0].astype(BF16), w_ref[ATT_WIDTH:, :])
    o_ref[0] = x_ref[0] + g_ref[0] * mix


def _out_proj(att, rw, w_out_bf, x3, gate3, tm):
    nb, t, d = x3.shape
    if gate3.shape[1] == 1:
        gspec = pl.BlockSpec((1, 1, d), lambda b, i: (b, 0, 0))
    else:
        gspec = pl.BlockSpec((1, tm, d), lambda b, i: (b, i, 0))
    blk = lambda width: pl.BlockSpec((1, tm, width), lambda b, i: (b, i, 0))
    return pl.pallas_call(
        _outproj_kernel,
        grid=(nb, t // tm),
        in_specs=[blk(ATT_WIDTH), blk(RWKV_WIDTH), pl.BlockSpec(w_out_bf.shape, lambda b, i: (0, 0)), blk(d), gspec],
        out_specs=blk(d),
        out_shape=jax.ShapeDtypeStruct((nb, t, d), F32),
        compiler_params=_cparams(("arbitrary", "arbitrary")),
        name="out_proj",
    )(att, rw, w_out_bf, x3, gate3)


def _dec_score_kernel(pt_ref, qi_ref, wi_ref, *refs):
    npg = len(refs) - 1
    o_ref = refs[npg]
    qi = qi_ref[0]
    qi16 = jnp.concatenate([qi, jnp.zeros_like(qi)], axis=0).astype(BF16)
    wi = wi_ref[0]
    for i in range(npg):
        s = _dot(qi16, refs[i][0].astype(BF16))[:N_IDX_HEADS]
        o_ref[0, :, i * PAGE_SIZE:(i + 1) * PAGE_SIZE] = jnp.sum(jnp.maximum(s, 0.0) * wi, axis=0, keepdims=True)


def _page_specs(n, block, npg):
    zeros = (0,) * (len(block) - 1)
    return [pl.BlockSpec(block, functools.partial(lambda b, g, pt, i: (pt[b, g * npg + i],) + zeros, i=i))
            for i in range(n)]


def _dec_scores(page_table, qi3, wi3, kidx_t):
    nb, n_pages = page_table.shape
    npg = min(SCORE_PAGES_PER_STEP, n_pages)
    grid_spec = pltpu.PrefetchScalarGridSpec(
        num_scalar_prefetch=1,
        grid=(nb, n_pages // npg),
        in_specs=[pl.BlockSpec((1, N_IDX_HEADS, IDX_DIM), lambda b, g, pt: (b, 0, 0)),
                  pl.BlockSpec((1, N_IDX_HEADS, 1), lambda b, g, pt: (b, 0, 0))]
                 + _page_specs(npg, (1, IDX_DIM, PAGE_SIZE), npg),
        out_specs=pl.BlockSpec((1, 1, npg * PAGE_SIZE), lambda b, g, pt: (b, 0, g)),
    )
    return pl.pallas_call(
        _dec_score_kernel,
        grid_spec=grid_spec,
        out_shape=jax.ShapeDtypeStruct((nb, 1, n_pages * PAGE_SIZE), F32),
        compiler_params=_cparams(("arbitrary", "arbitrary")),
        name="dec_scores",
    )(page_table, qi3, wi3, *([kidx_t] * npg))


def _dec_select_kernel(s_ref, qi_ref, ki_ref, wi_ref, sel_ref, sx, *, topk):
    nb, past = s_ref.shape
    inf = jnp.inf
    qi = qi_ref[...].astype(F32)
    ki = ki_ref[...].astype(F32)
    prod = qi * jnp.concatenate([ki] * N_IDX_HEADS, axis=1)
    nq = N_IDX_HEADS * IDX_DIM
    hsel = jnp.where(_iota((nq, LANES), 0) // IDX_DIM == _iota((nq, LANES), 1), 1.0, 0.0).astype(F32)
    sh = _dot_hi(prod, hsel)[:, :N_IDX_HEADS]
    s_new = jnp.sum(jnp.maximum(sh, 0.0) * wi_ref[...], axis=1, keepdims=True)
    sx[:, :past] = s_ref[...]
    sx[:, past:] = jnp.where(_iota((nb, LANES), 1) == 0, s_new, -inf)

    rsum = lambda a: jnp.sum(a, axis=1, keepdims=True)
    k_q = float(topk)
    s = sx[...]
    lo = jnp.min(jnp.where(s > -inf, s, inf), axis=1, keepdims=True)
    hi = jnp.max(s, axis=1, keepdims=True)

    def bisect(_, lh):
        lo, hi = lh
        mid = 0.5 * lo + 0.5 * hi
        ge = rsum(jnp.where(sx[...] >= mid, 1.0, 0.0)) >= k_q
        return jnp.where(ge, mid, lo), jnp.where(ge, hi, mid)

    lo, hi = lax.fori_loop(0, N_BISECT, bisect, (lo, hi))
    s = sx[...]
    v0 = jnp.min(jnp.where(s >= lo, s, inf), axis=1, keepdims=True)

    def refine(state):
        v, _, _ = state
        s = sx[...]
        g = s > v
        cnt = rsum(jnp.where(g, 1.0, 0.0))
        v2 = jnp.min(jnp.where(g, s, inf), axis=1, keepdims=True)
        ok = cnt < k_q
        return jnp.where(ok, v, v2), cnt, jnp.sum(jnp.where(ok, 0.0, 1.0))

    thr, c_gt, _ = lax.while_loop(lambda st: st[2] > 0.0, refine,
                                  (v0, jnp.zeros((nb, 1), F32), jnp.float32(1.0)))
    need = k_q - c_gt
    upper = jnp.where(_iota((LANES, LANES), 0) < _iota((LANES, LANES), 1), 1.0, 0.0).astype(BF16)
    carry = jnp.zeros((nb, 1), F32)
    for c in range((past + LANES) // LANES):
        sc = sx[:, c * LANES:(c + 1) * LANES]
        eq = sc == thr
        eqf = jnp.where(eq, 1.0, 0.0)
        rank = carry + _dot(eqf.astype(BF16), upper)
        sel_ref[:, c * LANES:(c + 1) * LANES] = jnp.where((sc > thr) | (eq & (rank < need)), 1.0, 0.0)
        carry = carry + rsum(eqf)


def _dec_select(scores, qi, ki, wi, topk):
    nb, past = scores.shape
    args = [scores, qi, ki, wi]
    return pl.pallas_call(
        functools.partial(_dec_select_kernel, topk=topk),
        in_specs=[pl.BlockSpec(a.shape, lambda: (0, 0)) for a in args],
        out_specs=pl.BlockSpec((nb, past + LANES), lambda: (0, 0)),
        out_shape=jax.ShapeDtypeStruct((nb, past + LANES), F32),
        scratch_shapes=[pltpu.VMEM((nb, past + LANES), F32)],
        compiler_params=pltpu.CompilerParams(vmem_limit_bytes=VMEM_LIMIT),
        name="dec_select",
    )(*args)


def _dec_attn_kernel(pt_ref, q_ref, sel_ref, selnew_ref, knew_ref, vnew_ref, sg_ref, *refs, npg):
    k_pages, v_pages = refs[:npg], refs[npg:2 * npg]
    o_ref, m_s, l_s, acc_s = refs[2 * npg:]
    g = pl.program_id(1)
    inf = jnp.inf
    nh = N_Q_HEADS
    R = N_Q_HEADS // N_KV_HEADS

    @pl.when(g == 0)
    def _():
        m_s[...] = jnp.full(m_s.shape, -inf, F32)
        l_s[...] = jnp.zeros_like(l_s)
        acc_s[...] = jnp.zeros_like(acc_s)

    q8 = q_ref[0].astype(F32)
    own = (_iota((nh, KV_WIDTH), 0) // R) == (_iota((nh, KV_WIDTH), 1) // HEAD_DIM)
    q_ext = jnp.where(own, jnp.concatenate([q8, q8], axis=1), 0.0)
    q16 = jnp.concatenate([q_ext, jnp.zeros_like(q_ext)], axis=0).astype(BF16)
    m, l, acc = m_s[...], l_s[...], acc_s[...]

    def fold(m, l, acc, lm, pv_fn):
        m_new = jnp.maximum(m, jnp.max(lm, axis=1, keepdims=True))
        m_safe = jnp.where(m_new == -inf, 0.0, m_new)
        alpha = jnp.exp2(m - m_safe)
        p = jnp.exp2(lm - m_safe)
        return m_new, alpha * l + jnp.sum(p, axis=1, keepdims=True), alpha * acc + pv_fn(p)

    logits = jnp.concatenate([_dot(q16, k_pages[i][0].astype(BF16))[:nh] for i in range(npg)], axis=1)
    lm = jnp.where(sel_ref[0] > 0.5, logits, -inf)

    def pv(p):
        p16 = jnp.concatenate([p, jnp.zeros_like(p)], axis=0).astype(BF16)
        out = _dot_nt(p16[:, :PAGE_SIZE], v_pages[0][0].astype(BF16))
        for i in range(1, npg):
            out = out + _dot_nt(p16[:, i * PAGE_SIZE:(i + 1) * PAGE_SIZE], v_pages[i][0].astype(BF16))
        return out[:nh]

    m, l, acc = fold(m, l, acc, lm, pv)
    m_s[...], l_s[...], acc_s[...] = m, l, acc

    @pl.when(g == pl.num_programs(1) - 1)
    def _():
        k_new = knew_ref[0].astype(BF16).astype(F32)
        v_new = vnew_ref[0].astype(BF16).astype(F32)
        lg = jnp.sum(q16[:nh].astype(F32) * k_new, axis=1, keepdims=True)
        lm = jnp.where(selnew_ref[0, :, 0:1] > 0.5, lg, -inf)
        m2, l2, acc2 = fold(m, l, acc, lm, lambda p: p.astype(BF16).astype(F32) * v_new)
        out = acc2 / l2
        res = jnp.where(_iota((nh, HEAD_DIM), 0) < R, out[:, :HEAD_DIM], out[:, HEAD_DIM:])
        o_ref[0] = res * sg_ref[0]


def _dec_attn(page_table, q3, sel3, k_new3, v_new3, sg3, cache_k2, cache_v2):
    nb, n_pages = page_table.shape
    npg = min(ATTN_PAGES_PER_STEP, n_pages)
    past = n_pages * PAGE_SIZE
    per_seq = lambda r, c: pl.BlockSpec((1, r, c), lambda b, g, pt: (b, 0, 0))
    grid_spec = pltpu.PrefetchScalarGridSpec(
        num_scalar_prefetch=1,
        grid=(nb, n_pages // npg),
        in_specs=[per_seq(N_Q_HEADS, HEAD_DIM),
                  pl.BlockSpec((1, 1, npg * PAGE_SIZE), lambda b, g, pt: (b, 0, g)),
                  pl.BlockSpec((1, 1, LANES), lambda b, g, pt: (b, 0, past // LANES)),
                  per_seq(1, KV_WIDTH), per_seq(1, KV_WIDTH), per_seq(N_Q_HEADS, HEAD_DIM)]
                 + _page_specs(npg, (1, KV_WIDTH, PAGE_SIZE), npg) + _page_specs(npg, (1, KV_WIDTH, PAGE_SIZE), npg),
        out_specs=per_seq(N_Q_HEADS, HEAD_DIM),
        scratch_shapes=[pltpu.VMEM((N_Q_HEADS, 1), F32), pltpu.VMEM((N_Q_HEADS, 1), F32),
                        pltpu.VMEM((N_Q_HEADS, KV_WIDTH), F32)],
    )
    return pl.pallas_call(
        functools.partial(_dec_attn_kernel, npg=npg),
        grid_spec=grid_spec,
        out_shape=jax.ShapeDtypeStruct((nb, N_Q_HEADS, HEAD_DIM), F32),
        compiler_params=_cparams(("arbitrary", "arbitrary")),
        name="dec_attn",
    )(page_table, q3, sel3, sel3, k_new3, v_new3, sg3, *([cache_k2] * npg), *([cache_v2] * npg))


def kernel(x_prompt, x_sample, cache_k, cache_v, cache_kidx, state_wkv, state_shift, page_table,
           c_prompt, c_sample, norm_w, w_ada, b_ada, w_in, q_norm_w, k_norm_w, mu_shift, w0, w_up,
           a0, a_up, k_k, k_a, r_k, ln_x_w, ln_x_b, w_out):
    nb, t, d = x_prompt.shape
    nd = x_sample.shape[0]
    assert x_sample.shape[1] == 1
    n_pages = page_table.shape[1]
    past = n_pages * PAGE_SIZE
    w = RWKV_WIDTH

    offs = [0]
    for sz in (ATT_WIDTH, KV_WIDTH, KV_WIDTH, N_IDX_HEADS * IDX_DIM, N_IDX_HEADS, IDX_DIM, ATT_WIDTH,
               w, w, w, LORA, LORA, w):
        offs.append(offs[-1] + sz)
    col = lambda i: w_in[:, offs[i]:offs[i + 1]]
    w_main = jnp.concatenate([col(0), col(1), col(2), col(3), col(6), col(7), col(8), col(9), col(12),
                              col(10), col(11), col(5), jnp.zeros((d, LANES - IDX_DIM), w_in.dtype)], axis=1).astype(BF16)
    zl = jnp.zeros((LORA, w), F32)
    wts = {
        "norm_w": norm_w.reshape(1, d), "w_main": w_main,
        "w_wit": col(4).T.astype(BF16), "w_vt": col(2).T.astype(BF16),
        "qnw": jnp.concatenate([q_norm_w, q_norm_w]).reshape(1, LANES),
        "knw": jnp.concatenate([k_norm_w, k_norm_w]).reshape(1, LANES),
    }
    rw_params = {
        "mu_rkv": mu_shift[:3 * w].reshape(1, 3 * w), "mu_lora": mu_shift[3 * w:].reshape(1, 2 * LORA),
        "w0": w0.reshape(1, w), "a0": a0.reshape(1, w),
        "w_lora": jnp.concatenate([jnp.concatenate([w_up, zl], axis=1),
                                   jnp.concatenate([zl, a_up], axis=1)], axis=0).astype(BF16),
        "k_k": k_k.reshape(1, w), "k_a": k_a.reshape(1, w), "r_k": r_k.reshape(1, w),
        "ln_w": ln_x_w.reshape(1, w), "ln_b": ln_x_b.reshape(1, w),
    }
    w_out_bf = w_out.astype(BF16)

    n_c = nb + nd
    pad = (-n_c) % 8
    c_all = jnp.concatenate([c_prompt, c_sample, jnp.zeros((pad, d), F32)], axis=0)
    mod = _adaln_mod(c_all, w_ada, b_ada)
    shift_p, scale_p, gate_p = (mod[:nb, s * d:(s + 1) * d].reshape(nb, 1, d) for s in range(3))
    shift_s, scale_s, gate_s = (mod[nb:n_c, s * d:(s + 1) * d].reshape(1, nd, d) for s in range(3))

    tm = min(512, t)
    pr = _in_proj(x_prompt, shift_p, scale_p, _rope_tables(jnp.arange(t)), wts, tm)
    att_p = _attn_prompt(pr["q"], pr["qi"], pr["wit"], pr["kbf"], pr["vt"], pr["kibf"], pr["sg"])
    rw_p, s_p = _rwkv_prompt(pr["rkv"], pr["lora"], pr["gr"], rw_params)
    y_prompt = _out_proj(att_p, rw_p, w_out_bf, x_prompt, gate_p, tm)
    shift_prompt = jnp.concatenate([pr["rkv"][:, t - 1:], pr["lora"][:, t - 1:]], axis=-1)

    xs3 = x_sample.reshape(1, nd, d)
    sm = _in_proj(xs3, shift_s, scale_s, _rope_tables(jnp.full((1,), past)), wts, nd)
    qi_s, ki_s = sm["qi"][0], sm["ki"][0]
    wi_s = sm["wit"][0].T
    n_phys = cache_k.shape[0]
    k_pages = jnp.transpose(cache_k, (0, 2, 3, 1)).reshape(n_phys, KV_WIDTH, PAGE_SIZE)
    v_pages = jnp.transpose(cache_v, (0, 2, 3, 1)).reshape(n_phys, KV_WIDTH, PAGE_SIZE)
    scores = _dec_scores(page_table, qi_s.reshape(nd, N_IDX_HEADS, IDX_DIM), wi_s.reshape(nd, N_IDX_HEADS, 1),
                         jnp.transpose(cache_kidx, (0, 2, 1)))
    topk = min(TOPK_MAX, (past + 1) // 4)
    sel = _dec_select(scores.reshape(nd, past), qi_s, sm["kibf"][0], wi_s, topk)
    att_s = _dec_attn(page_table, sm["q"][0].reshape(nd, N_Q_HEADS, HEAD_DIM), sel.reshape(nd, 1, past + LANES),
                      sm["k"][0].reshape(nd, 1, KV_WIDTH), sm["v"][0].reshape(nd, 1, KV_WIDTH),
                      sm["sg"][0].reshape(nd, N_Q_HEADS, HEAD_DIM),
                      k_pages, v_pages)
    prev = state_shift.reshape(nd, -1)
    vecs = _rwkv_dec_prep(sm["rkv"][0], sm["lora"][0], prev[:, :3 * w], prev[:, 3 * w:], rw_params)
    s_t, y_t = _rwkv_dec_state(vecs[:6], jnp.transpose(state_wkv, (1, 2, 3, 0)))
    s_s = jnp.transpose(s_t, (3, 0, 1, 2))
    rw_s = _rwkv_dec_epi(y_t.reshape(w, nd), vecs[6], sm["gr"][0], rw_params)
    y_sample = _out_proj(att_s.reshape(1, nd, ATT_WIDTH), rw_s.reshape(1, nd, w), w_out_bf, xs3, gate_s, nd)
    shift_sample = jnp.concatenate([sm["rkv"][0], sm["lora"][0]], axis=-1).reshape(nd, 1, 3 * w + 2 * LORA)

    return (y_prompt, y_sample.reshape(nd, 1, d),
            pr["k"].reshape(nb, t, N_KV_HEADS, HEAD_DIM), pr["v"].reshape(nb, t, N_KV_HEADS, HEAD_DIM), pr["ki"],
            s_p, shift_prompt,
            sm["k"][0].reshape(nd, 1, N_KV_HEADS, HEAD_DIM), sm["v"][0].reshape(nd, 1, N_KV_HEADS, HEAD_DIM),
            ki_s.reshape(nd, 1, IDX_DIM), s_s, shift_sample)
```

```python
import functools

import jax
import jax.numpy as jnp
from jax import lax
from jax.experimental import pallas as pl
from jax.experimental.pallas import tpu as pltpu

F32, BF16 = jnp.float32, jnp.bfloat16

HEAD_DIM = 64
N_Q_HEADS = 8
N_KV_HEADS = 2
N_IDX_HEADS = 8
IDX_DIM = 64
N_RWKV_HEADS = 8
ATT_WIDTH = N_Q_HEADS * HEAD_DIM
RWKV_WIDTH = N_RWKV_HEADS * HEAD_DIM
KV_WIDTH = N_KV_HEADS * HEAD_DIM
LORA = 64
TOPK_MAX = 256
ROPE_THETA = 500000.0
ROPE_DIMS = HEAD_DIM // 4
NORM_EPS = 1e-6
GN_EPS = 64e-5
PAGE_SIZE = 128
LANES = 128
Q_SCALE = HEAD_DIM ** -0.5 * 1.4426950408889634
VMEM_LIMIT = 56 * 1024 * 1024

Q_BLOCK = 128
KEY_CHUNK = 512
N_BISECT = 16
RW_CHUNK = 64
RW_BLOCK = 512
SCORE_PAGES_PER_STEP = 64
ATTN_PAGES_PER_STEP = 32


def _dot(a, b):
    return jnp.dot(a, b, preferred_element_type=F32)


def _dot_nt(a, b):
    return lax.dot_general(a, b, (((1,), (1,)), ((), ())), preferred_element_type=F32)


def _split(x):
    hi = x.astype(BF16)
    return hi, (x - hi.astype(F32)).astype(BF16)


def _sum_dot(x, sel01):
    hi, lo = _split(x)
    return _dot(hi, sel01) + _dot(lo, sel01)


def _sum_dot_left(sel01, x):
    hi, lo = _split(x)
    return _dot(sel01, hi) + _dot(sel01, lo)


def _dot_tn(a, b):
    return lax.dot_general(a, b, (((0,), (0,)), ((), ())), preferred_element_type=F32)


def _iota(shape, dim):
    return lax.broadcasted_iota(jnp.int32, shape, dim)


def _head_block_ones(n):
    return jnp.where(_iota((n, n), 0) // HEAD_DIM == _iota((n, n), 1) // HEAD_DIM, 1.0, 0.0).astype(BF16)


def _silu(x):
    return x * jax.nn.sigmoid(x)


def _cparams(sem):
    return pltpu.CompilerParams(dimension_semantics=sem, vmem_limit_bytes=VMEM_LIMIT)


def _mod_kernel(c_ref, w_ref, b_ref, o_ref):
    s = _silu(c_ref[...])
    o_ref[...] = _dot(s.astype(BF16), w_ref[...].astype(BF16)) + b_ref[...]


def _adaln_mod(c, w_ada, b_ada):
    rows, d = c.shape
    n = w_ada.shape[1]
    tn = 1024
    return pl.pallas_call(
        _mod_kernel,
        grid=(n // tn,),
        in_specs=[pl.BlockSpec((rows, d), lambda j: (0, 0)),
                  pl.BlockSpec((d, tn), lambda j: (0, j)),
                  pl.BlockSpec((1, tn), lambda j: (0, j))],
        out_specs=pl.BlockSpec((rows, tn), lambda j: (0, j)),
        out_shape=jax.ShapeDtypeStruct((rows, n), F32),
        compiler_params=_cparams(("arbitrary",)),
        name="adaln_mod",
    )(c, w_ada, b_ada.reshape(1, n))


_W_Q, _W_KV, _W_QI, _W_GA, _W_RKV, _W_GR, _W_TAIL, _W_END = 0, 512, 768, 1280, 1792, 3328, 3840, 4096


def _inproj_kernel(x_ref, shift_ref, scale_ref, nw_ref, w_ref, wwi_ref, wvt_ref, cos_ref, s1_ref, s2_ref,
                   qnw_ref, knw_ref,
                   q_o, k_o, kbf_o, v_o, vt_o, qi_o, ki_o, kibf_o, wit_o, sg_o, rkv_o, lora_o, gr_o):
    x = x_ref[0]
    ms = jnp.mean(x * x, axis=-1, keepdims=True)
    h = x * lax.rsqrt(ms + NORM_EPS) * nw_ref[...]
    h = h * (1.0 + scale_ref[0]) + shift_ref[0]
    hb = h.astype(BF16)
    cos, s1, s2 = cos_ref[...], s1_ref[...], s2_ref[...]
    bsum = _head_block_ones(LANES)

    def seg(a, b):
        return _dot(hb, w_ref[:, a:b])

    def rope(t):
        return t * cos + pltpu.roll(t, LANES - ROPE_DIMS // 2, 1) * s1 + pltpu.roll(t, ROPE_DIMS // 2, 1) * s2

    def head_norm(t, w):
        return t * lax.rsqrt(_sum_dot(t * t, bsum) * (1.0 / HEAD_DIM) + NORM_EPS) * w

    yq = seg(_W_Q, _W_KV)
    for s in range(ATT_WIDTH // LANES):
        t = rope(head_norm(yq[:, s * LANES:(s + 1) * LANES], qnw_ref[...]))
        q_o[0, :, s * LANES:(s + 1) * LANES] = (t * Q_SCALE).astype(BF16)

    ykv = seg(_W_KV, _W_QI)
    k = rope(head_norm(ykv[:, :KV_WIDTH], knw_ref[...]))
    k_o[0] = k
    kbf_o[0] = k.astype(BF16)
    v_o[0] = ykv[:, KV_WIDTH:]
    vt_o[0] = _dot_nt(wvt_ref[...], hb).astype(BF16)

    yqi = seg(_W_QI, _W_GA)
    for s in range(N_IDX_HEADS * IDX_DIM // LANES):
        qi_o[0, :, s * LANES:(s + 1) * LANES] = rope(yqi[:, s * LANES:(s + 1) * LANES]).astype(BF16)

    sg_o[0] = _silu(seg(_W_GA, _W_RKV))
    for s in range(3):
        rkv_o[0, :, s * RWKV_WIDTH:(s + 1) * RWKV_WIDTH] = seg(_W_RKV + s * RWKV_WIDTH, _W_RKV + (s + 1) * RWKV_WIDTH)
    gr_o[0] = seg(_W_GR, _W_TAIL)

    yt = seg(_W_TAIL, _W_END)
    lora_o[0] = yt[:, :2 * LORA]
    ki = rope(yt[:, 2 * LORA:])[:, :IDX_DIM]
    ki_o[0] = ki
    kibf_o[0] = ki.astype(BF16)
    wit_o[0] = _dot_nt(wwi_ref[...], hb) * (N_IDX_HEADS ** -0.5 * IDX_DIM ** -0.5)


def _in_proj(x3, shift3, scale3, tabs, wts, tm):
    nb, t, d = x3.shape
    mod_rows = shift3.shape[1]
    tab_rows = tabs[0].shape[0]
    grid = (nb, t // tm)
    if mod_rows == 1:
        mod_spec = pl.BlockSpec((1, 1, d), lambda b, i: (b, 0, 0))
    else:
        mod_spec = pl.BlockSpec((1, tm, d), lambda b, i: (b, i, 0))
    if tab_rows == 1:
        tab_spec = pl.BlockSpec((1, LANES), lambda b, i: (0, 0))
    else:
        tab_spec = pl.BlockSpec((tm, LANES), lambda b, i: (i, 0))
    const = lambda shape: pl.BlockSpec(shape, lambda b, i: tuple(0 for _ in shape))
    row = lambda width: pl.BlockSpec((1, tm, width), lambda b, i: (b, i, 0))
    col = lambda rows: pl.BlockSpec((1, rows, tm), lambda b, i: (b, 0, i))
    out_defs = [
        ("q", (t, ATT_WIDTH), BF16, row(ATT_WIDTH)),
        ("k", (t, KV_WIDTH), F32, row(KV_WIDTH)),
        ("kbf", (t, KV_WIDTH), BF16, row(KV_WIDTH)),
        ("v", (t, KV_WIDTH), F32, row(KV_WIDTH)),
        ("vt", (KV_WIDTH, t), BF16, col(KV_WIDTH)),
        ("qi", (t, N_IDX_HEADS * IDX_DIM), BF16, row(N_IDX_HEADS * IDX_DIM)),
        ("ki", (t, IDX_DIM), F32, row(IDX_DIM)),
        ("kibf", (t, IDX_DIM), BF16, row(IDX_DIM)),
        ("wit", (N_IDX_HEADS, t), F32, col(N_IDX_HEADS)),
        ("sg", (t, ATT_WIDTH), F32, row(ATT_WIDTH)),
        ("rkv", (t, 3 * RWKV_WIDTH), F32, row(3 * RWKV_WIDTH)),
        ("lora", (t, 2 * LORA), F32, row(2 * LORA)),
        ("gr", (t, RWKV_WIDTH), F32, row(RWKV_WIDTH)),
    ]
    outs = pl.pallas_call(
        _inproj_kernel,
        grid=grid,
        in_specs=[pl.BlockSpec((1, tm, d), lambda b, i: (b, i, 0)), mod_spec, mod_spec, const((1, d)),
                  const(wts["w_main"].shape), const(wts["w_wit"].shape), const(wts["w_vt"].shape),
                  tab_spec, tab_spec, tab_spec, const((1, LANES)), const((1, LANES))],
        out_specs=[o[3] for o in out_defs],
        out_shape=[jax.ShapeDtypeStruct((nb,) + o[1], o[2]) for o in out_defs],
        compiler_params=_cparams(("arbitrary", "arbitrary")),
        name="in_proj",
    )(x3, shift3, scale3, wts["norm_w"], wts["w_main"], wts["w_wit"], wts["w_vt"], tabs[0], tabs[1], tabs[2],
      wts["qnw"], wts["knw"])
    return {o[0]: arr for o, arr in zip(out_defs, outs)}


def _rope_tables(pos):
    half = ROPE_DIMS // 2
    inv = jnp.power(ROPE_THETA, -jnp.arange(half, dtype=F32) / half)
    ang = pos.astype(F32)[:, None] * inv[None, :]
    cos, sin = jnp.cos(ang), jnp.sin(ang)
    n = pos.shape[0]
    pad = jnp.zeros((n, HEAD_DIM - ROPE_DIMS), F32)
    zero = jnp.zeros((n, half), F32)
    c_head = jnp.concatenate([cos, cos, pad + 1.0], axis=1)
    s1_head = jnp.concatenate([-sin, zero, pad], axis=1)
    s2_head = jnp.concatenate([zero, sin, pad], axis=1)
    two = lambda a: jnp.concatenate([a, a], axis=1)
    return two(c_head), two(s1_head), two(s2_head)


def _head_sum(t):
    ones = _head_block_ones(LANES)
    parts = [_sum_dot(t[:, s * LANES:(s + 1) * LANES], ones) for s in range(t.shape[1] // LANES)]
    return parts[0] if len(parts) == 1 else jnp.concatenate(parts, axis=1)


def _softplus(z):
    return jnp.maximum(z, 0.0) + jnp.log(1.0 + jnp.exp(-jnp.abs(z)))


def _rwkv_prep(xs, xl, prev, prevl, p):
    xm = xs + p["mu_rkv"] * (prev - xs)
    xml = xl + p["mu_lora"] * (prevl - xl)
    w = RWKV_WIDTH
    r, k, v = xm[:, :w], xm[:, w:2 * w], xm[:, 2 * w:]
    lane = _iota(xml.shape, 1)
    lhs = jnp.where(lane < LORA, jnp.tanh(xml), xml)
    lo = _dot(lhs.astype(BF16), p["w_lora"])
    wlog = -_softplus(-(p["w0"] + lo[:, :w])) - 0.5
    logw = -jnp.exp(wlog)
    a = jax.nn.sigmoid(p["a0"] + lo[:, w:])
    kkr = k * p["k_k"]
    kk = kkr / jnp.maximum(jnp.sqrt(_head_sum(kkr * kkr)), 1e-12)
    k_mod = k * (1.0 + (a - 1.0) * p["k_a"])
    bonus = _head_sum(r * k_mod * p["r_k"]) * v
    return r, logw, k_mod, v, -kk, kk * a, bonus


def _rwkv_epilogue(y, bonus, gate, p):
    mean = _head_sum(y) * (1.0 / HEAD_DIM)
    d = y - mean
    var = _head_sum(d * d) * (1.0 / HEAD_DIM)
    yn = d * lax.rsqrt(var + GN_EPS) * p["ln_w"] + p["ln_b"]
    return (yn + bonus) * _silu(gate)


_RW_PARAM_NAMES = ("mu_rkv", "mu_lora", "w0", "a0", "w_lora", "k_k", "k_a", "r_k", "ln_w", "ln_b")


def _load_params(refs):
    return {n: r[...] for n, r in zip(_RW_PARAM_NAMES, refs)}


def _rwkv_prompt_kernel(rkv_ref, lora_ref, gr_ref, *rest):
    nparam = len(_RW_PARAM_NAMES)
    p = _load_params(rest[:nparam])
    out_ref, sfin_ref = rest[nparam:nparam + 2]
    (st_s, c_rkv, c_lora, r_s, lw_s, k_s, v_s, al_s, be_s, y_s,
     at_s, rt_s, bk_s, ec_s, wl_s, wh_s, zv_s, inv_s) = rest[nparam + 2:]
    i = pl.program_id(1)
    C = RW_CHUNK
    W = 2 * C

    @pl.when(i == 0)
    def _():
        st_s[...] = jnp.zeros_like(st_s)
        c_rkv[...] = jnp.zeros_like(c_rkv)
        c_lora[...] = jnp.zeros_like(c_lora)

    xs, xl = rkv_ref[0], lora_ref[0]
    tb = xs.shape[0]
    row = _iota((tb, 1), 0)
    prev = jnp.where(row == 0, c_rkv[...], pltpu.roll(xs, 1, 0))
    prevl = jnp.where(row == 0, c_lora[...], pltpu.roll(xl, 1, 0))
    c_rkv[...] = xs[tb - 1:tb, :]
    c_lora[...] = xl[tb - 1:tb, :]
    r, logw, k_mod, v, al, be, bonus = _rwkv_prep(xs, xl, prev, prevl, p)
    r_s[...], lw_s[...], k_s[...], v_s[...], al_s[...], be_s[...] = r, logw, k_mod, v, al, be

    rm, cn = _iota((W, W), 0), _iota((W, W), 1)
    tt, ss = rm % C, cn % C
    top, left = rm < C, cn < C
    strict, incl = ss < tt, ss <= tt
    mk_l_lo, mk_l_hi = top & left & strict, (~top) & (~left) & strict
    mk_z_lo, mk_z_hi = top & (~left) & strict, (~top) & left & strict
    mk_w = (_iota((C, W), 1) % C) <= _iota((C, W), 0)
    bd = (rm // HEAD_DIM) == (cn // HEAD_DIM)
    eye = jnp.where(rm == cn, 1.0, 0.0).astype(F32)
    tril = jnp.where(_iota((C, C), 1) <= _iota((C, C), 0), 1.0, 0.0).astype(BF16)
    lane_lo = _iota((C, LANES), 1) < HEAD_DIM
    m_lo = jnp.where(lane_lo, 1.0, 0.0).astype(F32)
    m_hi = 1.0 - m_lo

    bf = lambda a: a.astype(BF16)
    pairs = range(N_RWKV_HEADS // 2)
    lanes_of = lambda pr: slice(pr * LANES, (pr + 1) * LANES)
    rows_of = lambda c: pl.ds(pl.multiple_of(c * C, C), C)
    unroll = 2

    def phase_a(it, carry):
        items = [(it * unroll + u, pr) for u in range(unroll) for pr in pairs]
        cums = []
        for u in range(unroll):
            cum_all = _sum_dot_left(tril, lw_s[rows_of(it * unroll + u), :])
            cums += [cum_all[:, lanes_of(pr)] for pr in pairs]
        g_lo, g_hi = [], []
        for n, (c, pr) in enumerate(items):
            rows, sl = rows_of(c), lanes_of(pr)
            cum = cums[n]
            cum_c = cum[C - 1:C, :]
            e_neg, e_rel = jnp.exp(-cum), jnp.exp(cum_c - cum)
            rr, kk, aa, bb = r_s[rows, sl], k_s[rows, sl], al_s[rows, sl], be_s[rows, sl]
            a_t, r_t = aa * jnp.exp(cum - lw_s[rows, sl]), rr * jnp.exp(cum)
            b_t, k_t = bb * e_neg, kk * e_neg
            at_s[c, pr], rt_s[c, pr] = bf(a_t), bf(r_t)
            bk_s[c, pr] = jnp.concatenate([bb * e_rel, kk * e_rel], axis=0)
            ec_s[c, pr] = jnp.exp(cum_c)
            xk = bf(jnp.concatenate([b_t, k_t], axis=0))
            kx = bf(jnp.concatenate([k_t, b_t], axis=0))
            g_lo.append(_dot_nt(bf(jnp.concatenate([a_t * m_lo, r_t * m_lo], axis=0)), xk))
            g_hi.append(_dot_nt(bf(jnp.concatenate([r_t * m_hi, a_t * m_hi], axis=0)), kx))
        l_mat = []
        for n, (c, pr) in enumerate(items):
            l_mat.append(jnp.where(mk_l_lo, g_lo[n], 0.0) + jnp.where(mk_l_hi, g_hi[n], 0.0))
            z_mat = jnp.where(mk_z_lo, g_lo[n], 0.0) + jnp.where(mk_z_hi, g_hi[n], 0.0)
            wl_s[c, pr] = bf(jnp.where(mk_w, g_lo[n][C:, :], 0.0))
            wh_s[c, pr] = bf(jnp.where(mk_w, g_hi[n][:C, :], 0.0))
            vv_b = bf(v_s[rows_of(c), lanes_of(pr)])
            zv_s[c, pr] = _dot(bf(z_mat), jnp.concatenate([vv_b, vv_b], axis=0))
        inv = [eye + m for m in l_mat]
        pw = l_mat
        for _ in range(5):
            pw = [_dot(bf(m), bf(m)) for m in pw]
            inv = [x + _dot(bf(x), bf(m)) for x, m in zip(inv, pw)]
        for n, (c, pr) in enumerate(items):
            inv_s[c, pr] = bf(inv[n])
        return carry

    lax.fori_loop(0, tb // (C * unroll), phase_a, 0)

    def phase_b(c, carry):
        rows = rows_of(c)
        st = [st_s[pr] for pr in pairs]
        st_b = [bf(s) for s in st]
        vv = [v_s[rows, lanes_of(pr)] for pr in pairs]
        a_s = [_dot_nt(at_s[c, pr], st_b[pr]) for pr in pairs]
        u_st = [_dot(inv_s[c, pr], bf(zv_s[c, pr] + jnp.concatenate([a_s[pr], a_s[pr]], axis=0))) for pr in pairs]
        u = [jnp.where(lane_lo, x[:C], x[C:]) for x in u_st]
        uv = [jnp.concatenate([u[pr], vv[pr]], axis=0) for pr in pairs]
        upd = [_dot_tn(uv[pr], bk_s[c, pr]) for pr in pairs]
        for pr in pairs:
            st_s[pr] = st[pr] * ec_s[c, pr] + jnp.where(bd, upd[pr], 0.0)
        for pr in pairs:
            uv_b = bf(uv[pr])
            vu_b = jnp.concatenate([uv_b[C:], uv_b[:C]], axis=0)
            y_in = jnp.where(lane_lo, _dot(wl_s[c, pr], uv_b), _dot(wh_s[c, pr], vu_b))
            y_s[rows, lanes_of(pr)] = _dot_nt(rt_s[c, pr], st_b[pr]) + y_in
        return carry

    lax.fori_loop(0, tb // C, phase_b, 0)
    out_ref[0] = _rwkv_epilogue(y_s[...], bonus, gr_ref[0], p)

    @pl.when(i == pl.num_programs(1) - 1)
    def _():
        for hd in range(N_RWKV_HEADS):
            o = (hd % 2) * HEAD_DIM
            sfin_ref[0, hd] = st_s[hd // 2][o:o + HEAD_DIM, o:o + HEAD_DIM]


def _rwkv_prompt(rkv, lora, gr, params):
    nb, t, _ = rkv.shape
    tb = min(RW_BLOCK, t)
    w = RWKV_WIDTH
    blk = lambda width: pl.BlockSpec((1, tb, width), lambda b, i: (b, i, 0))
    pspecs = [pl.BlockSpec(params[n].shape, lambda b, i: (0, 0)) for n in _RW_PARAM_NAMES]
    vec = lambda: pltpu.VMEM((tb, w), F32)
    nch, npair, c = tb // RW_CHUNK, N_RWKV_HEADS // 2, RW_CHUNK
    per_chunk = lambda rows, dt: pltpu.VMEM((nch, npair, rows, LANES), dt)
    return pl.pallas_call(
        _rwkv_prompt_kernel,
        grid=(nb, t // tb),
        in_specs=[blk(3 * w), blk(2 * LORA), blk(w)] + pspecs,
        out_specs=[blk(w), pl.BlockSpec((1, N_RWKV_HEADS, HEAD_DIM, HEAD_DIM), lambda b, i: (b, 0, 0, 0))],
        out_shape=[jax.ShapeDtypeStruct((nb, t, w), F32),
                   jax.ShapeDtypeStruct((nb, N_RWKV_HEADS, HEAD_DIM, HEAD_DIM), F32)],
        scratch_shapes=[pltpu.VMEM((N_RWKV_HEADS // 2, LANES, LANES), F32),
                        pltpu.VMEM((1, 3 * w), F32), pltpu.VMEM((1, 2 * LORA), F32),
                        vec(), vec(), vec(), vec(), vec(), vec(), vec(),
                        per_chunk(c, BF16), per_chunk(c, BF16), per_chunk(2 * c, F32), per_chunk(1, F32),
                        per_chunk(c, BF16), per_chunk(c, BF16), per_chunk(2 * c, F32), per_chunk(2 * c, BF16)],
        compiler_params=_cparams(("arbitrary", "arbitrary")),
        name="rwkv_prompt",
    )(rkv, lora, gr, *[params[n] for n in _RW_PARAM_NAMES])


def _rwkv_dec_prep_kernel(rkv_ref, lora_ref, prev_ref, prevl_ref, *rest):
    nparam = len(_RW_PARAM_NAMES)
    p = _load_params(rest[:nparam])
    outs = rest[nparam:]
    r, logw, k_mod, v, al, be, bonus = _rwkv_prep(rkv_ref[...], lora_ref[...], prev_ref[...], prevl_ref[...], p)
    for o, val in zip(outs[:6], (r, jnp.exp(logw), k_mod, v, al, be)):
        o[...] = val.T
    outs[6][...] = bonus


def _rwkv_dec_prep(rkv, lora, prev, prevl, params):
    n = rkv.shape[0]
    full = lambda a: pl.BlockSpec(a.shape, lambda: tuple(0 for _ in a.shape))
    args = [rkv, lora, prev, prevl] + [params[k] for k in _RW_PARAM_NAMES]
    t_shape, shape = (RWKV_WIDTH, n), (n, RWKV_WIDTH)
    return pl.pallas_call(
        _rwkv_dec_prep_kernel,
        in_specs=[full(a) for a in args],
        out_specs=[pl.BlockSpec(t_shape, lambda: (0, 0))] * 6 + [pl.BlockSpec(shape, lambda: (0, 0))],
        out_shape=[jax.ShapeDtypeStruct(t_shape, F32)] * 6 + [jax.ShapeDtypeStruct(shape, F32)],
        compiler_params=pltpu.CompilerParams(vmem_limit_bytes=VMEM_LIMIT),
        name="rwkv_dec_prep",
    )(*args)


def _rwkv_dec_state_kernel(r_ref, w_ref, k_ref, v_ref, a_ref, b_ref, s_ref, so_ref, y_ref):
    r, w, k, a, b = r_ref[0], w_ref[0], k_ref[0], a_ref[0], b_ref[0]

    def one(i, carry):
        st = s_ref[0, i]
        sa = jnp.sum(st * a, axis=0, keepdims=True)
        sn = st * w + sa * b + v_ref[0, pl.ds(i, 1), :] * k
        so_ref[0, i] = sn
        y_ref[0, pl.ds(i, 1), :] = jnp.sum(sn * r, axis=0, keepdims=True)
        return carry

    lax.fori_loop(0, HEAD_DIM, one, 0)


def _rwkv_dec_state(vecs_t, state_t):
    nh, n, _, nb = state_t.shape
    vspec = pl.BlockSpec((1, n, nb), lambda h: (h, 0, 0))
    sspec = pl.BlockSpec((1, n, n, nb), lambda h: (h, 0, 0, 0))
    return pl.pallas_call(
        _rwkv_dec_state_kernel,
        grid=(nh,),
        in_specs=[vspec] * 6 + [sspec],
        out_specs=[sspec, vspec],
        out_shape=[jax.ShapeDtypeStruct(state_t.shape, F32), jax.ShapeDtypeStruct((nh, n, nb), F32)],
        compiler_params=_cparams(("arbitrary",)),
        name="rwkv_dec_state",
    )(*[a.reshape(nh, n, nb) for a in vecs_t], state_t)


def _rwkv_dec_epi_kernel(yt_ref, bonus_ref, gr_ref, lnw_ref, lnb_ref, o_ref):
    o_ref[...] = _rwkv_epilogue(yt_ref[...].T, bonus_ref[...], gr_ref[...],
                                {"ln_w": lnw_ref[...], "ln_b": lnb_ref[...]})


def _rwkv_dec_epi(y_t, bonus, gr, params):
    n = bonus.shape[0]
    args = [y_t, bonus, gr, params["ln_w"], params["ln_b"]]
    return pl.pallas_call(
        _rwkv_dec_epi_kernel,
        in_specs=[pl.BlockSpec(a.shape, lambda: (0, 0)) for a in args],
        out_specs=pl.BlockSpec((n, RWKV_WIDTH), lambda: (0, 0)),
        out_shape=jax.ShapeDtypeStruct((n, RWKV_WIDTH), F32),
        name="rwkv_dec_epi",
    )(*args)


def _attn_prompt_kernel(q_ref, qi_ref, wit_ref, k_ref, vt_ref, kidx_ref, sg_ref, tri_ref, o_ref,
                        s_scr, m_scr, acc_scr, *, topk):
    j = pl.program_id(1)
    QB, KC = Q_BLOCK, KEY_CHUNK
    nch = (j * QB) // KC + 1
    inf = jnp.inf
    qpos_i = j * QB + _iota((1, QB), 1)
    k_q = jnp.minimum(float(topk), qpos_i.astype(F32) + 1.0)
    wit = wit_ref[0]
    qi = qi_ref[0]
    qi_stack = jnp.concatenate([qi[:, h * IDX_DIM:(h + 1) * IDX_DIM] for h in range(N_IDX_HEADS)], axis=0)

    def chunk_rows(c):
        return pl.ds(pl.multiple_of(c * KC, KC), KC)

    SUB = 32
    part = lambda a: a.reshape(KC // SUB, SUB, QB)
    psum = lambda a: jnp.sum(part(a), axis=0)
    pmin = lambda a: jnp.min(part(a), axis=0)
    pmax = lambda a: jnp.max(part(a), axis=0)
    full8 = lambda val: jnp.full((SUB, QB), val, F32)
    fold_sum = lambda a: jnp.sum(a, axis=0, keepdims=True)
    fold_min = lambda a: jnp.min(a, axis=0, keepdims=True)
    fold_max = lambda a: jnp.max(a, axis=0, keepdims=True)

    def score_chunk(c, lohi):
        rows = chunk_rows(c)
        st = _dot_nt(kidx_ref[0, rows, :], qi_stack)
        acc = jnp.zeros((KC, QB), F32)
        for h in range(N_IDX_HEADS):
            acc = acc + jnp.maximum(st[:, h * QB:(h + 1) * QB], 0.0) * wit[h:h + 1, :]
        ok = (c * KC + _iota((KC, QB), 0)) <= qpos_i
        s_scr[rows, :] = jnp.where(ok, acc, -inf)
        return (jnp.minimum(lohi[0], pmin(jnp.where(ok, acc, inf))),
                jnp.maximum(lohi[1], pmax(jnp.where(ok, acc, -inf))))

    lo8, hi8 = lax.fori_loop(0, nch, score_chunk, (full8(inf), full8(-inf)))
    lo, hi = fold_min(lo8), fold_max(hi8)

    def sweep(fn, init):
        return lax.fori_loop(0, nch, lambda c, car: fn(car, s_scr[chunk_rows(c), :]), init)

    def bisect(_, lh):
        lo, hi = lh
        mid = 0.5 * lo + 0.5 * hi
        cnt = fold_sum(sweep(lambda c, s: c + psum(jnp.where(s >= mid, 1.0, 0.0)), full8(0.0)))
        ge = cnt >= k_q
        return jnp.where(ge, mid, lo), jnp.where(ge, hi, mid)

    lo, hi = lax.fori_loop(0, N_BISECT, bisect, (lo, hi))
    v0 = fold_min(sweep(lambda c, s: jnp.minimum(c, pmin(jnp.where(s >= lo, s, inf))), full8(inf)))

    def refine(state):
        v, _, _ = state

        def f(car, s):
            g = s > v
            return car[0] + psum(jnp.where(g, 1.0, 0.0)), jnp.minimum(car[1], pmin(jnp.where(g, s, inf)))

        cnt8, v28 = sweep(f, (full8(0.0), full8(inf)))
        cnt, v2 = fold_sum(cnt8), fold_min(v28)
        ok = cnt < k_q
        return jnp.where(ok, v, v2), cnt, jnp.sum(jnp.where(ok, 0.0, 1.0))

    full = lambda val: jnp.full((1, QB), val, F32)
    thr, c_gt, _ = lax.while_loop(lambda st: st[2] > 0.0, refine, (v0, full(0.0), jnp.float32(1.0)))
    need = k_q - c_gt

    q = q_ref[0]
    G, R = N_KV_HEADS, N_Q_HEADS // N_KV_HEADS
    q_stack = [jnp.concatenate([q[:, (g * R + r) * HEAD_DIM:(g * R + r + 1) * HEAD_DIM] for r in range(R)], axis=0)
               for g in range(G)]
    ones_rows = jnp.where(_iota((HEAD_DIM, KC), 0) == 0, 1.0, 0.0).astype(BF16)
    m_scr[...] = jnp.full(m_scr.shape, -inf, F32)
    acc_scr[...] = jnp.zeros_like(acc_scr)

    n_ge = fold_sum(sweep(lambda c, s: c + psum(jnp.where(s >= thr, 1.0, 0.0)), full8(0.0)))
    any_ties = jnp.sum(jnp.where(n_ge == k_q, 0.0, 1.0)) > 0.0

    def make_attend(with_ties):
        def attend(c, tie_carry):
            rows = chunk_rows(c)
            s = s_scr[rows, :]
            if with_ties:
                eq = s == thr
                rank = tie_carry + _dot(tri_ref[...], jnp.where(eq, 1.0, 0.0).astype(BF16))
                sel = (s > thr) | (eq & (rank < need))
                tie_carry = tie_carry + fold_sum(psum(jnp.where(eq, 1.0, 0.0)))
            else:
                sel = s >= thr
            bias = jnp.where(sel, 0.0, -inf).astype(BF16)
            bias_r = jnp.concatenate([bias] * R, axis=1)
            kc = k_ref[0, rows, :]
            logits = [_dot_nt(kc[:, g * HEAD_DIM:(g + 1) * HEAD_DIM], q_stack[g]) for g in range(G)]
            probs, alphas = [], []
            for g in range(G):
                lm = logits[g].astype(BF16) + bias_r
                m_old = m_scr[g]
                m_part = jnp.max(lm.reshape(KC // 16, 16, R * QB), axis=0).astype(F32)
                m_new = jnp.maximum(m_old, jnp.max(m_part, axis=0, keepdims=True))
                m_safe = jnp.where(m_new == -inf, 0.0, m_new)
                alphas.append(jnp.exp2(m_old - m_safe))
                probs.append(jnp.exp2(lm - m_safe.astype(BF16)))
                m_scr[g] = m_new
            for g in range(G):
                vt_ext = jnp.concatenate([vt_ref[0, g * HEAD_DIM:(g + 1) * HEAD_DIM, rows], ones_rows], axis=0)
                acc_scr[g] = acc_scr[g] * alphas[g] + _dot(vt_ext, probs[g])
            return tie_carry
        return attend

    @pl.when(any_ties)
    def _():
        lax.fori_loop(0, nch, make_attend(True), full(0.0))

    @pl.when(jnp.logical_not(any_ties))
    def _():
        lax.fori_loop(0, nch, make_attend(False), full(0.0))

    out_t = jnp.concatenate([acc_scr[g][:HEAD_DIM] / acc_scr[g][HEAD_DIM:HEAD_DIM + 1] for g in range(G)], axis=0)
    per_r = [out_t[:, r * QB:(r + 1) * QB].T for r in range(R)]
    att = jnp.concatenate([per_r[r][:, g * HEAD_DIM:(g + 1) * HEAD_DIM] for g in range(G) for r in range(R)], axis=1)
    o_ref[0] = att * sg_ref[0]


def _attn_prompt(q, qi, wit, kbf, vt, kibf, sg):
    nb, t, _ = q.shape
    topk = min(TOPK_MAX, t // 4)
    kc = KEY_CHUNK
    tri = jnp.where(jnp.arange(kc)[None, :] < jnp.arange(kc)[:, None], 1.0, 0.0).astype(BF16)
    qblk = lambda width: pl.BlockSpec((1, Q_BLOCK, width), lambda b, j: (b, j, 0))
    whole = lambda r, c: pl.BlockSpec((1, r, c), lambda b, j: (b, 0, 0))
    return pl.pallas_call(
        functools.partial(_attn_prompt_kernel, topk=topk),
        grid=(nb, t // Q_BLOCK),
        in_specs=[qblk(ATT_WIDTH), qblk(N_IDX_HEADS * IDX_DIM),
                  pl.BlockSpec((1, N_IDX_HEADS, Q_BLOCK), lambda b, j: (b, 0, j)),
                  whole(t, KV_WIDTH), whole(KV_WIDTH, t), whole(t, IDX_DIM), qblk(ATT_WIDTH),
                  pl.BlockSpec((kc, kc), lambda b, j: (0, 0))],
        out_specs=qblk(ATT_WIDTH),
        out_shape=jax.ShapeDtypeStruct((nb, t, ATT_WIDTH), F32),
        scratch_shapes=[pltpu.VMEM((t, Q_BLOCK), F32),
                        pltpu.VMEM((N_KV_HEADS, 1, N_Q_HEADS // N_KV_HEADS * Q_BLOCK), F32),
                        pltpu.VMEM((N_KV_HEADS, 2 * HEAD_DIM, N_Q_HEADS // N_KV_HEADS * Q_BLOCK), F32)],
        compiler_params=_cparams(("arbitrary", "arbitrary")),
        name="attn_prompt",
    )(q, qi, wit, kbf, vt, kibf, sg, tri)


def _outproj_kernel(att_ref, rw_ref, w_ref, x_ref, g_ref, o_ref):
    mix = _dot(att_ref[0].astype(BF16), w_ref[:ATT_WIDTH, :]) + _dot(rw_ref[0].astype(BF16), w_ref[ATT_WIDTH:, :])
    o_ref[0] = x_ref[0] + g_ref[0] * mix


def _out_proj(att, rw, w_out_bf, x3, gate3, tm):
    nb, t, d = x3.shape
    if gate3.shape[1] == 1:
        gspec = pl.BlockSpec((1, 1, d), lambda b, i: (b, 0, 0))
    else:
        gspec = pl.BlockSpec((1, tm, d), lambda b, i: (b, i, 0))
    blk = lambda width: pl.BlockSpec((1, tm, width), lambda b, i: (b, i, 0))
    return pl.pallas_call(
        _outproj_kernel,
        grid=(nb, t // tm),
        in_specs=[blk(ATT_WIDTH), blk(RWKV_WIDTH), pl.BlockSpec(w_out_bf.shape, lambda b, i: (0, 0)), blk(d), gspec],
        out_specs=blk(d),
        out_shape=jax.ShapeDtypeStruct((nb, t, d), F32),
        compiler_params=_cparams(("arbitrary", "arbitrary")),
        name="out_proj",
    )(att, rw, w_out_bf, x3, gate3)


def _dec_score_kernel(pt_ref, qi_ref, wi_ref, *refs):
    npg = len(refs) - 1
    o_ref = refs[npg]
    qi = qi_ref[0]
    qi16 = jnp.concatenate([qi, jnp.zeros_like(qi)], axis=0).astype(BF16)
    wi = wi_ref[0]
    for i in range(npg):
        s = _dot(qi16, refs[i][0].astype(BF16))[:N_IDX_HEADS]
        o_ref[0, :, i * PAGE_SIZE:(i + 1) * PAGE_SIZE] = jnp.sum(jnp.maximum(s, 0.0) * wi, axis=0, keepdims=True)


def _page_specs(n, block, npg):
    zeros = (0,) * (len(block) - 1)
    return [pl.BlockSpec(block, functools.partial(lambda b, g, pt, i: (pt[b, g * npg + i],) + zeros, i=i))
            for i in range(n)]


def _dec_scores(page_table, qi3, wi3, kidx_t):
    nb, n_pages = page_table.shape
    npg = min(SCORE_PAGES_PER_STEP, n_pages)
    grid_spec = pltpu.PrefetchScalarGridSpec(
        num_scalar_prefetch=1,
        grid=(nb, n_pages // npg),
        in_specs=[pl.BlockSpec((1, N_IDX_HEADS, IDX_DIM), lambda b, g, pt: (b, 0, 0)),
                  pl.BlockSpec((1, N_IDX_HEADS, 1), lambda b, g, pt: (b, 0, 0))]
                 + _page_specs(npg, (1, IDX_DIM, PAGE_SIZE), npg),
        out_specs=pl.BlockSpec((1, 1, npg * PAGE_SIZE), lambda b, g, pt: (b, 0, g)),
    )
    return pl.pallas_call(
        _dec_score_kernel,
        grid_spec=grid_spec,
        out_shape=jax.ShapeDtypeStruct((nb, 1, n_pages * PAGE_SIZE), F32),
        compiler_params=_cparams(("arbitrary", "arbitrary")),
        name="dec_scores",
    )(page_table, qi3, wi3, *([kidx_t] * npg))


def _dec_select_kernel(s_ref, qi_ref, ki_ref, wi_ref, sel_ref, sx, *, topk):
    nb, past = s_ref.shape
    inf = jnp.inf
    qi = qi_ref[...].astype(F32)
    ki = ki_ref[...].astype(F32)
    prod = qi * jnp.concatenate([ki] * N_IDX_HEADS, axis=1)
    nq = N_IDX_HEADS * IDX_DIM
    hsel = jnp.where(_iota((nq, LANES), 0) // IDX_DIM == _iota((nq, LANES), 1), 1.0, 0.0).astype(BF16)
    sh = _sum_dot(prod, hsel)[:, :N_IDX_HEADS]
    s_new = jnp.sum(jnp.maximum(sh, 0.0) * wi_ref[...], axis=1, keepdims=True)
    sx[:, :past] = s_ref[...]
    sx[:, past:] = jnp.where(_iota((nb, LANES), 1) == 0, s_new, -inf)

    rsum = lambda a: jnp.sum(a, axis=1, keepdims=True)
    k_q = float(topk)
    s = sx[...]
    lo = jnp.min(jnp.where(s > -inf, s, inf), axis=1, keepdims=True)
    hi = jnp.max(s, axis=1, keepdims=True)

    def bisect(_, lh):
        lo, hi = lh
        mid = 0.5 * lo + 0.5 * hi
        ge = rsum(jnp.where(sx[...] >= mid, 1.0, 0.0)) >= k_q
        return jnp.where(ge, mid, lo), jnp.where(ge, hi, mid)

    lo, hi = lax.fori_loop(0, N_BISECT, bisect, (lo, hi))
    s = sx[...]
    v0 = jnp.min(jnp.where(s >= lo, s, inf), axis=1, keepdims=True)

    def refine(state):
        v, _, _ = state
        s = sx[...]
        g = s > v
        cnt = rsum(jnp.where(g, 1.0, 0.0))
        v2 = jnp.min(jnp.where(g, s, inf), axis=1, keepdims=True)
        ok = cnt < k_q
        return jnp.where(ok, v, v2), cnt, jnp.sum(jnp.where(ok, 0.0, 1.0))

    thr, c_gt, _ = lax.while_loop(lambda st: st[2] > 0.0, refine,
                                  (v0, jnp.zeros((nb, 1), F32), jnp.float32(1.0)))
    need = k_q - c_gt
    upper = jnp.where(_iota((LANES, LANES), 0) < _iota((LANES, LANES), 1), 1.0, 0.0).astype(BF16)
    carry = jnp.zeros((nb, 1), F32)
    for c in range((past + LANES) // LANES):
        sc = sx[:, c * LANES:(c + 1) * LANES]
        eq = sc == thr
        eqf = jnp.where(eq, 1.0, 0.0)
        rank = carry + _dot(eqf.astype(BF16), upper)
        sel_ref[:, c * LANES:(c + 1) * LANES] = jnp.where((sc > thr) | (eq & (rank < need)), 1.0, 0.0)
        carry = carry + rsum(eqf)


def _dec_select(scores, qi, ki, wi, topk):
    nb, past = scores.shape
    args = [scores, qi, ki, wi]
    return pl.pallas_call(
        functools.partial(_dec_select_kernel, topk=topk),
        in_specs=[pl.BlockSpec(a.shape, lambda: (0, 0)) for a in args],
        out_specs=pl.BlockSpec((nb, past + LANES), lambda: (0, 0)),
        out_shape=jax.ShapeDtypeStruct((nb, past + LANES), F32),
        scratch_shapes=[pltpu.VMEM((nb, past + LANES), F32)],
        compiler_params=pltpu.CompilerParams(vmem_limit_bytes=VMEM_LIMIT),
        name="dec_select",
    )(*args)


def _dec_attn_kernel(pt_ref, q_ref, sel_ref, selnew_ref, knew_ref, vnew_ref, sg_ref, *refs, npg):
    k_pages, v_pages = refs[:npg], refs[npg:2 * npg]
    o_ref, m_s, l_s, acc_s = refs[2 * npg:]
    g = pl.program_id(1)
    inf = jnp.inf
    nh = N_Q_HEADS
    R = N_Q_HEADS // N_KV_HEADS

    @pl.when(g == 0)
    def _():
        m_s[...] = jnp.full(m_s.shape, -inf, F32)
        l_s[...] = jnp.zeros_like(l_s)
        acc_s[...] = jnp.zeros_like(acc_s)

    q8 = q_ref[0].astype(F32)
    own = (_iota((nh, KV_WIDTH), 0) // R) == (_iota((nh, KV_WIDTH), 1) // HEAD_DIM)
    q_ext = jnp.where(own, jnp.concatenate([q8, q8], axis=1), 0.0)
    q16 = jnp.concatenate([q_ext, jnp.zeros_like(q_ext)], axis=0).astype(BF16)
    m, l, acc = m_s[...], l_s[...], acc_s[...]

    def fold(m, l, acc, lm, pv_fn):
        m_new = jnp.maximum(m, jnp.max(lm, axis=1, keepdims=True))
        m_safe = jnp.where(m_new == -inf, 0.0, m_new)
        alpha = jnp.exp2(m - m_safe)
        p = jnp.exp2(lm - m_safe)
        return m_new, alpha * l + jnp.sum(p, axis=1, keepdims=True), alpha * acc + pv_fn(p)

    logits = jnp.concatenate([_dot(q16, k_pages[i][0].astype(BF16))[:nh] for i in range(npg)], axis=1)
    lm = jnp.where(sel_ref[0] > 0.5, logits, -inf)

    def pv(p):
        p16 = jnp.concatenate([p, jnp.zeros_like(p)], axis=0).astype(BF16)
        out = _dot_nt(p16[:, :PAGE_SIZE], v_pages[0][0].astype(BF16))
        for i in range(1, npg):
            out = out + _dot_nt(p16[:, i * PAGE_SIZE:(i + 1) * PAGE_SIZE], v_pages[i][0].astype(BF16))
        return out[:nh]

    m, l, acc = fold(m, l, acc, lm, pv)
    m_s[...], l_s[...], acc_s[...] = m, l, acc

    @pl.when(g == pl.num_programs(1) - 1)
    def _():
        k_new = knew_ref[0].astype(BF16).astype(F32)
        v_new = vnew_ref[0].astype(BF16).astype(F32)
        lg = jnp.sum(q16[:nh].astype(F32) * k_new, axis=1, keepdims=True)
        lm = jnp.where(selnew_ref[0, :, 0:1] > 0.5, lg, -inf)
        m2, l2, acc2 = fold(m, l, acc, lm, lambda p: p.astype(BF16).astype(F32) * v_new)
        out = acc2 / l2
        res = jnp.where(_iota((nh, HEAD_DIM), 0) < R, out[:, :HEAD_DIM], out[:, HEAD_DIM:])
        o_ref[0] = res * sg_ref[0]


def _dec_attn(page_table, q3, sel3, k_new3, v_new3, sg3, cache_k2, cache_v2):
    nb, n_pages = page_table.shape
    npg = min(ATTN_PAGES_PER_STEP, n_pages)
    past = n_pages * PAGE_SIZE
    per_seq = lambda r, c: pl.BlockSpec((1, r, c), lambda b, g, pt: (b, 0, 0))
    grid_spec = pltpu.PrefetchScalarGridSpec(
        num_scalar_prefetch=1,
        grid=(nb, n_pages // npg),
        in_specs=[per_seq(N_Q_HEADS, HEAD_DIM),
                  pl.BlockSpec((1, 1, npg * PAGE_SIZE), lambda b, g, pt: (b, 0, g)),
                  pl.BlockSpec((1, 1, LANES), lambda b, g, pt: (b, 0, past // LANES)),
                  per_seq(1, KV_WIDTH), per_seq(1, KV_WIDTH), per_seq(N_Q_HEADS, HEAD_DIM)]
                 + _page_specs(npg, (1, KV_WIDTH, PAGE_SIZE), npg) + _page_specs(npg, (1, KV_WIDTH, PAGE_SIZE), npg),
        out_specs=per_seq(N_Q_HEADS, HEAD_DIM),
        scratch_shapes=[pltpu.VMEM((N_Q_HEADS, 1), F32), pltpu.VMEM((N_Q_HEADS, 1), F32),
                        pltpu.VMEM((N_Q_HEADS, KV_WIDTH), F32)],
    )
    return pl.pallas_call(
        functools.partial(_dec_attn_kernel, npg=npg),
        grid_spec=grid_spec,
        out_shape=jax.ShapeDtypeStruct((nb, N_Q_HEADS, HEAD_DIM), F32),
        compiler_params=_cparams(("arbitrary", "arbitrary")),
        name="dec_attn",
    )(page_table, q3, sel3, sel3, k_new3, v_new3, sg3, *([cache_k2] * npg), *([cache_v2] * npg))


def kernel(x_prompt, x_sample, cache_k, cache_v, cache_kidx, state_wkv, state_shift, page_table,
           c_prompt, c_sample, norm_w, w_ada, b_ada, w_in, q_norm_w, k_norm_w, mu_shift, w0, w_up,
           a0, a_up, k_k, k_a, r_k, ln_x_w, ln_x_b, w_out):
    nb, t, d = x_prompt.shape
    nd = x_sample.shape[0]
    assert x_sample.shape[1] == 1
    n_pages = page_table.shape[1]
    past = n_pages * PAGE_SIZE
    w = RWKV_WIDTH

    offs = [0]
    for sz in (ATT_WIDTH, KV_WIDTH, KV_WIDTH, N_IDX_HEADS * IDX_DIM, N_IDX_HEADS, IDX_DIM, ATT_WIDTH,
               w, w, w, LORA, LORA, w):
        offs.append(offs[-1] + sz)
    col = lambda i: w_in[:, offs[i]:offs[i + 1]]
    w_main = jnp.concatenate([col(0), col(1), col(2), col(3), col(6), col(7), col(8), col(9), col(12),
                              col(10), col(11), col(5), jnp.zeros((d, LANES - IDX_DIM), w_in.dtype)], axis=1).astype(BF16)
    zl = jnp.zeros((LORA, w), F32)
    wts = {
        "norm_w": norm_w.reshape(1, d), "w_main": w_main,
        "w_wit": col(4).T.astype(BF16), "w_vt": col(2).T.astype(BF16),
        "qnw": jnp.concatenate([q_norm_w, q_norm_w]).reshape(1, LANES),
        "knw": jnp.concatenate([k_norm_w, k_norm_w]).reshape(1, LANES),
    }
    rw_params = {
        "mu_rkv": mu_shift[:3 * w].reshape(1, 3 * w), "mu_lora": mu_shift[3 * w:].reshape(1, 2 * LORA),
        "w0": w0.reshape(1, w), "a0": a0.reshape(1, w),
        "w_lora": jnp.concatenate([jnp.concatenate([w_up, zl], axis=1),
                                   jnp.concatenate([zl, a_up], axis=1)], axis=0).astype(BF16),
        "k_k": k_k.reshape(1, w), "k_a": k_a.reshape(1, w), "r_k": r_k.reshape(1, w),
        "ln_w": ln_x_w.reshape(1, w), "ln_b": ln_x_b.reshape(1, w),
    }
    w_out_bf = w_out.astype(BF16)

    n_c = nb + nd
    pad = (-n_c) % 8
    c_all = jnp.concatenate([c_prompt, c_sample, jnp.zeros((pad, d), F32)], axis=0)
    mod = _adaln_mod(c_all, w_ada, b_ada)
    shift_p, scale_p, gate_p = (mod[:nb, s * d:(s + 1) * d].reshape(nb, 1, d) for s in range(3))
    shift_s, scale_s, gate_s = (mod[nb:n_c, s * d:(s + 1) * d].reshape(1, nd, d) for s in range(3))

    tm = min(512, t)
    pr = _in_proj(x_prompt, shift_p, scale_p, _rope_tables(jnp.arange(t)), wts, tm)
    att_p = _attn_prompt(pr["q"], pr["qi"], pr["wit"], pr["kbf"], pr["vt"], pr["kibf"], pr["sg"])
    rw_p, s_p = _rwkv_prompt(pr["rkv"], pr["lora"], pr["gr"], rw_params)
    y_prompt = _out_proj(att_p, rw_p, w_out_bf, x_prompt, gate_p, tm)
    shift_prompt = jnp.concatenate([pr["rkv"][:, t - 1:], pr["lora"][:, t - 1:]], axis=-1)

    xs3 = x_sample.reshape(1, nd, d)
    sm = _in_proj(xs3, shift_s, scale_s, _rope_tables(jnp.full((1,), past)), wts, nd)
    qi_s, ki_s = sm["qi"][0], sm["ki"][0]
    wi_s = sm["wit"][0].T
    n_phys = cache_k.shape[0]
    k_pages = jnp.transpose(cache_k, (0, 2, 3, 1)).reshape(n_phys, KV_WIDTH, PAGE_SIZE)
    v_pages = jnp.transpose(cache_v, (0, 2, 3, 1)).reshape(n_phys, KV_WIDTH, PAGE_SIZE)
    scores = _dec_scores(page_table, qi_s.reshape(nd, N_IDX_HEADS, IDX_DIM), wi_s.reshape(nd, N_IDX_HEADS, 1),
                         jnp.transpose(cache_kidx, (0, 2, 1)))
    topk = min(TOPK_MAX, (past + 1) // 4)
    sel = _dec_select(scores.reshape(nd, past), qi_s, sm["kibf"][0], wi_s, topk)
    att_s = _dec_attn(page_table, sm["q"][0].reshape(nd, N_Q_HEADS, HEAD_DIM), sel.reshape(nd, 1, past + LANES),
                      sm["k"][0].reshape(nd, 1, KV_WIDTH), sm["v"][0].reshape(nd, 1, KV_WIDTH),
                      sm["sg"][0].reshape(nd, N_Q_HEADS, HEAD_DIM),
                      k_pages, v_pages)
    prev = state_shift.reshape(nd, -1)
    vecs = _rwkv_dec_prep(sm["rkv"][0], sm["lora"][0], prev[:, :3 * w], prev[:, 3 * w:], rw_params)
    s_t, y_t = _rwkv_dec_state(vecs[:6], jnp.transpose(state_wkv, (1, 2, 3, 0)))
    s_s = jnp.transpose(s_t, (3, 0, 1, 2))
    rw_s = _rwkv_dec_epi(y_t.reshape(w, nd), vecs[6], sm["gr"][0], rw_params)
    y_sample = _out_proj(att_s.reshape(1, nd, ATT_WIDTH), rw_s.reshape(1, nd, w), w_out_bf, xs3, gate_s, nd)
    shift_sample = jnp.concatenate([sm["rkv"][0], sm["lora"][0]], axis=-1).reshape(nd, 1, 3 * w + 2 * LORA)

    return (y_prompt, y_sample.reshape(nd, 1, d),
            pr["k"].reshape(nb, t, N_KV_HEADS, HEAD_DIM), pr["v"].reshape(nb, t, N_KV_HEADS, HEAD_DIM), pr["ki"],
            s_p, shift_prompt,
            sm["k"][0].reshape(nd, 1, N_KV_HEADS, HEAD_DIM), sm["v"][0].reshape(nd, 1, N_KV_HEADS, HEAD_DIM),
            ki_s.reshape(nd, 1, IDX_DIM), s_s, shift_sample)
```

```python
import functools

import jax
import jax.numpy as jnp
from jax import lax
from jax.experimental import pallas as pl
from jax.experimental.pallas import tpu as pltpu

F32, BF16 = jnp.float32, jnp.bfloat16

HEAD_DIM = 64
N_Q_HEADS = 8
N_KV_HEADS = 2
N_IDX_HEADS = 8
IDX_DIM = 64
N_RWKV_HEADS = 8
ATT_WIDTH = N_Q_HEADS * HEAD_DIM
RWKV_WIDTH = N_RWKV_HEADS * HEAD_DIM
KV_WIDTH = N_KV_HEADS * HEAD_DIM
LORA = 64
TOPK_MAX = 256
ROPE_THETA = 500000.0
ROPE_DIMS = HEAD_DIM // 4
NORM_EPS = 1e-6
GN_EPS = 64e-5
PAGE_SIZE = 128
LANES = 128
Q_SCALE = HEAD_DIM ** -0.5 * 1.4426950408889634
VMEM_LIMIT = 56 * 1024 * 1024

Q_BLOCK = 128
KEY_CHUNK = 512
N_BISECT = 16
RW_CHUNK = 64
RW_BLOCK = 512


def _dot(a, b):
    return jnp.dot(a, b, preferred_element_type=F32)


def _dot_nt(a, b):
    return lax.dot_general(a, b, (((1,), (1,)), ((), ())), preferred_element_type=F32)


def _split(x):
    hi = x.astype(BF16)
    return hi, (x - hi.astype(F32)).astype(BF16)


def _sum_dot(x, sel01):
    hi, lo = _split(x)
    return _dot(hi, sel01) + _dot(lo, sel01)


def _sum_dot_left(sel01, x):
    hi, lo = _split(x)
    return _dot(sel01, hi) + _dot(sel01, lo)


def _dot_tn(a, b):
    return lax.dot_general(a, b, (((0,), (0,)), ((), ())), preferred_element_type=F32)


def _iota(shape, dim):
    return lax.broadcasted_iota(jnp.int32, shape, dim)


def _head_block_ones(n):
    return jnp.where(_iota((n, n), 0) // HEAD_DIM == _iota((n, n), 1) // HEAD_DIM, 1.0, 0.0).astype(BF16)


def _silu(x):
    return x * jax.nn.sigmoid(x)


def _cparams(sem):
    return pltpu.CompilerParams(dimension_semantics=sem, vmem_limit_bytes=VMEM_LIMIT)


def _mod_kernel(c_ref, w_ref, b_ref, o_ref):
    s = _silu(c_ref[...])
    o_ref[...] = _dot(s.astype(BF16), w_ref[...].astype(BF16)) + b_ref[...]


def _adaln_mod(c, w_ada, b_ada):
    rows, d = c.shape
    n = w_ada.shape[1]
    tn = 1024
    return pl.pallas_call(
        _mod_kernel,
        grid=(n // tn,),
        in_specs=[pl.BlockSpec((rows, d), lambda j: (0, 0)),
                  pl.BlockSpec((d, tn), lambda j: (0, j)),
                  pl.BlockSpec((1, tn), lambda j: (0, j))],
        out_specs=pl.BlockSpec((rows, tn), lambda j: (0, j)),
        out_shape=jax.ShapeDtypeStruct((rows, n), F32),
        compiler_params=_cparams(("arbitrary",)),
        name="adaln_mod",
    )(c, w_ada, b_ada.reshape(1, n))


_W_Q, _W_KV, _W_QI, _W_GA, _W_RKV, _W_GR, _W_TAIL, _W_END = 0, 512, 768, 1280, 1792, 3328, 3840, 4096


def _inproj_kernel(x_ref, shift_ref, scale_ref, nw_ref, w_ref, wwi_ref, wvt_ref, cos_ref, s1_ref, s2_ref,
                   qnw_ref, knw_ref,
                   q_o, k_o, kbf_o, v_o, vt_o, qi_o, ki_o, kibf_o, wit_o, sg_o, rkv_o, lora_o, gr_o):
    x = x_ref[0]
    ms = jnp.mean(x * x, axis=-1, keepdims=True)
    h = x * lax.rsqrt(ms + NORM_EPS) * nw_ref[...]
    h = h * (1.0 + scale_ref[0]) + shift_ref[0]
    hb = h.astype(BF16)
    cos, s1, s2 = cos_ref[...], s1_ref[...], s2_ref[...]
    bsum = _head_block_ones(LANES)

    def seg(a, b):
        return _dot(hb, w_ref[:, a:b])

    def rope(t):
        return t * cos + pltpu.roll(t, LANES - ROPE_DIMS // 2, 1) * s1 + pltpu.roll(t, ROPE_DIMS // 2, 1) * s2

    def head_norm(t, w):
        return t * lax.rsqrt(_sum_dot(t * t, bsum) * (1.0 / HEAD_DIM) + NORM_EPS) * w

    yq = seg(_W_Q, _W_KV)
    for s in range(ATT_WIDTH // LANES):
        t = rope(head_norm(yq[:, s * LANES:(s + 1) * LANES], qnw_ref[...]))
        q_o[0, :, s * LANES:(s + 1) * LANES] = (t * Q_SCALE).astype(BF16)

    ykv = seg(_W_KV, _W_QI)
    k = rope(head_norm(ykv[:, :KV_WIDTH], knw_ref[...]))
    k_o[0] = k
    kbf_o[0] = k.astype(BF16)
    v_o[0] = ykv[:, KV_WIDTH:]
    vt_o[0] = _dot_nt(wvt_ref[...], hb).astype(BF16)

    yqi = seg(_W_QI, _W_GA)
    for s in range(N_IDX_HEADS * IDX_DIM // LANES):
        qi_o[0, :, s * LANES:(s + 1) * LANES] = rope(yqi[:, s * LANES:(s + 1) * LANES]).astype(BF16)

    sg_o[0] = _silu(seg(_W_GA, _W_RKV))
    for s in range(3):
        rkv_o[0, :, s * RWKV_WIDTH:(s + 1) * RWKV_WIDTH] = seg(_W_RKV + s * RWKV_WIDTH, _W_RKV + (s + 1) * RWKV_WIDTH)
    gr_o[0] = seg(_W_GR, _W_TAIL)

    yt = seg(_W_TAIL, _W_END)
    lora_o[0] = yt[:, :2 * LORA]
    ki = rope(yt[:, 2 * LORA:])[:, :IDX_DIM]
    ki_o[0] = ki
    kibf_o[0] = ki.astype(BF16)
    wit_o[0] = _dot_nt(wwi_ref[...], hb) * (N_IDX_HEADS ** -0.5 * IDX_DIM ** -0.5)


def _in_proj(x3, shift3, scale3, tabs, wts, tm):
    nb, t, d = x3.shape
    mod_rows = shift3.shape[1]
    tab_rows = tabs[0].shape[0]
    grid = (nb, t // tm)
    if mod_rows == 1:
        mod_spec = pl.BlockSpec((1, 1, d), lambda b, i: (b, 0, 0))
    else:
        mod_spec = pl.BlockSpec((1, tm, d), lambda b, i: (b, i, 0))
    if tab_rows == 1:
        tab_spec = pl.BlockSpec((1, LANES), lambda b, i: (0, 0))
    else:
        tab_spec = pl.BlockSpec((tm, LANES), lambda b, i: (i, 0))
    const = lambda shape: pl.BlockSpec(shape, lambda b, i: tuple(0 for _ in shape))
    row = lambda width: pl.BlockSpec((1, tm, width), lambda b, i: (b, i, 0))
    col = lambda rows: pl.BlockSpec((1, rows, tm), lambda b, i: (b, 0, i))
    out_defs = [
        ("q", (t, ATT_WIDTH), BF16, row(ATT_WIDTH)),
        ("k", (t, KV_WIDTH), F32, row(KV_WIDTH)),
        ("kbf", (t, KV_WIDTH), BF16, row(KV_WIDTH)),
        ("v", (t, KV_WIDTH), F32, row(KV_WIDTH)),
        ("vt", (KV_WIDTH, t), BF16, col(KV_WIDTH)),
        ("qi", (t, N_IDX_HEADS * IDX_DIM), BF16, row(N_IDX_HEADS * IDX_DIM)),
        ("ki", (t, IDX_DIM), F32, row(IDX_DIM)),
        ("kibf", (t, IDX_DIM), BF16, row(IDX_DIM)),
        ("wit", (N_IDX_HEADS, t), F32, col(N_IDX_HEADS)),
        ("sg", (t, ATT_WIDTH), F32, row(ATT_WIDTH)),
        ("rkv", (t, 3 * RWKV_WIDTH), F32, row(3 * RWKV_WIDTH)),
        ("lora", (t, 2 * LORA), F32, row(2 * LORA)),
        ("gr", (t, RWKV_WIDTH), F32, row(RWKV_WIDTH)),
    ]
    outs = pl.pallas_call(
        _inproj_kernel,
        grid=grid,
        in_specs=[pl.BlockSpec((1, tm, d), lambda b, i: (b, i, 0)), mod_spec, mod_spec, const((1, d)),
                  const(wts["w_main"].shape), const(wts["w_wit"].shape), const(wts["w_vt"].shape),
                  tab_spec, tab_spec, tab_spec, const((1, LANES)), const((1, LANES))],
        out_specs=[o[3] for o in out_defs],
        out_shape=[jax.ShapeDtypeStruct((nb,) + o[1], o[2]) for o in out_defs],
        compiler_params=_cparams(("arbitrary", "arbitrary")),
        name="in_proj",
    )(x3, shift3, scale3, wts["norm_w"], wts["w_main"], wts["w_wit"], wts["w_vt"], tabs[0], tabs[1], tabs[2],
      wts["qnw"], wts["knw"])
    return {o[0]: arr for o, arr in zip(out_defs, outs)}


def _rope_tables(pos):
    half = ROPE_DIMS // 2
    inv = jnp.power(ROPE_THETA, -jnp.arange(half, dtype=F32) / half)
    ang = pos.astype(F32)[:, None] * inv[None, :]
    cos, sin = jnp.cos(ang), jnp.sin(ang)
    n = pos.shape[0]
    pad = jnp.zeros((n, HEAD_DIM - ROPE_DIMS), F32)
    zero = jnp.zeros((n, half), F32)
    c_head = jnp.concatenate([cos, cos, pad + 1.0], axis=1)
    s1_head = jnp.concatenate([-sin, zero, pad], axis=1)
    s2_head = jnp.concatenate([zero, sin, pad], axis=1)
    two = lambda a: jnp.concatenate([a, a], axis=1)
    return two(c_head), two(s1_head), two(s2_head)


def _head_sum(t):
    ones = _head_block_ones(LANES)
    parts = [_sum_dot(t[:, s * LANES:(s + 1) * LANES], ones) for s in range(t.shape[1] // LANES)]
    return parts[0] if len(parts) == 1 else jnp.concatenate(parts, axis=1)


def _softplus(z):
    return jnp.maximum(z, 0.0) + jnp.log(1.0 + jnp.exp(-jnp.abs(z)))


def _rwkv_prep(xs, xl, prev, prevl, p):
    xm = xs + p["mu_rkv"] * (prev - xs)
    xml = xl + p["mu_lora"] * (prevl - xl)
    w = RWKV_WIDTH
    r, k, v = xm[:, :w], xm[:, w:2 * w], xm[:, 2 * w:]
    lane = _iota(xml.shape, 1)
    lhs = jnp.where(lane < LORA, jnp.tanh(xml), xml)
    lo = _dot(lhs.astype(BF16), p["w_lora"])
    wlog = -_softplus(-(p["w0"] + lo[:, :w])) - 0.5
    logw = -jnp.exp(wlog)
    a = jax.nn.sigmoid(p["a0"] + lo[:, w:])
    kkr = k * p["k_k"]
    kk = kkr / jnp.maximum(jnp.sqrt(_head_sum(kkr * kkr)), 1e-12)
    k_mod = k * (1.0 + (a - 1.0) * p["k_a"])
    bonus = _head_sum(r * k_mod * p["r_k"]) * v
    return r, logw, k_mod, v, -kk, kk * a, bonus


def _rwkv_epilogue(y, bonus, gate, p):
    mean = _head_sum(y) * (1.0 / HEAD_DIM)
    d = y - mean
    var = _head_sum(d * d) * (1.0 / HEAD_DIM)
    yn = d * lax.rsqrt(var + GN_EPS) * p["ln_w"] + p["ln_b"]
    return (yn + bonus) * _silu(gate)


_RW_PARAM_NAMES = ("mu_rkv", "mu_lora", "w0", "a0", "w_lora", "k_k", "k_a", "r_k", "ln_w", "ln_b")


def _load_params(refs):
    return {n: r[...] for n, r in zip(_RW_PARAM_NAMES, refs)}


def _rwkv_prompt_kernel(rkv_ref, lora_ref, gr_ref, *rest):
    nparam = len(_RW_PARAM_NAMES)
    p = _load_params(rest[:nparam])
    out_ref, sfin_ref = rest[nparam:nparam + 2]
    (st_s, c_rkv, c_lora, r_s, lw_s, k_s, v_s, al_s, be_s, y_s,
     at_s, rt_s, bk_s, ec_s, wl_s, wh_s, zv_s, inv_s) = rest[nparam + 2:]
    i = pl.program_id(1)
    C = RW_CHUNK
    W = 2 * C

    @pl.when(i == 0)
    def _():
        st_s[...] = jnp.zeros_like(st_s)
        c_rkv[...] = jnp.zeros_like(c_rkv)
        c_lora[...] = jnp.zeros_like(c_lora)

    xs, xl = rkv_ref[0], lora_ref[0]
    tb = xs.shape[0]
    row = _iota((tb, 1), 0)
    prev = jnp.where(row == 0, c_rkv[...], pltpu.roll(xs, 1, 0))
    prevl = jnp.where(row == 0, c_lora[...], pltpu.roll(xl, 1, 0))
    c_rkv[...] = xs[tb - 1:tb, :]
    c_lora[...] = xl[tb - 1:tb, :]
    r, logw, k_mod, v, al, be, bonus = _rwkv_prep(xs, xl, prev, prevl, p)
    r_s[...], lw_s[...], k_s[...], v_s[...], al_s[...], be_s[...] = r, logw, k_mod, v, al, be

    rm, cn = _iota((W, W), 0), _iota((W, W), 1)
    tt, ss = rm % C, cn % C
    top, left = rm < C, cn < C
    strict, incl = ss < tt, ss <= tt
    mk_l_lo, mk_l_hi = top & left & strict, (~top) & (~left) & strict
    mk_z_lo, mk_z_hi = top & (~left) & strict, (~top) & left & strict
    mk_w = (_iota((C, W), 1) % C) <= _iota((C, W), 0)
    bd = (rm // HEAD_DIM) == (cn // HEAD_DIM)
    eye = jnp.where(rm == cn, 1.0, 0.0).astype(F32)
    tril = jnp.where(_iota((C, C), 1) <= _iota((C, C), 0), 1.0, 0.0).astype(BF16)
    lane_lo = _iota((C, LANES), 1) < HEAD_DIM
    m_lo = jnp.where(lane_lo, 1.0, 0.0).astype(F32)
    m_hi = 1.0 - m_lo

    bf = lambda a: a.astype(BF16)
    pairs = range(N_RWKV_HEADS // 2)
    lanes_of = lambda pr: slice(pr * LANES, (pr + 1) * LANES)
    rows_of = lambda c: pl.ds(pl.multiple_of(c * C, C), C)
    unroll = 2

    def phase_a(it, carry):
        items = [(it * unroll + u, pr) for u in range(unroll) for pr in pairs]
        cums = []
        for u in range(unroll):
            cum_all = _sum_dot_left(tril, lw_s[rows_of(it * unroll + u), :])
            cums += [cum_all[:, lanes_of(pr)] for pr in pairs]
        g_lo, g_hi = [], []
        for n, (c, pr) in enumerate(items):
            rows, sl = rows_of(c), lanes_of(pr)
            cum = cums[n]
            cum_c = cum[C - 1:C, :]
            e_neg, e_rel = jnp.exp(-cum), jnp.exp(cum_c - cum)
            rr, kk, aa, bb = r_s[rows, sl], k_s[rows, sl], al_s[rows, sl], be_s[rows, sl]
            a_t, r_t = aa * jnp.exp(cum - lw_s[rows, sl]), rr * jnp.exp(cum)
            b_t, k_t = bb * e_neg, kk * e_neg
            at_s[c, pr], rt_s[c, pr] = bf(a_t), bf(r_t)
            bk_s[c, pr] = jnp.concatenate([bb * e_rel, kk * e_rel], axis=0)
            ec_s[c, pr] = jnp.exp(cum_c)
            xk = bf(jnp.concatenate([b_t, k_t], axis=0))
            kx = bf(jnp.concatenate([k_t, b_t], axis=0))
            g_lo.append(_dot_nt(bf(jnp.concatenate([a_t * m_lo, r_t * m_lo], axis=0)), xk))
            g_hi.append(_dot_nt(bf(jnp.concatenate([r_t * m_hi, a_t * m_hi], axis=0)), kx))
        l_mat = []
        for n, (c, pr) in enumerate(items):
            l_mat.append(jnp.where(mk_l_lo, g_lo[n], 0.0) + jnp.where(mk_l_hi, g_hi[n], 0.0))
            z_mat = jnp.where(mk_z_lo, g_lo[n], 0.0) + jnp.where(mk_z_hi, g_hi[n], 0.0)
            wl_s[c, pr] = bf(jnp.where(mk_w, g_lo[n][C:, :], 0.0))
            wh_s[c, pr] = bf(jnp.where(mk_w, g_hi[n][:C, :], 0.0))
            vv_b = bf(v_s[rows_of(c), lanes_of(pr)])
            zv_s[c, pr] = _dot(bf(z_mat), jnp.concatenate([vv_b, vv_b], axis=0))
        inv = [eye + m for m in l_mat]
        pw = l_mat
        for _ in range(5):
            pw = [_dot(bf(m), bf(m)) for m in pw]
            inv = [x + _dot(bf(x), bf(m)) for x, m in zip(inv, pw)]
        for n, (c, pr) in enumerate(items):
            inv_s[c, pr] = bf(inv[n])
        return carry

    lax.fori_loop(0, tb // (C * unroll), phase_a, 0)

    def phase_b(c, carry):
        rows = rows_of(c)
        st = [st_s[pr] for pr in pairs]
        st_b = [bf(s) for s in st]
        vv = [v_s[rows, lanes_of(pr)] for pr in pairs]
        a_s = [_dot_nt(at_s[c, pr], st_b[pr]) for pr in pairs]
        u_st = [_dot(inv_s[c, pr], bf(zv_s[c, pr] + jnp.concatenate([a_s[pr], a_s[pr]], axis=0))) for pr in pairs]
        u = [jnp.where(lane_lo, x[:C], x[C:]) for x in u_st]
        uv = [jnp.concatenate([u[pr], vv[pr]], axis=0) for pr in pairs]
        upd = [_dot_tn(uv[pr], bk_s[c, pr]) for pr in pairs]
        for pr in pairs:
            st_s[pr] = st[pr] * ec_s[c, pr] + jnp.where(bd, upd[pr], 0.0)
        for pr in pairs:
            uv_b = bf(uv[pr])
            vu_b = jnp.concatenate([uv_b[C:], uv_b[:C]], axis=0)
            y_in = jnp.where(lane_lo, _dot(wl_s[c, pr], uv_b), _dot(wh_s[c, pr], vu_b))
            y_s[rows, lanes_of(pr)] = _dot_nt(rt_s[c, pr], st_b[pr]) + y_in
        return carry

    lax.fori_loop(0, tb // C, phase_b, 0)
    out_ref[0] = _rwkv_epilogue(y_s[...], bonus, gr_ref[0], p)

    @pl.when(i == pl.num_programs(1) - 1)
    def _():
        for hd in range(N_RWKV_HEADS):
            o = (hd % 2) * HEAD_DIM
            sfin_ref[0, hd] = st_s[hd // 2][o:o + HEAD_DIM, o:o + HEAD_DIM]


def _rwkv_prompt(rkv, lora, gr, params):
    nb, t, _ = rkv.shape
    tb = min(RW_BLOCK, t)
    w = RWKV_WIDTH
    blk = lambda width: pl.BlockSpec((1, tb, width), lambda b, i: (b, i, 0))
    pspecs = [pl.BlockSpec(params[n].shape, lambda b, i: (0, 0)) for n in _RW_PARAM_NAMES]
    vec = lambda: pltpu.VMEM((tb, w), F32)
    nch, npair, c = tb // RW_CHUNK, N_RWKV_HEADS // 2, RW_CHUNK
    per_chunk = lambda rows, dt: pltpu.VMEM((nch, npair, rows, LANES), dt)
    return pl.pallas_call(
        _rwkv_prompt_kernel,
        grid=(nb, t // tb),
        in_specs=[blk(3 * w), blk(2 * LORA), blk(w)] + pspecs,
        out_specs=[blk(w), pl.BlockSpec((1, N_RWKV_HEADS, HEAD_DIM, HEAD_DIM), lambda b, i: (b, 0, 0, 0))],
        out_shape=[jax.ShapeDtypeStruct((nb, t, w), F32),
                   jax.ShapeDtypeStruct((nb, N_RWKV_HEADS, HEAD_DIM, HEAD_DIM), F32)],
        scratch_shapes=[pltpu.VMEM((N_RWKV_HEADS // 2, LANES, LANES), F32),
                        pltpu.VMEM((1, 3 * w), F32), pltpu.VMEM((1, 2 * LORA), F32),
                        vec(), vec(), vec(), vec(), vec(), vec(), vec(),
                        per_chunk(c, BF16), per_chunk(c, BF16), per_chunk(2 * c, F32), per_chunk(1, F32),
                        per_chunk(c, BF16), per_chunk(c, BF16), per_chunk(2 * c, F32), per_chunk(2 * c, BF16)],
        compiler_params=_cparams(("arbitrary", "arbitrary")),
        name="rwkv_prompt",
    )(rkv, lora, gr, *[params[n] for n in _RW_PARAM_NAMES])


def _rwkv_dec_prep_kernel(rkv_ref, lora_ref, prev_ref, prevl_ref, *rest):
    nparam = len(_RW_PARAM_NAMES)
    p = _load_params(rest[:nparam])
    outs = rest[nparam:]
    r, logw, k_mod, v, al, be, bonus = _rwkv_prep(rkv_ref[...], lora_ref[...], prev_ref[...], prevl_ref[...], p)
    for o, val in zip(outs[:6], (r, jnp.exp(logw), k_mod, v, al, be)):
        o[...] = val.T
    outs[6][...] = bonus


def _rwkv_dec_prep(rkv, lora, prev, prevl, params):
    n = rkv.shape[0]
    full = lambda a: pl.BlockSpec(a.shape, lambda: tuple(0 for _ in a.shape))
    args = [rkv, lora, prev, prevl] + [params[k] for k in _RW_PARAM_NAMES]
    t_shape, shape = (RWKV_WIDTH, n), (n, RWKV_WIDTH)
    return pl.pallas_call(
        _rwkv_dec_prep_kernel,
        in_specs=[full(a) for a in args],
        out_specs=[pl.BlockSpec(t_shape, lambda: (0, 0))] * 6 + [pl.BlockSpec(shape, lambda: (0, 0))],
        out_shape=[jax.ShapeDtypeStruct(t_shape, F32)] * 6 + [jax.ShapeDtypeStruct(shape, F32)],
        compiler_params=pltpu.CompilerParams(vmem_limit_bytes=VMEM_LIMIT),
        name="rwkv_dec_prep",
    )(*args)


def _rwkv_dec_state_kernel(r_ref, w_ref, k_ref, v_ref, a_ref, b_ref, s_ref, so_ref, y_ref):
    r, w, k, a, b = r_ref[0], w_ref[0], k_ref[0], a_ref[0], b_ref[0]

    def one(i, carry):
        st = s_ref[0, i]
        sa = jnp.sum(st * a, axis=0, keepdims=True)
        sn = st * w + sa * b + v_ref[0, pl.ds(i, 1), :] * k
        so_ref[0, i] = sn
        y_ref[0, pl.ds(i, 1), :] = jnp.sum(sn * r, axis=0, keepdims=True)
        return carry

    lax.fori_loop(0, HEAD_DIM, one, 0)


def _rwkv_dec_state(vecs_t, state_t):
    nh, n, _, nb = state_t.shape
    vspec = pl.BlockSpec((1, n, nb), lambda h: (h, 0, 0))
    sspec = pl.BlockSpec((1, n, n, nb), lambda h: (h, 0, 0, 0))
    return pl.pallas_call(
        _rwkv_dec_state_kernel,
        grid=(nh,),
        in_specs=[vspec] * 6 + [sspec],
        out_specs=[sspec, vspec],
        out_shape=[jax.ShapeDtypeStruct(state_t.shape, F32), jax.ShapeDtypeStruct((nh, n, nb), F32)],
        compiler_params=_cparams(("arbitrary",)),
        name="rwkv_dec_state",
    )(*[a.reshape(nh, n, nb) for a in vecs_t], state_t)


def _rwkv_dec_epi_kernel(yt_ref, bonus_ref, gr_ref, lnw_ref, lnb_ref, o_ref):
    o_ref[...] = _rwkv_epilogue(yt_ref[...].T, bonus_ref[...], gr_ref[...],
                                {"ln_w": lnw_ref[...], "ln_b": lnb_ref[...]})


def _rwkv_dec_epi(y_t, bonus, gr, params):
    n = bonus.shape[0]
    args = [y_t, bonus, gr, params["ln_w"], params["ln_b"]]
    return pl.pallas_call(
        _rwkv_dec_epi_kernel,
        in_specs=[pl.BlockSpec(a.shape, lambda: (0, 0)) for a in args],
        out_specs=pl.BlockSpec((n, RWKV_WIDTH), lambda: (0, 0)),
        out_shape=jax.ShapeDtypeStruct((n, RWKV_WIDTH), F32),
        name="rwkv_dec_epi",
    )(*args)


def _attn_prompt_kernel(q_ref, qi_ref, wit_ref, k_ref, vt_ref, kidx_ref, sg_ref, tri_ref, o_ref,
                        s_scr, m_scr, acc_scr, *, topk):
    j = pl.program_id(1)
    QB, KC = Q_BLOCK, KEY_CHUNK
    nch = (j * QB) // KC + 1
    inf = jnp.inf
    qpos_i = j * QB + _iota((1, QB), 1)
    k_q = jnp.minimum(float(topk), qpos_i.astype(F32) + 1.0)
    wit = wit_ref[0]
    qi = qi_ref[0]
    qi_stack = jnp.concatenate([qi[:, h * IDX_DIM:(h + 1) * IDX_DIM] for h in range(N_IDX_HEADS)], axis=0)

    def chunk_rows(c):
        return pl.ds(pl.multiple_of(c * KC, KC), KC)

    SUB = 32
    part = lambda a: a.reshape(KC // SUB, SUB, QB)
    psum = lambda a: jnp.sum(part(a), axis=0)
    pmin = lambda a: jnp.min(part(a), axis=0)
    pmax = lambda a: jnp.max(part(a), axis=0)
    full8 = lambda val: jnp.full((SUB, QB), val, F32)
    fold_sum = lambda a: jnp.sum(a, axis=0, keepdims=True)
    fold_min = lambda a: jnp.min(a, axis=0, keepdims=True)
    fold_max = lambda a: jnp.max(a, axis=0, keepdims=True)

    def score_chunk(c, lohi):
        rows = chunk_rows(c)
        st = _dot_nt(kidx_ref[0, rows, :], qi_stack)
        acc = jnp.zeros((KC, QB), F32)
        for h in range(N_IDX_HEADS):
            acc = acc + jnp.maximum(st[:, h * QB:(h + 1) * QB], 0.0) * wit[h:h + 1, :]
        ok = (c * KC + _iota((KC, QB), 0)) <= qpos_i
        s_scr[rows, :] = jnp.where(ok, acc, -inf)
        return (jnp.minimum(lohi[0], pmin(jnp.where(ok, acc, inf))),
                jnp.maximum(lohi[1], pmax(jnp.where(ok, acc, -inf))))

    lo8, hi8 = lax.fori_loop(0, nch, score_chunk, (full8(inf), full8(-inf)))
    lo, hi = fold_min(lo8), fold_max(hi8)

    def sweep(fn, init):
        return lax.fori_loop(0, nch, lambda c, car: fn(car, s_scr[chunk_rows(c), :]), init)

    def bisect(_, lh):
        lo, hi = lh
        mid = 0.5 * lo + 0.5 * hi
        cnt = fold_sum(sweep(lambda c, s: c + psum(jnp.where(s >= mid, 1.0, 0.0)), full8(0.0)))
        ge = cnt >= k_q
        return jnp.where(ge, mid, lo), jnp.where(ge, hi, mid)

    lo, hi = lax.fori_loop(0, N_BISECT, bisect, (lo, hi))
    v0 = fold_min(sweep(lambda c, s: jnp.minimum(c, pmin(jnp.where(s >= lo, s, inf))), full8(inf)))

    def refine(state):
        v, _, _ = state

        def f(car, s):
            g = s > v
            return car[0] + psum(jnp.where(g, 1.0, 0.0)), jnp.minimum(car[1], pmin(jnp.where(g, s, inf)))

        cnt8, v28 = sweep(f, (full8(0.0), full8(inf)))
        cnt, v2 = fold_sum(cnt8), fold_min(v28)
        ok = cnt < k_q
        return jnp.where(ok, v, v2), cnt, jnp.sum(jnp.where(ok, 0.0, 1.0))

    full = lambda val: jnp.full((1, QB), val, F32)
    thr, c_gt, _ = lax.while_loop(lambda st: st[2] > 0.0, refine, (v0, full(0.0), jnp.float32(1.0)))
    need = k_q - c_gt

    q = q_ref[0]
    G, R = N_KV_HEADS, N_Q_HEADS // N_KV_HEADS
    q_stack = [jnp.concatenate([q[:, (g * R + r) * HEAD_DIM:(g * R + r + 1) * HEAD_DIM] for r in range(R)], axis=0)
               for g in range(G)]
    ones_rows = jnp.where(_iota((HEAD_DIM, KC), 0) == 0, 1.0, 0.0).astype(BF16)
    m_scr[...] = jnp.full(m_scr.shape, -inf, F32)
    acc_scr[...] = jnp.zeros_like(acc_scr)

    n_ge = fold_sum(sweep(lambda c, s: c + psum(jnp.where(s >= thr, 1.0, 0.0)), full8(0.0)))
    any_ties = jnp.sum(jnp.where(n_ge == k_q, 0.0, 1.0)) > 0.0

    def make_attend(with_ties):
        def attend(c, tie_carry):
            rows = chunk_rows(c)
            s = s_scr[rows, :]
            if with_ties:
                eq = s == thr
                rank = tie_carry + _dot(tri_ref[...], jnp.where(eq, 1.0, 0.0).astype(BF16))
                sel = (s > thr) | (eq & (rank < need))
                tie_carry = tie_carry + fold_sum(psum(jnp.where(eq, 1.0, 0.0)))
            else:
                sel = s >= thr
            bias = jnp.where(sel, 0.0, -inf).astype(BF16)
            bias_r = jnp.concatenate([bias] * R, axis=1)
            kc = k_ref[0, rows, :]
            logits = [_dot_nt(kc[:, g * HEAD_DIM:(g + 1) * HEAD_DIM], q_stack[g]) for g in range(G)]
            probs, alphas = [], []
            for g in range(G):
                lm = logits[g].astype(BF16) + bias_r
                m_old = m_scr[g]
                m_part = jnp.max(lm.reshape(KC // 16, 16, R * QB), axis=0).astype(F32)
                m_new = jnp.maximum(m_old, jnp.max(m_part, axis=0, keepdims=True))
                m_safe = jnp.where(m_new == -inf, 0.0, m_new)
                alphas.append(jnp.exp2(m_old - m_safe))
                probs.append(jnp.exp2(lm - m_safe.astype(BF16)))
                m_scr[g] = m_new
            for g in range(G):
                vt_ext = jnp.concatenate([vt_ref[0, g * HEAD_DIM:(g + 1) * HEAD_DIM, rows], ones_rows], axis=0)
                acc_scr[g] = acc_scr[g] * alphas[g] + _dot(vt_ext, probs[g])
            return tie_carry
        return attend

    @pl.when(any_ties)
    def _():
        lax.fori_loop(0, nch, make_attend(True), full(0.0))

    @pl.when(jnp.logical_not(any_ties))
    def _():
        lax.fori_loop(0, nch, make_attend(False), full(0.0))

    out_t = jnp.concatenate([acc_scr[g][:HEAD_DIM] / acc_scr[g][HEAD_DIM:HEAD_DIM + 1] for g in range(G)], axis=0)
    per_r = [out_t[:, r * QB:(r + 1) * QB].T for r in range(R)]
    att = jnp.concatenate([per_r[r][:, g * HEAD_DIM:(g + 1) * HEAD_DIM] for g in range(G) for r in range(R)], axis=1)
    o_ref[0] = att * sg_ref[0]


def _attn_prompt(q, qi, wit, kbf, vt, kibf, sg):
    nb, t, _ = q.shape
    topk = min(TOPK_MAX, t // 4)
    kc = KEY_CHUNK
    tri = jnp.where(jnp.arange(kc)[None, :] < jnp.arange(kc)[:, None], 1.0, 0.0).astype(BF16)
    qblk = lambda width: pl.BlockSpec((1, Q_BLOCK, width), lambda b, j: (b, j, 0))
    whole = lambda r, c: pl.BlockSpec((1, r, c), lambda b, j: (b, 0, 0))
    return pl.pallas_call(
        functools.partial(_attn_prompt_kernel, topk=topk),
        grid=(nb, t // Q_BLOCK),
        in_specs=[qblk(ATT_WIDTH), qblk(N_IDX_HEADS * IDX_DIM),
                  pl.BlockSpec((1, N_IDX_HEADS, Q_BLOCK), lambda b, j: (b, 0, j)),
                  whole(t, KV_WIDTH), whole(KV_WIDTH, t), whole(t, IDX_DIM), qblk(ATT_WIDTH),
                  pl.BlockSpec((kc, kc), lambda b, j: (0, 0))],
        out_specs=qblk(ATT_WIDTH),
        out_shape=jax.ShapeDtypeStruct((nb, t, ATT_WIDTH), F32),
        scratch_shapes=[pltpu.VMEM((t, Q_BLOCK), F32),
                        pltpu.VMEM((N_KV_HEADS, 1, N_Q_HEADS // N_KV_HEADS * Q_BLOCK), F32),
                        pltpu.VMEM((N_KV_HEADS, 2 * HEAD_DIM, N_Q_HEADS // N_KV_HEADS * Q_BLOCK), F32)],
        compiler_params=_cparams(("arbitrary", "arbitrary")),
        name="attn_prompt",
    )(q, qi, wit, kbf, vt, kibf, sg, tri)


def _outproj_kernel(att_ref, rw_ref, w_ref, x_ref, g_ref, o_ref):
    mix = _dot(att_ref[0].astype(BF16), w_ref[:ATT_WIDTH, :]) + _dot(rw_ref[0].astype(BF16), w_ref[ATT_WIDTH:, :])
    o_ref[0] = x_ref[0] + g_ref[0] * mix


def _out_proj(att, rw, w_out_bf, x3, gate3, tm):
    nb, t, d = x3.shape
    if gate3.shape[1] == 1:
        gspec = pl.BlockSpec((1, 1, d), lambda b, i: (b, 0, 0))
    else:
        gspec = pl.BlockSpec((1, tm, d), lambda b, i: (b, i, 0))
    blk = lambda width: pl.BlockSpec((1, tm, width), lambda b, i: (b, i, 0))
    return pl.pallas_call(
        _outproj_kernel,
        grid=(nb, t // tm),
        in_specs=[blk(ATT_WIDTH), blk(RWKV_WIDTH), pl.BlockSpec(w_out_bf.shape, lambda b, i: (0, 0)), blk(d), gspec],
        out_specs=blk(d),
        out_shape=jax.ShapeDtypeStruct((nb, t, d), F32),
        compiler_params=_cparams(("arbitrary", "arbitrary")),
        name="out_proj",
    )(att, rw, w_out_bf, x3, gate3)


class _PageFetcher:
    def __init__(self, pt_ref, hbm, buf, sem):
        self.pt, self.hbm, self.buf, self.sem = pt_ref, hbm, buf, sem
        self.n_pages = buf.shape[1]
        b, nb = pl.program_id(0), pl.num_programs(0)
        self.first, self.last = b == 0, b == nb - 1
        self.slot = b % 2
        self.next_seq = jnp.minimum(b + 1, nb - 1)

    def _copy(self, seq, j, slot):
        return pltpu.make_async_copy(self.hbm.at[self.pt[seq, j]], self.buf.at[slot, j], self.sem.at[slot])

    def _wait(self, slot):
        pltpu.make_async_copy(self.hbm.at[pl.ds(0, self.n_pages)], self.buf.at[slot], self.sem.at[slot]).wait()

    def begin(self):
        @pl.when(self.first)
        def _():
            for j in range(self.n_pages):
                self._copy(0, j, 0).start()
        self._wait(self.slot)

    def prefetch(self, j):
        self._copy(self.next_seq, j, 1 - self.slot).start()

    def page(self, j):
        return self.buf[self.slot, j]

    def end(self):
        @pl.when(self.last)
        def _():
            self._wait(1 - self.slot)


def _dec_score_kernel(pt_ref, qi_ref, wi_ref, kidx_hbm, o_ref, kbuf, sem):
    pages = _PageFetcher(pt_ref, kidx_hbm, kbuf, sem)
    pages.begin()
    qi = qi_ref[0]
    qi16 = jnp.concatenate([qi, jnp.zeros_like(qi)], axis=0).astype(BF16)
    wi = wi_ref[0]
    for j in range(pages.n_pages):
        pages.prefetch(j)
        s = _dot(qi16, pages.page(j).astype(BF16))[:N_IDX_HEADS]
        o_ref[0, :, j * PAGE_SIZE:(j + 1) * PAGE_SIZE] = jnp.sum(jnp.maximum(s, 0.0) * wi, axis=0, keepdims=True)
    pages.end()


def _dec_scores(page_table, qi3, wi3, kidx_t):
    nb, n_pages = page_table.shape
    grid_spec = pltpu.PrefetchScalarGridSpec(
        num_scalar_prefetch=1,
        grid=(nb,),
        in_specs=[pl.BlockSpec((1, N_IDX_HEADS, IDX_DIM), lambda b, pt: (b, 0, 0)),
                  pl.BlockSpec((1, N_IDX_HEADS, 1), lambda b, pt: (b, 0, 0)),
                  pl.BlockSpec(memory_space=pl.ANY)],
        out_specs=pl.BlockSpec((1, 1, n_pages * PAGE_SIZE), lambda b, pt: (b, 0, 0)),
        scratch_shapes=[pltpu.VMEM((2, n_pages, IDX_DIM, PAGE_SIZE), F32), pltpu.SemaphoreType.DMA((2,))],
    )
    return pl.pallas_call(
        _dec_score_kernel,
        grid_spec=grid_spec,
        out_shape=jax.ShapeDtypeStruct((nb, 1, n_pages * PAGE_SIZE), F32),
        compiler_params=_cparams(("arbitrary",)),
        name="dec_scores",
    )(page_table, qi3, wi3, kidx_t)


def _dec_select_kernel(s_ref, qi_ref, ki_ref, wi_ref, sel_ref, sx, *, topk):
    nb, past = s_ref.shape
    inf = jnp.inf
    qi = qi_ref[...].astype(F32)
    ki = ki_ref[...].astype(F32)
    prod = qi * jnp.concatenate([ki] * N_IDX_HEADS, axis=1)
    nq = N_IDX_HEADS * IDX_DIM
    hsel = jnp.where(_iota((nq, LANES), 0) // IDX_DIM == _iota((nq, LANES), 1), 1.0, 0.0).astype(BF16)
    sh = _sum_dot(prod, hsel)[:, :N_IDX_HEADS]
    s_new = jnp.sum(jnp.maximum(sh, 0.0) * wi_ref[...], axis=1, keepdims=True)
    sx[:, :past] = s_ref[...]
    sx[:, past:] = jnp.where(_iota((nb, LANES), 1) == 0, s_new, -inf)

    rsum = lambda a: jnp.sum(a, axis=1, keepdims=True)
    k_q = float(topk)
    s = sx[...]
    lo = jnp.min(jnp.where(s > -inf, s, inf), axis=1, keepdims=True)
    hi = jnp.max(s, axis=1, keepdims=True)

    def bisect(_, lh):
        lo, hi = lh
        mid = 0.5 * lo + 0.5 * hi
        ge = rsum(jnp.where(sx[...] >= mid, 1.0, 0.0)) >= k_q
        return jnp.where(ge, mid, lo), jnp.where(ge, hi, mid)

    lo, hi = lax.fori_loop(0, N_BISECT, bisect, (lo, hi))
    s = sx[...]
    v0 = jnp.min(jnp.where(s >= lo, s, inf), axis=1, keepdims=True)

    def refine(state):
        v, _, _ = state
        s = sx[...]
        g = s > v
        cnt = rsum(jnp.where(g, 1.0, 0.0))
        v2 = jnp.min(jnp.where(g, s, inf), axis=1, keepdims=True)
        ok = cnt < k_q
        return jnp.where(ok, v, v2), cnt, jnp.sum(jnp.where(ok, 0.0, 1.0))

    thr, c_gt, _ = lax.while_loop(lambda st: st[2] > 0.0, refine,
                                  (v0, jnp.zeros((nb, 1), F32), jnp.float32(1.0)))
    need = k_q - c_gt
    upper = jnp.where(_iota((LANES, LANES), 0) < _iota((LANES, LANES), 1), 1.0, 0.0).astype(BF16)
    carry = jnp.zeros((nb, 1), F32)
    for c in range((past + LANES) // LANES):
        sc = sx[:, c * LANES:(c + 1) * LANES]
        eq = sc == thr
        eqf = jnp.where(eq, 1.0, 0.0)
        rank = carry + _dot(eqf.astype(BF16), upper)
        sel_ref[:, c * LANES:(c + 1) * LANES] = jnp.where((sc > thr) | (eq & (rank < need)), 1.0, 0.0)
        carry = carry + rsum(eqf)


def _dec_select(scores, qi, ki, wi, topk):
    nb, past = scores.shape
    args = [scores, qi, ki, wi]
    return pl.pallas_call(
        functools.partial(_dec_select_kernel, topk=topk),
        in_specs=[pl.BlockSpec(a.shape, lambda: (0, 0)) for a in args],
        out_specs=pl.BlockSpec((nb, past + LANES), lambda: (0, 0)),
        out_shape=jax.ShapeDtypeStruct((nb, past + LANES), F32),
        scratch_shapes=[pltpu.VMEM((nb, past + LANES), F32)],
        compiler_params=pltpu.CompilerParams(vmem_limit_bytes=VMEM_LIMIT),
        name="dec_select",
    )(*args)


def _dec_attn_kernel(pt_ref, q_ref, sel_ref, knew_ref, vnew_ref, sg_ref, k_hbm, v_hbm, o_ref,
                     kbuf, vbuf, ksem, vsem):
    k_pages = _PageFetcher(pt_ref, k_hbm, kbuf, ksem)
    v_pages = _PageFetcher(pt_ref, v_hbm, vbuf, vsem)
    n_pages = k_pages.n_pages
    past = n_pages * PAGE_SIZE
    inf = jnp.inf
    nh = N_Q_HEADS
    R = N_Q_HEADS // N_KV_HEADS

    q8 = q_ref[0].astype(F32)
    own = (_iota((nh, KV_WIDTH), 0) // R) == (_iota((nh, KV_WIDTH), 1) // HEAD_DIM)
    q_ext = jnp.where(own, jnp.concatenate([q8, q8], axis=1), 0.0)
    q16 = jnp.concatenate([q_ext, jnp.zeros_like(q_ext)], axis=0).astype(BF16)
    sel = sel_ref[0]

    k_pages.begin()
    parts = []
    for j in range(n_pages):
        k_pages.prefetch(j)
        parts.append(_dot(q16, k_pages.page(j).astype(BF16))[:nh])
    k_pages.end()
    k_new = knew_ref[0].astype(BF16).astype(F32)
    v_new = vnew_ref[0].astype(BF16).astype(F32)
    lg_new = jnp.sum(q16[:nh].astype(F32) * k_new, axis=1, keepdims=True)
    lm = jnp.where(sel[:, :past] > 0.5, jnp.concatenate(parts, axis=1), -inf)
    lm_new = jnp.where(sel[:, past:past + 1] > 0.5, lg_new, -inf)
    m = jnp.maximum(jnp.max(lm, axis=1, keepdims=True), lm_new)
    p = jnp.exp2(lm - m)
    p_new = jnp.exp2(lm_new - m)
    l = jnp.sum(p, axis=1, keepdims=True) + p_new
    p16 = jnp.concatenate([p, jnp.zeros_like(p)], axis=0).astype(BF16)

    v_pages.begin()
    acc = jnp.zeros((2 * nh, KV_WIDTH), F32)
    for j in range(n_pages):
        v_pages.prefetch(j)
        acc = acc + _dot_nt(p16[:, j * PAGE_SIZE:(j + 1) * PAGE_SIZE], v_pages.page(j).astype(BF16))
    v_pages.end()
    out = (acc[:nh] + p_new.astype(BF16).astype(F32) * v_new) / l
    res = jnp.where(_iota((nh, HEAD_DIM), 0) < R, out[:, :HEAD_DIM], out[:, HEAD_DIM:])
    o_ref[0] = res * sg_ref[0]


def _dec_attn(page_table, q3, sel3, k_new3, v_new3, sg3, cache_k2, cache_v2):
    nb, n_pages = page_table.shape
    per_seq = lambda r, c: pl.BlockSpec((1, r, c), lambda b, pt: (b, 0, 0))
    page_buf = pltpu.VMEM((2, n_pages, KV_WIDTH, PAGE_SIZE), F32)
    grid_spec = pltpu.PrefetchScalarGridSpec(
        num_scalar_prefetch=1,
        grid=(nb,),
        in_specs=[per_seq(N_Q_HEADS, HEAD_DIM), per_seq(1, sel3.shape[2]),
                  per_seq(1, KV_WIDTH), per_seq(1, KV_WIDTH), per_seq(N_Q_HEADS, HEAD_DIM),
                  pl.BlockSpec(memory_space=pl.ANY), pl.BlockSpec(memory_space=pl.ANY)],
        out_specs=per_seq(N_Q_HEADS, HEAD_DIM),
        scratch_shapes=[page_buf, page_buf, pltpu.SemaphoreType.DMA((2,)), pltpu.SemaphoreType.DMA((2,))],
    )
    return pl.pallas_call(
        _dec_attn_kernel,
        grid_spec=grid_spec,
        out_shape=jax.ShapeDtypeStruct((nb, N_Q_HEADS, HEAD_DIM), F32),
        compiler_params=_cparams(("arbitrary",)),
        name="dec_attn",
    )(page_table, q3, sel3, k_new3, v_new3, sg3, cache_k2, cache_v2)


def kernel(x_prompt, x_sample, cache_k, cache_v, cache_kidx, state_wkv, state_shift, page_table,
           c_prompt, c_sample, norm_w, w_ada, b_ada, w_in, q_norm_w, k_norm_w, mu_shift, w0, w_up,
           a0, a_up, k_k, k_a, r_k, ln_x_w, ln_x_b, w_out):
    nb, t, d = x_prompt.shape
    nd = x_sample.shape[0]
    assert x_sample.shape[1] == 1
    n_pages = page_table.shape[1]
    past = n_pages * PAGE_SIZE
    w = RWKV_WIDTH

    offs = [0]
    for sz in (ATT_WIDTH, KV_WIDTH, KV_WIDTH, N_IDX_HEADS * IDX_DIM, N_IDX_HEADS, IDX_DIM, ATT_WIDTH,
               w, w, w, LORA, LORA, w):
        offs.append(offs[-1] + sz)
    col = lambda i: w_in[:, offs[i]:offs[i + 1]]
    w_main = jnp.concatenate([col(0), col(1), col(2), col(3), col(6), col(7), col(8), col(9), col(12),
                              col(10), col(11), col(5), jnp.zeros((d, LANES - IDX_DIM), w_in.dtype)], axis=1).astype(BF16)
    zl = jnp.zeros((LORA, w), F32)
    wts = {
        "norm_w": norm_w.reshape(1, d), "w_main": w_main,
        "w_wit": col(4).T.astype(BF16), "w_vt": col(2).T.astype(BF16),
        "qnw": jnp.concatenate([q_norm_w, q_norm_w]).reshape(1, LANES),
        "knw": jnp.concatenate([k_norm_w, k_norm_w]).reshape(1, LANES),
    }
    rw_params = {
        "mu_rkv": mu_shift[:3 * w].reshape(1, 3 * w), "mu_lora": mu_shift[3 * w:].reshape(1, 2 * LORA),
        "w0": w0.reshape(1, w), "a0": a0.reshape(1, w),
        "w_lora": jnp.concatenate([jnp.concatenate([w_up, zl], axis=1),
                                   jnp.concatenate([zl, a_up], axis=1)], axis=0).astype(BF16),
        "k_k": k_k.reshape(1, w), "k_a": k_a.reshape(1, w), "r_k": r_k.reshape(1, w),
        "ln_w": ln_x_w.reshape(1, w), "ln_b": ln_x_b.reshape(1, w),
    }
    w_out_bf = w_out.astype(BF16)

    n_c = nb + nd
    pad = (-n_c) % 8
    c_all = jnp.concatenate([c_prompt, c_sample, jnp.zeros((pad, d), F32)], axis=0)
    mod = _adaln_mod(c_all, w_ada, b_ada)
    shift_p, scale_p, gate_p = (mod[:nb, s * d:(s + 1) * d].reshape(nb, 1, d) for s in range(3))
    shift_s, scale_s, gate_s = (mod[nb:n_c, s * d:(s + 1) * d].reshape(1, nd, d) for s in range(3))

    tm = min(512, t)
    pr = _in_proj(x_prompt, shift_p, scale_p, _rope_tables(jnp.arange(t)), wts, tm)
    att_p = _attn_prompt(pr["q"], pr["qi"], pr["wit"], pr["kbf"], pr["vt"], pr["kibf"], pr["sg"])
    rw_p, s_p = _rwkv_prompt(pr["rkv"], pr["lora"], pr["gr"], rw_params)
    y_prompt = _out_proj(att_p, rw_p, w_out_bf, x_prompt, gate_p, tm)
    shift_prompt = jnp.concatenate([pr["rkv"][:, t - 1:], pr["lora"][:, t - 1:]], axis=-1)

    xs3 = x_sample.reshape(1, nd, d)
    sm = _in_proj(xs3, shift_s, scale_s, _rope_tables(jnp.full((1,), past)), wts, nd)
    qi_s, ki_s = sm["qi"][0], sm["ki"][0]
    wi_s = sm["wit"][0].T
    n_phys = cache_k.shape[0]
    k_pages = jnp.transpose(cache_k, (0, 2, 3, 1)).reshape(n_phys, KV_WIDTH, PAGE_SIZE)
    v_pages = jnp.transpose(cache_v, (0, 2, 3, 1)).reshape(n_phys, KV_WIDTH, PAGE_SIZE)
    scores = _dec_scores(page_table, qi_s.reshape(nd, N_IDX_HEADS, IDX_DIM), wi_s.reshape(nd, N_IDX_HEADS, 1),
                         jnp.transpose(cache_kidx, (0, 2, 1)))
    topk = min(TOPK_MAX, (past + 1) // 4)
    sel = _dec_select(scores.reshape(nd, past), qi_s, sm["kibf"][0], wi_s, topk)
    att_s = _dec_attn(page_table, sm["q"][0].reshape(nd, N_Q_HEADS, HEAD_DIM), sel.reshape(nd, 1, past + LANES),
                      sm["k"][0].reshape(nd, 1, KV_WIDTH), sm["v"][0].reshape(nd, 1, KV_WIDTH),
                      sm["sg"][0].reshape(nd, N_Q_HEADS, HEAD_DIM),
                      k_pages, v_pages)
    prev = state_shift.reshape(nd, -1)
    vecs = _rwkv_dec_prep(sm["rkv"][0], sm["lora"][0], prev[:, :3 * w], prev[:, 3 * w:], rw_params)
    s_t, y_t = _rwkv_dec_state(vecs[:6], jnp.transpose(state_wkv, (1, 2, 3, 0)))
    s_s = jnp.transpose(s_t, (3, 0, 1, 2))
    rw_s = _rwkv_dec_epi(y_t.reshape(w, nd), vecs[6], sm["gr"][0], rw_params)
    y_sample = _out_proj(att_s.reshape(1, nd, ATT_WIDTH), rw_s.reshape(1, nd, w), w_out_bf, xs3, gate_s, nd)
    shift_sample = jnp.concatenate([sm["rkv"][0], sm["lora"][0]], axis=-1).reshape(nd, 1, 3 * w + 2 * LORA)

    return (y_prompt, y_sample.reshape(nd, 1, d),
            pr["k"].reshape(nb, t, N_KV_HEADS, HEAD_DIM), pr["v"].reshape(nb, t, N_KV_HEADS, HEAD_DIM), pr["ki"],
            s_p, shift_prompt,
            sm["k"][0].reshape(nd, 1, N_KV_HEADS, HEAD_DIM), sm["v"][0].reshape(nd, 1, N_KV_HEADS, HEAD_DIM),
            ki_s.reshape(nd, 1, IDX_DIM), s_s, shift_sample)
```

```python
import functools

import jax
import jax.numpy as jnp
from jax import lax
from jax.experimental import pallas as pl
from jax.experimental.pallas import tpu as pltpu

F32, BF16 = jnp.float32, jnp.bfloat16

HEAD_DIM = 64
N_Q_HEADS = 8
N_KV_HEADS = 2
N_IDX_HEADS = 8
IDX_DIM = 64
N_RWKV_HEADS = 8
ATT_WIDTH = N_Q_HEADS * HEAD_DIM
RWKV_WIDTH = N_RWKV_HEADS * HEAD_DIM
KV_WIDTH = N_KV_HEADS * HEAD_DIM
LORA = 64
TOPK_MAX = 256
ROPE_THETA = 500000.0
ROPE_DIMS = HEAD_DIM // 4
NORM_EPS = 1e-6
GN_EPS = 64e-5
PAGE_SIZE = 128
LANES = 128
Q_SCALE = HEAD_DIM ** -0.5 * 1.4426950408889634
VMEM_LIMIT = 56 * 1024 * 1024

Q_BLOCK = 128
KEY_CHUNK = 512
N_BISECT = 16
RW_CHUNK = 64
RW_BLOCK = 128


def _dot(a, b):
    return jnp.dot(a, b, preferred_element_type=F32)


def _dot_nt(a, b):
    return lax.dot_general(a, b, (((1,), (1,)), ((), ())), preferred_element_type=F32)


def _split(x):
    hi = x.astype(BF16)
    return hi, (x - hi.astype(F32)).astype(BF16)


def _sum_dot(x, sel01):
    hi, lo = _split(x)
    return _dot(hi, sel01) + _dot(lo, sel01)


def _sum_dot_left(sel01, x):
    hi, lo = _split(x)
    return _dot(sel01, hi) + _dot(sel01, lo)


def _dot_tn(a, b):
    return lax.dot_general(a, b, (((0,), (0,)), ((), ())), preferred_element_type=F32)


def _iota(shape, dim):
    return lax.broadcasted_iota(jnp.int32, shape, dim)


def _head_block_ones(n):
    return jnp.where(_iota((n, n), 0) // HEAD_DIM == _iota((n, n), 1) // HEAD_DIM, 1.0, 0.0).astype(BF16)


def _silu(x):
    return x * jax.nn.sigmoid(x)


def _cparams(sem):
    return pltpu.CompilerParams(dimension_semantics=sem, vmem_limit_bytes=VMEM_LIMIT)


def _mod_kernel(c_ref, w_ref, b_ref, o_ref):
    s = _silu(c_ref[...])
    o_ref[...] = _dot(s.astype(BF16), w_ref[...].astype(BF16)) + b_ref[...]


def _adaln_mod(c, w_ada, b_ada):
    rows, d = c.shape
    n = w_ada.shape[1]
    tn = 1024
    return pl.pallas_call(
        _mod_kernel,
        grid=(n // tn,),
        in_specs=[pl.BlockSpec((rows, d), lambda j: (0, 0)),
                  pl.BlockSpec((d, tn), lambda j: (0, j)),
                  pl.BlockSpec((1, tn), lambda j: (0, j))],
        out_specs=pl.BlockSpec((rows, tn), lambda j: (0, j)),
        out_shape=jax.ShapeDtypeStruct((rows, n), F32),
        compiler_params=_cparams(("arbitrary",)),
        name="adaln_mod",
    )(c, w_ada, b_ada.reshape(1, n))


_W_Q, _W_KV, _W_QI, _W_GA, _W_RKV, _W_GR, _W_TAIL, _W_END = 0, 512, 768, 1280, 1792, 3328, 3840, 4096


def _inproj_kernel(x_ref, shift_ref, scale_ref, nw_ref, w_ref, wwi_ref, wvt_ref, cos_ref, s1_ref, s2_ref,
                   qnw_ref, knw_ref,
                   q_o, k_o, kbf_o, v_o, vt_o, qi_o, ki_o, kibf_o, wit_o, sg_o, rkv_o, lora_o, gr_o):
    x = x_ref[0]
    ms = jnp.mean(x * x, axis=-1, keepdims=True)
    h = x * lax.rsqrt(ms + NORM_EPS) * nw_ref[...]
    h = h * (1.0 + scale_ref[0]) + shift_ref[0]
    hb = h.astype(BF16)
    cos, s1, s2 = cos_ref[...], s1_ref[...], s2_ref[...]
    bsum = _head_block_ones(LANES)

    def seg(a, b):
        return _dot(hb, w_ref[:, a:b])

    def rope(t):
        return t * cos + pltpu.roll(t, LANES - ROPE_DIMS // 2, 1) * s1 + pltpu.roll(t, ROPE_DIMS // 2, 1) * s2

    def head_norm(t, w):
        return t * lax.rsqrt(_sum_dot(t * t, bsum) * (1.0 / HEAD_DIM) + NORM_EPS) * w

    yq = seg(_W_Q, _W_KV)
    for s in range(ATT_WIDTH // LANES):
        t = rope(head_norm(yq[:, s * LANES:(s + 1) * LANES], qnw_ref[...]))
        q_o[0, :, s * LANES:(s + 1) * LANES] = (t * Q_SCALE).astype(BF16)

    ykv = seg(_W_KV, _W_QI)
    k = rope(head_norm(ykv[:, :KV_WIDTH], knw_ref[...]))
    k_o[0] = k
    kbf_o[0] = k.astype(BF16)
    v_o[0] = ykv[:, KV_WIDTH:]
    vt_o[0] = _dot_nt(wvt_ref[...], hb).astype(BF16)

    yqi = seg(_W_QI, _W_GA)
    for s in range(N_IDX_HEADS * IDX_DIM // LANES):
        qi_o[0, :, s * LANES:(s + 1) * LANES] = rope(yqi[:, s * LANES:(s + 1) * LANES]).astype(BF16)

    sg_o[0] = _silu(seg(_W_GA, _W_RKV))
    for s in range(3):
        rkv_o[0, :, s * RWKV_WIDTH:(s + 1) * RWKV_WIDTH] = seg(_W_RKV + s * RWKV_WIDTH, _W_RKV + (s + 1) * RWKV_WIDTH)
    gr_o[0] = seg(_W_GR, _W_TAIL)

    yt = seg(_W_TAIL, _W_END)
    lora_o[0] = yt[:, :2 * LORA]
    ki = rope(yt[:, 2 * LORA:])[:, :IDX_DIM]
    ki_o[0] = ki
    kibf_o[0] = ki.astype(BF16)
    wit_o[0] = _dot_nt(wwi_ref[...], hb) * (N_IDX_HEADS ** -0.5 * IDX_DIM ** -0.5)


def _in_proj(x3, shift3, scale3, tabs, wts, tm):
    nb, t, d = x3.shape
    mod_rows = shift3.shape[1]
    tab_rows = tabs[0].shape[0]
    grid = (nb, t // tm)
    if mod_rows == 1:
        mod_spec = pl.BlockSpec((1, 1, d), lambda b, i: (b, 0, 0))
    else:
        mod_spec = pl.BlockSpec((1, tm, d), lambda b, i: (b, i, 0))
    if tab_rows == 1:
        tab_spec = pl.BlockSpec((1, LANES), lambda b, i: (0, 0))
    else:
        tab_spec = pl.BlockSpec((tm, LANES), lambda b, i: (i, 0))
    const = lambda shape: pl.BlockSpec(shape, lambda b, i: tuple(0 for _ in shape))
    row = lambda width: pl.BlockSpec((1, tm, width), lambda b, i: (b, i, 0))
    col = lambda rows: pl.BlockSpec((1, rows, tm), lambda b, i: (b, 0, i))
    out_defs = [
        ("q", (t, ATT_WIDTH), BF16, row(ATT_WIDTH)),
        ("k", (t, KV_WIDTH), F32, row(KV_WIDTH)),
        ("kbf", (t, KV_WIDTH), BF16, row(KV_WIDTH)),
        ("v", (t, KV_WIDTH), F32, row(KV_WIDTH)),
        ("vt", (KV_WIDTH, t), BF16, col(KV_WIDTH)),
        ("qi", (t, N_IDX_HEADS * IDX_DIM), BF16, row(N_IDX_HEADS * IDX_DIM)),
        ("ki", (t, IDX_DIM), F32, row(IDX_DIM)),
        ("kibf", (t, IDX_DIM), BF16, row(IDX_DIM)),
        ("wit", (N_IDX_HEADS, t), F32, col(N_IDX_HEADS)),
        ("sg", (t, ATT_WIDTH), F32, row(ATT_WIDTH)),
        ("rkv", (t, 3 * RWKV_WIDTH), F32, row(3 * RWKV_WIDTH)),
        ("lora", (t, 2 * LORA), F32, row(2 * LORA)),
        ("gr", (t, RWKV_WIDTH), F32, row(RWKV_WIDTH)),
    ]
    outs = pl.pallas_call(
        _inproj_kernel,
        grid=grid,
        in_specs=[pl.BlockSpec((1, tm, d), lambda b, i: (b, i, 0)), mod_spec, mod_spec, const((1, d)),
                  const(wts["w_main"].shape), const(wts["w_wit"].shape), const(wts["w_vt"].shape),
                  tab_spec, tab_spec, tab_spec, const((1, LANES)), const((1, LANES))],
        out_specs=[o[3] for o in out_defs],
        out_shape=[jax.ShapeDtypeStruct((nb,) + o[1], o[2]) for o in out_defs],
        compiler_params=_cparams(("arbitrary", "arbitrary")),
        name="in_proj",
    )(x3, shift3, scale3, wts["norm_w"], wts["w_main"], wts["w_wit"], wts["w_vt"], tabs[0], tabs[1], tabs[2],
      wts["qnw"], wts["knw"])
    return {o[0]: arr for o, arr in zip(out_defs, outs)}


def _rope_tables(pos):
    half = ROPE_DIMS // 2
    inv = jnp.power(ROPE_THETA, -jnp.arange(half, dtype=F32) / half)
    ang = pos.astype(F32)[:, None] * inv[None, :]
    cos, sin = jnp.cos(ang), jnp.sin(ang)
    n = pos.shape[0]
    pad = jnp.zeros((n, HEAD_DIM - ROPE_DIMS), F32)
    zero = jnp.zeros((n, half), F32)
    c_head = jnp.concatenate([cos, cos, pad + 1.0], axis=1)
    s1_head = jnp.concatenate([-sin, zero, pad], axis=1)
    s2_head = jnp.concatenate([zero, sin, pad], axis=1)
    two = lambda a: jnp.concatenate([a, a], axis=1)
    return two(c_head), two(s1_head), two(s2_head)


def _head_sum(t):
    ones = _head_block_ones(LANES)
    parts = [_sum_dot(t[:, s * LANES:(s + 1) * LANES], ones) for s in range(t.shape[1] // LANES)]
    return parts[0] if len(parts) == 1 else jnp.concatenate(parts, axis=1)


def _softplus(z):
    return jnp.maximum(z, 0.0) + jnp.log(1.0 + jnp.exp(-jnp.abs(z)))


def _rwkv_prep(xs, xl, prev, prevl, p):
    xm = xs + p["mu_rkv"] * (prev - xs)
    xml = xl + p["mu_lora"] * (prevl - xl)
    w = RWKV_WIDTH
    r, k, v = xm[:, :w], xm[:, w:2 * w], xm[:, 2 * w:]
    lane = _iota(xml.shape, 1)
    lhs = jnp.where(lane < LORA, jnp.tanh(xml), xml)
    lo = _dot(lhs.astype(BF16), p["w_lora"])
    wlog = -_softplus(-(p["w0"] + lo[:, :w])) - 0.5
    logw = -jnp.exp(wlog)
    a = jax.nn.sigmoid(p["a0"] + lo[:, w:])
    kkr = k * p["k_k"]
    kk = kkr / jnp.maximum(jnp.sqrt(_head_sum(kkr * kkr)), 1e-12)
    k_mod = k * (1.0 + (a - 1.0) * p["k_a"])
    bonus = _head_sum(r * k_mod * p["r_k"]) * v
    return r, logw, k_mod, v, -kk, kk * a, bonus


def _rwkv_epilogue(y, bonus, gate, p):
    mean = _head_sum(y) * (1.0 / HEAD_DIM)
    d = y - mean
    var = _head_sum(d * d) * (1.0 / HEAD_DIM)
    yn = d * lax.rsqrt(var + GN_EPS) * p["ln_w"] + p["ln_b"]
    return (yn + bonus) * _silu(gate)


_RW_PARAM_NAMES = ("mu_rkv", "mu_lora", "w0", "a0", "w_lora", "k_k", "k_a", "r_k", "ln_w", "ln_b")


def _load_params(refs):
    return {n: r[...] for n, r in zip(_RW_PARAM_NAMES, refs)}


def _rwkv_prompt_kernel(rkv_ref, lora_ref, gr_ref, *rest):
    nparam = len(_RW_PARAM_NAMES)
    p = _load_params(rest[:nparam])
    out_ref, sfin_ref = rest[nparam:nparam + 2]
    (st_s, c_rkv, c_lora, r_s, lw_s, k_s, v_s, al_s, be_s, y_s, bon_s,
     at_s, rt_s, bk_s, ec_s, wl_s, wh_s, zv_s, inv_s) = rest[nparam + 2:]
    i = pl.program_id(0)
    nb, tb = rkv_ref.shape[0], rkv_ref.shape[1]
    C = RW_CHUNK
    W = 2 * C

    @pl.when(i == 0)
    def _():
        st_s[...] = jnp.zeros_like(st_s)
        c_rkv[...] = jnp.zeros_like(c_rkv)
        c_lora[...] = jnp.zeros_like(c_lora)

    row = _iota((tb, 1), 0)
    for b in range(nb):
        xs, xl = rkv_ref[b], lora_ref[b]
        prev = jnp.where(row == 0, c_rkv[b], pltpu.roll(xs, 1, 0))
        prevl = jnp.where(row == 0, c_lora[b], pltpu.roll(xl, 1, 0))
        c_rkv[b] = xs[tb - 1:tb, :]
        c_lora[b] = xl[tb - 1:tb, :]
        r_s[b], lw_s[b], k_s[b], v_s[b], al_s[b], be_s[b], bon_s[b] = _rwkv_prep(xs, xl, prev, prevl, p)

    rm, cn = _iota((W, W), 0), _iota((W, W), 1)
    tt, ss = rm % C, cn % C
    top, left = rm < C, cn < C
    strict, incl = ss < tt, ss <= tt
    mk_l_lo, mk_l_hi = top & left & strict, (~top) & (~left) & strict
    mk_z_lo, mk_z_hi = top & (~left) & strict, (~top) & left & strict
    mk_w = (_iota((C, W), 1) % C) <= _iota((C, W), 0)
    bd = (rm // HEAD_DIM) == (cn // HEAD_DIM)
    eye = jnp.where(rm == cn, 1.0, 0.0).astype(F32)
    tril = jnp.where(_iota((C, C), 1) <= _iota((C, C), 0), 1.0, 0.0).astype(BF16)
    lane_lo = _iota((C, LANES), 1) < HEAD_DIM
    m_lo = jnp.where(lane_lo, 1.0, 0.0).astype(F32)
    m_hi = 1.0 - m_lo

    bf = lambda a: a.astype(BF16)
    pairs = range(N_RWKV_HEADS // 2)
    lanes_of = lambda pr: slice(pr * LANES, (pr + 1) * LANES)
    rows_of = lambda c: pl.ds(pl.multiple_of(c * C, C), C)
    items = [(b, pr) for b in range(nb) for pr in pairs]

    def phase_a(c, carry):
        rows = rows_of(c)
        cums = []
        for b in range(nb):
            cum_all = _sum_dot_left(tril, lw_s[b, rows, :])
            cums += [cum_all[:, lanes_of(pr)] for pr in pairs]
        g_lo, g_hi = [], []
        for n, (b, pr) in enumerate(items):
            sl = lanes_of(pr)
            cum = cums[n]
            cum_c = cum[C - 1:C, :]
            e_neg, e_rel = jnp.exp(-cum), jnp.exp(cum_c - cum)
            rr, kk, aa, bb = r_s[b, rows, sl], k_s[b, rows, sl], al_s[b, rows, sl], be_s[b, rows, sl]
            a_t, r_t = aa * jnp.exp(cum - lw_s[b, rows, sl]), rr * jnp.exp(cum)
            b_t, k_t = bb * e_neg, kk * e_neg
            at_s[b, c, pr], rt_s[b, c, pr] = bf(a_t), bf(r_t)
            bk_s[b, c, pr] = jnp.concatenate([bb * e_rel, kk * e_rel], axis=0)
            ec_s[b, c, pr] = jnp.exp(cum_c)
            xk = bf(jnp.concatenate([b_t, k_t], axis=0))
            kx = bf(jnp.concatenate([k_t, b_t], axis=0))
            g_lo.append(_dot_nt(bf(jnp.concatenate([a_t * m_lo, r_t * m_lo], axis=0)), xk))
            g_hi.append(_dot_nt(bf(jnp.concatenate([r_t * m_hi, a_t * m_hi], axis=0)), kx))
        l_mat = []
        for n, (b, pr) in enumerate(items):
            l_mat.append(jnp.where(mk_l_lo, g_lo[n], 0.0) + jnp.where(mk_l_hi, g_hi[n], 0.0))
            z_mat = jnp.where(mk_z_lo, g_lo[n], 0.0) + jnp.where(mk_z_hi, g_hi[n], 0.0)
            wl_s[b, c, pr] = bf(jnp.where(mk_w, g_lo[n][C:, :], 0.0))
            wh_s[b, c, pr] = bf(jnp.where(mk_w, g_hi[n][:C, :], 0.0))
            vv_b = bf(v_s[b, rows, lanes_of(pr)])
            zv_s[b, c, pr] = _dot(bf(z_mat), jnp.concatenate([vv_b, vv_b], axis=0))
        inv = [eye + m for m in l_mat]
        pw = l_mat
        for _ in range(5):
            pw = [_dot(bf(m), bf(m)) for m in pw]
            inv = [x + _dot(bf(x), bf(m)) for x, m in zip(inv, pw)]
        for n, (b, pr) in enumerate(items):
            inv_s[b, c, pr] = bf(inv[n])
        return carry

    lax.fori_loop(0, tb // C, phase_a, 0)

    def phase_b(c, carry):
        rows = rows_of(c)
        nit = range(len(items))
        st = [st_s[b, pr] for b, pr in items]
        st_b = [bf(s) for s in st]
        vv = [v_s[b, rows, lanes_of(pr)] for b, pr in items]
        a_s = [_dot_nt(at_s[b, c, pr], st_b[n]) for n, (b, pr) in enumerate(items)]
        u_st = [_dot(inv_s[b, c, pr], bf(zv_s[b, c, pr] + jnp.concatenate([a_s[n], a_s[n]], axis=0)))
                for n, (b, pr) in enumerate(items)]
        u = [jnp.where(lane_lo, x[:C], x[C:]) for x in u_st]
        uv = [jnp.concatenate([u[n], vv[n]], axis=0) for n in nit]
        upd = [_dot_tn(uv[n], bk_s[b, c, pr]) for n, (b, pr) in enumerate(items)]
        for n, (b, pr) in enumerate(items):
            st_s[b, pr] = st[n] * ec_s[b, c, pr] + jnp.where(bd, upd[n], 0.0)
        for n, (b, pr) in enumerate(items):
            uv_b = bf(uv[n])
            vu_b = jnp.concatenate([uv_b[C:], uv_b[:C]], axis=0)
            y_in = jnp.where(lane_lo, _dot(wl_s[b, c, pr], uv_b), _dot(wh_s[b, c, pr], vu_b))
            y_s[b, rows, lanes_of(pr)] = _dot_nt(rt_s[b, c, pr], st_b[n]) + y_in
        return carry

    lax.fori_loop(0, tb // C, phase_b, 0)
    for b in range(nb):
        out_ref[b] = _rwkv_epilogue(y_s[b], bon_s[b], gr_ref[b], p)

    @pl.when(i == pl.num_programs(0) - 1)
    def _():
        for b in range(nb):
            for hd in range(N_RWKV_HEADS):
                o = (hd % 2) * HEAD_DIM
                sfin_ref[b, hd] = st_s[b, hd // 2][o:o + HEAD_DIM, o:o + HEAD_DIM]


def _rwkv_prompt(rkv, lora, gr, params):
    nb, t, _ = rkv.shape
    tb = min(RW_BLOCK, t)
    w = RWKV_WIDTH
    blk = lambda width: pl.BlockSpec((nb, tb, width), lambda i: (0, i, 0))
    pspecs = [pl.BlockSpec(params[n].shape, lambda i: (0, 0)) for n in _RW_PARAM_NAMES]
    vec = lambda: pltpu.VMEM((nb, tb, w), F32)
    nch, npair, c = tb // RW_CHUNK, N_RWKV_HEADS // 2, RW_CHUNK
    per_chunk = lambda rows, dt: pltpu.VMEM((nb, nch, npair, rows, LANES), dt)
    state_shape = (nb, N_RWKV_HEADS, HEAD_DIM, HEAD_DIM)
    return pl.pallas_call(
        _rwkv_prompt_kernel,
        grid=(t // tb,),
        in_specs=[blk(3 * w), blk(2 * LORA), blk(w)] + pspecs,
        out_specs=[blk(w), pl.BlockSpec(state_shape, lambda i: (0, 0, 0, 0))],
        out_shape=[jax.ShapeDtypeStruct((nb, t, w), F32), jax.ShapeDtypeStruct(state_shape, F32)],
        scratch_shapes=[pltpu.VMEM((nb, npair, LANES, LANES), F32),
                        pltpu.VMEM((nb, 1, 3 * w), F32), pltpu.VMEM((nb, 1, 2 * LORA), F32),
                        vec(), vec(), vec(), vec(), vec(), vec(), vec(), vec(),
                        per_chunk(c, BF16), per_chunk(c, BF16), per_chunk(2 * c, F32), per_chunk(1, F32),
                        per_chunk(c, BF16), per_chunk(c, BF16), per_chunk(2 * c, F32), per_chunk(2 * c, BF16)],
        compiler_params=_cparams(("arbitrary",)),
        name="rwkv_prompt",
    )(rkv, lora, gr, *[params[n] for n in _RW_PARAM_NAMES])


def _rwkv_dec_prep_kernel(rkv_ref, lora_ref, prev_ref, prevl_ref, *rest):
    nparam = len(_RW_PARAM_NAMES)
    p = _load_params(rest[:nparam])
    outs = rest[nparam:]
    r, logw, k_mod, v, al, be, bonus = _rwkv_prep(rkv_ref[...], lora_ref[...], prev_ref[...], prevl_ref[...], p)
    for o, val in zip(outs[:6], (r, jnp.exp(logw), k_mod, v, al, be)):
        o[...] = val.T
    outs[6][...] = bonus


def _rwkv_dec_prep(rkv, lora, prev, prevl, params):
    n = rkv.shape[0]
    full = lambda a: pl.BlockSpec(a.shape, lambda: tuple(0 for _ in a.shape))
    args = [rkv, lora, prev, prevl] + [params[k] for k in _RW_PARAM_NAMES]
    t_shape, shape = (RWKV_WIDTH, n), (n, RWKV_WIDTH)
    return pl.pallas_call(
        _rwkv_dec_prep_kernel,
        in_specs=[full(a) for a in args],
        out_specs=[pl.BlockSpec(t_shape, lambda: (0, 0))] * 6 + [pl.BlockSpec(shape, lambda: (0, 0))],
        out_shape=[jax.ShapeDtypeStruct(t_shape, F32)] * 6 + [jax.ShapeDtypeStruct(shape, F32)],
        compiler_params=pltpu.CompilerParams(vmem_limit_bytes=VMEM_LIMIT),
        name="rwkv_dec_prep",
    )(*args)


def _rwkv_dec_state_kernel(r_ref, w_ref, k_ref, v_ref, a_ref, b_ref, s_ref, so_ref, y_ref):
    r, w, k, a, b = r_ref[0], w_ref[0], k_ref[0], a_ref[0], b_ref[0]

    def one(i, carry):
        st = s_ref[0, i]
        sa = jnp.sum(st * a, axis=0, keepdims=True)
        sn = st * w + sa * b + v_ref[0, pl.ds(i, 1), :] * k
        so_ref[0, i] = sn
        y_ref[0, pl.ds(i, 1), :] = jnp.sum(sn * r, axis=0, keepdims=True)
        return carry

    lax.fori_loop(0, HEAD_DIM, one, 0)


def _rwkv_dec_state(vecs_t, state_t):
    nh, n, _, nb = state_t.shape
    vspec = pl.BlockSpec((1, n, nb), lambda h: (h, 0, 0))
    sspec = pl.BlockSpec((1, n, n, nb), lambda h: (h, 0, 0, 0))
    return pl.pallas_call(
        _rwkv_dec_state_kernel,
        grid=(nh,),
        in_specs=[vspec] * 6 + [sspec],
        out_specs=[sspec, vspec],
        out_shape=[jax.ShapeDtypeStruct(state_t.shape, F32), jax.ShapeDtypeStruct((nh, n, nb), F32)],
        compiler_params=_cparams(("arbitrary",)),
        name="rwkv_dec_state",
    )(*[a.reshape(nh, n, nb) for a in vecs_t], state_t)


def _rwkv_dec_epi_kernel(yt_ref, bonus_ref, gr_ref, lnw_ref, lnb_ref, o_ref):
    o_ref[...] = _rwkv_epilogue(yt_ref[...].T, bonus_ref[...], gr_ref[...],
                                {"ln_w": lnw_ref[...], "ln_b": lnb_ref[...]})


def _rwkv_dec_epi(y_t, bonus, gr, params):
    n = bonus.shape[0]
    args = [y_t, bonus, gr, params["ln_w"], params["ln_b"]]
    return pl.pallas_call(
        _rwkv_dec_epi_kernel,
        in_specs=[pl.BlockSpec(a.shape, lambda: (0, 0)) for a in args],
        out_specs=pl.BlockSpec((n, RWKV_WIDTH), lambda: (0, 0)),
        out_shape=jax.ShapeDtypeStruct((n, RWKV_WIDTH), F32),
        name="rwkv_dec_epi",
    )(*args)


def _attn_prompt_kernel(q_ref, qi_ref, wit_ref, k_ref, vt_ref, kidx_ref, sg_ref, tri_ref, o_ref,
                        s_scr, m_scr, acc_scr, *, topk):
    j = pl.program_id(1)
    QB, KC = Q_BLOCK, KEY_CHUNK
    nch = (j * QB) // KC + 1
    inf = jnp.inf
    qpos_i = j * QB + _iota((1, QB), 1)
    k_q = jnp.minimum(float(topk), qpos_i.astype(F32) + 1.0)
    wit = wit_ref[0]
    qi = qi_ref[0]
    qi_stack = jnp.concatenate([qi[:, h * IDX_DIM:(h + 1) * IDX_DIM] for h in range(N_IDX_HEADS)], axis=0)

    def chunk_rows(c):
        return pl.ds(pl.multiple_of(c * KC, KC), KC)

    SUB = 32
    part = lambda a: a.reshape(KC // SUB, SUB, QB)
    psum = lambda a: jnp.sum(part(a), axis=0)
    pmin = lambda a: jnp.min(part(a), axis=0)
    pmax = lambda a: jnp.max(part(a), axis=0)
    full8 = lambda val: jnp.full((SUB, QB), val, F32)
    fold_sum = lambda a: jnp.sum(a, axis=0, keepdims=True)
    fold_min = lambda a: jnp.min(a, axis=0, keepdims=True)
    fold_max = lambda a: jnp.max(a, axis=0, keepdims=True)

    def score_chunk(c, lohi):
        rows = chunk_rows(c)
        st = _dot_nt(kidx_ref[0, rows, :], qi_stack)
        acc = jnp.zeros((KC, QB), F32)
        for h in range(N_IDX_HEADS):
            acc = acc + jnp.maximum(st[:, h * QB:(h + 1) * QB], 0.0) * wit[h:h + 1, :]
        ok = (c * KC + _iota((KC, QB), 0)) <= qpos_i
        s_scr[rows, :] = jnp.where(ok, acc, -inf)
        return (jnp.minimum(lohi[0], pmin(jnp.where(ok, acc, inf))),
                jnp.maximum(lohi[1], pmax(jnp.where(ok, acc, -inf))))

    lo8, hi8 = lax.fori_loop(0, nch, score_chunk, (full8(inf), full8(-inf)))
    lo, hi = fold_min(lo8), fold_max(hi8)

    def sweep(fn, init):
        return lax.fori_loop(0, nch, lambda c, car: fn(car, s_scr[chunk_rows(c), :]), init)

    def bisect(_, lh):
        lo, hi = lh
        mid = 0.5 * lo + 0.5 * hi
        cnt = fold_sum(sweep(lambda c, s: c + psum(jnp.where(s >= mid, 1.0, 0.0)), full8(0.0)))
        ge = cnt >= k_q
        return jnp.where(ge, mid, lo), jnp.where(ge, hi, mid)

    lo, hi = lax.fori_loop(0, N_BISECT, bisect, (lo, hi))
    v0 = fold_min(sweep(lambda c, s: jnp.minimum(c, pmin(jnp.where(s >= lo, s, inf))), full8(inf)))

    def refine(state):
        v, _, _ = state

        def f(car, s):
            g = s > v
            return car[0] + psum(jnp.where(g, 1.0, 0.0)), jnp.minimum(car[1], pmin(jnp.where(g, s, inf)))

        cnt8, v28 = sweep(f, (full8(0.0), full8(inf)))
        cnt, v2 = fold_sum(cnt8), fold_min(v28)
        ok = cnt < k_q
        return jnp.where(ok, v, v2), cnt, jnp.sum(jnp.where(ok, 0.0, 1.0))

    full = lambda val: jnp.full((1, QB), val, F32)
    thr, c_gt, _ = lax.while_loop(lambda st: st[2] > 0.0, refine, (v0, full(0.0), jnp.float32(1.0)))
    need = k_q - c_gt

    q = q_ref[0]
    G, R = N_KV_HEADS, N_Q_HEADS // N_KV_HEADS
    q_stack = [jnp.concatenate([q[:, (g * R + r) * HEAD_DIM:(g * R + r + 1) * HEAD_DIM] for r in range(R)], axis=0)
               for g in range(G)]
    ones_rows = jnp.where(_iota((HEAD_DIM, KC), 0) == 0, 1.0, 0.0).astype(BF16)
    m_scr[...] = jnp.full(m_scr.shape, -inf, F32)
    acc_scr[...] = jnp.zeros_like(acc_scr)

    n_ge = fold_sum(sweep(lambda c, s: c + psum(jnp.where(s >= thr, 1.0, 0.0)), full8(0.0)))
    any_ties = jnp.sum(jnp.where(n_ge == k_q, 0.0, 1.0)) > 0.0

    def make_attend(with_ties):
        def attend(c, tie_carry):
            rows = chunk_rows(c)
            s = s_scr[rows, :]
            if with_ties:
                eq = s == thr
                rank = tie_carry + _dot(tri_ref[...], jnp.where(eq, 1.0, 0.0).astype(BF16))
                sel = (s > thr) | (eq & (rank < need))
                tie_carry = tie_carry + fold_sum(psum(jnp.where(eq, 1.0, 0.0)))
            else:
                sel = s >= thr
            bias = jnp.where(sel, 0.0, -inf).astype(BF16)
            bias_r = jnp.concatenate([bias] * R, axis=1)
            kc = k_ref[0, rows, :]
            logits = [_dot_nt(kc[:, g * HEAD_DIM:(g + 1) * HEAD_DIM], q_stack[g]) for g in range(G)]
            probs, alphas = [], []
            for g in range(G):
                lm = logits[g].astype(BF16) + bias_r
                m_old = m_scr[g]
                m_part = jnp.max(lm.reshape(KC // 16, 16, R * QB), axis=0).astype(F32)
                m_new = jnp.maximum(m_old, jnp.max(m_part, axis=0, keepdims=True))
                m_safe = jnp.where(m_new == -inf, 0.0, m_new)
                alphas.append(jnp.exp2(m_old - m_safe))
                probs.append(jnp.exp2(lm - m_safe.astype(BF16)))
                m_scr[g] = m_new
            for g in range(G):
                vt_ext = jnp.concatenate([vt_ref[0, g * HEAD_DIM:(g + 1) * HEAD_DIM, rows], ones_rows], axis=0)
                acc_scr[g] = acc_scr[g] * alphas[g] + _dot(vt_ext, probs[g])
            return tie_carry
        return attend

    @pl.when(any_ties)
    def _():
        lax.fori_loop(0, nch, make_attend(True), full(0.0))

    @pl.when(jnp.logical_not(any_ties))
    def _():
        lax.fori_loop(0, nch, make_attend(False), full(0.0))

    out_t = jnp.concatenate([acc_scr[g][:HEAD_DIM] / acc_scr[g][HEAD_DIM:HEAD_DIM + 1] for g in range(G)], axis=0)
    per_r = [out_t[:, r * QB:(r + 1) * QB].T for r in range(R)]
    att = jnp.concatenate([per_r[r][:, g * HEAD_DIM:(g + 1) * HEAD_DIM] for g in range(G) for r in range(R)], axis=1)
    o_ref[0] = att * sg_ref[0]


def _attn_prompt(q, qi, wit, kbf, vt, kibf, sg):
    nb, t, _ = q.shape
    topk = min(TOPK_MAX, t // 4)
    kc = KEY_CHUNK
    tri = jnp.where(jnp.arange(kc)[None, :] < jnp.arange(kc)[:, None], 1.0, 0.0).astype(BF16)
    qblk = lambda width: pl.BlockSpec((1, Q_BLOCK, width), lambda b, j: (b, j, 0))
    whole = lambda r, c: pl.BlockSpec((1, r, c), lambda b, j: (b, 0, 0))
    return pl.pallas_call(
        functools.partial(_attn_prompt_kernel, topk=topk),
        grid=(nb, t // Q_BLOCK),
        in_specs=[qblk(ATT_WIDTH), qblk(N_IDX_HEADS * IDX_DIM),
                  pl.BlockSpec((1, N_IDX_HEADS, Q_BLOCK), lambda b, j: (b, 0, j)),
                  whole(t, KV_WIDTH), whole(KV_WIDTH, t), whole(t, IDX_DIM), qblk(ATT_WIDTH),
                  pl.BlockSpec((kc, kc), lambda b, j: (0, 0))],
        out_specs=qblk(ATT_WIDTH),
        out_shape=jax.ShapeDtypeStruct((nb, t, ATT_WIDTH), F32),
        scratch_shapes=[pltpu.VMEM((t, Q_BLOCK), F32),
                        pltpu.VMEM((N_KV_HEADS, 1, N_Q_HEADS // N_KV_HEADS * Q_BLOCK), F32),
                        pltpu.VMEM((N_KV_HEADS, 2 * HEAD_DIM, N_Q_HEADS // N_KV_HEADS * Q_BLOCK), F32)],
        compiler_params=_cparams(("arbitrary", "arbitrary")),
        name="attn_prompt",
    )(q, qi, wit, kbf, vt, kibf, sg, tri)


def _outproj_kernel(att_ref, rw_ref, w_ref, x_ref, g_ref, o_ref):
    mix = _dot(att_ref[0].astype(BF16), w_ref[:ATT_WIDTH, :]) + _dot(rw_ref[0].astype(BF16), w_ref[ATT_WIDTH:, :])
    o_ref[0] = x_ref[0] + g_ref[0] * mix


def _out_proj(att, rw, w_out_bf, x3, gate3, tm):
    nb, t, d = x3.shape
    if gate3.shape[1] == 1:
        gspec = pl.BlockSpec((1, 1, d), lambda b, i: (b, 0, 0))
    else:
        gspec = pl.BlockSpec((1, tm, d), lambda b, i: (b, i, 0))
    blk = lambda width: pl.BlockSpec((1, tm, width), lambda b, i: (b, i, 0))
    return pl.pallas_call(
        _outproj_kernel,
        grid=(nb, t // tm),
        in_specs=[blk(ATT_WIDTH), blk(RWKV_WIDTH), pl.BlockSpec(w_out_bf.shape, lambda b, i: (0, 0)), blk(d), gspec],
        out_specs=blk(d),
        out_shape=jax.ShapeDtypeStruct((nb, t, d), F32),
        compiler_params=_cparams(("arbitrary", "arbitrary")),
        name="out_proj",
    )(att, rw, w_out_bf, x3, gate3)


class _PageFetcher:
    def __init__(self, pt_ref, hbm, buf, sem):
        self.pt, self.hbm, self.buf, self.sem = pt_ref, hbm, buf, sem
        self.n_pages = buf.shape[1]
        b, nb = pl.program_id(0), pl.num_programs(0)
        self.first, self.last = b == 0, b == nb - 1
        self.slot = b % 2
        self.next_seq = jnp.minimum(b + 1, nb - 1)

    def _copy(self, seq, j, slot):
        return pltpu.make_async_copy(self.hbm.at[self.pt[seq, j]], self.buf.at[slot, j], self.sem.at[slot])

    def _wait(self, slot):
        pltpu.make_async_copy(self.hbm.at[pl.ds(0, self.n_pages)], self.buf.at[slot], self.sem.at[slot]).wait()

    def begin(self):
        @pl.when(self.first)
        def _():
            for j in range(self.n_pages):
                self._copy(0, j, 0).start()
        self._wait(self.slot)

    def prefetch(self, j):
        self._copy(self.next_seq, j, 1 - self.slot).start(priority=j % 2)

    def page(self, j):
        return self.buf[self.slot, j]

    def end(self):
        @pl.when(self.last)
        def _():
            self._wait(1 - self.slot)


def _dec_score_kernel(pt_ref, qi_ref, wi_ref, kidx_hbm, o_ref, kbuf, sem):
    pages = _PageFetcher(pt_ref, kidx_hbm, kbuf, sem)
    pages.begin()
    qi = qi_ref[0]
    qi16 = jnp.concatenate([qi, jnp.zeros_like(qi)], axis=0).astype(BF16)
    wi = wi_ref[0]
    for j in range(pages.n_pages):
        pages.prefetch(j)
        s = _dot(qi16, pages.page(j).astype(BF16))[:N_IDX_HEADS]
        o_ref[0, :, j * PAGE_SIZE:(j + 1) * PAGE_SIZE] = jnp.sum(jnp.maximum(s, 0.0) * wi, axis=0, keepdims=True)
    pages.end()


def _dec_scores(page_table, qi3, wi3, kidx_t):
    nb, n_pages = page_table.shape
    grid_spec = pltpu.PrefetchScalarGridSpec(
        num_scalar_prefetch=1,
        grid=(nb,),
        in_specs=[pl.BlockSpec((1, N_IDX_HEADS, IDX_DIM), lambda b, pt: (b, 0, 0)),
                  pl.BlockSpec((1, N_IDX_HEADS, 1), lambda b, pt: (b, 0, 0)),
                  pl.BlockSpec(memory_space=pl.ANY)],
        out_specs=pl.BlockSpec((1, 1, n_pages * PAGE_SIZE), lambda b, pt: (b, 0, 0)),
        scratch_shapes=[pltpu.VMEM((2, n_pages, IDX_DIM, PAGE_SIZE), F32), pltpu.SemaphoreType.DMA((2,))],
    )
    return pl.pallas_call(
        _dec_score_kernel,
        grid_spec=grid_spec,
        out_shape=jax.ShapeDtypeStruct((nb, 1, n_pages * PAGE_SIZE), F32),
        compiler_params=_cparams(("arbitrary",)),
        name="dec_scores",
    )(page_table, qi3, wi3, kidx_t)


def _dec_select_kernel(s_ref, qi_ref, ki_ref, wi_ref, sel_ref, sx, *, topk):
    nb, past = s_ref.shape
    inf = jnp.inf
    qi = qi_ref[...].astype(F32)
    ki = ki_ref[...].astype(F32)
    prod = qi * jnp.concatenate([ki] * N_IDX_HEADS, axis=1)
    nq = N_IDX_HEADS * IDX_DIM
    hsel = jnp.where(_iota((nq, LANES), 0) // IDX_DIM == _iota((nq, LANES), 1), 1.0, 0.0).astype(BF16)
    sh = _sum_dot(prod, hsel)[:, :N_IDX_HEADS]
    s_new = jnp.sum(jnp.maximum(sh, 0.0) * wi_ref[...], axis=1, keepdims=True)
    sx[:, :past] = s_ref[...]
    sx[:, past:] = jnp.where(_iota((nb, LANES), 1) == 0, s_new, -inf)

    rsum = lambda a: jnp.sum(a, axis=1, keepdims=True)
    k_q = float(topk)
    s = sx[...]
    lo = jnp.min(jnp.where(s > -inf, s, inf), axis=1, keepdims=True)
    hi = jnp.max(s, axis=1, keepdims=True)

    def bisect(_, lh):
        lo, hi = lh
        mid = 0.5 * lo + 0.5 * hi
        ge = rsum(jnp.where(sx[...] >= mid, 1.0, 0.0)) >= k_q
        return jnp.where(ge, mid, lo), jnp.where(ge, hi, mid)

    lo, hi = lax.fori_loop(0, N_BISECT, bisect, (lo, hi))
    s = sx[...]
    v0 = jnp.min(jnp.where(s >= lo, s, inf), axis=1, keepdims=True)

    def refine(state):
        v, _, _ = state
        s = sx[...]
        g = s > v
        cnt = rsum(jnp.where(g, 1.0, 0.0))
        v2 = jnp.min(jnp.where(g, s, inf), axis=1, keepdims=True)
        ok = cnt < k_q
        return jnp.where(ok, v, v2), cnt, jnp.sum(jnp.where(ok, 0.0, 1.0))

    thr, c_gt, _ = lax.while_loop(lambda st: st[2] > 0.0, refine,
                                  (v0, jnp.zeros((nb, 1), F32), jnp.float32(1.0)))
    need = k_q - c_gt
    upper = jnp.where(_iota((LANES, LANES), 0) < _iota((LANES, LANES), 1), 1.0, 0.0).astype(BF16)
    carry = jnp.zeros((nb, 1), F32)
    for c in range((past + LANES) // LANES):
        sc = sx[:, c * LANES:(c + 1) * LANES]
        eq = sc == thr
        eqf = jnp.where(eq, 1.0, 0.0)
        rank = carry + _dot(eqf.astype(BF16), upper)
        sel_ref[:, c * LANES:(c + 1) * LANES] = jnp.where((sc > thr) | (eq & (rank < need)), 1.0, 0.0)
        carry = carry + rsum(eqf)


def _dec_select(scores, qi, ki, wi, topk):
    nb, past = scores.shape
    args = [scores, qi, ki, wi]
    return pl.pallas_call(
        functools.partial(_dec_select_kernel, topk=topk),
        in_specs=[pl.BlockSpec(a.shape, lambda: (0, 0)) for a in args],
        out_specs=pl.BlockSpec((nb, past + LANES), lambda: (0, 0)),
        out_shape=jax.ShapeDtypeStruct((nb, past + LANES), F32),
        scratch_shapes=[pltpu.VMEM((nb, past + LANES), F32)],
        compiler_params=pltpu.CompilerParams(vmem_limit_bytes=VMEM_LIMIT),
        name="dec_select",
    )(*args)


def _dec_attn_kernel(pt_ref, q_ref, sel_ref, knew_ref, vnew_ref, sg_ref, k_hbm, v_hbm, o_ref,
                     kbuf, vbuf, ksem, vsem):
    k_pages = _PageFetcher(pt_ref, k_hbm, kbuf, ksem)
    v_pages = _PageFetcher(pt_ref, v_hbm, vbuf, vsem)
    n_pages = k_pages.n_pages
    past = n_pages * PAGE_SIZE
    inf = jnp.inf
    nh = N_Q_HEADS
    R = N_Q_HEADS // N_KV_HEADS

    q8 = q_ref[0].astype(F32)
    own = (_iota((nh, KV_WIDTH), 0) // R) == (_iota((nh, KV_WIDTH), 1) // HEAD_DIM)
    q_ext = jnp.where(own, jnp.concatenate([q8, q8], axis=1), 0.0)
    q16 = jnp.concatenate([q_ext, jnp.zeros_like(q_ext)], axis=0).astype(BF16)
    sel = sel_ref[0]

    k_pages.begin()
    parts = []
    for j in range(n_pages):
        k_pages.prefetch(j)
        parts.append(_dot(q16, k_pages.page(j).astype(BF16))[:nh])
    k_pages.end()
    k_new = knew_ref[0].astype(BF16).astype(F32)
    v_new = vnew_ref[0].astype(BF16).astype(F32)
    lg_new = jnp.sum(q16[:nh].astype(F32) * k_new, axis=1, keepdims=True)
    lm = jnp.where(sel[:, :past] > 0.5, jnp.concatenate(parts, axis=1), -inf)
    lm_new = jnp.where(sel[:, past:past + 1] > 0.5, lg_new, -inf)
    m = jnp.maximum(jnp.max(lm, axis=1, keepdims=True), lm_new)
    p = jnp.exp2(lm - m)
    p_new = jnp.exp2(lm_new - m)
    l = jnp.sum(p, axis=1, keepdims=True) + p_new
    p16 = jnp.concatenate([p, jnp.zeros_like(p)], axis=0).astype(BF16)

    v_pages.begin()
    acc = jnp.zeros((2 * nh, KV_WIDTH), F32)
    for j in range(n_pages):
        v_pages.prefetch(j)
        acc = acc + _dot_nt(p16[:, j * PAGE_SIZE:(j + 1) * PAGE_SIZE], v_pages.page(j).astype(BF16))
    v_pages.end()
    out = (acc[:nh] + p_new.astype(BF16).astype(F32) * v_new) / l
    res = jnp.where(_iota((nh, HEAD_DIM), 0) < R, out[:, :HEAD_DIM], out[:, HEAD_DIM:])
    o_ref[0] = res * sg_ref[0]


def _dec_attn(page_table, q3, sel3, k_new3, v_new3, sg3, cache_k2, cache_v2):
    nb, n_pages = page_table.shape
    per_seq = lambda r, c: pl.BlockSpec((1, r, c), lambda b, pt: (b, 0, 0))
    page_buf = pltpu.VMEM((2, n_pages, KV_WIDTH, PAGE_SIZE), F32)
    grid_spec = pltpu.PrefetchScalarGridSpec(
        num_scalar_prefetch=1,
        grid=(nb,),
        in_specs=[per_seq(N_Q_HEADS, HEAD_DIM), per_seq(1, sel3.shape[2]),
                  per_seq(1, KV_WIDTH), per_seq(1, KV_WIDTH), per_seq(N_Q_HEADS, HEAD_DIM),
                  pl.BlockSpec(memory_space=pl.ANY), pl.BlockSpec(memory_space=pl.ANY)],
        out_specs=per_seq(N_Q_HEADS, HEAD_DIM),
        scratch_shapes=[page_buf, page_buf, pltpu.SemaphoreType.DMA((2,)), pltpu.SemaphoreType.DMA((2,))],
    )
    return pl.pallas_call(
        _dec_attn_kernel,
        grid_spec=grid_spec,
        out_shape=jax.ShapeDtypeStruct((nb, N_Q_HEADS, HEAD_DIM), F32),
        compiler_params=_cparams(("arbitrary",)),
        name="dec_attn",
    )(page_table, q3, sel3, k_new3, v_new3, sg3, cache_k2, cache_v2)


def kernel(x_prompt, x_sample, cache_k, cache_v, cache_kidx, state_wkv, state_shift, page_table,
           c_prompt, c_sample, norm_w, w_ada, b_ada, w_in, q_norm_w, k_norm_w, mu_shift, w0, w_up,
           a0, a_up, k_k, k_a, r_k, ln_x_w, ln_x_b, w_out):
    nb, t, d = x_prompt.shape
    nd = x_sample.shape[0]
    assert x_sample.shape[1] == 1
    n_pages = page_table.shape[1]
    past = n_pages * PAGE_SIZE
    w = RWKV_WIDTH

    offs = [0]
    for sz in (ATT_WIDTH, KV_WIDTH, KV_WIDTH, N_IDX_HEADS * IDX_DIM, N_IDX_HEADS, IDX_DIM, ATT_WIDTH,
               w, w, w, LORA, LORA, w):
        offs.append(offs[-1] + sz)
    col = lambda i: w_in[:, offs[i]:offs[i + 1]]
    w_main = jnp.concatenate([col(0), col(1), col(2), col(3), col(6), col(7), col(8), col(9), col(12),
                              col(10), col(11), col(5), jnp.zeros((d, LANES - IDX_DIM), w_in.dtype)], axis=1).astype(BF16)
    zl = jnp.zeros((LORA, w), F32)
    wts = {
        "norm_w": norm_w.reshape(1, d), "w_main": w_main,
        "w_wit": col(4).T.astype(BF16), "w_vt": col(2).T.astype(BF16),
        "qnw": jnp.concatenate([q_norm_w, q_norm_w]).reshape(1, LANES),
        "knw": jnp.concatenate([k_norm_w, k_norm_w]).reshape(1, LANES),
    }
    rw_params = {
        "mu_rkv": mu_shift[:3 * w].reshape(1, 3 * w), "mu_lora": mu_shift[3 * w:].reshape(1, 2 * LORA),
        "w0": w0.reshape(1, w), "a0": a0.reshape(1, w),
        "w_lora": jnp.concatenate([jnp.concatenate([w_up, zl], axis=1),
                                   jnp.concatenate([zl, a_up], axis=1)], axis=0).astype(BF16),
        "k_k": k_k.reshape(1, w), "k_a": k_a.reshape(1, w), "r_k": r_k.reshape(1, w),
        "ln_w": ln_x_w.reshape(1, w), "ln_b": ln_x_b.reshape(1, w),
    }
    w_out_bf = w_out.astype(BF16)

    n_c = nb + nd
    pad = (-n_c) % 8
    c_all = jnp.concatenate([c_prompt, c_sample, jnp.zeros((pad, d), F32)], axis=0)
    mod = _adaln_mod(c_all, w_ada, b_ada)
    shift_p, scale_p, gate_p = (mod[:nb, s * d:(s + 1) * d].reshape(nb, 1, d) for s in range(3))
    shift_s, scale_s, gate_s = (mod[nb:n_c, s * d:(s + 1) * d].reshape(1, nd, d) for s in range(3))

    tm = min(512, t)
    pr = _in_proj(x_prompt, shift_p, scale_p, _rope_tables(jnp.arange(t)), wts, tm)
    att_p = _attn_prompt(pr["q"], pr["qi"], pr["wit"], pr["kbf"], pr["vt"], pr["kibf"], pr["sg"])
    rw_p, s_p = _rwkv_prompt(pr["rkv"], pr["lora"], pr["gr"], rw_params)
    y_prompt = _out_proj(att_p, rw_p, w_out_bf, x_prompt, gate_p, tm)
    shift_prompt = jnp.concatenate([pr["rkv"][:, t - 1:], pr["lora"][:, t - 1:]], axis=-1)

    xs3 = x_sample.reshape(1, nd, d)
    sm = _in_proj(xs3, shift_s, scale_s, _rope_tables(jnp.full((1,), past)), wts, nd)
    qi_s, ki_s = sm["qi"][0], sm["ki"][0]
    wi_s = sm["wit"][0].T
    n_phys = cache_k.shape[0]
    k_pages = jnp.transpose(cache_k, (0, 2, 3, 1)).reshape(n_phys, KV_WIDTH, PAGE_SIZE)
    v_pages = jnp.transpose(cache_v, (0, 2, 3, 1)).reshape(n_phys, KV_WIDTH, PAGE_SIZE)
    scores = _dec_scores(page_table, qi_s.reshape(nd, N_IDX_HEADS, IDX_DIM), wi_s.reshape(nd, N_IDX_HEADS, 1),
                         jnp.transpose(cache_kidx, (0, 2, 1)))
    topk = min(TOPK_MAX, (past + 1) // 4)
    sel = _dec_select(scores.reshape(nd, past), qi_s, sm["kibf"][0], wi_s, topk)
    att_s = _dec_attn(page_table, sm["q"][0].reshape(nd, N_Q_HEADS, HEAD_DIM), sel.reshape(nd, 1, past + LANES),
                      sm["k"][0].reshape(nd, 1, KV_WIDTH), sm["v"][0].reshape(nd, 1, KV_WIDTH),
                      sm["sg"][0].reshape(nd, N_Q_HEADS, HEAD_DIM),
                      k_pages, v_pages)
    prev = state_shift.reshape(nd, -1)
    vecs = _rwkv_dec_prep(sm["rkv"][0], sm["lora"][0], prev[:, :3 * w], prev[:, 3 * w:], rw_params)
    s_t, y_t = _rwkv_dec_state(vecs[:6], jnp.transpose(state_wkv, (1, 2, 3, 0)))
    s_s = jnp.transpose(s_t, (3, 0, 1, 2))
    rw_s = _rwkv_dec_epi(y_t.reshape(w, nd), vecs[6], sm["gr"][0], rw_params)
    y_sample = _out_proj(att_s.reshape(1, nd, ATT_WIDTH), rw_s.reshape(1, nd, w), w_out_bf, xs3, gate_s, nd)
    shift_sample = jnp.concatenate([sm["rkv"][0], sm["lora"][0]], axis=-1).reshape(nd, 1, 3 * w + 2 * LORA)

    return (y_prompt, y_sample.reshape(nd, 1, d),
            pr["k"].reshape(nb, t, N_KV_HEADS, HEAD_DIM), pr["v"].reshape(nb, t, N_KV_HEADS, HEAD_DIM), pr["ki"],
            s_p, shift_prompt,
            sm["k"][0].reshape(nd, 1, N_KV_HEADS, HEAD_DIM), sm["v"][0].reshape(nd, 1, N_KV_HEADS, HEAD_DIM),
            ki_s.reshape(nd, 1, IDX_DIM), s_s, shift_sample)
```

```python
import functools

import jax
import jax.numpy as jnp
from jax import lax
from jax.experimental import pallas as pl
from jax.experimental.pallas import tpu as pltpu

F32, BF16 = jnp.float32, jnp.bfloat16

HEAD_DIM = 64
N_Q_HEADS = 8
N_KV_HEADS = 2
N_IDX_HEADS = 8
IDX_DIM = 64
N_RWKV_HEADS = 8
ATT_WIDTH = N_Q_HEADS * HEAD_DIM
RWKV_WIDTH = N_RWKV_HEADS * HEAD_DIM
KV_WIDTH = N_KV_HEADS * HEAD_DIM
LORA = 64
TOPK_MAX = 256
ROPE_THETA = 500000.0
ROPE_DIMS = HEAD_DIM // 4
NORM_EPS = 1e-6
GN_EPS = 64e-5
PAGE_SIZE = 128
LANES = 128
Q_SCALE = HEAD_DIM ** -0.5 * 1.4426950408889634
VMEM_LIMIT = 56 * 1024 * 1024

Q_BLOCK = 128
KEY_CHUNK = 512
N_BISECT = 16
RW_CHUNK = 64
RW_BLOCK = 128


def _dot(a, b):
    return jnp.dot(a, b, preferred_element_type=F32)


def _dot_nt(a, b):
    return lax.dot_general(a, b, (((1,), (1,)), ((), ())), preferred_element_type=F32)


def _split(x):
    hi = x.astype(BF16)
    return hi, (x - hi.astype(F32)).astype(BF16)


def _sum_dot(x, sel01):
    hi, lo = _split(x)
    return _dot(hi, sel01) + _dot(lo, sel01)


def _sum_dot_left(sel01, x):
    hi, lo = _split(x)
    return _dot(sel01, hi) + _dot(sel01, lo)


def _dot_tn(a, b):
    return lax.dot_general(a, b, (((0,), (0,)), ((), ())), preferred_element_type=F32)


def _iota(shape, dim):
    return lax.broadcasted_iota(jnp.int32, shape, dim)


def _head_block_ones(n):
    return jnp.where(_iota((n, n), 0) // HEAD_DIM == _iota((n, n), 1) // HEAD_DIM, 1.0, 0.0).astype(BF16)


def _silu(x):
    return x * jax.nn.sigmoid(x)


def _cparams(sem):
    return pltpu.CompilerParams(dimension_semantics=sem, vmem_limit_bytes=VMEM_LIMIT)


def _mod_kernel(c_ref, w_ref, b_ref, o_ref):
    s = _silu(c_ref[...])
    o_ref[...] = _dot(s.astype(BF16), w_ref[...].astype(BF16)) + b_ref[...]


def _adaln_mod(c, w_ada, b_ada):
    rows, d = c.shape
    n = w_ada.shape[1]
    tn = 1024
    return pl.pallas_call(
        _mod_kernel,
        grid=(n // tn,),
        in_specs=[pl.BlockSpec((rows, d), lambda j: (0, 0)),
                  pl.BlockSpec((d, tn), lambda j: (0, j)),
                  pl.BlockSpec((1, tn), lambda j: (0, j))],
        out_specs=pl.BlockSpec((rows, tn), lambda j: (0, j)),
        out_shape=jax.ShapeDtypeStruct((rows, n), F32),
        compiler_params=_cparams(("arbitrary",)),
        name="adaln_mod",
    )(c, w_ada, b_ada.reshape(1, n))


_W_Q, _W_KV, _W_QI, _W_GA, _W_RKV, _W_GR, _W_TAIL, _W_END = 0, 512, 768, 1280, 1792, 3328, 3840, 4096


def _inproj_kernel(x_ref, shift_ref, scale_ref, nw_ref, w_ref, wwi_ref, wvt_ref, cos_ref, s1_ref, s2_ref,
                   qnw_ref, knw_ref,
                   q_o, k_o, kbf_o, v_o, vt_o, qi_o, ki_o, kibf_o, wit_o, sg_o, rkv_o, lora_o, gr_o):
    x = x_ref[0]
    ms = jnp.mean(x * x, axis=-1, keepdims=True)
    h = x * lax.rsqrt(ms + NORM_EPS) * nw_ref[...]
    h = h * (1.0 + scale_ref[0]) + shift_ref[0]
    hb = h.astype(BF16)
    cos, s1, s2 = cos_ref[...], s1_ref[...], s2_ref[...]
    bsum = _head_block_ones(LANES)

    def seg(a, b):
        return _dot(hb, w_ref[:, a:b])

    def rope(t):
        return t * cos + pltpu.roll(t, LANES - ROPE_DIMS // 2, 1) * s1 + pltpu.roll(t, ROPE_DIMS // 2, 1) * s2

    def head_norm(t, w):
        return t * lax.rsqrt(_sum_dot(t * t, bsum) * (1.0 / HEAD_DIM) + NORM_EPS) * w

    yq = seg(_W_Q, _W_KV)
    for s in range(ATT_WIDTH // LANES):
        t = rope(head_norm(yq[:, s * LANES:(s + 1) * LANES], qnw_ref[...]))
        q_o[0, :, s * LANES:(s + 1) * LANES] = (t * Q_SCALE).astype(BF16)

    ykv = seg(_W_KV, _W_QI)
    k = rope(head_norm(ykv[:, :KV_WIDTH], knw_ref[...]))
    k_o[0] = k
    kbf_o[0] = k.astype(BF16)
    v_o[0] = ykv[:, KV_WIDTH:]
    vt_o[0] = _dot_nt(wvt_ref[...], hb).astype(BF16)

    yqi = seg(_W_QI, _W_GA)
    for s in range(N_IDX_HEADS * IDX_DIM // LANES):
        qi_o[0, :, s * LANES:(s + 1) * LANES] = rope(yqi[:, s * LANES:(s + 1) * LANES]).astype(BF16)

    sg_o[0] = _silu(seg(_W_GA, _W_RKV))
    for s in range(3):
        rkv_o[0, :, s * RWKV_WIDTH:(s + 1) * RWKV_WIDTH] = seg(_W_RKV + s * RWKV_WIDTH, _W_RKV + (s + 1) * RWKV_WIDTH)
    gr_o[0] = seg(_W_GR, _W_TAIL)

    yt = seg(_W_TAIL, _W_END)
    lora_o[0] = yt[:, :2 * LORA]
    ki = rope(yt[:, 2 * LORA:])[:, :IDX_DIM]
    ki_o[0] = ki
    kibf_o[0] = ki.astype(BF16)
    wit_o[0] = _dot_nt(wwi_ref[...], hb) * (N_IDX_HEADS ** -0.5 * IDX_DIM ** -0.5)


def _in_proj(x3, shift3, scale3, tabs, wts, tm):
    nb, t, d = x3.shape
    mod_rows = shift3.shape[1]
    tab_rows = tabs[0].shape[0]
    grid = (nb, t // tm)
    if mod_rows == 1:
        mod_spec = pl.BlockSpec((1, 1, d), lambda b, i: (b, 0, 0))
    else:
        mod_spec = pl.BlockSpec((1, tm, d), lambda b, i: (b, i, 0))
    if tab_rows == 1:
        tab_spec = pl.BlockSpec((1, LANES), lambda b, i: (0, 0))
    else:
        tab_spec = pl.BlockSpec((tm, LANES), lambda b, i: (i, 0))
    const = lambda shape: pl.BlockSpec(shape, lambda b, i: tuple(0 for _ in shape))
    row = lambda width: pl.BlockSpec((1, tm, width), lambda b, i: (b, i, 0))
    col = lambda rows: pl.BlockSpec((1, rows, tm), lambda b, i: (b, 0, i))
    out_defs = [
        ("q", (t, ATT_WIDTH), BF16, row(ATT_WIDTH)),
        ("k", (t, KV_WIDTH), F32, row(KV_WIDTH)),
        ("kbf", (t, KV_WIDTH), BF16, row(KV_WIDTH)),
        ("v", (t, KV_WIDTH), F32, row(KV_WIDTH)),
        ("vt", (KV_WIDTH, t), BF16, col(KV_WIDTH)),
        ("qi", (t, N_IDX_HEADS * IDX_DIM), BF16, row(N_IDX_HEADS * IDX_DIM)),
        ("ki", (t, IDX_DIM), F32, row(IDX_DIM)),
        ("kibf", (t, IDX_DIM), BF16, row(IDX_DIM)),
        ("wit", (N_IDX_HEADS, t), F32, col(N_IDX_HEADS)),
        ("sg", (t, ATT_WIDTH), F32, row(ATT_WIDTH)),
        ("rkv", (t, 3 * RWKV_WIDTH), F32, row(3 * RWKV_WIDTH)),
        ("lora", (t, 2 * LORA), F32, row(2 * LORA)),
        ("gr", (t, RWKV_WIDTH), F32, row(RWKV_WIDTH)),
    ]
    outs = pl.pallas_call(
        _inproj_kernel,
        grid=grid,
        in_specs=[pl.BlockSpec((1, tm, d), lambda b, i: (b, i, 0)), mod_spec, mod_spec, const((1, d)),
                  const(wts["w_main"].shape), const(wts["w_wit"].shape), const(wts["w_vt"].shape),
                  tab_spec, tab_spec, tab_spec, const((1, LANES)), const((1, LANES))],
        out_specs=[o[3] for o in out_defs],
        out_shape=[jax.ShapeDtypeStruct((nb,) + o[1], o[2]) for o in out_defs],
        compiler_params=_cparams(("arbitrary", "arbitrary")),
        name="in_proj",
    )(x3, shift3, scale3, wts["norm_w"], wts["w_main"], wts["w_wit"], wts["w_vt"], tabs[0], tabs[1], tabs[2],
      wts["qnw"], wts["knw"])
    return {o[0]: arr for o, arr in zip(out_defs, outs)}


def _rope_tables(pos):
    half = ROPE_DIMS // 2
    inv = jnp.power(ROPE_THETA, -jnp.arange(half, dtype=F32) / half)
    ang = pos.astype(F32)[:, None] * inv[None, :]
    cos, sin = jnp.cos(ang), jnp.sin(ang)
    n = pos.shape[0]
    pad = jnp.zeros((n, HEAD_DIM - ROPE_DIMS), F32)
    zero = jnp.zeros((n, half), F32)
    c_head = jnp.concatenate([cos, cos, pad + 1.0], axis=1)
    s1_head = jnp.concatenate([-sin, zero, pad], axis=1)
    s2_head = jnp.concatenate([zero, sin, pad], axis=1)
    two = lambda a: jnp.concatenate([a, a], axis=1)
    return two(c_head), two(s1_head), two(s2_head)


def _head_sum(t):
    ones = _head_block_ones(LANES)
    parts = [_sum_dot(t[:, s * LANES:(s + 1) * LANES], ones) for s in range(t.shape[1] // LANES)]
    return parts[0] if len(parts) == 1 else jnp.concatenate(parts, axis=1)


def _softplus(z):
    return jnp.maximum(z, 0.0) + jnp.log(1.0 + jnp.exp(-jnp.abs(z)))


def _rwkv_prep(xs, xl, prev, prevl, p):
    xm = xs + p["mu_rkv"] * (prev - xs)
    xml = xl + p["mu_lora"] * (prevl - xl)
    w = RWKV_WIDTH
    r, k, v = xm[:, :w], xm[:, w:2 * w], xm[:, 2 * w:]
    lane = _iota(xml.shape, 1)
    lhs = jnp.where(lane < LORA, jnp.tanh(xml), xml)
    lo = _dot(lhs.astype(BF16), p["w_lora"])
    wlog = -_softplus(-(p["w0"] + lo[:, :w])) - 0.5
    logw = -jnp.exp(wlog)
    a = jax.nn.sigmoid(p["a0"] + lo[:, w:])
    kkr = k * p["k_k"]
    kk = kkr / jnp.maximum(jnp.sqrt(_head_sum(kkr * kkr)), 1e-12)
    k_mod = k * (1.0 + (a - 1.0) * p["k_a"])
    bonus = _head_sum(r * k_mod * p["r_k"]) * v
    return r, logw, k_mod, v, -kk, kk * a, bonus


def _rwkv_epilogue(y, bonus, gate, p):
    mean = _head_sum(y) * (1.0 / HEAD_DIM)
    d = y - mean
    var = _head_sum(d * d) * (1.0 / HEAD_DIM)
    yn = d * lax.rsqrt(var + GN_EPS) * p["ln_w"] + p["ln_b"]
    return (yn + bonus) * _silu(gate)


_RW_PARAM_NAMES = ("mu_rkv", "mu_lora", "w0", "a0", "w_lora", "k_k", "k_a", "r_k", "ln_w", "ln_b")


def _load_params(refs):
    return {n: r[...] for n, r in zip(_RW_PARAM_NAMES, refs)}


def _rwkv_prompt_kernel(rkv_ref, lora_ref, gr_ref, *rest):
    nparam = len(_RW_PARAM_NAMES)
    p = _load_params(rest[:nparam])
    out_ref, sfin_ref = rest[nparam:nparam + 2]
    (st_s, c_rkv, c_lora, r_s, lw_s, k_s, v_s, al_s, be_s, y_s, bon_s,
     at_s, rt_s, bk_s, ec_s, wl_s, wh_s, zv_s, inv_s) = rest[nparam + 2:]
    i = pl.program_id(0)
    nb, tb = rkv_ref.shape[0], rkv_ref.shape[1]
    C = RW_CHUNK
    W = 2 * C

    @pl.when(i == 0)
    def _():
        st_s[...] = jnp.zeros_like(st_s)
        c_rkv[...] = jnp.zeros_like(c_rkv)
        c_lora[...] = jnp.zeros_like(c_lora)

    row = _iota((tb, 1), 0)
    for b in range(nb):
        xs, xl = rkv_ref[b], lora_ref[b]
        prev = jnp.where(row == 0, c_rkv[b], pltpu.roll(xs, 1, 0))
        prevl = jnp.where(row == 0, c_lora[b], pltpu.roll(xl, 1, 0))
        c_rkv[b] = xs[tb - 1:tb, :]
        c_lora[b] = xl[tb - 1:tb, :]
        r_s[b], lw_s[b], k_s[b], v_s[b], al_s[b], be_s[b], bon_s[b] = _rwkv_prep(xs, xl, prev, prevl, p)

    rm, cn = _iota((W, W), 0), _iota((W, W), 1)
    tt, ss = rm % C, cn % C
    top, left = rm < C, cn < C
    strict, incl = ss < tt, ss <= tt
    mk_l_lo, mk_l_hi = top & left & strict, (~top) & (~left) & strict
    mk_z_lo, mk_z_hi = top & (~left) & strict, (~top) & left & strict
    mk_w = (_iota((C, W), 1) % C) <= _iota((C, W), 0)
    bd = (rm // HEAD_DIM) == (cn // HEAD_DIM)
    eye = jnp.where(rm == cn, 1.0, 0.0).astype(F32)
    tril = jnp.where(_iota((C, C), 1) <= _iota((C, C), 0), 1.0, 0.0).astype(BF16)
    lane_lo = _iota((C, LANES), 1) < HEAD_DIM
    m_lo = jnp.where(lane_lo, 1.0, 0.0).astype(F32)
    m_hi = 1.0 - m_lo

    bf = lambda a: a.astype(BF16)
    pairs = range(N_RWKV_HEADS // 2)
    lanes_of = lambda pr: slice(pr * LANES, (pr + 1) * LANES)
    rows_of = lambda c: pl.ds(pl.multiple_of(c * C, C), C)
    items = [(b, pr) for b in range(nb) for pr in pairs]

    def phase_a(c, carry):
        rows = rows_of(c)
        cums = []
        for b in range(nb):
            cum_all = _sum_dot_left(tril, lw_s[b, rows, :])
            cums += [cum_all[:, lanes_of(pr)] for pr in pairs]
        g_lo, g_hi = [], []
        for n, (b, pr) in enumerate(items):
            sl = lanes_of(pr)
            cum = cums[n]
            cum_c = cum[C - 1:C, :]
            e_neg, e_rel = jnp.exp(-cum), jnp.exp(cum_c - cum)
            rr, kk, aa, bb = r_s[b, rows, sl], k_s[b, rows, sl], al_s[b, rows, sl], be_s[b, rows, sl]
            a_t, r_t = aa * jnp.exp(cum - lw_s[b, rows, sl]), rr * jnp.exp(cum)
            b_t, k_t = bb * e_neg, kk * e_neg
            at_s[b, c, pr], rt_s[b, c, pr] = bf(a_t), bf(r_t)
            bk_s[b, c, pr] = jnp.concatenate([bb * e_rel, kk * e_rel], axis=0)
            ec_s[b, c, pr] = jnp.exp(cum_c)
            xk = bf(jnp.concatenate([b_t, k_t], axis=0))
            kx = bf(jnp.concatenate([k_t, b_t], axis=0))
            g_lo.append(_dot_nt(bf(jnp.concatenate([a_t * m_lo, r_t * m_lo], axis=0)), xk))
            g_hi.append(_dot_nt(bf(jnp.concatenate([r_t * m_hi, a_t * m_hi], axis=0)), kx))
        l_mat = []
        for n, (b, pr) in enumerate(items):
            l_mat.append(jnp.where(mk_l_lo, g_lo[n], 0.0) + jnp.where(mk_l_hi, g_hi[n], 0.0))
            z_mat = jnp.where(mk_z_lo, g_lo[n], 0.0) + jnp.where(mk_z_hi, g_hi[n], 0.0)
            wl_s[b, c, pr] = bf(jnp.where(mk_w, g_lo[n][C:, :], 0.0))
            wh_s[b, c, pr] = bf(jnp.where(mk_w, g_hi[n][:C, :], 0.0))
            vv_b = bf(v_s[b, rows, lanes_of(pr)])
            zv_s[b, c, pr] = _dot(bf(z_mat), jnp.concatenate([vv_b, vv_b], axis=0))
        inv = [eye + m for m in l_mat]
        pw = l_mat
        for _ in range(5):
            pw = [_dot(bf(m), bf(m)) for m in pw]
            inv = [x + _dot(bf(x), bf(m)) for x, m in zip(inv, pw)]
        for n, (b, pr) in enumerate(items):
            inv_s[b, c, pr] = bf(inv[n])
        return carry

    lax.fori_loop(0, tb // C, phase_a, 0)

    def phase_b(c, carry):
        rows = rows_of(c)
        nit = range(len(items))
        st = [st_s[b, pr] for b, pr in items]
        st_b = [bf(s) for s in st]
        vv = [v_s[b, rows, lanes_of(pr)] for b, pr in items]
        a_s = [_dot_nt(at_s[b, c, pr], st_b[n]) for n, (b, pr) in enumerate(items)]
        u_st = [_dot(inv_s[b, c, pr], bf(zv_s[b, c, pr] + jnp.concatenate([a_s[n], a_s[n]], axis=0)))
                for n, (b, pr) in enumerate(items)]
        u = [jnp.where(lane_lo, x[:C], x[C:]) for x in u_st]
        uv = [jnp.concatenate([u[n], vv[n]], axis=0) for n in nit]
        upd = [_dot_tn(uv[n], bk_s[b, c, pr]) for n, (b, pr) in enumerate(items)]
        for n, (b, pr) in enumerate(items):
            st_s[b, pr] = st[n] * ec_s[b, c, pr] + jnp.where(bd, upd[n], 0.0)
        for n, (b, pr) in enumerate(items):
            uv_b = bf(uv[n])
            vu_b = jnp.concatenate([uv_b[C:], uv_b[:C]], axis=0)
            y_in = jnp.where(lane_lo, _dot(wl_s[b, c, pr], uv_b), _dot(wh_s[b, c, pr], vu_b))
            y_s[b, rows, lanes_of(pr)] = _dot_nt(rt_s[b, c, pr], st_b[n]) + y_in
        return carry

    lax.fori_loop(0, tb // C, phase_b, 0)
    for b in range(nb):
        out_ref[b] = _rwkv_epilogue(y_s[b], bon_s[b], gr_ref[b], p)

    @pl.when(i == pl.num_programs(0) - 1)
    def _():
        for b in range(nb):
            for hd in range(N_RWKV_HEADS):
                o = (hd % 2) * HEAD_DIM
                sfin_ref[b, hd] = st_s[b, hd // 2][o:o + HEAD_DIM, o:o + HEAD_DIM]


def _rwkv_prompt(rkv, lora, gr, params):
    nb, t, _ = rkv.shape
    tb = min(RW_BLOCK, t)
    w = RWKV_WIDTH
    blk = lambda width: pl.BlockSpec((nb, tb, width), lambda i: (0, i, 0))
    pspecs = [pl.BlockSpec(params[n].shape, lambda i: (0, 0)) for n in _RW_PARAM_NAMES]
    vec = lambda: pltpu.VMEM((nb, tb, w), F32)
    nch, npair, c = tb // RW_CHUNK, N_RWKV_HEADS // 2, RW_CHUNK
    per_chunk = lambda rows, dt: pltpu.VMEM((nb, nch, npair, rows, LANES), dt)
    state_shape = (nb, N_RWKV_HEADS, HEAD_DIM, HEAD_DIM)
    return pl.pallas_call(
        _rwkv_prompt_kernel,
        grid=(t // tb,),
        in_specs=[blk(3 * w), blk(2 * LORA), blk(w)] + pspecs,
        out_specs=[blk(w), pl.BlockSpec(state_shape, lambda i: (0, 0, 0, 0))],
        out_shape=[jax.ShapeDtypeStruct((nb, t, w), F32), jax.ShapeDtypeStruct(state_shape, F32)],
        scratch_shapes=[pltpu.VMEM((nb, npair, LANES, LANES), F32),
                        pltpu.VMEM((nb, 1, 3 * w), F32), pltpu.VMEM((nb, 1, 2 * LORA), F32),
                        vec(), vec(), vec(), vec(), vec(), vec(), vec(), vec(),
                        per_chunk(c, BF16), per_chunk(c, BF16), per_chunk(2 * c, F32), per_chunk(1, F32),
                        per_chunk(c, BF16), per_chunk(c, BF16), per_chunk(2 * c, F32), per_chunk(2 * c, BF16)],
        compiler_params=_cparams(("arbitrary",)),
        name="rwkv_prompt",
    )(rkv, lora, gr, *[params[n] for n in _RW_PARAM_NAMES])


def _rwkv_dec_prep_kernel(rkv_ref, lora_ref, prev_ref, prevl_ref, *rest):
    nparam = len(_RW_PARAM_NAMES)
    p = _load_params(rest[:nparam])
    outs = rest[nparam:]
    r, logw, k_mod, v, al, be, bonus = _rwkv_prep(rkv_ref[...], lora_ref[...], prev_ref[...], prevl_ref[...], p)
    for o, val in zip(outs[:6], (r, jnp.exp(logw), k_mod, v, al, be)):
        o[...] = val.T
    outs[6][...] = bonus


def _rwkv_dec_prep(rkv, lora, prev, prevl, params):
    n = rkv.shape[0]
    full = lambda a: pl.BlockSpec(a.shape, lambda: tuple(0 for _ in a.shape))
    args = [rkv, lora, prev, prevl] + [params[k] for k in _RW_PARAM_NAMES]
    t_shape, shape = (RWKV_WIDTH, n), (n, RWKV_WIDTH)
    return pl.pallas_call(
        _rwkv_dec_prep_kernel,
        in_specs=[full(a) for a in args],
        out_specs=[pl.BlockSpec(t_shape, lambda: (0, 0))] * 6 + [pl.BlockSpec(shape, lambda: (0, 0))],
        out_shape=[jax.ShapeDtypeStruct(t_shape, F32)] * 6 + [jax.ShapeDtypeStruct(shape, F32)],
        compiler_params=pltpu.CompilerParams(vmem_limit_bytes=VMEM_LIMIT),
        name="rwkv_dec_prep",
    )(*args)


def _rwkv_dec_state_kernel(r_ref, w_ref, k_ref, v_ref, a_ref, b_ref, s_ref, so_ref, y_ref):
    r, w, k, a, b = r_ref[0], w_ref[0], k_ref[0], a_ref[0], b_ref[0]

    def one(i, carry):
        st = s_ref[0, i]
        sa = jnp.sum(st * a, axis=0, keepdims=True)
        sn = st * w + sa * b + v_ref[0, pl.ds(i, 1), :] * k
        so_ref[0, i] = sn
        y_ref[0, pl.ds(i, 1), :] = jnp.sum(sn * r, axis=0, keepdims=True)
        return carry

    lax.fori_loop(0, HEAD_DIM, one, 0)


def _rwkv_dec_state(vecs_t, state_t):
    nh, n, _, nb = state_t.shape
    vspec = pl.BlockSpec((1, n, nb), lambda h: (h, 0, 0))
    sspec = pl.BlockSpec((1, n, n, nb), lambda h: (h, 0, 0, 0))
    return pl.pallas_call(
        _rwkv_dec_state_kernel,
        grid=(nh,),
        in_specs=[vspec] * 6 + [sspec],
        out_specs=[sspec, vspec],
        out_shape=[jax.ShapeDtypeStruct(state_t.shape, F32), jax.ShapeDtypeStruct((nh, n, nb), F32)],
        compiler_params=_cparams(("arbitrary",)),
        name="rwkv_dec_state",
    )(*[a.reshape(nh, n, nb) for a in vecs_t], state_t)


def _rwkv_dec_epi_kernel(yt_ref, bonus_ref, gr_ref, lnw_ref, lnb_ref, o_ref):
    o_ref[...] = _rwkv_epilogue(yt_ref[...].T, bonus_ref[...], gr_ref[...],
                                {"ln_w": lnw_ref[...], "ln_b": lnb_ref[...]})


def _rwkv_dec_epi(y_t, bonus, gr, params):
    n = bonus.shape[0]
    args = [y_t, bonus, gr, params["ln_w"], params["ln_b"]]
    return pl.pallas_call(
        _rwkv_dec_epi_kernel,
        in_specs=[pl.BlockSpec(a.shape, lambda: (0, 0)) for a in args],
        out_specs=pl.BlockSpec((n, RWKV_WIDTH), lambda: (0, 0)),
        out_shape=jax.ShapeDtypeStruct((n, RWKV_WIDTH), F32),
        name="rwkv_dec_epi",
    )(*args)


def _attn_prompt_kernel(pt_ref, q_ref, qi_ref, wit_ref, k_ref, vt_ref, kidx_ref, sg_ref, tri_ref,
                        dqi_ref, dwi_ref, dkidx_hbm, o_ref, dsc_ref,
                        s_scr, m_scr, acc_scr, dbuf, dsem, *, topk):
    j = pl.program_id(1)
    step = pl.program_id(0) * pl.num_programs(1) + j
    _score_pages(_PageFetcher(pt_ref, dkidx_hbm, dbuf, dsem, step, pl.num_programs(0) * pl.num_programs(1)),
                 dqi_ref, dwi_ref, dsc_ref)
    QB, KC = Q_BLOCK, KEY_CHUNK
    nch = (j * QB) // KC + 1
    inf = jnp.inf
    qpos_i = j * QB + _iota((1, QB), 1)
    k_q = jnp.minimum(float(topk), qpos_i.astype(F32) + 1.0)
    wit = wit_ref[0]
    qi = qi_ref[0]
    qi_stack = jnp.concatenate([qi[:, h * IDX_DIM:(h + 1) * IDX_DIM] for h in range(N_IDX_HEADS)], axis=0)

    def chunk_rows(c):
        return pl.ds(pl.multiple_of(c * KC, KC), KC)

    SUB = 32
    part = lambda a: a.reshape(KC // SUB, SUB, QB)
    psum = lambda a: jnp.sum(part(a), axis=0)
    pmin = lambda a: jnp.min(part(a), axis=0)
    pmax = lambda a: jnp.max(part(a), axis=0)
    full8 = lambda val: jnp.full((SUB, QB), val, F32)
    fold_sum = lambda a: jnp.sum(a, axis=0, keepdims=True)
    fold_min = lambda a: jnp.min(a, axis=0, keepdims=True)
    fold_max = lambda a: jnp.max(a, axis=0, keepdims=True)

    def score_chunk(c, lohi):
        rows = chunk_rows(c)
        st = _dot_nt(kidx_ref[0, rows, :], qi_stack)
        acc = jnp.zeros((KC, QB), F32)
        for h in range(N_IDX_HEADS):
            acc = acc + jnp.maximum(st[:, h * QB:(h + 1) * QB], 0.0) * wit[h:h + 1, :]
        ok = (c * KC + _iota((KC, QB), 0)) <= qpos_i
        s_scr[rows, :] = jnp.where(ok, acc, -inf)
        return (jnp.minimum(lohi[0], pmin(jnp.where(ok, acc, inf))),
                jnp.maximum(lohi[1], pmax(jnp.where(ok, acc, -inf))))

    lo8, hi8 = lax.fori_loop(0, nch, score_chunk, (full8(inf), full8(-inf)))
    lo, hi = fold_min(lo8), fold_max(hi8)

    def sweep(fn, init):
        return lax.fori_loop(0, nch, lambda c, car: fn(car, s_scr[chunk_rows(c), :]), init)

    def bisect(_, lh):
        lo, hi = lh
        mid = 0.5 * lo + 0.5 * hi
        cnt = fold_sum(sweep(lambda c, s: c + psum(jnp.where(s >= mid, 1.0, 0.0)), full8(0.0)))
        ge = cnt >= k_q
        return jnp.where(ge, mid, lo), jnp.where(ge, hi, mid)

    lo, hi = lax.fori_loop(0, N_BISECT, bisect, (lo, hi))
    v0 = fold_min(sweep(lambda c, s: jnp.minimum(c, pmin(jnp.where(s >= lo, s, inf))), full8(inf)))

    def refine(state):
        v, _, _ = state

        def f(car, s):
            g = s > v
            return car[0] + psum(jnp.where(g, 1.0, 0.0)), jnp.minimum(car[1], pmin(jnp.where(g, s, inf)))

        cnt8, v28 = sweep(f, (full8(0.0), full8(inf)))
        cnt, v2 = fold_sum(cnt8), fold_min(v28)
        ok = cnt < k_q
        return jnp.where(ok, v, v2), cnt, jnp.sum(jnp.where(ok, 0.0, 1.0))

    full = lambda val: jnp.full((1, QB), val, F32)
    thr, c_gt, _ = lax.while_loop(lambda st: st[2] > 0.0, refine, (v0, full(0.0), jnp.float32(1.0)))
    need = k_q - c_gt

    q = q_ref[0]
    G, R = N_KV_HEADS, N_Q_HEADS // N_KV_HEADS
    q_stack = [jnp.concatenate([q[:, (g * R + r) * HEAD_DIM:(g * R + r + 1) * HEAD_DIM] for r in range(R)], axis=0)
               for g in range(G)]
    ones_rows = jnp.where(_iota((HEAD_DIM, KC), 0) == 0, 1.0, 0.0).astype(BF16)
    m_scr[...] = jnp.full(m_scr.shape, -inf, F32)
    acc_scr[...] = jnp.zeros_like(acc_scr)

    n_ge = fold_sum(sweep(lambda c, s: c + psum(jnp.where(s >= thr, 1.0, 0.0)), full8(0.0)))
    any_ties = jnp.sum(jnp.where(n_ge == k_q, 0.0, 1.0)) > 0.0

    def make_attend(with_ties):
        def attend(c, tie_carry):
            rows = chunk_rows(c)
            s = s_scr[rows, :]
            if with_ties:
                eq = s == thr
                rank = tie_carry + _dot(tri_ref[...], jnp.where(eq, 1.0, 0.0).astype(BF16))
                sel = (s > thr) | (eq & (rank < need))
                tie_carry = tie_carry + fold_sum(psum(jnp.where(eq, 1.0, 0.0)))
            else:
                sel = s >= thr
            bias = jnp.where(sel, 0.0, -inf).astype(BF16)
            bias_r = jnp.concatenate([bias] * R, axis=1)
            kc = k_ref[0, rows, :]
            logits = [_dot_nt(kc[:, g * HEAD_DIM:(g + 1) * HEAD_DIM], q_stack[g]) for g in range(G)]
            probs, alphas = [], []
            for g in range(G):
                lm = logits[g].astype(BF16) + bias_r
                m_old = m_scr[g]
                m_part = jnp.max(lm.reshape(KC // 16, 16, R * QB), axis=0).astype(F32)
                m_new = jnp.maximum(m_old, jnp.max(m_part, axis=0, keepdims=True))
                m_safe = jnp.where(m_new == -inf, 0.0, m_new)
                alphas.append(jnp.exp2(m_old - m_safe))
                probs.append(jnp.exp2(lm - m_safe.astype(BF16)))
                m_scr[g] = m_new
            for g in range(G):
                vt_ext = jnp.concatenate([vt_ref[0, g * HEAD_DIM:(g + 1) * HEAD_DIM, rows], ones_rows], axis=0)
                acc_scr[g] = acc_scr[g] * alphas[g] + _dot(vt_ext, probs[g])
            return tie_carry
        return attend

    @pl.when(any_ties)
    def _():
        lax.fori_loop(0, nch, make_attend(True), full(0.0))

    @pl.when(jnp.logical_not(any_ties))
    def _():
        lax.fori_loop(0, nch, make_attend(False), full(0.0))

    out_t = jnp.concatenate([acc_scr[g][:HEAD_DIM] / acc_scr[g][HEAD_DIM:HEAD_DIM + 1] for g in range(G)], axis=0)
    per_r = [out_t[:, r * QB:(r + 1) * QB].T for r in range(R)]
    att = jnp.concatenate([per_r[r][:, g * HEAD_DIM:(g + 1) * HEAD_DIM] for g in range(G) for r in range(R)], axis=1)
    o_ref[0] = att * sg_ref[0]


def _attn_prompt(q, qi, wit, kbf, vt, kibf, sg, page_table, dqi3, dwi3, dkidx_t):
    nb, t, _ = q.shape
    nj = t // Q_BLOCK
    nd, n_pages = page_table.shape
    assert nd == nb * nj
    topk = min(TOPK_MAX, t // 4)
    kc = KEY_CHUNK
    tri = jnp.where(jnp.arange(kc)[None, :] < jnp.arange(kc)[:, None], 1.0, 0.0).astype(BF16)
    qblk = lambda width: pl.BlockSpec((1, Q_BLOCK, width), lambda b, j, pt: (b, j, 0))
    whole = lambda r, c: pl.BlockSpec((1, r, c), lambda b, j, pt: (b, 0, 0))
    per_step = lambda r, c: pl.BlockSpec((1, r, c), lambda b, j, pt: (b * nj + j, 0, 0))
    grid_spec = pltpu.PrefetchScalarGridSpec(
        num_scalar_prefetch=1,
        grid=(nb, nj),
        in_specs=[qblk(ATT_WIDTH), qblk(N_IDX_HEADS * IDX_DIM),
                  pl.BlockSpec((1, N_IDX_HEADS, Q_BLOCK), lambda b, j, pt: (b, 0, j)),
                  whole(t, KV_WIDTH), whole(KV_WIDTH, t), whole(t, IDX_DIM), qblk(ATT_WIDTH),
                  pl.BlockSpec((kc, kc), lambda b, j, pt: (0, 0)),
                  per_step(N_IDX_HEADS, IDX_DIM), per_step(N_IDX_HEADS, 1), pl.BlockSpec(memory_space=pl.ANY)],
        out_specs=[qblk(ATT_WIDTH), per_step(1, n_pages * PAGE_SIZE)],
        scratch_shapes=[pltpu.VMEM((t, Q_BLOCK), F32),
                        pltpu.VMEM((N_KV_HEADS, 1, N_Q_HEADS // N_KV_HEADS * Q_BLOCK), F32),
                        pltpu.VMEM((N_KV_HEADS, 2 * HEAD_DIM, N_Q_HEADS // N_KV_HEADS * Q_BLOCK), F32),
                        pltpu.VMEM((2, n_pages, IDX_DIM, PAGE_SIZE), F32), pltpu.SemaphoreType.DMA((2,))],
    )
    return pl.pallas_call(
        functools.partial(_attn_prompt_kernel, topk=topk),
        grid_spec=grid_spec,
        out_shape=[jax.ShapeDtypeStruct((nb, t, ATT_WIDTH), F32),
                   jax.ShapeDtypeStruct((nd, 1, n_pages * PAGE_SIZE), F32)],
        compiler_params=_cparams(("arbitrary", "arbitrary")),
        name="attn_prompt",
    )(page_table, q, qi, wit, kbf, vt, kibf, sg, tri, dqi3, dwi3, dkidx_t)


def _outproj_kernel(att_ref, rw_ref, w_ref, x_ref, g_ref, o_ref):
    mix = _dot(att_ref[0].astype(BF16), w_ref[:ATT_WIDTH, :]) + _dot(rw_ref[0].astype(BF16), w_ref[ATT_WIDTH:, :])
    o_ref[0] = x_ref[0] + g_ref[0] * mix


def _out_proj(att, rw, w_out_bf, x3, gate3, tm):
    nb, t, d = x3.shape
    if gate3.shape[1] == 1:
        gspec = pl.BlockSpec((1, 1, d), lambda b, i: (b, 0, 0))
    else:
        gspec = pl.BlockSpec((1, tm, d), lambda b, i: (b, i, 0))
    blk = lambda width: pl.BlockSpec((1, tm, width), lambda b, i: (b, i, 0))
    return pl.pallas_call(
        _outproj_kernel,
        grid=(nb, t // tm),
        in_specs=[blk(ATT_WIDTH), blk(RWKV_WIDTH), pl.BlockSpec(w_out_bf.shape, lambda b, i: (0, 0)), blk(d), gspec],
        out_specs=blk(d),
        out_shape=jax.ShapeDtypeStruct((nb, t, d), F32),
        compiler_params=_cparams(("arbitrary", "arbitrary")),
        name="out_proj",
    )(att, rw, w_out_bf, x3, gate3)


class _PageFetcher:
    def __init__(self, pt_ref, hbm, buf, sem, step=None, nsteps=None):
        self.pt, self.hbm, self.buf, self.sem = pt_ref, hbm, buf, sem
        self.n_pages = buf.shape[1]
        b = pl.program_id(0) if step is None else step
        nb = pl.num_programs(0) if nsteps is None else nsteps
        self.first, self.last = b == 0, b == nb - 1
        self.slot = b % 2
        self.next_seq = jnp.minimum(b + 1, nb - 1)

    def _copy(self, seq, j, slot):
        return pltpu.make_async_copy(self.hbm.at[self.pt[seq, j]], self.buf.at[slot, j], self.sem.at[slot])

    def _wait(self, slot):
        pltpu.make_async_copy(self.hbm.at[pl.ds(0, self.n_pages)], self.buf.at[slot], self.sem.at[slot]).wait()

    def begin(self):
        @pl.when(self.first)
        def _():
            for j in range(self.n_pages):
                self._copy(0, j, 0).start()
        self._wait(self.slot)

    def prefetch(self, j):
        self._copy(self.next_seq, j, 1 - self.slot).start(priority=j % 2)

    def page(self, j):
        return self.buf[self.slot, j]

    def end(self):
        @pl.when(self.last)
        def _():
            self._wait(1 - self.slot)


def _score_pages(pages, qi_ref, wi_ref, o_ref):
    pages.begin()
    qi = qi_ref[0]
    qi16 = jnp.concatenate([qi, jnp.zeros_like(qi)], axis=0).astype(BF16)
    wi = wi_ref[0]
    for j in range(pages.n_pages):
        pages.prefetch(j)
        s = _dot(qi16, pages.page(j).astype(BF16))[:N_IDX_HEADS]
        o_ref[0, :, j * PAGE_SIZE:(j + 1) * PAGE_SIZE] = jnp.sum(jnp.maximum(s, 0.0) * wi, axis=0, keepdims=True)
    pages.end()


def _dec_select_kernel(s_ref, qi_ref, ki_ref, wi_ref, sel_ref, sx, *, topk):
    nb, past = s_ref.shape
    inf = jnp.inf
    qi = qi_ref[...].astype(F32)
    ki = ki_ref[...].astype(F32)
    prod = qi * jnp.concatenate([ki] * N_IDX_HEADS, axis=1)
    nq = N_IDX_HEADS * IDX_DIM
    hsel = jnp.where(_iota((nq, LANES), 0) // IDX_DIM == _iota((nq, LANES), 1), 1.0, 0.0).astype(BF16)
    sh = _sum_dot(prod, hsel)[:, :N_IDX_HEADS]
    s_new = jnp.sum(jnp.maximum(sh, 0.0) * wi_ref[...], axis=1, keepdims=True)
    sx[:, :past] = s_ref[...]
    sx[:, past:] = jnp.where(_iota((nb, LANES), 1) == 0, s_new, -inf)

    rsum = lambda a: jnp.sum(a, axis=1, keepdims=True)
    k_q = float(topk)
    s = sx[...]
    lo = jnp.min(jnp.where(s > -inf, s, inf), axis=1, keepdims=True)
    hi = jnp.max(s, axis=1, keepdims=True)

    def bisect(_, lh):
        lo, hi = lh
        mid = 0.5 * lo + 0.5 * hi
        ge = rsum(jnp.where(sx[...] >= mid, 1.0, 0.0)) >= k_q
        return jnp.where(ge, mid, lo), jnp.where(ge, hi, mid)

    lo, hi = lax.fori_loop(0, N_BISECT, bisect, (lo, hi))
    s = sx[...]
    v0 = jnp.min(jnp.where(s >= lo, s, inf), axis=1, keepdims=True)

    def refine(state):
        v, _, _ = state
        s = sx[...]
        g = s > v
        cnt = rsum(jnp.where(g, 1.0, 0.0))
        v2 = jnp.min(jnp.where(g, s, inf), axis=1, keepdims=True)
        ok = cnt < k_q
        return jnp.where(ok, v, v2), cnt, jnp.sum(jnp.where(ok, 0.0, 1.0))

    thr, c_gt, _ = lax.while_loop(lambda st: st[2] > 0.0, refine,
                                  (v0, jnp.zeros((nb, 1), F32), jnp.float32(1.0)))
    need = k_q - c_gt
    upper = jnp.where(_iota((LANES, LANES), 0) < _iota((LANES, LANES), 1), 1.0, 0.0).astype(BF16)
    carry = jnp.zeros((nb, 1), F32)
    for c in range((past + LANES) // LANES):
        sc = sx[:, c * LANES:(c + 1) * LANES]
        eq = sc == thr
        eqf = jnp.where(eq, 1.0, 0.0)
        rank = carry + _dot(eqf.astype(BF16), upper)
        sel_ref[:, c * LANES:(c + 1) * LANES] = jnp.where((sc > thr) | (eq & (rank < need)), 1.0, 0.0)
        carry = carry + rsum(eqf)


def _dec_select(scores, qi, ki, wi, topk):
    nb, past = scores.shape
    args = [scores, qi, ki, wi]
    return pl.pallas_call(
        functools.partial(_dec_select_kernel, topk=topk),
        in_specs=[pl.BlockSpec(a.shape, lambda: (0, 0)) for a in args],
        out_specs=pl.BlockSpec((nb, past + LANES), lambda: (0, 0)),
        out_shape=jax.ShapeDtypeStruct((nb, past + LANES), F32),
        scratch_shapes=[pltpu.VMEM((nb, past + LANES), F32)],
        compiler_params=pltpu.CompilerParams(vmem_limit_bytes=VMEM_LIMIT),
        name="dec_select",
    )(*args)


def _dec_attn_kernel(pt_ref, q_ref, sel_ref, knew_ref, vnew_ref, sg_ref, k_hbm, v_hbm, o_ref,
                     kbuf, vbuf, ksem, vsem):
    k_pages = _PageFetcher(pt_ref, k_hbm, kbuf, ksem)
    v_pages = _PageFetcher(pt_ref, v_hbm, vbuf, vsem)
    n_pages = k_pages.n_pages
    past = n_pages * PAGE_SIZE
    inf = jnp.inf
    nh = N_Q_HEADS
    R = N_Q_HEADS // N_KV_HEADS

    q8 = q_ref[0].astype(F32)
    own = (_iota((nh, KV_WIDTH), 0) // R) == (_iota((nh, KV_WIDTH), 1) // HEAD_DIM)
    q_ext = jnp.where(own, jnp.concatenate([q8, q8], axis=1), 0.0)
    q16 = jnp.concatenate([q_ext, jnp.zeros_like(q_ext)], axis=0).astype(BF16)
    sel = sel_ref[0]

    k_pages.begin()
    parts = []
    for j in range(n_pages):
        k_pages.prefetch(j)
        parts.append(_dot(q16, k_pages.page(j).astype(BF16))[:nh])
    k_pages.end()
    k_new = knew_ref[0].astype(BF16).astype(F32)
    v_new = vnew_ref[0].astype(BF16).astype(F32)
    lg_new = jnp.sum(q16[:nh].astype(F32) * k_new, axis=1, keepdims=True)
    lm = jnp.where(sel[:, :past] > 0.5, jnp.concatenate(parts, axis=1), -inf)
    lm_new = jnp.where(sel[:, past:past + 1] > 0.5, lg_new, -inf)
    m = jnp.maximum(jnp.max(lm, axis=1, keepdims=True), lm_new)
    p = jnp.exp2(lm - m)
    p_new = jnp.exp2(lm_new - m)
    l = jnp.sum(p, axis=1, keepdims=True) + p_new
    p16 = jnp.concatenate([p, jnp.zeros_like(p)], axis=0).astype(BF16)

    v_pages.begin()
    acc = jnp.zeros((2 * nh, KV_WIDTH), F32)
    for j in range(n_pages):
        v_pages.prefetch(j)
        acc = acc + _dot_nt(p16[:, j * PAGE_SIZE:(j + 1) * PAGE_SIZE], v_pages.page(j).astype(BF16))
    v_pages.end()
    out = (acc[:nh] + p_new.astype(BF16).astype(F32) * v_new) / l
    res = jnp.where(_iota((nh, HEAD_DIM), 0) < R, out[:, :HEAD_DIM], out[:, HEAD_DIM:])
    o_ref[0] = res * sg_ref[0]


def _dec_attn(page_table, q3, sel3, k_new3, v_new3, sg3, cache_k2, cache_v2):
    nb, n_pages = page_table.shape
    per_seq = lambda r, c: pl.BlockSpec((1, r, c), lambda b, pt: (b, 0, 0))
    page_buf = pltpu.VMEM((2, n_pages, KV_WIDTH, PAGE_SIZE), F32)
    grid_spec = pltpu.PrefetchScalarGridSpec(
        num_scalar_prefetch=1,
        grid=(nb,),
        in_specs=[per_seq(N_Q_HEADS, HEAD_DIM), per_seq(1, sel3.shape[2]),
                  per_seq(1, KV_WIDTH), per_seq(1, KV_WIDTH), per_seq(N_Q_HEADS, HEAD_DIM),
                  pl.BlockSpec(memory_space=pl.ANY), pl.BlockSpec(memory_space=pl.ANY)],
        out_specs=per_seq(N_Q_HEADS, HEAD_DIM),
        scratch_shapes=[page_buf, page_buf, pltpu.SemaphoreType.DMA((2,)), pltpu.SemaphoreType.DMA((2,))],
    )
    return pl.pallas_call(
        _dec_attn_kernel,
        grid_spec=grid_spec,
        out_shape=jax.ShapeDtypeStruct((nb, N_Q_HEADS, HEAD_DIM), F32),
        compiler_params=_cparams(("arbitrary",)),
        name="dec_attn",
    )(page_table, q3, sel3, k_new3, v_new3, sg3, cache_k2, cache_v2)


def kernel(x_prompt, x_sample, cache_k, cache_v, cache_kidx, state_wkv, state_shift, page_table,
           c_prompt, c_sample, norm_w, w_ada, b_ada, w_in, q_norm_w, k_norm_w, mu_shift, w0, w_up,
           a0, a_up, k_k, k_a, r_k, ln_x_w, ln_x_b, w_out):
    nb, t, d = x_prompt.shape
    nd = x_sample.shape[0]
    assert x_sample.shape[1] == 1
    n_pages = page_table.shape[1]
    past = n_pages * PAGE_SIZE
    w = RWKV_WIDTH

    offs = [0]
    for sz in (ATT_WIDTH, KV_WIDTH, KV_WIDTH, N_IDX_HEADS * IDX_DIM, N_IDX_HEADS, IDX_DIM, ATT_WIDTH,
               w, w, w, LORA, LORA, w):
        offs.append(offs[-1] + sz)
    col = lambda i: w_in[:, offs[i]:offs[i + 1]]
    w_main = jnp.concatenate([col(0), col(1), col(2), col(3), col(6), col(7), col(8), col(9), col(12),
                              col(10), col(11), col(5), jnp.zeros((d, LANES - IDX_DIM), w_in.dtype)], axis=1).astype(BF16)
    zl = jnp.zeros((LORA, w), F32)
    wts = {
        "norm_w": norm_w.reshape(1, d), "w_main": w_main,
        "w_wit": col(4).T.astype(BF16), "w_vt": col(2).T.astype(BF16),
        "qnw": jnp.concatenate([q_norm_w, q_norm_w]).reshape(1, LANES),
        "knw": jnp.concatenate([k_norm_w, k_norm_w]).reshape(1, LANES),
    }
    rw_params = {
        "mu_rkv": mu_shift[:3 * w].reshape(1, 3 * w), "mu_lora": mu_shift[3 * w:].reshape(1, 2 * LORA),
        "w0": w0.reshape(1, w), "a0": a0.reshape(1, w),
        "w_lora": jnp.concatenate([jnp.concatenate([w_up, zl], axis=1),
                                   jnp.concatenate([zl, a_up], axis=1)], axis=0).astype(BF16),
        "k_k": k_k.reshape(1, w), "k_a": k_a.reshape(1, w), "r_k": r_k.reshape(1, w),
        "ln_w": ln_x_w.reshape(1, w), "ln_b": ln_x_b.reshape(1, w),
    }
    w_out_bf = w_out.astype(BF16)

    n_c = nb + nd
    pad = (-n_c) % 8
    c_all = jnp.concatenate([c_prompt, c_sample, jnp.zeros((pad, d), F32)], axis=0)
    mod = _adaln_mod(c_all, w_ada, b_ada)
    shift_p, scale_p, gate_p = (mod[:nb, s * d:(s + 1) * d].reshape(nb, 1, d) for s in range(3))
    shift_s, scale_s, gate_s = (mod[nb:n_c, s * d:(s + 1) * d].reshape(1, nd, d) for s in range(3))

    tm = min(512, t)
    pr = _in_proj(x_prompt, shift_p, scale_p, _rope_tables(jnp.arange(t)), wts, tm)
    xs3 = x_sample.reshape(1, nd, d)
    sm = _in_proj(xs3, shift_s, scale_s, _rope_tables(jnp.full((1,), past)), wts, nd)
    qi_s, ki_s = sm["qi"][0], sm["ki"][0]
    wi_s = sm["wit"][0].T
    n_phys = cache_k.shape[0]
    k_pages = jnp.transpose(cache_k, (0, 2, 3, 1)).reshape(n_phys, KV_WIDTH, PAGE_SIZE)
    v_pages = jnp.transpose(cache_v, (0, 2, 3, 1)).reshape(n_phys, KV_WIDTH, PAGE_SIZE)

    att_p, scores = _attn_prompt(pr["q"], pr["qi"], pr["wit"], pr["kbf"], pr["vt"], pr["kibf"], pr["sg"],
                                 page_table, qi_s.reshape(nd, N_IDX_HEADS, IDX_DIM),
                                 wi_s.reshape(nd, N_IDX_HEADS, 1), jnp.transpose(cache_kidx, (0, 2, 1)))
    rw_p, s_p = _rwkv_prompt(pr["rkv"], pr["lora"], pr["gr"], rw_params)
    y_prompt = _out_proj(att_p, rw_p, w_out_bf, x_prompt, gate_p, tm)
    shift_prompt = jnp.concatenate([pr["rkv"][:, t - 1:], pr["lora"][:, t - 1:]], axis=-1)

    topk = min(TOPK_MAX, (past + 1) // 4)
    sel = _dec_select(scores.reshape(nd, past), qi_s, sm["kibf"][0], wi_s, topk)
    att_s = _dec_attn(page_table, sm["q"][0].reshape(nd, N_Q_HEADS, HEAD_DIM), sel.reshape(nd, 1, past + LANES),
                      sm["k"][0].reshape(nd, 1, KV_WIDTH), sm["v"][0].reshape(nd, 1, KV_WIDTH),
                      sm["sg"][0].reshape(nd, N_Q_HEADS, HEAD_DIM),
                      k_pages, v_pages)
    prev = state_shift.reshape(nd, -1)
    vecs = _rwkv_dec_prep(sm["rkv"][0], sm["lora"][0], prev[:, :3 * w], prev[:, 3 * w:], rw_params)
    s_t, y_t = _rwkv_dec_state(vecs[:6], jnp.transpose(state_wkv, (1, 2, 3, 0)))
    s_s = jnp.transpose(s_t, (3, 0, 1, 2))
    rw_s = _rwkv_dec_epi(y_t.reshape(w, nd), vecs[6], sm["gr"][0], rw_params)
    y_sample = _out_proj(att_s.reshape(1, nd, ATT_WIDTH), rw_s.reshape(1, nd, w), w_out_bf, xs3, gate_s, nd)
    shift_sample = jnp.concatenate([sm["rkv"][0], sm["lora"][0]], axis=-1).reshape(nd, 1, 3 * w + 2 * LORA)

    return (y_prompt, y_sample.reshape(nd, 1, d),
            pr["k"].reshape(nb, t, N_KV_HEADS, HEAD_DIM), pr["v"].reshape(nb, t, N_KV_HEADS, HEAD_DIM), pr["ki"],
            s_p, shift_prompt,
            sm["k"][0].reshape(nd, 1, N_KV_HEADS, HEAD_DIM), sm["v"][0].reshape(nd, 1, N_KV_HEADS, HEAD_DIM),
            ki_s.reshape(nd, 1, IDX_DIM), s_s, shift_sample)
```

```python
import functools

import jax
import jax.numpy as jnp
from jax import lax
from jax.experimental import pallas as pl
from jax.experimental.pallas import tpu as pltpu

F32, BF16 = jnp.float32, jnp.bfloat16

HEAD_DIM = 64
N_Q_HEADS = 8
N_KV_HEADS = 2
N_IDX_HEADS = 8
IDX_DIM = 64
N_RWKV_HEADS = 8
ATT_WIDTH = N_Q_HEADS * HEAD_DIM
RWKV_WIDTH = N_RWKV_HEADS * HEAD_DIM
KV_WIDTH = N_KV_HEADS * HEAD_DIM
LORA = 64
TOPK_MAX = 256
ROPE_THETA = 500000.0
ROPE_DIMS = HEAD_DIM // 4
NORM_EPS = 1e-6
GN_EPS = 64e-5
PAGE_SIZE = 128
LANES = 128
Q_SCALE = HEAD_DIM ** -0.5 * 1.4426950408889634
VMEM_LIMIT = 56 * 1024 * 1024

Q_BLOCK = 256
KEY_CHUNK = 512
N_BISECT = 16
RW_CHUNK = 64
RW_BLOCK = 128


def _dot(a, b):
    return jnp.dot(a, b, preferred_element_type=F32)


def _dot_nt(a, b):
    return lax.dot_general(a, b, (((1,), (1,)), ((), ())), preferred_element_type=F32)


def _split(x):
    hi = x.astype(BF16)
    return hi, (x - hi.astype(F32)).astype(BF16)


def _sum_dot(x, sel01):
    hi, lo = _split(x)
    return _dot(hi, sel01) + _dot(lo, sel01)


def _sum_dot_left(sel01, x):
    hi, lo = _split(x)
    return _dot(sel01, hi) + _dot(sel01, lo)


def _dot_tn(a, b):
    return lax.dot_general(a, b, (((0,), (0,)), ((), ())), preferred_element_type=F32)


def _iota(shape, dim):
    return lax.broadcasted_iota(jnp.int32, shape, dim)


def _head_block_ones(n):
    return jnp.where(_iota((n, n), 0) // HEAD_DIM == _iota((n, n), 1) // HEAD_DIM, 1.0, 0.0).astype(BF16)


def _silu(x):
    return x * jax.nn.sigmoid(x)


def _cparams(sem):
    return pltpu.CompilerParams(dimension_semantics=sem, vmem_limit_bytes=VMEM_LIMIT)


def _mod_kernel(c_ref, w_ref, b_ref, o_ref):
    s = _silu(c_ref[...])
    o_ref[...] = _dot(s.astype(BF16), w_ref[...].astype(BF16)) + b_ref[...]


def _adaln_mod(c, w_ada, b_ada):
    rows, d = c.shape
    n = w_ada.shape[1]
    tn = 1024
    return pl.pallas_call(
        _mod_kernel,
        grid=(n // tn,),
        in_specs=[pl.BlockSpec((rows, d), lambda j: (0, 0)),
                  pl.BlockSpec((d, tn), lambda j: (0, j)),
                  pl.BlockSpec((1, tn), lambda j: (0, j))],
        out_specs=pl.BlockSpec((rows, tn), lambda j: (0, j)),
        out_shape=jax.ShapeDtypeStruct((rows, n), F32),
        compiler_params=_cparams(("arbitrary",)),
        name="adaln_mod",
    )(c, w_ada, b_ada.reshape(1, n))


_W_Q, _W_KV, _W_QI, _W_GA, _W_RKV, _W_GR, _W_TAIL, _W_END = 0, 512, 768, 1280, 1792, 3328, 3840, 4096


def _inproj_kernel(x_ref, shift_ref, scale_ref, nw_ref, w_ref, wwi_ref, wvt_ref, cos_ref, s1_ref, s2_ref,
                   qnw_ref, knw_ref,
                   q_o, k_o, kbf_o, v_o, vt_o, qi_o, ki_o, kibf_o, wit_o, sg_o, rkv_o, lora_o, gr_o):
    x = x_ref[0]
    ms = jnp.mean(x * x, axis=-1, keepdims=True)
    h = x * lax.rsqrt(ms + NORM_EPS) * nw_ref[...]
    h = h * (1.0 + scale_ref[0]) + shift_ref[0]
    hb = h.astype(BF16)
    cos, s1, s2 = cos_ref[...], s1_ref[...], s2_ref[...]
    bsum = _head_block_ones(LANES)

    def seg(a, b):
        return _dot(hb, w_ref[:, a:b])

    def rope(t):
        return t * cos + pltpu.roll(t, LANES - ROPE_DIMS // 2, 1) * s1 + pltpu.roll(t, ROPE_DIMS // 2, 1) * s2

    def head_norm(t, w):
        return t * lax.rsqrt(_sum_dot(t * t, bsum) * (1.0 / HEAD_DIM) + NORM_EPS) * w

    yq = seg(_W_Q, _W_KV)
    for s in range(ATT_WIDTH // LANES):
        t = rope(head_norm(yq[:, s * LANES:(s + 1) * LANES], qnw_ref[...]))
        q_o[0, :, s * LANES:(s + 1) * LANES] = (t * Q_SCALE).astype(BF16)

    ykv = seg(_W_KV, _W_QI)
    k = rope(head_norm(ykv[:, :KV_WIDTH], knw_ref[...]))
    k_o[0] = k
    kbf_o[0] = k.astype(BF16)
    v_o[0] = ykv[:, KV_WIDTH:]
    vt_o[0] = _dot_nt(wvt_ref[...], hb).astype(BF16)

    yqi = seg(_W_QI, _W_GA)
    for s in range(N_IDX_HEADS * IDX_DIM // LANES):
        qi_o[0, :, s * LANES:(s + 1) * LANES] = rope(yqi[:, s * LANES:(s + 1) * LANES]).astype(BF16)

    sg_o[0] = _silu(seg(_W_GA, _W_RKV))
    for s in range(3):
        rkv_o[0, :, s * RWKV_WIDTH:(s + 1) * RWKV_WIDTH] = seg(_W_RKV + s * RWKV_WIDTH, _W_RKV + (s + 1) * RWKV_WIDTH)
    gr_o[0] = seg(_W_GR, _W_TAIL)

    yt = seg(_W_TAIL, _W_END)
    lora_o[0] = yt[:, :2 * LORA]
    ki = rope(yt[:, 2 * LORA:])[:, :IDX_DIM]
    ki_o[0] = ki
    kibf_o[0] = ki.astype(BF16)
    wit_o[0] = _dot_nt(wwi_ref[...], hb) * (N_IDX_HEADS ** -0.5 * IDX_DIM ** -0.5)


def _in_proj(x3, shift3, scale3, tabs, wts, tm):
    nb, t, d = x3.shape
    mod_rows = shift3.shape[1]
    tab_rows = tabs[0].shape[0]
    grid = (nb, t // tm)
    if mod_rows == 1:
        mod_spec = pl.BlockSpec((1, 1, d), lambda b, i: (b, 0, 0))
    else:
        mod_spec = pl.BlockSpec((1, tm, d), lambda b, i: (b, i, 0))
    if tab_rows == 1:
        tab_spec = pl.BlockSpec((1, LANES), lambda b, i: (0, 0))
    else:
        tab_spec = pl.BlockSpec((tm, LANES), lambda b, i: (i, 0))
    const = lambda shape: pl.BlockSpec(shape, lambda b, i: tuple(0 for _ in shape))
    row = lambda width: pl.BlockSpec((1, tm, width), lambda b, i: (b, i, 0))
    col = lambda rows: pl.BlockSpec((1, rows, tm), lambda b, i: (b, 0, i))
    out_defs = [
        ("q", (t, ATT_WIDTH), BF16, row(ATT_WIDTH)),
        ("k", (t, KV_WIDTH), F32, row(KV_WIDTH)),
        ("kbf", (t, KV_WIDTH), BF16, row(KV_WIDTH)),
        ("v", (t, KV_WIDTH), F32, row(KV_WIDTH)),
        ("vt", (KV_WIDTH, t), BF16, col(KV_WIDTH)),
        ("qi", (t, N_IDX_HEADS * IDX_DIM), BF16, row(N_IDX_HEADS * IDX_DIM)),
        ("ki", (t, IDX_DIM), F32, row(IDX_DIM)),
        ("kibf", (t, IDX_DIM), BF16, row(IDX_DIM)),
        ("wit", (N_IDX_HEADS, t), F32, col(N_IDX_HEADS)),
        ("sg", (t, ATT_WIDTH), F32, row(ATT_WIDTH)),
        ("rkv", (t, 3 * RWKV_WIDTH), F32, row(3 * RWKV_WIDTH)),
        ("lora", (t, 2 * LORA), F32, row(2 * LORA)),
        ("gr", (t, RWKV_WIDTH), F32, row(RWKV_WIDTH)),
    ]
    outs = pl.pallas_call(
        _inproj_kernel,
        grid=grid,
        in_specs=[pl.BlockSpec((1, tm, d), lambda b, i: (b, i, 0)), mod_spec, mod_spec, const((1, d)),
                  const(wts["w_main"].shape), const(wts["w_wit"].shape), const(wts["w_vt"].shape),
                  tab_spec, tab_spec, tab_spec, const((1, LANES)), const((1, LANES))],
        out_specs=[o[3] for o in out_defs],
        out_shape=[jax.ShapeDtypeStruct((nb,) + o[1], o[2]) for o in out_defs],
        compiler_params=_cparams(("arbitrary", "arbitrary")),
        name="in_proj",
    )(x3, shift3, scale3, wts["norm_w"], wts["w_main"], wts["w_wit"], wts["w_vt"], tabs[0], tabs[1], tabs[2],
      wts["qnw"], wts["knw"])
    return {o[0]: arr for o, arr in zip(out_defs, outs)}


def _rope_tables(pos):
    half = ROPE_DIMS // 2
    inv = jnp.power(ROPE_THETA, -jnp.arange(half, dtype=F32) / half)
    ang = pos.astype(F32)[:, None] * inv[None, :]
    cos, sin = jnp.cos(ang), jnp.sin(ang)
    n = pos.shape[0]
    pad = jnp.zeros((n, HEAD_DIM - ROPE_DIMS), F32)
    zero = jnp.zeros((n, half), F32)
    c_head = jnp.concatenate([cos, cos, pad + 1.0], axis=1)
    s1_head = jnp.concatenate([-sin, zero, pad], axis=1)
    s2_head = jnp.concatenate([zero, sin, pad], axis=1)
    two = lambda a: jnp.concatenate([a, a], axis=1)
    return two(c_head), two(s1_head), two(s2_head)


def _head_sum(t):
    ones = _head_block_ones(LANES)
    parts = [_sum_dot(t[:, s * LANES:(s + 1) * LANES], ones) for s in range(t.shape[1] // LANES)]
    return parts[0] if len(parts) == 1 else jnp.concatenate(parts, axis=1)


def _softplus(z):
    return jnp.maximum(z, 0.0) + jnp.log(1.0 + jnp.exp(-jnp.abs(z)))


def _rwkv_prep(xs, xl, prev, prevl, p):
    xm = xs + p["mu_rkv"] * (prev - xs)
    xml = xl + p["mu_lora"] * (prevl - xl)
    w = RWKV_WIDTH
    r, k, v = xm[:, :w], xm[:, w:2 * w], xm[:, 2 * w:]
    lane = _iota(xml.shape, 1)
    lhs = jnp.where(lane < LORA, jnp.tanh(xml), xml)
    lo = _dot(lhs.astype(BF16), p["w_lora"])
    wlog = -_softplus(-(p["w0"] + lo[:, :w])) - 0.5
    logw = -jnp.exp(wlog)
    a = jax.nn.sigmoid(p["a0"] + lo[:, w:])
    kkr = k * p["k_k"]
    kk = kkr / jnp.maximum(jnp.sqrt(_head_sum(kkr * kkr)), 1e-12)
    k_mod = k * (1.0 + (a - 1.0) * p["k_a"])
    bonus = _head_sum(r * k_mod * p["r_k"]) * v
    return r, logw, k_mod, v, -kk, kk * a, bonus


def _rwkv_epilogue(y, bonus, gate, p):
    mean = _head_sum(y) * (1.0 / HEAD_DIM)
    d = y - mean
    var = _head_sum(d * d) * (1.0 / HEAD_DIM)
    yn = d * lax.rsqrt(var + GN_EPS) * p["ln_w"] + p["ln_b"]
    return (yn + bonus) * _silu(gate)


_RW_PARAM_NAMES = ("mu_rkv", "mu_lora", "w0", "a0", "w_lora", "k_k", "k_a", "r_k", "ln_w", "ln_b")


def _load_params(refs):
    return {n: r[...] for n, r in zip(_RW_PARAM_NAMES, refs)}


def _rwkv_prompt_kernel(rkv_ref, lora_ref, gr_ref, *rest):
    nparam = len(_RW_PARAM_NAMES)
    p = _load_params(rest[:nparam])
    out_ref, sfin_ref = rest[nparam:nparam + 2]
    (st_s, c_rkv, c_lora, r_s, lw_s, k_s, v_s, al_s, be_s, y_s, bon_s,
     at_s, rt_s, bk_s, ec_s, wl_s, wh_s, zv_s, inv_s) = rest[nparam + 2:]
    i = pl.program_id(0)
    nb, tb = rkv_ref.shape[0], rkv_ref.shape[1]
    C = RW_CHUNK
    W = 2 * C

    @pl.when(i == 0)
    def _():
        st_s[...] = jnp.zeros_like(st_s)
        c_rkv[...] = jnp.zeros_like(c_rkv)
        c_lora[...] = jnp.zeros_like(c_lora)

    row = _iota((tb, 1), 0)
    for b in range(nb):
        xs, xl = rkv_ref[b], lora_ref[b]
        prev = jnp.where(row == 0, c_rkv[b], pltpu.roll(xs, 1, 0))
        prevl = jnp.where(row == 0, c_lora[b], pltpu.roll(xl, 1, 0))
        c_rkv[b] = xs[tb - 1:tb, :]
        c_lora[b] = xl[tb - 1:tb, :]
        r_s[b], lw_s[b], k_s[b], v_s[b], al_s[b], be_s[b], bon_s[b] = _rwkv_prep(xs, xl, prev, prevl, p)

    rm, cn = _iota((W, W), 0), _iota((W, W), 1)
    tt, ss = rm % C, cn % C
    top, left = rm < C, cn < C
    strict, incl = ss < tt, ss <= tt
    mk_l_lo, mk_l_hi = top & left & strict, (~top) & (~left) & strict
    mk_z_lo, mk_z_hi = top & (~left) & strict, (~top) & left & strict
    mk_w = (_iota((C, W), 1) % C) <= _iota((C, W), 0)
    bd = (rm // HEAD_DIM) == (cn // HEAD_DIM)
    eye = jnp.where(rm == cn, 1.0, 0.0).astype(F32)
    tril = jnp.where(_iota((C, C), 1) <= _iota((C, C), 0), 1.0, 0.0).astype(BF16)
    lane_lo = _iota((C, LANES), 1) < HEAD_DIM
    m_lo = jnp.where(lane_lo, 1.0, 0.0).astype(F32)
    m_hi = 1.0 - m_lo

    bf = lambda a: a.astype(BF16)
    pairs = range(N_RWKV_HEADS // 2)
    lanes_of = lambda pr: slice(pr * LANES, (pr + 1) * LANES)
    rows_of = lambda c: pl.ds(pl.multiple_of(c * C, C), C)
    items = [(b, pr) for b in range(nb) for pr in pairs]

    def phase_a(c, carry):
        rows = rows_of(c)
        cums = []
        for b in range(nb):
            cum_all = _sum_dot_left(tril, lw_s[b, rows, :])
            cums += [cum_all[:, lanes_of(pr)] for pr in pairs]
        g_lo, g_hi = [], []
        for n, (b, pr) in enumerate(items):
            sl = lanes_of(pr)
            cum = cums[n]
            cum_c = cum[C - 1:C, :]
            e_neg, e_rel = jnp.exp(-cum), jnp.exp(cum_c - cum)
            rr, kk, aa, bb = r_s[b, rows, sl], k_s[b, rows, sl], al_s[b, rows, sl], be_s[b, rows, sl]
            a_t, r_t = aa * jnp.exp(cum - lw_s[b, rows, sl]), rr * jnp.exp(cum)
            b_t, k_t = bb * e_neg, kk * e_neg
            at_s[b, c, pr], rt_s[b, c, pr] = bf(a_t), bf(r_t)
            bk_s[b, c, pr] = jnp.concatenate([bb * e_rel, kk * e_rel], axis=0)
            ec_s[b, c, pr] = jnp.exp(cum_c)
            xk = bf(jnp.concatenate([b_t, k_t], axis=0))
            kx = bf(jnp.concatenate([k_t, b_t], axis=0))
            g_lo.append(_dot_nt(bf(jnp.concatenate([a_t * m_lo, r_t * m_lo], axis=0)), xk))
            g_hi.append(_dot_nt(bf(jnp.concatenate([r_t * m_hi, a_t * m_hi], axis=0)), kx))
        l_mat = []
        for n, (b, pr) in enumerate(items):
            l_mat.append(jnp.where(mk_l_lo, g_lo[n], 0.0) + jnp.where(mk_l_hi, g_hi[n], 0.0))
            z_mat = jnp.where(mk_z_lo, g_lo[n], 0.0) + jnp.where(mk_z_hi, g_hi[n], 0.0)
            wl_s[b, c, pr] = bf(jnp.where(mk_w, g_lo[n][C:, :], 0.0))
            wh_s[b, c, pr] = bf(jnp.where(mk_w, g_hi[n][:C, :], 0.0))
            vv_b = bf(v_s[b, rows, lanes_of(pr)])
            zv_s[b, c, pr] = _dot(bf(z_mat), jnp.concatenate([vv_b, vv_b], axis=0))
        inv = [eye + m for m in l_mat]
        pw = l_mat
        for _ in range(5):
            pw = [_dot(bf(m), bf(m)) for m in pw]
            inv = [x + _dot(bf(x), bf(m)) for x, m in zip(inv, pw)]
        for n, (b, pr) in enumerate(items):
            inv_s[b, c, pr] = bf(inv[n])
        return carry

    lax.fori_loop(0, tb // C, phase_a, 0)

    def phase_b(c, carry):
        rows = rows_of(c)
        nit = range(len(items))
        st = [st_s[b, pr] for b, pr in items]
        st_b = [bf(s) for s in st]
        vv = [v_s[b, rows, lanes_of(pr)] for b, pr in items]
        a_s = [_dot_nt(at_s[b, c, pr], st_b[n]) for n, (b, pr) in enumerate(items)]
        u_st = [_dot(inv_s[b, c, pr], bf(zv_s[b, c, pr] + jnp.concatenate([a_s[n], a_s[n]], axis=0)))
                for n, (b, pr) in enumerate(items)]
        u = [jnp.where(lane_lo, x[:C], x[C:]) for x in u_st]
        uv = [jnp.concatenate([u[n], vv[n]], axis=0) for n in nit]
        upd = [_dot_tn(uv[n], bk_s[b, c, pr]) for n, (b, pr) in enumerate(items)]
        for n, (b, pr) in enumerate(items):
            st_s[b, pr] = st[n] * ec_s[b, c, pr] + jnp.where(bd, upd[n], 0.0)
        for n, (b, pr) in enumerate(items):
            uv_b = bf(uv[n])
            vu_b = jnp.concatenate([uv_b[C:], uv_b[:C]], axis=0)
            y_in = jnp.where(lane_lo, _dot(wl_s[b, c, pr], uv_b), _dot(wh_s[b, c, pr], vu_b))
            y_s[b, rows, lanes_of(pr)] = _dot_nt(rt_s[b, c, pr], st_b[n]) + y_in
        return carry

    lax.fori_loop(0, tb // C, phase_b, 0)
    for b in range(nb):
        out_ref[b] = _rwkv_epilogue(y_s[b], bon_s[b], gr_ref[b], p)

    @pl.when(i == pl.num_programs(0) - 1)
    def _():
        for b in range(nb):
            for hd in range(N_RWKV_HEADS):
                o = (hd % 2) * HEAD_DIM
                sfin_ref[b, hd] = st_s[b, hd // 2][o:o + HEAD_DIM, o:o + HEAD_DIM]


def _rwkv_prompt(rkv, lora, gr, params):
    nb, t, _ = rkv.shape
    tb = min(RW_BLOCK, t)
    w = RWKV_WIDTH
    blk = lambda width: pl.BlockSpec((nb, tb, width), lambda i: (0, i, 0))
    pspecs = [pl.BlockSpec(params[n].shape, lambda i: (0, 0)) for n in _RW_PARAM_NAMES]
    vec = lambda: pltpu.VMEM((nb, tb, w), F32)
    nch, npair, c = tb // RW_CHUNK, N_RWKV_HEADS // 2, RW_CHUNK
    per_chunk = lambda rows, dt: pltpu.VMEM((nb, nch, npair, rows, LANES), dt)
    state_shape = (nb, N_RWKV_HEADS, HEAD_DIM, HEAD_DIM)
    return pl.pallas_call(
        _rwkv_prompt_kernel,
        grid=(t // tb,),
        in_specs=[blk(3 * w), blk(2 * LORA), blk(w)] + pspecs,
        out_specs=[blk(w), pl.BlockSpec(state_shape, lambda i: (0, 0, 0, 0))],
        out_shape=[jax.ShapeDtypeStruct((nb, t, w), F32), jax.ShapeDtypeStruct(state_shape, F32)],
        scratch_shapes=[pltpu.VMEM((nb, npair, LANES, LANES), F32),
                        pltpu.VMEM((nb, 1, 3 * w), F32), pltpu.VMEM((nb, 1, 2 * LORA), F32),
                        vec(), vec(), vec(), vec(), vec(), vec(), vec(), vec(),
                        per_chunk(c, BF16), per_chunk(c, BF16), per_chunk(2 * c, F32), per_chunk(1, F32),
                        per_chunk(c, BF16), per_chunk(c, BF16), per_chunk(2 * c, F32), per_chunk(2 * c, BF16)],
        compiler_params=_cparams(("arbitrary",)),
        name="rwkv_prompt",
    )(rkv, lora, gr, *[params[n] for n in _RW_PARAM_NAMES])


def _rwkv_dec_prep_kernel(rkv_ref, lora_ref, prev_ref, prevl_ref, *rest):
    nparam = len(_RW_PARAM_NAMES)
    p = _load_params(rest[:nparam])
    outs = rest[nparam:]
    r, logw, k_mod, v, al, be, bonus = _rwkv_prep(rkv_ref[...], lora_ref[...], prev_ref[...], prevl_ref[...], p)
    for o, val in zip(outs[:6], (r, jnp.exp(logw), k_mod, v, al, be)):
        o[...] = val.T
    outs[6][...] = bonus


def _rwkv_dec_prep(rkv, lora, prev, prevl, params):
    n = rkv.shape[0]
    full = lambda a: pl.BlockSpec(a.shape, lambda: tuple(0 for _ in a.shape))
    args = [rkv, lora, prev, prevl] + [params[k] for k in _RW_PARAM_NAMES]
    t_shape, shape = (RWKV_WIDTH, n), (n, RWKV_WIDTH)
    return pl.pallas_call(
        _rwkv_dec_prep_kernel,
        in_specs=[full(a) for a in args],
        out_specs=[pl.BlockSpec(t_shape, lambda: (0, 0))] * 6 + [pl.BlockSpec(shape, lambda: (0, 0))],
        out_shape=[jax.ShapeDtypeStruct(t_shape, F32)] * 6 + [jax.ShapeDtypeStruct(shape, F32)],
        compiler_params=pltpu.CompilerParams(vmem_limit_bytes=VMEM_LIMIT),
        name="rwkv_dec_prep",
    )(*args)


def _rwkv_dec_state_kernel(r_ref, w_ref, k_ref, v_ref, a_ref, b_ref, s_ref, so_ref, y_ref):
    r, w, k, a, b = r_ref[0], w_ref[0], k_ref[0], a_ref[0], b_ref[0]

    def one(i, carry):
        st = s_ref[0, i]
        sa = jnp.sum(st * a, axis=0, keepdims=True)
        sn = st * w + sa * b + v_ref[0, pl.ds(i, 1), :] * k
        so_ref[0, i] = sn
        y_ref[0, pl.ds(i, 1), :] = jnp.sum(sn * r, axis=0, keepdims=True)
        return carry

    lax.fori_loop(0, HEAD_DIM, one, 0)


def _rwkv_dec_state(vecs_t, state_t):
    nh, n, _, nb = state_t.shape
    vspec = pl.BlockSpec((1, n, nb), lambda h: (h, 0, 0))
    sspec = pl.BlockSpec((1, n, n, nb), lambda h: (h, 0, 0, 0))
    return pl.pallas_call(
        _rwkv_dec_state_kernel,
        grid=(nh,),
        in_specs=[vspec] * 6 + [sspec],
        out_specs=[sspec, vspec],
        out_shape=[jax.ShapeDtypeStruct(state_t.shape, F32), jax.ShapeDtypeStruct((nh, n, nb), F32)],
        compiler_params=_cparams(("arbitrary",)),
        name="rwkv_dec_state",
    )(*[a.reshape(nh, n, nb) for a in vecs_t], state_t)


def _rwkv_dec_epi_kernel(yt_ref, bonus_ref, gr_ref, lnw_ref, lnb_ref, o_ref):
    o_ref[...] = _rwkv_epilogue(yt_ref[...].T, bonus_ref[...], gr_ref[...],
                                {"ln_w": lnw_ref[...], "ln_b": lnb_ref[...]})


def _rwkv_dec_epi(y_t, bonus, gr, params):
    n = bonus.shape[0]
    args = [y_t, bonus, gr, params["ln_w"], params["ln_b"]]
    return pl.pallas_call(
        _rwkv_dec_epi_kernel,
        in_specs=[pl.BlockSpec(a.shape, lambda: (0, 0)) for a in args],
        out_specs=pl.BlockSpec((n, RWKV_WIDTH), lambda: (0, 0)),
        out_shape=jax.ShapeDtypeStruct((n, RWKV_WIDTH), F32),
        name="rwkv_dec_epi",
    )(*args)


def _attn_prompt_kernel(pt_ref, q_ref, qi_ref, wit_ref, k_ref, vt_ref, kidx_ref, sg_ref, tri_ref,
                        dqi_ref, dwi_ref, dkidx_hbm, o_ref, dsc_ref,
                        s_scr, m_scr, acc_scr, dbuf, dsem, *, topk):
    j = pl.program_id(1)
    step = pl.program_id(0) * pl.num_programs(1) + j
    nsteps = pl.num_programs(0) * pl.num_programs(1)
    spp = dqi_ref.shape[0]
    for u in range(spp):
        pages = _PageFetcher(pt_ref, dkidx_hbm, dbuf.at[u], dsem.at[u], step, nsteps, stride=spp, offset=u)
        _score_pages(pages, dqi_ref, dwi_ref, dsc_ref, u)
    QB, KC = Q_BLOCK, KEY_CHUNK
    nch = (j * QB) // KC + 1
    inf = jnp.inf
    qpos_i = j * QB + _iota((1, QB), 1)
    k_q = jnp.minimum(float(topk), qpos_i.astype(F32) + 1.0)
    wit = wit_ref[0]
    qi = qi_ref[0]
    qi_stack = jnp.concatenate([qi[:, h * IDX_DIM:(h + 1) * IDX_DIM] for h in range(N_IDX_HEADS)], axis=0)

    def chunk_rows(c):
        return pl.ds(pl.multiple_of(c * KC, KC), KC)

    SUB = 32
    part = lambda a: a.reshape(KC // SUB, SUB, QB)
    psum = lambda a: jnp.sum(part(a), axis=0)
    pmin = lambda a: jnp.min(part(a), axis=0)
    pmax = lambda a: jnp.max(part(a), axis=0)
    full8 = lambda val: jnp.full((SUB, QB), val, F32)
    fold_sum = lambda a: jnp.sum(a, axis=0, keepdims=True)
    fold_min = lambda a: jnp.min(a, axis=0, keepdims=True)
    fold_max = lambda a: jnp.max(a, axis=0, keepdims=True)

    def score_chunk(c, lohi):
        rows = chunk_rows(c)
        st = _dot_nt(kidx_ref[0, rows, :], qi_stack)
        acc = jnp.zeros((KC, QB), F32)
        for h in range(N_IDX_HEADS):
            acc = acc + jnp.maximum(st[:, h * QB:(h + 1) * QB], 0.0) * wit[h:h + 1, :]
        ok = (c * KC + _iota((KC, QB), 0)) <= qpos_i
        s_scr[rows, :] = jnp.where(ok, acc, -inf)
        return (jnp.minimum(lohi[0], pmin(jnp.where(ok, acc, inf))),
                jnp.maximum(lohi[1], pmax(jnp.where(ok, acc, -inf))))

    lo8, hi8 = lax.fori_loop(0, nch, score_chunk, (full8(inf), full8(-inf)))
    lo, hi = fold_min(lo8), fold_max(hi8)

    def sweep(fn, init):
        return lax.fori_loop(0, nch, lambda c, car: fn(car, s_scr[chunk_rows(c), :]), init)

    def bisect(_, lh):
        lo, hi = lh
        mid = 0.5 * lo + 0.5 * hi
        cnt = fold_sum(sweep(lambda c, s: c + psum(jnp.where(s >= mid, 1.0, 0.0)), full8(0.0)))
        ge = cnt >= k_q
        return jnp.where(ge, mid, lo), jnp.where(ge, hi, mid)

    lo, hi = lax.fori_loop(0, N_BISECT, bisect, (lo, hi))
    v0 = fold_min(sweep(lambda c, s: jnp.minimum(c, pmin(jnp.where(s >= lo, s, inf))), full8(inf)))

    def refine(state):
        v, _, _ = state

        def f(car, s):
            g = s > v
            return car[0] + psum(jnp.where(g, 1.0, 0.0)), jnp.minimum(car[1], pmin(jnp.where(g, s, inf)))

        cnt8, v28 = sweep(f, (full8(0.0), full8(inf)))
        cnt, v2 = fold_sum(cnt8), fold_min(v28)
        ok = cnt < k_q
        return jnp.where(ok, v, v2), cnt, jnp.sum(jnp.where(ok, 0.0, 1.0))

    full = lambda val: jnp.full((1, QB), val, F32)
    thr, c_gt, _ = lax.while_loop(lambda st: st[2] > 0.0, refine, (v0, full(0.0), jnp.float32(1.0)))
    need = k_q - c_gt

    q = q_ref[0]
    G, R = N_KV_HEADS, N_Q_HEADS // N_KV_HEADS
    q_stack = [jnp.concatenate([q[:, (g * R + r) * HEAD_DIM:(g * R + r + 1) * HEAD_DIM] for r in range(R)], axis=0)
               for g in range(G)]
    ones_rows = jnp.where(_iota((HEAD_DIM, KC), 0) == 0, 1.0, 0.0).astype(BF16)
    m_scr[...] = jnp.full(m_scr.shape, -inf, F32)
    acc_scr[...] = jnp.zeros_like(acc_scr)

    n_ge = fold_sum(sweep(lambda c, s: c + psum(jnp.where(s >= thr, 1.0, 0.0)), full8(0.0)))
    any_ties = jnp.sum(jnp.where(n_ge == k_q, 0.0, 1.0)) > 0.0

    def make_attend(with_ties):
        def attend(c, tie_carry):
            rows = chunk_rows(c)
            s = s_scr[rows, :]
            if with_ties:
                eq = s == thr
                rank = tie_carry + _dot(tri_ref[...], jnp.where(eq, 1.0, 0.0).astype(BF16))
                sel = (s > thr) | (eq & (rank < need))
                tie_carry = tie_carry + fold_sum(psum(jnp.where(eq, 1.0, 0.0)))
            else:
                sel = s >= thr
            bias = jnp.where(sel, 0.0, -inf).astype(BF16)
            bias_r = jnp.concatenate([bias] * R, axis=1)
            kc = k_ref[0, rows, :]
            logits = [_dot_nt(kc[:, g * HEAD_DIM:(g + 1) * HEAD_DIM], q_stack[g]) for g in range(G)]
            probs, alphas = [], []
            for g in range(G):
                lm = logits[g].astype(BF16) + bias_r
                m_old = m_scr[g]
                m_part = jnp.max(lm.reshape(KC // 16, 16, R * QB), axis=0).astype(F32)
                m_new = jnp.maximum(m_old, jnp.max(m_part, axis=0, keepdims=True))
                m_safe = jnp.where(m_new == -inf, 0.0, m_new)
                alphas.append(jnp.exp2(m_old - m_safe))
                probs.append(jnp.exp2(lm - m_safe.astype(BF16)))
                m_scr[g] = m_new
            for g in range(G):
                vt_ext = jnp.concatenate([vt_ref[0, g * HEAD_DIM:(g + 1) * HEAD_DIM, rows], ones_rows], axis=0)
                acc_scr[g] = acc_scr[g] * alphas[g] + _dot(vt_ext, probs[g])
            return tie_carry
        return attend

    @pl.when(any_ties)
    def _():
        lax.fori_loop(0, nch, make_attend(True), full(0.0))

    @pl.when(jnp.logical_not(any_ties))
    def _():
        lax.fori_loop(0, nch, make_attend(False), full(0.0))

    out_t = jnp.concatenate([acc_scr[g][:HEAD_DIM] / acc_scr[g][HEAD_DIM:HEAD_DIM + 1] for g in range(G)], axis=0)
    per_r = [out_t[:, r * QB:(r + 1) * QB].T for r in range(R)]
    att = jnp.concatenate([per_r[r][:, g * HEAD_DIM:(g + 1) * HEAD_DIM] for g in range(G) for r in range(R)], axis=1)
    o_ref[0] = att * sg_ref[0]


def _attn_prompt(q, qi, wit, kbf, vt, kibf, sg, page_table, dqi3, dwi3, dkidx_t):
    nb, t, _ = q.shape
    nj = t // Q_BLOCK
    nd, n_pages = page_table.shape
    spp = nd // (nb * nj)
    assert spp * nb * nj == nd
    topk = min(TOPK_MAX, t // 4)
    kc = KEY_CHUNK
    tri = jnp.where(jnp.arange(kc)[None, :] < jnp.arange(kc)[:, None], 1.0, 0.0).astype(BF16)
    qblk = lambda width: pl.BlockSpec((1, Q_BLOCK, width), lambda b, j, pt: (b, j, 0))
    whole = lambda r, c: pl.BlockSpec((1, r, c), lambda b, j, pt: (b, 0, 0))
    per_step = lambda r, c: pl.BlockSpec((spp, r, c), lambda b, j, pt: (b * nj + j, 0, 0))
    grid_spec = pltpu.PrefetchScalarGridSpec(
        num_scalar_prefetch=1,
        grid=(nb, nj),
        in_specs=[qblk(ATT_WIDTH), qblk(N_IDX_HEADS * IDX_DIM),
                  pl.BlockSpec((1, N_IDX_HEADS, Q_BLOCK), lambda b, j, pt: (b, 0, j)),
                  whole(t, KV_WIDTH), whole(KV_WIDTH, t), whole(t, IDX_DIM), qblk(ATT_WIDTH),
                  pl.BlockSpec((kc, kc), lambda b, j, pt: (0, 0)),
                  per_step(N_IDX_HEADS, IDX_DIM), per_step(N_IDX_HEADS, 1), pl.BlockSpec(memory_space=pl.ANY)],
        out_specs=[qblk(ATT_WIDTH), per_step(1, n_pages * PAGE_SIZE)],
        scratch_shapes=[pltpu.VMEM((t, Q_BLOCK), F32),
                        pltpu.VMEM((N_KV_HEADS, 1, N_Q_HEADS // N_KV_HEADS * Q_BLOCK), F32),
                        pltpu.VMEM((N_KV_HEADS, 2 * HEAD_DIM, N_Q_HEADS // N_KV_HEADS * Q_BLOCK), F32),
                        pltpu.VMEM((spp, 2, n_pages, IDX_DIM, PAGE_SIZE), F32), pltpu.SemaphoreType.DMA((spp, 2))],
    )
    return pl.pallas_call(
        functools.partial(_attn_prompt_kernel, topk=topk),
        grid_spec=grid_spec,
        out_shape=[jax.ShapeDtypeStruct((nb, t, ATT_WIDTH), F32),
                   jax.ShapeDtypeStruct((nd, 1, n_pages * PAGE_SIZE), F32)],
        compiler_params=_cparams(("arbitrary", "arbitrary")),
        name="attn_prompt",
    )(page_table, q, qi, wit, kbf, vt, kibf, sg, tri, dqi3, dwi3, dkidx_t)


def _outproj_kernel(att_ref, rw_ref, w_ref, x_ref, g_ref, o_ref):
    mix = _dot(att_ref[0].astype(BF16), w_ref[:ATT_WIDTH, :]) + _dot(rw_ref[0].astype(BF16), w_ref[ATT_WIDTH:, :])
    o_ref[0] = x_ref[0] + g_ref[0] * mix


def _out_proj(att, rw, w_out_bf, x3, gate3, tm):
    nb, t, d = x3.shape
    if gate3.shape[1] == 1:
        gspec = pl.BlockSpec((1, 1, d), lambda b, i: (b, 0, 0))
    else:
        gspec = pl.BlockSpec((1, tm, d), lambda b, i: (b, i, 0))
    blk = lambda width: pl.BlockSpec((1, tm, width), lambda b, i: (b, i, 0))
    return pl.pallas_call(
        _outproj_kernel,
        grid=(nb, t // tm),
        in_specs=[blk(ATT_WIDTH), blk(RWKV_WIDTH), pl.BlockSpec(w_out_bf.shape, lambda b, i: (0, 0)), blk(d), gspec],
        out_specs=blk(d),
        out_shape=jax.ShapeDtypeStruct((nb, t, d), F32),
        compiler_params=_cparams(("arbitrary", "arbitrary")),
        name="out_proj",
    )(att, rw, w_out_bf, x3, gate3)


class _PageFetcher:
    def __init__(self, pt_ref, hbm, buf, sem, step=None, nsteps=None, stride=1, offset=0):
        self.pt, self.hbm, self.buf, self.sem = pt_ref, hbm, buf, sem
        self.n_pages = buf.shape[1]
        b = pl.program_id(0) if step is None else step
        nb = pl.num_programs(0) if nsteps is None else nsteps
        self.first, self.last = b == 0, b == nb - 1
        self.slot = b % 2
        self.first_seq = offset
        self.next_seq = jnp.minimum(b + 1, nb - 1) * stride + offset

    def _copy(self, seq, j, slot):
        return pltpu.make_async_copy(self.hbm.at[self.pt[seq, j]], self.buf.at[slot, j], self.sem.at[slot])

    def _wait(self, slot):
        pltpu.make_async_copy(self.hbm.at[pl.ds(0, self.n_pages)], self.buf.at[slot], self.sem.at[slot]).wait()

    def begin(self):
        @pl.when(self.first)
        def _():
            for j in range(self.n_pages):
                self._copy(self.first_seq, j, 0).start()
        self._wait(self.slot)

    def prefetch(self, j):
        self._copy(self.next_seq, j, 1 - self.slot).start(priority=j % 2)

    def page(self, j):
        return self.buf[self.slot, j]

    def end(self):
        @pl.when(self.last)
        def _():
            self._wait(1 - self.slot)


def _score_pages(pages, qi_ref, wi_ref, o_ref, u):
    pages.begin()
    qi = qi_ref[u]
    qi16 = jnp.concatenate([qi, jnp.zeros_like(qi)], axis=0).astype(BF16)
    wi = wi_ref[u]
    for j in range(pages.n_pages):
        pages.prefetch(j)
        s = _dot(qi16, pages.page(j).astype(BF16))[:N_IDX_HEADS]
        o_ref[u, :, j * PAGE_SIZE:(j + 1) * PAGE_SIZE] = jnp.sum(jnp.maximum(s, 0.0) * wi, axis=0, keepdims=True)
    pages.end()


def _dec_select_kernel(s_ref, qi_ref, ki_ref, wi_ref, sel_ref, sx, *, topk):
    nb, past = s_ref.shape
    inf = jnp.inf
    qi = qi_ref[...].astype(F32)
    ki = ki_ref[...].astype(F32)
    prod = qi * jnp.concatenate([ki] * N_IDX_HEADS, axis=1)
    nq = N_IDX_HEADS * IDX_DIM
    hsel = jnp.where(_iota((nq, LANES), 0) // IDX_DIM == _iota((nq, LANES), 1), 1.0, 0.0).astype(BF16)
    sh = _sum_dot(prod, hsel)[:, :N_IDX_HEADS]
    s_new = jnp.sum(jnp.maximum(sh, 0.0) * wi_ref[...], axis=1, keepdims=True)
    sx[:, :past] = s_ref[...]
    sx[:, past:] = jnp.where(_iota((nb, LANES), 1) == 0, s_new, -inf)

    rsum = lambda a: jnp.sum(a, axis=1, keepdims=True)
    k_q = float(topk)
    s = sx[...]
    lo = jnp.min(jnp.where(s > -inf, s, inf), axis=1, keepdims=True)
    hi = jnp.max(s, axis=1, keepdims=True)

    def bisect(_, lh):
        lo, hi = lh
        mid = 0.5 * lo + 0.5 * hi
        ge = rsum(jnp.where(sx[...] >= mid, 1.0, 0.0)) >= k_q
        return jnp.where(ge, mid, lo), jnp.where(ge, hi, mid)

    lo, hi = lax.fori_loop(0, N_BISECT, bisect, (lo, hi))
    s = sx[...]
    v0 = jnp.min(jnp.where(s >= lo, s, inf), axis=1, keepdims=True)

    def refine(state):
        v, _, _ = state
        s = sx[...]
        g = s > v
        cnt = rsum(jnp.where(g, 1.0, 0.0))
        v2 = jnp.min(jnp.where(g, s, inf), axis=1, keepdims=True)
        ok = cnt < k_q
        return jnp.where(ok, v, v2), cnt, jnp.sum(jnp.where(ok, 0.0, 1.0))

    thr, c_gt, _ = lax.while_loop(lambda st: st[2] > 0.0, refine,
                                  (v0, jnp.zeros((nb, 1), F32), jnp.float32(1.0)))
    need = k_q - c_gt
    upper = jnp.where(_iota((LANES, LANES), 0) < _iota((LANES, LANES), 1), 1.0, 0.0).astype(BF16)
    carry = jnp.zeros((nb, 1), F32)
    for c in range((past + LANES) // LANES):
        sc = sx[:, c * LANES:(c + 1) * LANES]
        eq = sc == thr
        eqf = jnp.where(eq, 1.0, 0.0)
        rank = carry + _dot(eqf.astype(BF16), upper)
        sel_ref[:, c * LANES:(c + 1) * LANES] = jnp.where((sc > thr) | (eq & (rank < need)), 1.0, 0.0)
        carry = carry + rsum(eqf)


def _dec_select(scores, qi, ki, wi, topk):
    nb, past = scores.shape
    args = [scores, qi, ki, wi]
    return pl.pallas_call(
        functools.partial(_dec_select_kernel, topk=topk),
        in_specs=[pl.BlockSpec(a.shape, lambda: (0, 0)) for a in args],
        out_specs=pl.BlockSpec((nb, past + LANES), lambda: (0, 0)),
        out_shape=jax.ShapeDtypeStruct((nb, past + LANES), F32),
        scratch_shapes=[pltpu.VMEM((nb, past + LANES), F32)],
        compiler_params=pltpu.CompilerParams(vmem_limit_bytes=VMEM_LIMIT),
        name="dec_select",
    )(*args)


def _dec_attn_kernel(pt_ref, q_ref, sel_ref, knew_ref, vnew_ref, sg_ref, k_hbm, v_hbm, o_ref,
                     kbuf, vbuf, ksem, vsem):
    k_pages = _PageFetcher(pt_ref, k_hbm, kbuf, ksem)
    v_pages = _PageFetcher(pt_ref, v_hbm, vbuf, vsem)
    n_pages = k_pages.n_pages
    past = n_pages * PAGE_SIZE
    inf = jnp.inf
    nh = N_Q_HEADS
    R = N_Q_HEADS // N_KV_HEADS

    q8 = q_ref[0].astype(F32)
    own = (_iota((nh, KV_WIDTH), 0) // R) == (_iota((nh, KV_WIDTH), 1) // HEAD_DIM)
    q_ext = jnp.where(own, jnp.concatenate([q8, q8], axis=1), 0.0)
    q16 = jnp.concatenate([q_ext, jnp.zeros_like(q_ext)], axis=0).astype(BF16)
    sel = sel_ref[0]

    k_pages.begin()
    parts = []
    for j in range(n_pages):
        k_pages.prefetch(j)
        parts.append(_dot(q16, k_pages.page(j).astype(BF16))[:nh])
    k_pages.end()
    k_new = knew_ref[0].astype(BF16).astype(F32)
    v_new = vnew_ref[0].astype(BF16).astype(F32)
    lg_new = jnp.sum(q16[:nh].astype(F32) * k_new, axis=1, keepdims=True)
    lm = jnp.where(sel[:, :past] > 0.5, jnp.concatenate(parts, axis=1), -inf)
    lm_new = jnp.where(sel[:, past:past + 1] > 0.5, lg_new, -inf)
    m = jnp.maximum(jnp.max(lm, axis=1, keepdims=True), lm_new)
    p = jnp.exp2(lm - m)
    p_new = jnp.exp2(lm_new - m)
    l = jnp.sum(p, axis=1, keepdims=True) + p_new
    p16 = jnp.concatenate([p, jnp.zeros_like(p)], axis=0).astype(BF16)

    v_pages.begin()
    acc = jnp.zeros((2 * nh, KV_WIDTH), F32)
    for j in range(n_pages):
        v_pages.prefetch(j)
        acc = acc + _dot_nt(p16[:, j * PAGE_SIZE:(j + 1) * PAGE_SIZE], v_pages.page(j).astype(BF16))
    v_pages.end()
    out = (acc[:nh] + p_new.astype(BF16).astype(F32) * v_new) / l
    res = jnp.where(_iota((nh, HEAD_DIM), 0) < R, out[:, :HEAD_DIM], out[:, HEAD_DIM:])
    o_ref[0] = res * sg_ref[0]


def _dec_attn(page_table, q3, sel3, k_new3, v_new3, sg3, cache_k2, cache_v2):
    nb, n_pages = page_table.shape
    per_seq = lambda r, c: pl.BlockSpec((1, r, c), lambda b, pt: (b, 0, 0))
    page_buf = pltpu.VMEM((2, n_pages, KV_WIDTH, PAGE_SIZE), F32)
    grid_spec = pltpu.PrefetchScalarGridSpec(
        num_scalar_prefetch=1,
        grid=(nb,),
        in_specs=[per_seq(N_Q_HEADS, HEAD_DIM), per_seq(1, sel3.shape[2]),
                  per_seq(1, KV_WIDTH), per_seq(1, KV_WIDTH), per_seq(N_Q_HEADS, HEAD_DIM),
                  pl.BlockSpec(memory_space=pl.ANY), pl.BlockSpec(memory_space=pl.ANY)],
        out_specs=per_seq(N_Q_HEADS, HEAD_DIM),
        scratch_shapes=[page_buf, page_buf, pltpu.SemaphoreType.DMA((2,)), pltpu.SemaphoreType.DMA((2,))],
    )
    return pl.pallas_call(
        _dec_attn_kernel,
        grid_spec=grid_spec,
        out_shape=jax.ShapeDtypeStruct((nb, N_Q_HEADS, HEAD_DIM), F32),
        compiler_params=_cparams(("arbitrary",)),
        name="dec_attn",
    )(page_table, q3, sel3, k_new3, v_new3, sg3, cache_k2, cache_v2)


def kernel(x_prompt, x_sample, cache_k, cache_v, cache_kidx, state_wkv, state_shift, page_table,
           c_prompt, c_sample, norm_w, w_ada, b_ada, w_in, q_norm_w, k_norm_w, mu_shift, w0, w_up,
           a0, a_up, k_k, k_a, r_k, ln_x_w, ln_x_b, w_out):
    nb, t, d = x_prompt.shape
    nd = x_sample.shape[0]
    assert x_sample.shape[1] == 1
    n_pages = page_table.shape[1]
    past = n_pages * PAGE_SIZE
    w = RWKV_WIDTH

    offs = [0]
    for sz in (ATT_WIDTH, KV_WIDTH, KV_WIDTH, N_IDX_HEADS * IDX_DIM, N_IDX_HEADS, IDX_DIM, ATT_WIDTH,
               w, w, w, LORA, LORA, w):
        offs.append(offs[-1] + sz)
    col = lambda i: w_in[:, offs[i]:offs[i + 1]]
    w_main = jnp.concatenate([col(0), col(1), col(2), col(3), col(6), col(7), col(8), col(9), col(12),
                              col(10), col(11), col(5), jnp.zeros((d, LANES - IDX_DIM), w_in.dtype)], axis=1).astype(BF16)
    zl = jnp.zeros((LORA, w), F32)
    wts = {
        "norm_w": norm_w.reshape(1, d), "w_main": w_main,
        "w_wit": col(4).T.astype(BF16), "w_vt": col(2).T.astype(BF16),
        "qnw": jnp.concatenate([q_norm_w, q_norm_w]).reshape(1, LANES),
        "knw": jnp.concatenate([k_norm_w, k_norm_w]).reshape(1, LANES),
    }
    rw_params = {
        "mu_rkv": mu_shift[:3 * w].reshape(1, 3 * w), "mu_lora": mu_shift[3 * w:].reshape(1, 2 * LORA),
        "w0": w0.reshape(1, w), "a0": a0.reshape(1, w),
        "w_lora": jnp.concatenate([jnp.concatenate([w_up, zl], axis=1),
                                   jnp.concatenate([zl, a_up], axis=1)], axis=0).astype(BF16),
        "k_k": k_k.reshape(1, w), "k_a": k_a.reshape(1, w), "r_k": r_k.reshape(1, w),
        "ln_w": ln_x_w.reshape(1, w), "ln_b": ln_x_b.reshape(1, w),
    }
    w_out_bf = w_out.astype(BF16)

    n_c = nb + nd
    pad = (-n_c) % 8
    c_all = jnp.concatenate([c_prompt, c_sample, jnp.zeros((pad, d), F32)], axis=0)
    mod = _adaln_mod(c_all, w_ada, b_ada)
    shift_p, scale_p, gate_p = (mod[:nb, s * d:(s + 1) * d].reshape(nb, 1, d) for s in range(3))
    shift_s, scale_s, gate_s = (mod[nb:n_c, s * d:(s + 1) * d].reshape(1, nd, d) for s in range(3))

    tm = min(512, t)
    pr = _in_proj(x_prompt, shift_p, scale_p, _rope_tables(jnp.arange(t)), wts, tm)
    xs3 = x_sample.reshape(1, nd, d)
    sm = _in_proj(xs3, shift_s, scale_s, _rope_tables(jnp.full((1,), past)), wts, nd)
    qi_s, ki_s = sm["qi"][0], sm["ki"][0]
    wi_s = sm["wit"][0].T
    n_phys = cache_k.shape[0]
    k_pages = jnp.transpose(cache_k, (0, 2, 3, 1)).reshape(n_phys, KV_WIDTH, PAGE_SIZE)
    v_pages = jnp.transpose(cache_v, (0, 2, 3, 1)).reshape(n_phys, KV_WIDTH, PAGE_SIZE)

    att_p, scores = _attn_prompt(pr["q"], pr["qi"], pr["wit"], pr["kbf"], pr["vt"], pr["kibf"], pr["sg"],
                                 page_table, qi_s.reshape(nd, N_IDX_HEADS, IDX_DIM),
                                 wi_s.reshape(nd, N_IDX_HEADS, 1), jnp.transpose(cache_kidx, (0, 2, 1)))
    rw_p, s_p = _rwkv_prompt(pr["rkv"], pr["lora"], pr["gr"], rw_params)
    y_prompt = _out_proj(att_p, rw_p, w_out_bf, x_prompt, gate_p, tm)
    shift_prompt = jnp.concatenate([pr["rkv"][:, t - 1:], pr["lora"][:, t - 1:]], axis=-1)

    topk = min(TOPK_MAX, (past + 1) // 4)
    sel = _dec_select(scores.reshape(nd, past), qi_s, sm["kibf"][0], wi_s, topk)
    att_s = _dec_attn(page_table, sm["q"][0].reshape(nd, N_Q_HEADS, HEAD_DIM), sel.reshape(nd, 1, past + LANES),
                      sm["k"][0].reshape(nd, 1, KV_WIDTH), sm["v"][0].reshape(nd, 1, KV_WIDTH),
                      sm["sg"][0].reshape(nd, N_Q_HEADS, HEAD_DIM),
                      k_pages, v_pages)
    prev = state_shift.reshape(nd, -1)
    vecs = _rwkv_dec_prep(sm["rkv"][0], sm["lora"][0], prev[:, :3 * w], prev[:, 3 * w:], rw_params)
    s_t, y_t = _rwkv_dec_state(vecs[:6], jnp.transpose(state_wkv, (1, 2, 3, 0)))
    s_s = jnp.transpose(s_t, (3, 0, 1, 2))
    rw_s = _rwkv_dec_epi(y_t.reshape(w, nd), vecs[6], sm["gr"][0], rw_params)
    y_sample = _out_proj(att_s.reshape(1, nd, ATT_WIDTH), rw_s.reshape(1, nd, w), w_out_bf, xs3, gate_s, nd)
    shift_sample = jnp.concatenate([sm["rkv"][0], sm["lora"][0]], axis=-1).reshape(nd, 1, 3 * w + 2 * LORA)

    return (y_prompt, y_sample.reshape(nd, 1, d),
            pr["k"].reshape(nb, t, N_KV_HEADS, HEAD_DIM), pr["v"].reshape(nb, t, N_KV_HEADS, HEAD_DIM), pr["ki"],
            s_p, shift_prompt,
            sm["k"][0].reshape(nd, 1, N_KV_HEADS, HEAD_DIM), sm["v"][0].reshape(nd, 1, N_KV_HEADS, HEAD_DIM),
            ki_s.reshape(nd, 1, IDX_DIM), s_s, shift_sample)
```

```python
import functools

import jax
import jax.numpy as jnp
from jax import lax
from jax.experimental import pallas as pl
from jax.experimental.pallas import tpu as pltpu

F32, BF16 = jnp.float32, jnp.bfloat16

HEAD_DIM = 64
N_Q_HEADS = 8
N_KV_HEADS = 2
N_IDX_HEADS = 8
IDX_DIM = 64
N_RWKV_HEADS = 8
ATT_WIDTH = N_Q_HEADS * HEAD_DIM
RWKV_WIDTH = N_RWKV_HEADS * HEAD_DIM
KV_WIDTH = N_KV_HEADS * HEAD_DIM
LORA = 64
TOPK_MAX = 256
ROPE_THETA = 500000.0
ROPE_DIMS = HEAD_DIM // 4
NORM_EPS = 1e-6
GN_EPS = 64e-5
PAGE_SIZE = 128
LANES = 128
SUBLANES = 8
Q_SCALE = HEAD_DIM ** -0.5 * 1.4426950408889634
VMEM_LIMIT = 56 * 1024 * 1024

Q_BLOCK = 256
KEY_CHUNK = 512
N_BISECT = 16
RW_CHUNK = 64
RW_BLOCK = 128


def _dot(a, b):
    return jnp.dot(a, b, preferred_element_type=F32)


def _dot_nt(a, b):
    return lax.dot_general(a, b, (((1,), (1,)), ((), ())), preferred_element_type=F32)


def _split(x):
    hi = x.astype(BF16)
    return hi, (x - hi.astype(F32)).astype(BF16)


def _sum_dot(x, sel01):
    hi, lo = _split(x)
    return _dot(hi, sel01) + _dot(lo, sel01)


def _sum_dot_left(sel01, x):
    hi, lo = _split(x)
    return _dot(sel01, hi) + _dot(sel01, lo)


def _dot_tn(a, b):
    return lax.dot_general(a, b, (((0,), (0,)), ((), ())), preferred_element_type=F32)


def _iota(shape, dim):
    return lax.broadcasted_iota(jnp.int32, shape, dim)


def _head_block_ones(n):
    return jnp.where(_iota((n, n), 0) // HEAD_DIM == _iota((n, n), 1) // HEAD_DIM, 1.0, 0.0).astype(BF16)


def _silu(x):
    return x * jax.nn.sigmoid(x)


def _cparams(sem):
    return pltpu.CompilerParams(dimension_semantics=sem, vmem_limit_bytes=VMEM_LIMIT)


def _mod_kernel(c_ref, w_ref, b_ref, o_ref):
    s = _silu(c_ref[...])
    o_ref[...] = _dot(s.astype(BF16), w_ref[...].astype(BF16)) + b_ref[...]


def _adaln_mod(c, w_ada, b_ada):
    rows, d = c.shape
    n = w_ada.shape[1]
    tn = 1024
    return pl.pallas_call(
        _mod_kernel,
        grid=(n // tn,),
        in_specs=[pl.BlockSpec((rows, d), lambda j: (0, 0)),
                  pl.BlockSpec((d, tn), lambda j: (0, j)),
                  pl.BlockSpec((1, tn), lambda j: (0, j))],
        out_specs=pl.BlockSpec((rows, tn), lambda j: (0, j)),
        out_shape=jax.ShapeDtypeStruct((rows, n), F32),
        compiler_params=_cparams(("arbitrary",)),
        name="adaln_mod",
    )(c, w_ada, b_ada.reshape(1, n))


_W_Q, _W_KV, _W_QI, _W_GA, _W_RKV, _W_GR, _W_TAIL, _W_END = 0, 512, 768, 1280, 1792, 3328, 3840, 4096


def _inproj_kernel(x_ref, shift_ref, scale_ref, nw_ref, w_ref, wwi_ref, wvt_ref, cos_ref, s1_ref, s2_ref,
                   qnw_ref, knw_ref,
                   q_o, k_o, kbf_o, v_o, vt_o, qi_o, ki_o, kibf_o, wit_o, sg_o, rkv_o, lora_o, gr_o):
    x = x_ref[0]
    ms = jnp.mean(x * x, axis=-1, keepdims=True)
    h = x * lax.rsqrt(ms + NORM_EPS) * nw_ref[...]
    h = h * (1.0 + scale_ref[0]) + shift_ref[0]
    hb = h.astype(BF16)
    cos, s1, s2 = cos_ref[...], s1_ref[...], s2_ref[...]
    bsum = _head_block_ones(LANES)

    def seg(a, b):
        return _dot_nt(hb, w_ref[a:b, :])

    def rope(t):
        return t * cos + pltpu.roll(t, LANES - ROPE_DIMS // 2, 1) * s1 + pltpu.roll(t, ROPE_DIMS // 2, 1) * s2

    def head_norm(t, w):
        return t * lax.rsqrt(_sum_dot(t * t, bsum) * (1.0 / HEAD_DIM) + NORM_EPS) * w

    yq = seg(_W_Q, _W_KV)
    for s in range(ATT_WIDTH // LANES):
        t = rope(head_norm(yq[:, s * LANES:(s + 1) * LANES], qnw_ref[...]))
        q_o[0, :, s * LANES:(s + 1) * LANES] = (t * Q_SCALE).astype(BF16)

    ykv = seg(_W_KV, _W_QI)
    k = rope(head_norm(ykv[:, :KV_WIDTH], knw_ref[...]))
    k_o[0] = k
    kbf_o[0] = k.astype(BF16)
    v_o[0] = ykv[:, KV_WIDTH:]
    vt_o[0] = _dot_nt(wvt_ref[...], hb).astype(BF16)

    yqi = seg(_W_QI, _W_GA)
    for s in range(N_IDX_HEADS * IDX_DIM // LANES):
        qi_o[0, :, s * LANES:(s + 1) * LANES] = rope(yqi[:, s * LANES:(s + 1) * LANES]).astype(BF16)

    sg_o[0] = _silu(seg(_W_GA, _W_RKV))
    for s in range(3):
        rkv_o[0, :, s * RWKV_WIDTH:(s + 1) * RWKV_WIDTH] = seg(_W_RKV + s * RWKV_WIDTH, _W_RKV + (s + 1) * RWKV_WIDTH)
    gr_o[0] = seg(_W_GR, _W_TAIL)

    yt = seg(_W_TAIL, _W_END)
    lora_o[0] = yt[:, :2 * LORA]
    ki = rope(yt[:, 2 * LORA:])[:, :IDX_DIM]
    ki_o[0] = ki
    kibf_o[0] = ki.astype(BF16)
    wit_o[0] = _dot_nt(wwi_ref[...], hb) * (N_IDX_HEADS ** -0.5 * IDX_DIM ** -0.5)


def _in_proj(x3, shift3, scale3, tabs, wts, tm):
    nb, t, d = x3.shape
    mod_rows = shift3.shape[1]
    tab_rows = tabs[0].shape[0]
    grid = (nb, t // tm)
    if mod_rows == 1:
        mod_spec = pl.BlockSpec((1, 1, d), lambda b, i: (b, 0, 0))
    else:
        mod_spec = pl.BlockSpec((1, tm, d), lambda b, i: (b, i, 0))
    if tab_rows == 1:
        tab_spec = pl.BlockSpec((1, LANES), lambda b, i: (0, 0))
    else:
        tab_spec = pl.BlockSpec((tm, LANES), lambda b, i: (i, 0))
    const = lambda shape: pl.BlockSpec(shape, lambda b, i: tuple(0 for _ in shape))
    row = lambda width: pl.BlockSpec((1, tm, width), lambda b, i: (b, i, 0))
    col = lambda rows: pl.BlockSpec((1, rows, tm), lambda b, i: (b, 0, i))
    out_defs = [
        ("q", (t, ATT_WIDTH), BF16, row(ATT_WIDTH)),
        ("k", (t, KV_WIDTH), F32, row(KV_WIDTH)),
        ("kbf", (t, KV_WIDTH), BF16, row(KV_WIDTH)),
        ("v", (t, KV_WIDTH), F32, row(KV_WIDTH)),
        ("vt", (KV_WIDTH, t), BF16, col(KV_WIDTH)),
        ("qi", (t, N_IDX_HEADS * IDX_DIM), BF16, row(N_IDX_HEADS * IDX_DIM)),
        ("ki", (t, IDX_DIM), F32, row(IDX_DIM)),
        ("kibf", (t, IDX_DIM), BF16, row(IDX_DIM)),
        ("wit", (N_IDX_HEADS, t), F32, col(N_IDX_HEADS)),
        ("sg", (t, ATT_WIDTH), F32, row(ATT_WIDTH)),
        ("rkv", (t, 3 * RWKV_WIDTH), F32, row(3 * RWKV_WIDTH)),
        ("lora", (t, 2 * LORA), F32, row(2 * LORA)),
        ("gr", (t, RWKV_WIDTH), F32, row(RWKV_WIDTH)),
    ]
    outs = pl.pallas_call(
        _inproj_kernel,
        grid=grid,
        in_specs=[pl.BlockSpec((1, tm, d), lambda b, i: (b, i, 0)), mod_spec, mod_spec, const((1, d)),
                  const(wts["w_main"].shape), const(wts["w_wit"].shape), const(wts["w_vt"].shape),
                  tab_spec, tab_spec, tab_spec, const((1, LANES)), const((1, LANES))],
        out_specs=[o[3] for o in out_defs],
        out_shape=[jax.ShapeDtypeStruct((nb,) + o[1], o[2]) for o in out_defs],
        compiler_params=_cparams(("arbitrary", "arbitrary")),
        name="in_proj",
    )(x3, shift3, scale3, wts["norm_w"], wts["w_main"], wts["w_wit"], wts["w_vt"], tabs[0], tabs[1], tabs[2],
      wts["qnw"], wts["knw"])
    return {o[0]: arr for o, arr in zip(out_defs, outs)}


def _rope_tables(pos):
    half = ROPE_DIMS // 2
    inv = jnp.power(ROPE_THETA, -jnp.arange(half, dtype=F32) / half)
    ang = pos.astype(F32)[:, None] * inv[None, :]
    cos, sin = jnp.cos(ang), jnp.sin(ang)
    n = pos.shape[0]
    pad = jnp.zeros((n, HEAD_DIM - ROPE_DIMS), F32)
    zero = jnp.zeros((n, half), F32)
    c_head = jnp.concatenate([cos, cos, pad + 1.0], axis=1)
    s1_head = jnp.concatenate([-sin, zero, pad], axis=1)
    s2_head = jnp.concatenate([zero, sin, pad], axis=1)
    two = lambda a: jnp.concatenate([a, a], axis=1)
    return two(c_head), two(s1_head), two(s2_head)


def _head_sum(t):
    ones = _head_block_ones(LANES)
    parts = [_sum_dot(t[:, s * LANES:(s + 1) * LANES], ones) for s in range(t.shape[1] // LANES)]
    return parts[0] if len(parts) == 1 else jnp.concatenate(parts, axis=1)


def _softplus(z):
    return jnp.maximum(z, 0.0) + jnp.log(1.0 + jnp.exp(-jnp.abs(z)))


def _rwkv_prep(xs, xl, prev, prevl, p):
    xm = xs + p["mu_rkv"] * (prev - xs)
    xml = xl + p["mu_lora"] * (prevl - xl)
    w = RWKV_WIDTH
    r, k, v = xm[:, :w], xm[:, w:2 * w], xm[:, 2 * w:]
    lane = _iota(xml.shape, 1)
    lhs = jnp.where(lane < LORA, jnp.tanh(xml), xml)
    lo = _dot(lhs.astype(BF16), p["w_lora"])
    wlog = -_softplus(-(p["w0"] + lo[:, :w])) - 0.5
    logw = -jnp.exp(wlog)
    a = jax.nn.sigmoid(p["a0"] + lo[:, w:])
    kkr = k * p["k_k"]
    kk = kkr / jnp.maximum(jnp.sqrt(_head_sum(kkr * kkr)), 1e-12)
    k_mod = k * (1.0 + (a - 1.0) * p["k_a"])
    bonus = _head_sum(r * k_mod * p["r_k"]) * v
    return r, logw, k_mod, v, -kk, kk * a, bonus


def _rwkv_epilogue(y, bonus, gate, p):
    mean = _head_sum(y) * (1.0 / HEAD_DIM)
    d = y - mean
    var = _head_sum(d * d) * (1.0 / HEAD_DIM)
    yn = d * lax.rsqrt(var + GN_EPS) * p["ln_w"] + p["ln_b"]
    return (yn + bonus) * _silu(gate)


_RW_PARAM_NAMES = ("mu_rkv", "mu_lora", "w0", "a0", "w_lora", "k_k", "k_a", "r_k", "ln_w", "ln_b")


def _load_params(refs):
    return {n: r[...] for n, r in zip(_RW_PARAM_NAMES, refs)}


def _rwkv_prompt_kernel(rkv_ref, lora_ref, gr_ref, *rest):
    nparam = len(_RW_PARAM_NAMES)
    p = _load_params(rest[:nparam])
    out_ref, sfin_ref = rest[nparam:nparam + 2]
    (st_s, c_rkv, c_lora, r_s, lw_s, k_s, v_s, al_s, be_s, y_s, bon_s,
     at_s, rt_s, bk_s, ec_s, wl_s, wh_s, zv_s, inv_s) = rest[nparam + 2:]
    i = pl.program_id(0)
    nb, tb = rkv_ref.shape[0], rkv_ref.shape[1]
    C = RW_CHUNK
    W = 2 * C

    @pl.when(i == 0)
    def _():
        st_s[...] = jnp.zeros_like(st_s)
        c_rkv[...] = jnp.zeros_like(c_rkv)
        c_lora[...] = jnp.zeros_like(c_lora)

    row = _iota((tb, 1), 0)
    for b in range(nb):
        xs, xl = rkv_ref[b], lora_ref[b]
        prev = jnp.where(row == 0, c_rkv[b], pltpu.roll(xs, 1, 0))
        prevl = jnp.where(row == 0, c_lora[b], pltpu.roll(xl, 1, 0))
        c_rkv[b] = xs[tb - 1:tb, :]
        c_lora[b] = xl[tb - 1:tb, :]
        r_s[b], lw_s[b], k_s[b], v_s[b], al_s[b], be_s[b], bon_s[b] = _rwkv_prep(xs, xl, prev, prevl, p)

    rm, cn = _iota((W, W), 0), _iota((W, W), 1)
    tt, ss = rm % C, cn % C
    top, left = rm < C, cn < C
    strict, incl = ss < tt, ss <= tt
    mk_l_lo, mk_l_hi = top & left & strict, (~top) & (~left) & strict
    mk_z_lo, mk_z_hi = top & (~left) & strict, (~top) & left & strict
    mk_w = (_iota((C, W), 1) % C) <= _iota((C, W), 0)
    bd = (rm // HEAD_DIM) == (cn // HEAD_DIM)
    eye = jnp.where(rm == cn, 1.0, 0.0).astype(F32)
    tril = jnp.where(_iota((C, C), 1) <= _iota((C, C), 0), 1.0, 0.0).astype(BF16)
    lane_lo = _iota((C, LANES), 1) < HEAD_DIM
    m_lo = jnp.where(lane_lo, 1.0, 0.0).astype(F32)
    m_hi = 1.0 - m_lo

    bf = lambda a: a.astype(BF16)
    pairs = range(N_RWKV_HEADS // 2)
    lanes_of = lambda pr: slice(pr * LANES, (pr + 1) * LANES)
    rows_of = lambda c: pl.ds(pl.multiple_of(c * C, C), C)
    items = [(b, pr) for b in range(nb) for pr in pairs]

    def phase_a(c, carry):
        rows = rows_of(c)
        cums = []
        for b in range(nb):
            cum_all = _sum_dot_left(tril, lw_s[b, rows, :])
            cums += [cum_all[:, lanes_of(pr)] for pr in pairs]
        g_lo, g_hi = [], []
        for n, (b, pr) in enumerate(items):
            sl = lanes_of(pr)
            cum = cums[n]
            cum_c = cum[C - 1:C, :]
            e_neg, e_rel = jnp.exp(-cum), jnp.exp(cum_c - cum)
            rr, kk, aa, bb = r_s[b, rows, sl], k_s[b, rows, sl], al_s[b, rows, sl], be_s[b, rows, sl]
            a_t, r_t = aa * jnp.exp(cum - lw_s[b, rows, sl]), rr * jnp.exp(cum)
            b_t, k_t = bb * e_neg, kk * e_neg
            at_s[b, c, pr], rt_s[b, c, pr] = bf(a_t), bf(r_t)
            bk_s[b, c, pr] = jnp.concatenate([bb * e_rel, kk * e_rel], axis=0)
            ec_s[b, c, pr] = jnp.exp(cum_c)
            xk = bf(jnp.concatenate([b_t, k_t], axis=0))
            kx = bf(jnp.concatenate([k_t, b_t], axis=0))
            g_lo.append(_dot_nt(bf(jnp.concatenate([a_t * m_lo, r_t * m_lo], axis=0)), xk))
            g_hi.append(_dot_nt(bf(jnp.concatenate([r_t * m_hi, a_t * m_hi], axis=0)), kx))
        l_mat = []
        for n, (b, pr) in enumerate(items):
            l_mat.append(jnp.where(mk_l_lo, g_lo[n], 0.0) + jnp.where(mk_l_hi, g_hi[n], 0.0))
            z_mat = jnp.where(mk_z_lo, g_lo[n], 0.0) + jnp.where(mk_z_hi, g_hi[n], 0.0)
            wl_s[b, c, pr] = bf(jnp.where(mk_w, g_lo[n][C:, :], 0.0))
            wh_s[b, c, pr] = bf(jnp.where(mk_w, g_hi[n][:C, :], 0.0))
            vv_b = bf(v_s[b, rows, lanes_of(pr)])
            zv_s[b, c, pr] = _dot(bf(z_mat), jnp.concatenate([vv_b, vv_b], axis=0))
        inv = [eye + m for m in l_mat]
        pw = l_mat
        for _ in range(5):
            pw = [_dot(bf(m), bf(m)) for m in pw]
            inv = [x + _dot(bf(x), bf(m)) for x, m in zip(inv, pw)]
        for n, (b, pr) in enumerate(items):
            inv_s[b, c, pr] = bf(inv[n])
        return carry

    lax.fori_loop(0, tb // C, phase_a, 0)

    def phase_b(c, carry):
        rows = rows_of(c)
        nit = range(len(items))
        st = [st_s[b, pr] for b, pr in items]
        st_b = [bf(s) for s in st]
        vv = [v_s[b, rows, lanes_of(pr)] for b, pr in items]
        a_s = [_dot_nt(at_s[b, c, pr], st_b[n]) for n, (b, pr) in enumerate(items)]
        u_st = [_dot(inv_s[b, c, pr], bf(zv_s[b, c, pr] + jnp.concatenate([a_s[n], a_s[n]], axis=0)))
                for n, (b, pr) in enumerate(items)]
        u = [jnp.where(lane_lo, x[:C], x[C:]) for x in u_st]
        uv = [jnp.concatenate([u[n], vv[n]], axis=0) for n in nit]
        upd = [_dot_tn(uv[n], bk_s[b, c, pr]) for n, (b, pr) in enumerate(items)]
        for n, (b, pr) in enumerate(items):
            st_s[b, pr] = st[n] * ec_s[b, c, pr] + jnp.where(bd, upd[n], 0.0)
        for n, (b, pr) in enumerate(items):
            uv_b = bf(uv[n])
            vu_b = jnp.concatenate([uv_b[C:], uv_b[:C]], axis=0)
            y_in = jnp.where(lane_lo, _dot(wl_s[b, c, pr], uv_b), _dot(wh_s[b, c, pr], vu_b))
            y_s[b, rows, lanes_of(pr)] = _dot_nt(rt_s[b, c, pr], st_b[n]) + y_in
        return carry

    lax.fori_loop(0, tb // C, phase_b, 0)
    for b in range(nb):
        out_ref[b] = _rwkv_epilogue(y_s[b], bon_s[b], gr_ref[b], p)

    @pl.when(i == pl.num_programs(0) - 1)
    def _():
        for b in range(nb):
            for hd in range(N_RWKV_HEADS):
                o = (hd % 2) * HEAD_DIM
                sfin_ref[b, hd] = st_s[b, hd // 2][o:o + HEAD_DIM, o:o + HEAD_DIM]


def _rwkv_prompt(rkv, lora, gr, params):
    nb, t, _ = rkv.shape
    tb = min(RW_BLOCK, t)
    w = RWKV_WIDTH
    blk = lambda width: pl.BlockSpec((nb, tb, width), lambda i: (0, i, 0))
    pspecs = [pl.BlockSpec(params[n].shape, lambda i: (0, 0)) for n in _RW_PARAM_NAMES]
    vec = lambda: pltpu.VMEM((nb, tb, w), F32)
    nch, npair, c = tb // RW_CHUNK, N_RWKV_HEADS // 2, RW_CHUNK
    per_chunk = lambda rows, dt: pltpu.VMEM((nb, nch, npair, rows, LANES), dt)
    state_shape = (nb, N_RWKV_HEADS, HEAD_DIM, HEAD_DIM)
    return pl.pallas_call(
        _rwkv_prompt_kernel,
        grid=(t // tb,),
        in_specs=[blk(3 * w), blk(2 * LORA), blk(w)] + pspecs,
        out_specs=[blk(w), pl.BlockSpec(state_shape, lambda i: (0, 0, 0, 0))],
        out_shape=[jax.ShapeDtypeStruct((nb, t, w), F32), jax.ShapeDtypeStruct(state_shape, F32)],
        scratch_shapes=[pltpu.VMEM((nb, npair, LANES, LANES), F32),
                        pltpu.VMEM((nb, 1, 3 * w), F32), pltpu.VMEM((nb, 1, 2 * LORA), F32),
                        vec(), vec(), vec(), vec(), vec(), vec(), vec(), vec(),
                        per_chunk(c, BF16), per_chunk(c, BF16), per_chunk(2 * c, F32), per_chunk(1, F32),
                        per_chunk(c, BF16), per_chunk(c, BF16), per_chunk(2 * c, F32), per_chunk(2 * c, BF16)],
        compiler_params=_cparams(("arbitrary",)),
        name="rwkv_prompt",
    )(rkv, lora, gr, *[params[n] for n in _RW_PARAM_NAMES])


def _rwkv_dec_prep_kernel(rkv_ref, lora_ref, prev_ref, prevl_ref, *rest):
    nparam = len(_RW_PARAM_NAMES)
    p = _load_params(rest[:nparam])
    outs = rest[nparam:]
    r, logw, k_mod, v, al, be, bonus = _rwkv_prep(rkv_ref[...], lora_ref[...], prev_ref[...], prevl_ref[...], p)
    for o, val in zip(outs[:6], (r, jnp.exp(logw), k_mod, v, al, be)):
        o[...] = val.T
    outs[6][...] = bonus


def _rwkv_dec_prep(rkv, lora, prev, prevl, params):
    n = rkv.shape[0]
    full = lambda a: pl.BlockSpec(a.shape, lambda: tuple(0 for _ in a.shape))
    args = [rkv, lora, prev, prevl] + [params[k] for k in _RW_PARAM_NAMES]
    t_shape, shape = (RWKV_WIDTH, n), (n, RWKV_WIDTH)
    return pl.pallas_call(
        _rwkv_dec_prep_kernel,
        in_specs=[full(a) for a in args],
        out_specs=[pl.BlockSpec(t_shape, lambda: (0, 0))] * 6 + [pl.BlockSpec(shape, lambda: (0, 0))],
        out_shape=[jax.ShapeDtypeStruct(t_shape, F32)] * 6 + [jax.ShapeDtypeStruct(shape, F32)],
        compiler_params=pltpu.CompilerParams(vmem_limit_bytes=VMEM_LIMIT),
        name="rwkv_dec_prep",
    )(*args)


def _rwkv_dec_state_kernel(r_ref, w_ref, k_ref, v_ref, a_ref, b_ref, s_ref, so_ref, y_ref):
    r, w, k, a, b = r_ref[0], w_ref[0], k_ref[0], a_ref[0], b_ref[0]

    def one(i, carry):
        st = s_ref[0, i]
        sa = jnp.sum(st * a, axis=0, keepdims=True)
        sn = st * w + sa * b + v_ref[0, pl.ds(i, 1), :] * k
        so_ref[0, i] = sn
        y_ref[0, pl.ds(i, 1), :] = jnp.sum(sn * r, axis=0, keepdims=True)
        return carry

    lax.fori_loop(0, HEAD_DIM, one, 0)


def _rwkv_dec_state(vecs_t, state_t):
    nh, n, _, nb = state_t.shape
    vspec = pl.BlockSpec((1, n, nb), lambda h: (h, 0, 0))
    sspec = pl.BlockSpec((1, n, n, nb), lambda h: (h, 0, 0, 0))
    return pl.pallas_call(
        _rwkv_dec_state_kernel,
        grid=(nh,),
        in_specs=[vspec] * 6 + [sspec],
        out_specs=[sspec, vspec],
        out_shape=[jax.ShapeDtypeStruct(state_t.shape, F32), jax.ShapeDtypeStruct((nh, n, nb), F32)],
        compiler_params=_cparams(("arbitrary",)),
        name="rwkv_dec_state",
    )(*[a.reshape(nh, n, nb) for a in vecs_t], state_t)


def _rwkv_dec_epi_kernel(yt_ref, bonus_ref, gr_ref, lnw_ref, lnb_ref, o_ref):
    o_ref[...] = _rwkv_epilogue(yt_ref[...].T, bonus_ref[...], gr_ref[...],
                                {"ln_w": lnw_ref[...], "ln_b": lnb_ref[...]})


def _rwkv_dec_epi(y_t, bonus, gr, params):
    n = bonus.shape[0]
    args = [y_t, bonus, gr, params["ln_w"], params["ln_b"]]
    return pl.pallas_call(
        _rwkv_dec_epi_kernel,
        in_specs=[pl.BlockSpec(a.shape, lambda: (0, 0)) for a in args],
        out_specs=pl.BlockSpec((n, RWKV_WIDTH), lambda: (0, 0)),
        out_shape=jax.ShapeDtypeStruct((n, RWKV_WIDTH), F32),
        name="rwkv_dec_epi",
    )(*args)


def _attn_prompt_kernel(pt_ref, q_ref, qi_ref, wit_ref, k_ref, vt_ref, kidx_ref, sg_ref, tri_ref,
                        dqi_ref, dwi_ref, dkidx_hbm, o_ref, dsc_ref,
                        s_scr, m_scr, acc_scr, *dscratch, topk):
    j = pl.program_id(1)
    step = pl.program_id(0) * pl.num_programs(1) + j
    nsteps = pl.num_programs(0) * pl.num_programs(1)
    spp = dqi_ref.shape[0]
    for u in range(spp):
        pages = _PageFetcher(pt_ref, dkidx_hbm, dscratch[u], dscratch[spp + u], step, nsteps, stride=spp, offset=u)
        _score_pages(pages, dqi_ref, dwi_ref, dsc_ref, u, (step % (SUBLANES // spp)) * spp + u)
    QB, KC = Q_BLOCK, KEY_CHUNK
    nch = (j * QB) // KC + 1
    inf = jnp.inf
    qpos_i = j * QB + _iota((1, QB), 1)
    k_q = jnp.minimum(float(topk), qpos_i.astype(F32) + 1.0)
    wit = wit_ref[0]
    qi = qi_ref[0]
    qi_stack = jnp.concatenate([qi[:, h * IDX_DIM:(h + 1) * IDX_DIM] for h in range(N_IDX_HEADS)], axis=0)

    def chunk_rows(c):
        return pl.ds(pl.multiple_of(c * KC, KC), KC)

    SUB = 32
    part = lambda a: a.reshape(KC // SUB, SUB, QB)
    psum = lambda a: jnp.sum(part(a), axis=0)
    pmin = lambda a: jnp.min(part(a), axis=0)
    pmax = lambda a: jnp.max(part(a), axis=0)
    full8 = lambda val: jnp.full((SUB, QB), val, F32)
    fold_sum = lambda a: jnp.sum(a, axis=0, keepdims=True)
    fold_min = lambda a: jnp.min(a, axis=0, keepdims=True)
    fold_max = lambda a: jnp.max(a, axis=0, keepdims=True)

    def score_chunk(c, lohi):
        rows = chunk_rows(c)
        st = _dot_nt(kidx_ref[0, rows, :], qi_stack)
        acc = jnp.zeros((KC, QB), F32)
        for h in range(N_IDX_HEADS):
            acc = acc + jnp.maximum(st[:, h * QB:(h + 1) * QB], 0.0) * wit[h:h + 1, :]
        ok = (c * KC + _iota((KC, QB), 0)) <= qpos_i
        s_scr[rows, :] = jnp.where(ok, acc, -inf)
        return (jnp.minimum(lohi[0], pmin(jnp.where(ok, acc, inf))),
                jnp.maximum(lohi[1], pmax(jnp.where(ok, acc, -inf))))

    lo8, hi8 = lax.fori_loop(0, nch, score_chunk, (full8(inf), full8(-inf)))
    lo, hi = fold_min(lo8), fold_max(hi8)

    def sweep(fn, init):
        return lax.fori_loop(0, nch, lambda c, car: fn(car, s_scr[chunk_rows(c), :]), init)

    def bisect(_, lh):
        lo, hi = lh
        mid = 0.5 * lo + 0.5 * hi
        cnt = fold_sum(sweep(lambda c, s: c + psum(jnp.where(s >= mid, 1.0, 0.0)), full8(0.0)))
        ge = cnt >= k_q
        return jnp.where(ge, mid, lo), jnp.where(ge, hi, mid)

    lo, hi = lax.fori_loop(0, N_BISECT, bisect, (lo, hi))
    v0 = fold_min(sweep(lambda c, s: jnp.minimum(c, pmin(jnp.where(s >= lo, s, inf))), full8(inf)))

    def refine(state):
        v, _, _ = state

        def f(car, s):
            g = s > v
            return car[0] + psum(jnp.where(g, 1.0, 0.0)), jnp.minimum(car[1], pmin(jnp.where(g, s, inf)))

        cnt8, v28 = sweep(f, (full8(0.0), full8(inf)))
        cnt, v2 = fold_sum(cnt8), fold_min(v28)
        ok = cnt < k_q
        return jnp.where(ok, v, v2), cnt, jnp.sum(jnp.where(ok, 0.0, 1.0))

    full = lambda val: jnp.full((1, QB), val, F32)
    thr, c_gt, _ = lax.while_loop(lambda st: st[2] > 0.0, refine, (v0, full(0.0), jnp.float32(1.0)))
    need = k_q - c_gt

    q = q_ref[0]
    G, R = N_KV_HEADS, N_Q_HEADS // N_KV_HEADS
    q_stack = [jnp.concatenate([q[:, (g * R + r) * HEAD_DIM:(g * R + r + 1) * HEAD_DIM] for r in range(R)], axis=0)
               for g in range(G)]
    ones_rows = jnp.where(_iota((HEAD_DIM, KC), 0) == 0, 1.0, 0.0).astype(BF16)
    m_scr[...] = jnp.full(m_scr.shape, -inf, F32)
    acc_scr[...] = jnp.zeros_like(acc_scr)

    n_ge = fold_sum(sweep(lambda c, s: c + psum(jnp.where(s >= thr, 1.0, 0.0)), full8(0.0)))
    any_ties = jnp.sum(jnp.where(n_ge == k_q, 0.0, 1.0)) > 0.0

    def make_attend(with_ties):
        def attend(c, tie_carry):
            rows = chunk_rows(c)
            s = s_scr[rows, :]
            if with_ties:
                eq = s == thr
                rank = tie_carry + _dot(tri_ref[...], jnp.where(eq, 1.0, 0.0).astype(BF16))
                sel = (s > thr) | (eq & (rank < need))
                tie_carry = tie_carry + fold_sum(psum(jnp.where(eq, 1.0, 0.0)))
            else:
                sel = s >= thr
            bias = jnp.where(sel, 0.0, -inf).astype(BF16)
            bias_r = jnp.concatenate([bias] * R, axis=1)
            kc = k_ref[0, rows, :]
            logits = [_dot_nt(kc[:, g * HEAD_DIM:(g + 1) * HEAD_DIM], q_stack[g]) for g in range(G)]
            probs, alphas = [], []
            for g in range(G):
                lm = logits[g].astype(BF16) + bias_r
                m_old = m_scr[g]
                m_part = jnp.max(lm.reshape(KC // 16, 16, R * QB), axis=0).astype(F32)
                m_new = jnp.maximum(m_old, jnp.max(m_part, axis=0, keepdims=True))
                m_safe = jnp.where(m_new == -inf, 0.0, m_new)
                alphas.append(jnp.exp2(m_old - m_safe))
                probs.append(jnp.exp2(lm - m_safe.astype(BF16)))
                m_scr[g] = m_new
            for g in range(G):
                vt_ext = jnp.concatenate([vt_ref[0, g * HEAD_DIM:(g + 1) * HEAD_DIM, rows], ones_rows], axis=0)
                acc_scr[g] = acc_scr[g] * alphas[g] + _dot(vt_ext, probs[g])
            return tie_carry
        return attend

    @pl.when(any_ties)
    def _():
        lax.fori_loop(0, nch, make_attend(True), full(0.0))

    @pl.when(jnp.logical_not(any_ties))
    def _():
        lax.fori_loop(0, nch, make_attend(False), full(0.0))

    out_t = jnp.concatenate([acc_scr[g][:HEAD_DIM] / acc_scr[g][HEAD_DIM:HEAD_DIM + 1] for g in range(G)], axis=0)
    per_r = [out_t[:, r * QB:(r + 1) * QB].T for r in range(R)]
    att = jnp.concatenate([per_r[r][:, g * HEAD_DIM:(g + 1) * HEAD_DIM] for g in range(G) for r in range(R)], axis=1)
    o_ref[0] = att * sg_ref[0]


def _attn_prompt(q, qi, wit, kbf, vt, kibf, sg, page_table, dqi3, dwi3, dkidx_t):
    nb, t, _ = q.shape
    nj = t // Q_BLOCK
    nd, n_pages = page_table.shape
    spp = nd // (nb * nj)
    assert spp * nb * nj == nd and SUBLANES % spp == 0
    topk = min(TOPK_MAX, t // 4)
    kc = KEY_CHUNK
    tri = jnp.where(jnp.arange(kc)[None, :] < jnp.arange(kc)[:, None], 1.0, 0.0).astype(BF16)
    qblk = lambda width: pl.BlockSpec((1, Q_BLOCK, width), lambda b, j, pt: (b, j, 0))
    whole = lambda r, c: pl.BlockSpec((1, r, c), lambda b, j, pt: (b, 0, 0))
    per_step = lambda r, c: pl.BlockSpec((spp, r, c), lambda b, j, pt: (b * nj + j, 0, 0))
    grid_spec = pltpu.PrefetchScalarGridSpec(
        num_scalar_prefetch=1,
        grid=(nb, nj),
        in_specs=[qblk(ATT_WIDTH), qblk(N_IDX_HEADS * IDX_DIM),
                  pl.BlockSpec((1, N_IDX_HEADS, Q_BLOCK), lambda b, j, pt: (b, 0, j)),
                  whole(t, KV_WIDTH), whole(KV_WIDTH, t), whole(t, IDX_DIM), qblk(ATT_WIDTH),
                  pl.BlockSpec((kc, kc), lambda b, j, pt: (0, 0)),
                  per_step(N_IDX_HEADS, IDX_DIM), per_step(N_IDX_HEADS, 1), pl.BlockSpec(memory_space=pl.ANY)],
        out_specs=[qblk(ATT_WIDTH),
                   pl.BlockSpec((SUBLANES, n_pages * PAGE_SIZE), lambda b, j, pt: ((b * nj + j) // (SUBLANES // spp), 0))],
        scratch_shapes=[pltpu.VMEM((t, Q_BLOCK), F32),
                        pltpu.VMEM((N_KV_HEADS, 1, N_Q_HEADS // N_KV_HEADS * Q_BLOCK), F32),
                        pltpu.VMEM((N_KV_HEADS, 2 * HEAD_DIM, N_Q_HEADS // N_KV_HEADS * Q_BLOCK), F32)]
                       + [pltpu.VMEM((2, n_pages, IDX_DIM, PAGE_SIZE), F32)] * spp
                       + [pltpu.SemaphoreType.DMA((2,))] * spp,
    )
    return pl.pallas_call(
        functools.partial(_attn_prompt_kernel, topk=topk),
        grid_spec=grid_spec,
        out_shape=[jax.ShapeDtypeStruct((nb, t, ATT_WIDTH), F32),
                   jax.ShapeDtypeStruct((nd, n_pages * PAGE_SIZE), F32)],
        compiler_params=_cparams(("arbitrary", "arbitrary")),
        name="attn_prompt",
    )(page_table, q, qi, wit, kbf, vt, kibf, sg, tri, dqi3, dwi3, dkidx_t)


def _outproj_kernel(att_ref, rw_ref, w_ref, x_ref, g_ref, o_ref):
    mix = _dot(att_ref[0].astype(BF16), w_ref[:ATT_WIDTH, :]) + _dot(rw_ref[0].astype(BF16), w_ref[ATT_WIDTH:, :])
    o_ref[0] = x_ref[0] + g_ref[0] * mix


def _out_proj(att, rw, w_out_bf, x3, gate3, tm):
    nb, t, d = x3.shape
    if gate3.shape[1] == 1:
        gspec = pl.BlockSpec((1, 1, d), lambda b, i: (b, 0, 0))
    else:
        gspec = pl.BlockSpec((1, tm, d), lambda b, i: (b, i, 0))
    blk = lambda width: pl.BlockSpec((1, tm, width), lambda b, i: (b, i, 0))
    return pl.pallas_call(
        _outproj_kernel,
        grid=(nb, t // tm),
        in_specs=[blk(ATT_WIDTH), blk(RWKV_WIDTH), pl.BlockSpec(w_out_bf.shape, lambda b, i: (0, 0)), blk(d), gspec],
        out_specs=blk(d),
        out_shape=jax.ShapeDtypeStruct((nb, t, d), F32),
        compiler_params=_cparams(("arbitrary", "arbitrary")),
        name="out_proj",
    )(att, rw, w_out_bf, x3, gate3)


class _PageFetcher:
    def __init__(self, pt_ref, hbm, buf, sem, step=None, nsteps=None, stride=1, offset=0):
        self.pt, self.hbm, self.buf, self.sem = pt_ref, hbm, buf, sem
        self.n_pages = buf.shape[1]
        b = pl.program_id(0) if step is None else step
        nb = pl.num_programs(0) if nsteps is None else nsteps
        self.first, self.last = b == 0, b == nb - 1
        self.slot = b % 2
        self.first_seq = offset
        self.next_seq = jnp.minimum(b + 1, nb - 1) * stride + offset

    def _copy(self, seq, j, slot):
        return pltpu.make_async_copy(self.hbm.at[self.pt[seq, j]], self.buf.at[slot, j], self.sem.at[slot])

    def _wait(self, slot):
        pltpu.make_async_copy(self.hbm.at[pl.ds(0, self.n_pages)], self.buf.at[slot], self.sem.at[slot]).wait()

    def begin(self):
        @pl.when(self.first)
        def _():
            for j in range(self.n_pages):
                self._copy(self.first_seq, j, 0).start()
        self._wait(self.slot)

    def prefetch(self, j):
        self._copy(self.next_seq, j, 1 - self.slot).start(priority=j % 2)

    def page(self, j):
        return self.buf[self.slot, j]

    def end(self):
        @pl.when(self.last)
        def _():
            self._wait(1 - self.slot)


def _score_pages(pages, qi_ref, wi_ref, o_ref, u, out_row):
    pages.begin()
    qi = qi_ref[u]
    qi16 = jnp.concatenate([qi, jnp.zeros_like(qi)], axis=0).astype(BF16)
    wi = wi_ref[u]
    parts = []
    for j in range(pages.n_pages):
        pages.prefetch(j)
        s = _dot(qi16, pages.page(j).astype(BF16))[:N_IDX_HEADS]
        parts.append(jnp.sum(jnp.maximum(s, 0.0) * wi, axis=0, keepdims=True))
    o_ref[pl.ds(out_row, 1), :] = jnp.concatenate(parts, axis=1)
    pages.end()


def _dec_select_kernel(s_ref, qi_ref, ki_ref, wi_ref, sel_ref, sx, *, topk):
    nb, past = s_ref.shape
    inf = jnp.inf
    qi = qi_ref[...].astype(F32)
    ki = ki_ref[...].astype(F32)
    prod = qi * jnp.concatenate([ki] * N_IDX_HEADS, axis=1)
    nq = N_IDX_HEADS * IDX_DIM
    hsel = jnp.where(_iota((nq, LANES), 0) // IDX_DIM == _iota((nq, LANES), 1), 1.0, 0.0).astype(BF16)
    sh = _sum_dot(prod, hsel)[:, :N_IDX_HEADS]
    s_new = jnp.sum(jnp.maximum(sh, 0.0) * wi_ref[...], axis=1, keepdims=True)
    sx[:, :past] = s_ref[...]
    sx[:, past:] = jnp.where(_iota((nb, LANES), 1) == 0, s_new, -inf)

    rsum = lambda a: jnp.sum(a, axis=1, keepdims=True)
    k_q = float(topk)
    s = sx[...]
    lo = jnp.min(jnp.where(s > -inf, s, inf), axis=1, keepdims=True)
    hi = jnp.max(s, axis=1, keepdims=True)

    def bisect(_, lh):
        lo, hi = lh
        mid = 0.5 * lo + 0.5 * hi
        ge = rsum(jnp.where(sx[...] >= mid, 1.0, 0.0)) >= k_q
        return jnp.where(ge, mid, lo), jnp.where(ge, hi, mid)

    lo, hi = lax.fori_loop(0, N_BISECT, bisect, (lo, hi))
    s = sx[...]
    v0 = jnp.min(jnp.where(s >= lo, s, inf), axis=1, keepdims=True)

    def refine(state):
        v, _, _ = state
        s = sx[...]
        g = s > v
        cnt = rsum(jnp.where(g, 1.0, 0.0))
        v2 = jnp.min(jnp.where(g, s, inf), axis=1, keepdims=True)
        ok = cnt < k_q
        return jnp.where(ok, v, v2), cnt, jnp.sum(jnp.where(ok, 0.0, 1.0))

    thr, c_gt, _ = lax.while_loop(lambda st: st[2] > 0.0, refine,
                                  (v0, jnp.zeros((nb, 1), F32), jnp.float32(1.0)))
    need = k_q - c_gt
    upper = jnp.where(_iota((LANES, LANES), 0) < _iota((LANES, LANES), 1), 1.0, 0.0).astype(BF16)
    carry = jnp.zeros((nb, 1), F32)
    for c in range((past + LANES) // LANES):
        sc = sx[:, c * LANES:(c + 1) * LANES]
        eq = sc == thr
        eqf = jnp.where(eq, 1.0, 0.0)
        rank = carry + _dot(eqf.astype(BF16), upper)
        sel_ref[:, c * LANES:(c + 1) * LANES] = jnp.where((sc > thr) | (eq & (rank < need)), 1.0, 0.0)
        carry = carry + rsum(eqf)


def _dec_select(scores, qi, ki, wi, topk):
    nb, past = scores.shape
    args = [scores, qi, ki, wi]
    return pl.pallas_call(
        functools.partial(_dec_select_kernel, topk=topk),
        in_specs=[pl.BlockSpec(a.shape, lambda: (0, 0)) for a in args],
        out_specs=pl.BlockSpec((nb, past + LANES), lambda: (0, 0)),
        out_shape=jax.ShapeDtypeStruct((nb, past + LANES), F32),
        scratch_shapes=[pltpu.VMEM((nb, past + LANES), F32)],
        compiler_params=pltpu.CompilerParams(vmem_limit_bytes=VMEM_LIMIT),
        name="dec_select",
    )(*args)


def _dec_attn_kernel(pt_ref, q_ref, sel_ref, knew_ref, vnew_ref, sg_ref, k_hbm, v_hbm, o_ref,
                     kbuf, vbuf, ksem, vsem):
    k_pages = _PageFetcher(pt_ref, k_hbm, kbuf, ksem)
    v_pages = _PageFetcher(pt_ref, v_hbm, vbuf, vsem)
    n_pages = k_pages.n_pages
    past = n_pages * PAGE_SIZE
    inf = jnp.inf
    nh = N_Q_HEADS
    R = N_Q_HEADS // N_KV_HEADS

    q8 = q_ref[0].astype(F32)
    own = (_iota((nh, KV_WIDTH), 0) // R) == (_iota((nh, KV_WIDTH), 1) // HEAD_DIM)
    q_ext = jnp.where(own, jnp.concatenate([q8, q8], axis=1), 0.0)
    q16 = jnp.concatenate([q_ext, jnp.zeros_like(q_ext)], axis=0).astype(BF16)
    sel = sel_ref[pl.ds(pl.program_id(0) % SUBLANES, 1), :]

    k_pages.begin()
    parts = []
    for j in range(n_pages):
        k_pages.prefetch(j)
        parts.append(_dot(q16, k_pages.page(j).astype(BF16))[:nh])
    k_pages.end()
    k_new = knew_ref[0].astype(BF16).astype(F32)
    v_new = vnew_ref[0].astype(BF16).astype(F32)
    lg_new = jnp.sum(q16[:nh].astype(F32) * k_new, axis=1, keepdims=True)
    lm = jnp.where(sel[:, :past] > 0.5, jnp.concatenate(parts, axis=1), -inf)
    lm_new = jnp.where(sel[:, past:past + 1] > 0.5, lg_new, -inf)
    m = jnp.maximum(jnp.max(lm, axis=1, keepdims=True), lm_new)
    p = jnp.exp2(lm - m)
    p_new = jnp.exp2(lm_new - m)
    l = jnp.sum(p, axis=1, keepdims=True) + p_new
    p16 = jnp.concatenate([p, jnp.zeros_like(p)], axis=0).astype(BF16)

    v_pages.begin()
    acc = jnp.zeros((2 * nh, KV_WIDTH), F32)
    for j in range(n_pages):
        v_pages.prefetch(j)
        acc = acc + _dot_nt(p16[:, j * PAGE_SIZE:(j + 1) * PAGE_SIZE], v_pages.page(j).astype(BF16))
    v_pages.end()
    out = (acc[:nh] + p_new.astype(BF16).astype(F32) * v_new) / l
    res = jnp.where(_iota((nh, HEAD_DIM), 0) < R, out[:, :HEAD_DIM], out[:, HEAD_DIM:])
    o_ref[0] = res * sg_ref[0]


def _dec_attn(page_table, q3, sel, k_new3, v_new3, sg3, cache_k2, cache_v2):
    nb, n_pages = page_table.shape
    per_seq = lambda r, c: pl.BlockSpec((1, r, c), lambda b, pt: (b, 0, 0))
    page_buf = pltpu.VMEM((2, n_pages, KV_WIDTH, PAGE_SIZE), F32)
    grid_spec = pltpu.PrefetchScalarGridSpec(
        num_scalar_prefetch=1,
        grid=(nb,),
        in_specs=[per_seq(N_Q_HEADS, HEAD_DIM),
                  pl.BlockSpec((SUBLANES, sel.shape[1]), lambda b, pt: (b // SUBLANES, 0)),
                  per_seq(1, KV_WIDTH), per_seq(1, KV_WIDTH), per_seq(N_Q_HEADS, HEAD_DIM),
                  pl.BlockSpec(memory_space=pl.ANY), pl.BlockSpec(memory_space=pl.ANY)],
        out_specs=per_seq(N_Q_HEADS, HEAD_DIM),
        scratch_shapes=[page_buf, page_buf, pltpu.SemaphoreType.DMA((2,)), pltpu.SemaphoreType.DMA((2,))],
    )
    return pl.pallas_call(
        _dec_attn_kernel,
        grid_spec=grid_spec,
        out_shape=jax.ShapeDtypeStruct((nb, N_Q_HEADS, HEAD_DIM), F32),
        compiler_params=_cparams(("arbitrary",)),
        name="dec_attn",
    )(page_table, q3, sel, k_new3, v_new3, sg3, cache_k2, cache_v2)


def kernel(x_prompt, x_sample, cache_k, cache_v, cache_kidx, state_wkv, state_shift, page_table,
           c_prompt, c_sample, norm_w, w_ada, b_ada, w_in, q_norm_w, k_norm_w, mu_shift, w0, w_up,
           a0, a_up, k_k, k_a, r_k, ln_x_w, ln_x_b, w_out):
    nb, t, d = x_prompt.shape
    nd = x_sample.shape[0]
    assert x_sample.shape[1] == 1
    n_pages = page_table.shape[1]
    past = n_pages * PAGE_SIZE
    w = RWKV_WIDTH

    offs = [0]
    for sz in (ATT_WIDTH, KV_WIDTH, KV_WIDTH, N_IDX_HEADS * IDX_DIM, N_IDX_HEADS, IDX_DIM, ATT_WIDTH,
               w, w, w, LORA, LORA, w):
        offs.append(offs[-1] + sz)
    w_in_t = w_in.T
    row = lambda i: w_in_t[offs[i]:offs[i + 1]]
    w_main = jnp.concatenate([row(0), row(1), row(2), row(3), row(6), row(7), row(8), row(9), row(12),
                              row(10), row(11), row(5), jnp.zeros((LANES - IDX_DIM, d), w_in.dtype)], axis=0).astype(BF16)
    zl = jnp.zeros((LORA, w), F32)
    wts = {
        "norm_w": norm_w.reshape(1, d), "w_main": w_main,
        "w_wit": row(4).astype(BF16), "w_vt": row(2).astype(BF16),
        "qnw": jnp.concatenate([q_norm_w, q_norm_w]).reshape(1, LANES),
        "knw": jnp.concatenate([k_norm_w, k_norm_w]).reshape(1, LANES),
    }
    rw_params = {
        "mu_rkv": mu_shift[:3 * w].reshape(1, 3 * w), "mu_lora": mu_shift[3 * w:].reshape(1, 2 * LORA),
        "w0": w0.reshape(1, w), "a0": a0.reshape(1, w),
        "w_lora": jnp.concatenate([jnp.concatenate([w_up, zl], axis=1),
                                   jnp.concatenate([zl, a_up], axis=1)], axis=0).astype(BF16),
        "k_k": k_k.reshape(1, w), "k_a": k_a.reshape(1, w), "r_k": r_k.reshape(1, w),
        "ln_w": ln_x_w.reshape(1, w), "ln_b": ln_x_b.reshape(1, w),
    }
    w_out_bf = w_out.astype(BF16)

    n_c = nb + nd
    pad = (-n_c) % 8
    c_all = jnp.concatenate([c_prompt, c_sample, jnp.zeros((pad, d), F32)], axis=0)
    mod = _adaln_mod(c_all, w_ada, b_ada)
    shift_p, scale_p, gate_p = (mod[:nb, s * d:(s + 1) * d].reshape(nb, 1, d) for s in range(3))
    shift_s, scale_s, gate_s = (mod[nb:n_c, s * d:(s + 1) * d].reshape(1, nd, d) for s in range(3))

    tm = min(512, t)
    pr = _in_proj(x_prompt, shift_p, scale_p, _rope_tables(jnp.arange(t)), wts, tm)
    xs3 = x_sample.reshape(1, nd, d)
    sm = _in_proj(xs3, shift_s, scale_s, _rope_tables(jnp.full((1,), past)), wts, nd)
    qi_s, ki_s = sm["qi"][0], sm["ki"][0]
    wi_s = sm["wit"][0].T
    n_phys = cache_k.shape[0]
    k_pages = jnp.transpose(cache_k, (0, 2, 3, 1)).reshape(n_phys, KV_WIDTH, PAGE_SIZE)
    v_pages = jnp.transpose(cache_v, (0, 2, 3, 1)).reshape(n_phys, KV_WIDTH, PAGE_SIZE)

    att_p, scores = _attn_prompt(pr["q"], pr["qi"], pr["wit"], pr["kbf"], pr["vt"], pr["kibf"], pr["sg"],
                                 page_table, qi_s.reshape(nd, N_IDX_HEADS, IDX_DIM),
                                 wi_s.reshape(nd, N_IDX_HEADS, 1), jnp.transpose(cache_kidx, (0, 2, 1)))
    rw_p, s_p = _rwkv_prompt(pr["rkv"], pr["lora"], pr["gr"], rw_params)
    y_prompt = _out_proj(att_p, rw_p, w_out_bf, x_prompt, gate_p, tm)
    shift_prompt = jnp.concatenate([pr["rkv"][:, t - 1:], pr["lora"][:, t - 1:]], axis=-1)

    topk = min(TOPK_MAX, (past + 1) // 4)
    sel = _dec_select(scores, qi_s, sm["kibf"][0], wi_s, topk)
    att_s = _dec_attn(page_table, sm["q"][0].reshape(nd, N_Q_HEADS, HEAD_DIM), sel,
                      sm["k"][0].reshape(nd, 1, KV_WIDTH), sm["v"][0].reshape(nd, 1, KV_WIDTH),
                      sm["sg"][0].reshape(nd, N_Q_HEADS, HEAD_DIM),
                      k_pages, v_pages)
    prev = state_shift.reshape(nd, -1)
    vecs = _rwkv_dec_prep(sm["rkv"][0], sm["lora"][0], prev[:, :3 * w], prev[:, 3 * w:], rw_params)
    s_t, y_t = _rwkv_dec_state(vecs[:6], jnp.transpose(state_wkv, (1, 2, 3, 0)))
    s_s = jnp.transpose(s_t, (3, 0, 1, 2))
    rw_s = _rwkv_dec_epi(y_t.reshape(w, nd), vecs[6], sm["gr"][0], rw_params)
    y_sample = _out_proj(att_s.reshape(1, nd, ATT_WIDTH), rw_s.reshape(1, nd, w), w_out_bf, xs3, gate_s, nd)
    shift_sample = jnp.concatenate([sm["rkv"][0], sm["lora"][0]], axis=-1).reshape(nd, 1, 3 * w + 2 * LORA)

    return (y_prompt, y_sample.reshape(nd, 1, d),
            pr["k"].reshape(nb, t, N_KV_HEADS, HEAD_DIM), pr["v"].reshape(nb, t, N_KV_HEADS, HEAD_DIM), pr["ki"],
            s_p, shift_prompt,
            sm["k"][0].reshape(nd, 1, N_KV_HEADS, HEAD_DIM), sm["v"][0].reshape(nd, 1, N_KV_HEADS, HEAD_DIM),
            ki_s.reshape(nd, 1, IDX_DIM), s_s, shift_sample)
```

```python
import functools

import jax
import jax.numpy as jnp
from jax import lax
from jax.experimental import pallas as pl
from jax.experimental.pallas import tpu as pltpu

F32, BF16 = jnp.float32, jnp.bfloat16

HEAD_DIM = 64
N_Q_HEADS = 8
N_KV_HEADS = 2
N_IDX_HEADS = 8
IDX_DIM = 64
N_RWKV_HEADS = 8
ATT_WIDTH = N_Q_HEADS * HEAD_DIM
RWKV_WIDTH = N_RWKV_HEADS * HEAD_DIM
KV_WIDTH = N_KV_HEADS * HEAD_DIM
LORA = 64
TOPK_MAX = 256
ROPE_THETA = 500000.0
ROPE_DIMS = HEAD_DIM // 4
NORM_EPS = 1e-6
GN_EPS = 64e-5
PAGE_SIZE = 128
LANES = 128
SUBLANES = 8
Q_SCALE = HEAD_DIM ** -0.5 * 1.4426950408889634
VMEM_LIMIT = 56 * 1024 * 1024

Q_BLOCK = 256
KEY_CHUNK = 512
N_BISECT = 16
RW_CHUNK = 64
RW_BLOCK = 128


def _dot(a, b):
    return jnp.dot(a, b, preferred_element_type=F32)


def _dot_nt(a, b):
    return lax.dot_general(a, b, (((1,), (1,)), ((), ())), preferred_element_type=F32)


def _split(x):
    hi = x.astype(BF16)
    return hi, (x - hi.astype(F32)).astype(BF16)


def _sum_dot(x, sel01):
    hi, lo = _split(x)
    return _dot(hi, sel01) + _dot(lo, sel01)


def _sum_dot_left(sel01, x):
    hi, lo = _split(x)
    return _dot(sel01, hi) + _dot(sel01, lo)


def _dot_tn(a, b):
    return lax.dot_general(a, b, (((0,), (0,)), ((), ())), preferred_element_type=F32)


def _iota(shape, dim):
    return lax.broadcasted_iota(jnp.int32, shape, dim)


def _head_block_ones(n):
    return jnp.where(_iota((n, n), 0) // HEAD_DIM == _iota((n, n), 1) // HEAD_DIM, 1.0, 0.0).astype(BF16)


def _silu(x):
    return x * jax.nn.sigmoid(x)


def _cparams(sem):
    return pltpu.CompilerParams(dimension_semantics=sem, vmem_limit_bytes=VMEM_LIMIT)


def _mod_kernel(c_ref, w_ref, b_ref, o_ref):
    s = _silu(c_ref[...])
    o_ref[...] = _dot(s.astype(BF16), w_ref[...].astype(BF16)) + b_ref[...]


def _adaln_mod(c, w_ada, b_ada):
    rows, d = c.shape
    n = w_ada.shape[1]
    tn = 1024
    return pl.pallas_call(
        _mod_kernel,
        grid=(n // tn,),
        in_specs=[pl.BlockSpec((rows, d), lambda j: (0, 0)),
                  pl.BlockSpec((d, tn), lambda j: (0, j)),
                  pl.BlockSpec((1, tn), lambda j: (0, j))],
        out_specs=pl.BlockSpec((rows, tn), lambda j: (0, j)),
        out_shape=jax.ShapeDtypeStruct((rows, n), F32),
        compiler_params=_cparams(("arbitrary",)),
        name="adaln_mod",
    )(c, w_ada, b_ada.reshape(1, n))


_W_Q, _W_KV, _W_QI, _W_GA, _W_RKV, _W_GR, _W_TAIL, _W_END = 0, 512, 768, 1280, 1792, 3328, 3840, 4096


def _inproj_kernel(x_ref, shift_ref, scale_ref, nw_ref, w_ref, wwi_ref, wvt_ref, cos_ref, s1_ref, s2_ref,
                   qnw_ref, knw_ref,
                   q_o, k_o, kbf_o, v_o, vt_o, qi_o, ki_o, kibf_o, wit_o, sg_o, rkv_o, lora_o, gr_o):
    x = x_ref[0]
    ms = jnp.mean(x * x, axis=-1, keepdims=True)
    h = x * lax.rsqrt(ms + NORM_EPS) * nw_ref[...]
    h = h * (1.0 + scale_ref[0]) + shift_ref[0]
    hb = h.astype(BF16)
    cos, s1, s2 = cos_ref[...], s1_ref[...], s2_ref[...]
    bsum = _head_block_ones(LANES)

    def seg(a, b):
        return _dot_nt(hb, w_ref[a:b, :])

    def rope(t):
        return t * cos + pltpu.roll(t, LANES - ROPE_DIMS // 2, 1) * s1 + pltpu.roll(t, ROPE_DIMS // 2, 1) * s2

    def head_norm(t, w):
        return t * lax.rsqrt(_sum_dot(t * t, bsum) * (1.0 / HEAD_DIM) + NORM_EPS) * w

    yq = seg(_W_Q, _W_KV)
    for s in range(ATT_WIDTH // LANES):
        t = rope(head_norm(yq[:, s * LANES:(s + 1) * LANES], qnw_ref[...]))
        q_o[0, :, s * LANES:(s + 1) * LANES] = (t * Q_SCALE).astype(BF16)

    ykv = seg(_W_KV, _W_QI)
    k = rope(head_norm(ykv[:, :KV_WIDTH], knw_ref[...]))
    k_o[0] = k
    kbf_o[0] = k.astype(BF16)
    v_o[0] = ykv[:, KV_WIDTH:]
    vt_o[0] = _dot_nt(wvt_ref[...], hb).astype(BF16)

    yqi = seg(_W_QI, _W_GA)
    for s in range(N_IDX_HEADS * IDX_DIM // LANES):
        qi_o[0, :, s * LANES:(s + 1) * LANES] = rope(yqi[:, s * LANES:(s + 1) * LANES]).astype(BF16)

    sg_o[0] = _silu(seg(_W_GA, _W_RKV))
    for s in range(3):
        rkv_o[0, :, s * RWKV_WIDTH:(s + 1) * RWKV_WIDTH] = seg(_W_RKV + s * RWKV_WIDTH, _W_RKV + (s + 1) * RWKV_WIDTH)
    gr_o[0] = seg(_W_GR, _W_TAIL)

    yt = seg(_W_TAIL, _W_END)
    lora_o[0] = yt[:, :2 * LORA]
    ki = rope(yt[:, 2 * LORA:])[:, :IDX_DIM]
    ki_o[0] = ki
    kibf_o[0] = ki.astype(BF16)
    wit_o[0] = _dot_nt(wwi_ref[...], hb) * (N_IDX_HEADS ** -0.5 * IDX_DIM ** -0.5)


def _in_proj(x3, shift3, scale3, tabs, wts, tm):
    nb, t, d = x3.shape
    mod_rows = shift3.shape[1]
    tab_rows = tabs[0].shape[0]
    grid = (nb, t // tm)
    if mod_rows == 1:
        mod_spec = pl.BlockSpec((1, 1, d), lambda b, i: (b, 0, 0))
    else:
        mod_spec = pl.BlockSpec((1, tm, d), lambda b, i: (b, i, 0))
    if tab_rows == 1:
        tab_spec = pl.BlockSpec((1, LANES), lambda b, i: (0, 0))
    else:
        tab_spec = pl.BlockSpec((tm, LANES), lambda b, i: (i, 0))
    const = lambda shape: pl.BlockSpec(shape, lambda b, i: tuple(0 for _ in shape))
    row = lambda width: pl.BlockSpec((1, tm, width), lambda b, i: (b, i, 0))
    col = lambda rows: pl.BlockSpec((1, rows, tm), lambda b, i: (b, 0, i))
    out_defs = [
        ("q", (t, ATT_WIDTH), BF16, row(ATT_WIDTH)),
        ("k", (t, KV_WIDTH), F32, row(KV_WIDTH)),
        ("kbf", (t, KV_WIDTH), BF16, row(KV_WIDTH)),
        ("v", (t, KV_WIDTH), F32, row(KV_WIDTH)),
        ("vt", (KV_WIDTH, t), BF16, col(KV_WIDTH)),
        ("qi", (t, N_IDX_HEADS * IDX_DIM), BF16, row(N_IDX_HEADS * IDX_DIM)),
        ("ki", (t, IDX_DIM), F32, row(IDX_DIM)),
        ("kibf", (t, IDX_DIM), BF16, row(IDX_DIM)),
        ("wit", (N_IDX_HEADS, t), F32, col(N_IDX_HEADS)),
        ("sg", (t, ATT_WIDTH), F32, row(ATT_WIDTH)),
        ("rkv", (t, 3 * RWKV_WIDTH), F32, row(3 * RWKV_WIDTH)),
        ("lora", (t, 2 * LORA), F32, row(2 * LORA)),
        ("gr", (t, RWKV_WIDTH), F32, row(RWKV_WIDTH)),
    ]
    outs = pl.pallas_call(
        _inproj_kernel,
        grid=grid,
        in_specs=[pl.BlockSpec((1, tm, d), lambda b, i: (b, i, 0)), mod_spec, mod_spec, const((1, d)),
                  const(wts["w_main"].shape), const(wts["w_wit"].shape), const(wts["w_vt"].shape),
                  tab_spec, tab_spec, tab_spec, const((1, LANES)), const((1, LANES))],
        out_specs=[o[3] for o in out_defs],
        out_shape=[jax.ShapeDtypeStruct((nb,) + o[1], o[2]) for o in out_defs],
        compiler_params=_cparams(("arbitrary", "arbitrary")),
        name="in_proj",
    )(x3, shift3, scale3, wts["norm_w"], wts["w_main"], wts["w_wit"], wts["w_vt"], tabs[0], tabs[1], tabs[2],
      wts["qnw"], wts["knw"])
    return {o[0]: arr for o, arr in zip(out_defs, outs)}


def _rope_tables(pos):
    half = ROPE_DIMS // 2
    inv = jnp.power(ROPE_THETA, -jnp.arange(half, dtype=F32) / half)
    ang = pos.astype(F32)[:, None] * inv[None, :]
    cos, sin = jnp.cos(ang), jnp.sin(ang)
    n = pos.shape[0]
    pad = jnp.zeros((n, HEAD_DIM - ROPE_DIMS), F32)
    zero = jnp.zeros((n, half), F32)
    c_head = jnp.concatenate([cos, cos, pad + 1.0], axis=1)
    s1_head = jnp.concatenate([-sin, zero, pad], axis=1)
    s2_head = jnp.concatenate([zero, sin, pad], axis=1)
    two = lambda a: jnp.concatenate([a, a], axis=1)
    return two(c_head), two(s1_head), two(s2_head)


def _head_sum(t):
    ones = _head_block_ones(LANES)
    parts = [_sum_dot(t[:, s * LANES:(s + 1) * LANES], ones) for s in range(t.shape[1] // LANES)]
    return parts[0] if len(parts) == 1 else jnp.concatenate(parts, axis=1)


def _softplus(z):
    return jnp.maximum(z, 0.0) + jnp.log(1.0 + jnp.exp(-jnp.abs(z)))


def _rwkv_prep(xs, xl, prev, prevl, p):
    xm = xs + p["mu_rkv"] * (prev - xs)
    xml = xl + p["mu_lora"] * (prevl - xl)
    w = RWKV_WIDTH
    r, k, v = xm[:, :w], xm[:, w:2 * w], xm[:, 2 * w:]
    lane = _iota(xml.shape, 1)
    lhs = jnp.where(lane < LORA, jnp.tanh(xml), xml)
    lo = _dot(lhs.astype(BF16), p["w_lora"])
    wlog = -_softplus(-(p["w0"] + lo[:, :w])) - 0.5
    logw = -jnp.exp(wlog)
    a = jax.nn.sigmoid(p["a0"] + lo[:, w:])
    kkr = k * p["k_k"]
    kk = kkr / jnp.maximum(jnp.sqrt(_head_sum(kkr * kkr)), 1e-12)
    k_mod = k * (1.0 + (a - 1.0) * p["k_a"])
    bonus = _head_sum(r * k_mod * p["r_k"]) * v
    return r, logw, k_mod, v, -kk, kk * a, bonus


def _rwkv_epilogue(y, bonus, gate, p):
    mean = _head_sum(y) * (1.0 / HEAD_DIM)
    d = y - mean
    var = _head_sum(d * d) * (1.0 / HEAD_DIM)
    yn = d * lax.rsqrt(var + GN_EPS) * p["ln_w"] + p["ln_b"]
    return (yn + bonus) * _silu(gate)


_RW_PARAM_NAMES = ("mu_rkv", "mu_lora", "w0", "a0", "w_lora", "k_k", "k_a", "r_k", "ln_w", "ln_b")


def _load_params(refs):
    return {n: r[...] for n, r in zip(_RW_PARAM_NAMES, refs)}


def _rwkv_prompt_kernel(rkv_ref, lora_ref, gr_ref, *rest):
    nparam = len(_RW_PARAM_NAMES)
    p = _load_params(rest[:nparam])
    out_ref, sfin_ref = rest[nparam:nparam + 2]
    (st_s, c_rkv, c_lora, r_s, lw_s, k_s, v_s, al_s, be_s, y_s, bon_s,
     at_s, rt_s, bk_s, ec_s, wl_s, wh_s, zv_s, inv_s) = rest[nparam + 2:]
    i = pl.program_id(0)
    nb, tb = rkv_ref.shape[0], rkv_ref.shape[1]
    C = RW_CHUNK
    W = 2 * C

    @pl.when(i == 0)
    def _():
        st_s[...] = jnp.zeros_like(st_s)
        c_rkv[...] = jnp.zeros_like(c_rkv)
        c_lora[...] = jnp.zeros_like(c_lora)

    row = _iota((tb, 1), 0)
    for b in range(nb):
        xs, xl = rkv_ref[b], lora_ref[b]
        prev = jnp.where(row == 0, c_rkv[b], pltpu.roll(xs, 1, 0))
        prevl = jnp.where(row == 0, c_lora[b], pltpu.roll(xl, 1, 0))
        c_rkv[b] = xs[tb - 1:tb, :]
        c_lora[b] = xl[tb - 1:tb, :]
        r_s[b], lw_s[b], k_s[b], v_s[b], al_s[b], be_s[b], bon_s[b] = _rwkv_prep(xs, xl, prev, prevl, p)

    rm, cn = _iota((W, W), 0), _iota((W, W), 1)
    tt, ss = rm % C, cn % C
    top, left = rm < C, cn < C
    strict, incl = ss < tt, ss <= tt
    mk_l_lo, mk_l_hi = top & left & strict, (~top) & (~left) & strict
    mk_z_lo, mk_z_hi = top & (~left) & strict, (~top) & left & strict
    mk_w = (_iota((C, W), 1) % C) <= _iota((C, W), 0)
    bd = (rm // HEAD_DIM) == (cn // HEAD_DIM)
    eye = jnp.where(rm == cn, 1.0, 0.0).astype(F32)
    tril = jnp.where(_iota((C, C), 1) <= _iota((C, C), 0), 1.0, 0.0).astype(BF16)
    lane_lo = _iota((C, LANES), 1) < HEAD_DIM
    m_lo = jnp.where(lane_lo, 1.0, 0.0).astype(F32)
    m_hi = 1.0 - m_lo

    bf = lambda a: a.astype(BF16)
    pairs = range(N_RWKV_HEADS // 2)
    lanes_of = lambda pr: slice(pr * LANES, (pr + 1) * LANES)
    rows_of = lambda c: pl.ds(pl.multiple_of(c * C, C), C)
    items = [(b, pr) for b in range(nb) for pr in pairs]

    def phase_a(c, carry):
        rows = rows_of(c)
        cums = []
        for b in range(nb):
            cum_all = _sum_dot_left(tril, lw_s[b, rows, :])
            cums += [cum_all[:, lanes_of(pr)] for pr in pairs]
        g_lo, g_hi = [], []
        for n, (b, pr) in enumerate(items):
            sl = lanes_of(pr)
            cum = cums[n]
            cum_c = cum[C - 1:C, :]
            e_neg, e_rel = jnp.exp(-cum), jnp.exp(cum_c - cum)
            rr, kk, aa, bb = r_s[b, rows, sl], k_s[b, rows, sl], al_s[b, rows, sl], be_s[b, rows, sl]
            a_t, r_t = aa * jnp.exp(cum - lw_s[b, rows, sl]), rr * jnp.exp(cum)
            b_t, k_t = bb * e_neg, kk * e_neg
            at_s[b, c, pr], rt_s[b, c, pr] = bf(a_t), bf(r_t)
            bk_s[b, c, pr] = jnp.concatenate([bb * e_rel, kk * e_rel], axis=0)
            ec_s[b, c, pr] = jnp.exp(cum_c)
            xk = bf(jnp.concatenate([b_t, k_t], axis=0))
            kx = bf(jnp.concatenate([k_t, b_t], axis=0))
            g_lo.append(_dot_nt(bf(jnp.concatenate([a_t * m_lo, r_t * m_lo], axis=0)), xk))
            g_hi.append(_dot_nt(bf(jnp.concatenate([r_t * m_hi, a_t * m_hi], axis=0)), kx))
        l_mat = []
        for n, (b, pr) in enumerate(items):
            l_mat.append(jnp.where(mk_l_lo, g_lo[n], 0.0) + jnp.where(mk_l_hi, g_hi[n], 0.0))
            z_mat = jnp.where(mk_z_lo, g_lo[n], 0.0) + jnp.where(mk_z_hi, g_hi[n], 0.0)
            wl_s[b, c, pr] = bf(jnp.where(mk_w, g_lo[n][C:, :], 0.0))
            wh_s[b, c, pr] = bf(jnp.where(mk_w, g_hi[n][:C, :], 0.0))
            vv_b = bf(v_s[b, rows, lanes_of(pr)])
            zv_s[b, c, pr] = _dot(bf(z_mat), jnp.concatenate([vv_b, vv_b], axis=0))
        inv = [eye + m for m in l_mat]
        pw = l_mat
        for _ in range(5):
            pw = [_dot(bf(m), bf(m)) for m in pw]
            inv = [x + _dot(bf(x), bf(m)) for x, m in zip(inv, pw)]
        for n, (b, pr) in enumerate(items):
            inv_s[b, c, pr] = bf(inv[n])
        return carry

    lax.fori_loop(0, tb // C, phase_a, 0)

    def phase_b(c, carry):
        rows = rows_of(c)
        nit = range(len(items))
        st = [st_s[b, pr] for b, pr in items]
        st_b = [bf(s) for s in st]
        vv = [v_s[b, rows, lanes_of(pr)] for b, pr in items]
        a_s = [_dot_nt(at_s[b, c, pr], st_b[n]) for n, (b, pr) in enumerate(items)]
        u_st = [_dot(inv_s[b, c, pr], bf(zv_s[b, c, pr] + jnp.concatenate([a_s[n], a_s[n]], axis=0)))
                for n, (b, pr) in enumerate(items)]
        u = [jnp.where(lane_lo, x[:C], x[C:]) for x in u_st]
        uv = [jnp.concatenate([u[n], vv[n]], axis=0) for n in nit]
        upd = [_dot_tn(uv[n], bk_s[b, c, pr]) for n, (b, pr) in enumerate(items)]
        for n, (b, pr) in enumerate(items):
            st_s[b, pr] = st[n] * ec_s[b, c, pr] + jnp.where(bd, upd[n], 0.0)
        for n, (b, pr) in enumerate(items):
            uv_b = bf(uv[n])
            vu_b = jnp.concatenate([uv_b[C:], uv_b[:C]], axis=0)
            y_in = jnp.where(lane_lo, _dot(wl_s[b, c, pr], uv_b), _dot(wh_s[b, c, pr], vu_b))
            y_s[b, rows, lanes_of(pr)] = _dot_nt(rt_s[b, c, pr], st_b[n]) + y_in
        return carry

    lax.fori_loop(0, tb // C, phase_b, 0)
    for b in range(nb):
        out_ref[b] = _rwkv_epilogue(y_s[b], bon_s[b], gr_ref[b], p).astype(BF16)

    @pl.when(i == pl.num_programs(0) - 1)
    def _():
        for b in range(nb):
            for hd in range(N_RWKV_HEADS):
                o = (hd % 2) * HEAD_DIM
                sfin_ref[b, hd] = st_s[b, hd // 2][o:o + HEAD_DIM, o:o + HEAD_DIM]


def _rwkv_prompt(rkv, lora, gr, params):
    nb, t, _ = rkv.shape
    tb = min(RW_BLOCK, t)
    w = RWKV_WIDTH
    blk = lambda width: pl.BlockSpec((nb, tb, width), lambda i: (0, i, 0))
    pspecs = [pl.BlockSpec(params[n].shape, lambda i: (0, 0)) for n in _RW_PARAM_NAMES]
    vec = lambda: pltpu.VMEM((nb, tb, w), F32)
    nch, npair, c = tb // RW_CHUNK, N_RWKV_HEADS // 2, RW_CHUNK
    per_chunk = lambda rows, dt: pltpu.VMEM((nb, nch, npair, rows, LANES), dt)
    state_shape = (nb, N_RWKV_HEADS, HEAD_DIM, HEAD_DIM)
    return pl.pallas_call(
        _rwkv_prompt_kernel,
        grid=(t // tb,),
        in_specs=[blk(3 * w), blk(2 * LORA), blk(w)] + pspecs,
        out_specs=[blk(w), pl.BlockSpec(state_shape, lambda i: (0, 0, 0, 0))],
        out_shape=[jax.ShapeDtypeStruct((nb, t, w), BF16), jax.ShapeDtypeStruct(state_shape, F32)],
        scratch_shapes=[pltpu.VMEM((nb, npair, LANES, LANES), F32),
                        pltpu.VMEM((nb, 1, 3 * w), F32), pltpu.VMEM((nb, 1, 2 * LORA), F32),
                        vec(), vec(), vec(), vec(), vec(), vec(), vec(), vec(),
                        per_chunk(c, BF16), per_chunk(c, BF16), per_chunk(2 * c, F32), per_chunk(1, F32),
                        per_chunk(c, BF16), per_chunk(c, BF16), per_chunk(2 * c, F32), per_chunk(2 * c, BF16)],
        compiler_params=_cparams(("arbitrary",)),
        name="rwkv_prompt",
    )(rkv, lora, gr, *[params[n] for n in _RW_PARAM_NAMES])


def _rwkv_dec_prep_kernel(rkv_ref, lora_ref, prev_ref, prevl_ref, *rest):
    nparam = len(_RW_PARAM_NAMES)
    p = _load_params(rest[:nparam])
    outs = rest[nparam:]
    r, logw, k_mod, v, al, be, bonus = _rwkv_prep(rkv_ref[...], lora_ref[...], prev_ref[...], prevl_ref[...], p)
    for o, val in zip(outs[:6], (r, jnp.exp(logw), k_mod, v, al, be)):
        o[...] = val.T
    outs[6][...] = bonus


def _rwkv_dec_prep(rkv, lora, prev, prevl, params):
    n = rkv.shape[0]
    full = lambda a: pl.BlockSpec(a.shape, lambda: tuple(0 for _ in a.shape))
    args = [rkv, lora, prev, prevl] + [params[k] for k in _RW_PARAM_NAMES]
    t_shape, shape = (RWKV_WIDTH, n), (n, RWKV_WIDTH)
    return pl.pallas_call(
        _rwkv_dec_prep_kernel,
        in_specs=[full(a) for a in args],
        out_specs=[pl.BlockSpec(t_shape, lambda: (0, 0))] * 6 + [pl.BlockSpec(shape, lambda: (0, 0))],
        out_shape=[jax.ShapeDtypeStruct(t_shape, F32)] * 6 + [jax.ShapeDtypeStruct(shape, F32)],
        compiler_params=pltpu.CompilerParams(vmem_limit_bytes=VMEM_LIMIT),
        name="rwkv_dec_prep",
    )(*args)


def _rwkv_dec_state_kernel(r_ref, w_ref, k_ref, v_ref, a_ref, b_ref, s_ref, so_ref, y_ref):
    r, w, k, a, b = r_ref[0], w_ref[0], k_ref[0], a_ref[0], b_ref[0]

    def one(i, carry):
        st = s_ref[0, i]
        sa = jnp.sum(st * a, axis=0, keepdims=True)
        sn = st * w + sa * b + v_ref[0, pl.ds(i, 1), :] * k
        so_ref[0, i] = sn
        y_ref[0, pl.ds(i, 1), :] = jnp.sum(sn * r, axis=0, keepdims=True)
        return carry

    lax.fori_loop(0, HEAD_DIM, one, 0)


def _rwkv_dec_state(vecs_t, state_t):
    nh, n, _, nb = state_t.shape
    vspec = pl.BlockSpec((1, n, nb), lambda h: (h, 0, 0))
    sspec = pl.BlockSpec((1, n, n, nb), lambda h: (h, 0, 0, 0))
    return pl.pallas_call(
        _rwkv_dec_state_kernel,
        grid=(nh,),
        in_specs=[vspec] * 6 + [sspec],
        out_specs=[sspec, vspec],
        out_shape=[jax.ShapeDtypeStruct(state_t.shape, F32), jax.ShapeDtypeStruct((nh, n, nb), F32)],
        compiler_params=_cparams(("arbitrary",)),
        name="rwkv_dec_state",
    )(*[a.reshape(nh, n, nb) for a in vecs_t], state_t)


def _rwkv_dec_epi_kernel(yt_ref, bonus_ref, gr_ref, lnw_ref, lnb_ref, o_ref):
    o_ref[...] = _rwkv_epilogue(yt_ref[...].T, bonus_ref[...], gr_ref[...],
                                {"ln_w": lnw_ref[...], "ln_b": lnb_ref[...]}).astype(BF16)


def _rwkv_dec_epi(y_t, bonus, gr, params):
    n = bonus.shape[0]
    args = [y_t, bonus, gr, params["ln_w"], params["ln_b"]]
    return pl.pallas_call(
        _rwkv_dec_epi_kernel,
        in_specs=[pl.BlockSpec(a.shape, lambda: (0, 0)) for a in args],
        out_specs=pl.BlockSpec((n, RWKV_WIDTH), lambda: (0, 0)),
        out_shape=jax.ShapeDtypeStruct((n, RWKV_WIDTH), BF16),
        name="rwkv_dec_epi",
    )(*args)


def _attn_prompt_kernel(pt_ref, q_ref, qi_ref, wit_ref, k_ref, vt_ref, kidx_ref, sg_ref, tri_ref,
                        dqi_ref, dwi_ref, dkidx_hbm, o_ref, dsc_ref,
                        s_scr, m_scr, acc_scr, *dscratch, topk):
    j = pl.program_id(1)
    step = pl.program_id(0) * pl.num_programs(1) + j
    nsteps = pl.num_programs(0) * pl.num_programs(1)
    spp = dqi_ref.shape[0]
    for u in range(spp):
        pages = _PageFetcher(pt_ref, dkidx_hbm, dscratch[u], dscratch[spp + u], step, nsteps, stride=spp, offset=u)
        _score_pages(pages, dqi_ref, dwi_ref, dsc_ref, u, (step % (SUBLANES // spp)) * spp + u)
    QB, KC = Q_BLOCK, KEY_CHUNK
    nch = (j * QB) // KC + 1
    inf = jnp.inf
    qpos_i = j * QB + _iota((1, QB), 1)
    k_q = jnp.minimum(float(topk), qpos_i.astype(F32) + 1.0)
    wit = wit_ref[0]
    qi = qi_ref[0]
    qi_stack = jnp.concatenate([qi[:, h * IDX_DIM:(h + 1) * IDX_DIM] for h in range(N_IDX_HEADS)], axis=0)

    def chunk_rows(c):
        return pl.ds(pl.multiple_of(c * KC, KC), KC)

    SUB = 32
    part = lambda a: a.reshape(KC // SUB, SUB, QB)
    psum = lambda a: jnp.sum(part(a), axis=0)
    pmin = lambda a: jnp.min(part(a), axis=0)
    pmax = lambda a: jnp.max(part(a), axis=0)
    full8 = lambda val: jnp.full((SUB, QB), val, F32)
    fold_sum = lambda a: jnp.sum(a, axis=0, keepdims=True)
    fold_min = lambda a: jnp.min(a, axis=0, keepdims=True)
    fold_max = lambda a: jnp.max(a, axis=0, keepdims=True)

    def score_chunk(c, lohi):
        rows = chunk_rows(c)
        st = _dot_nt(kidx_ref[0, rows, :], qi_stack)
        acc = jnp.zeros((KC, QB), F32)
        for h in range(N_IDX_HEADS):
            acc = acc + jnp.maximum(st[:, h * QB:(h + 1) * QB], 0.0) * wit[h:h + 1, :]
        ok = (c * KC + _iota((KC, QB), 0)) <= qpos_i
        s_scr[rows, :] = jnp.where(ok, acc, -inf)
        return (jnp.minimum(lohi[0], pmin(jnp.where(ok, acc, inf))),
                jnp.maximum(lohi[1], pmax(jnp.where(ok, acc, -inf))))

    lo8, hi8 = lax.fori_loop(0, nch, score_chunk, (full8(inf), full8(-inf)))
    lo, hi = fold_min(lo8), fold_max(hi8)

    def sweep(fn, init):
        return lax.fori_loop(0, nch, lambda c, car: fn(car, s_scr[chunk_rows(c), :]), init)

    def bisect(_, lh):
        lo, hi = lh
        mid = 0.5 * lo + 0.5 * hi
        cnt = fold_sum(sweep(lambda c, s: c + psum(jnp.where(s >= mid, 1.0, 0.0)), full8(0.0)))
        ge = cnt >= k_q
        return jnp.where(ge, mid, lo), jnp.where(ge, hi, mid)

    lo, hi = lax.fori_loop(0, N_BISECT, bisect, (lo, hi))
    v0 = fold_min(sweep(lambda c, s: jnp.minimum(c, pmin(jnp.where(s >= lo, s, inf))), full8(inf)))

    def refine(state):
        v, _, _ = state

        def f(car, s):
            g = s > v
            return car[0] + psum(jnp.where(g, 1.0, 0.0)), jnp.minimum(car[1], pmin(jnp.where(g, s, inf)))

        cnt8, v28 = sweep(f, (full8(0.0), full8(inf)))
        cnt, v2 = fold_sum(cnt8), fold_min(v28)
        ok = cnt < k_q
        return jnp.where(ok, v, v2), cnt, jnp.sum(jnp.where(ok, 0.0, 1.0))

    full = lambda val: jnp.full((1, QB), val, F32)
    thr, c_gt, _ = lax.while_loop(lambda st: st[2] > 0.0, refine, (v0, full(0.0), jnp.float32(1.0)))
    need = k_q - c_gt

    q = q_ref[0]
    G, R = N_KV_HEADS, N_Q_HEADS // N_KV_HEADS
    q_stack = [jnp.concatenate([q[:, (g * R + r) * HEAD_DIM:(g * R + r + 1) * HEAD_DIM] for r in range(R)], axis=0)
               for g in range(G)]
    ones_rows = jnp.where(_iota((HEAD_DIM, KC), 0) == 0, 1.0, 0.0).astype(BF16)
    m_scr[...] = jnp.full(m_scr.shape, -inf, F32)
    acc_scr[...] = jnp.zeros_like(acc_scr)

    n_ge = fold_sum(sweep(lambda c, s: c + psum(jnp.where(s >= thr, 1.0, 0.0)), full8(0.0)))
    any_ties = jnp.sum(jnp.where(n_ge == k_q, 0.0, 1.0)) > 0.0

    def make_attend(with_ties):
        def attend(c, tie_carry):
            rows = chunk_rows(c)
            s = s_scr[rows, :]
            if with_ties:
                eq = s == thr
                rank = tie_carry + _dot(tri_ref[...], jnp.where(eq, 1.0, 0.0).astype(BF16))
                sel = (s > thr) | (eq & (rank < need))
                tie_carry = tie_carry + fold_sum(psum(jnp.where(eq, 1.0, 0.0)))
            else:
                sel = s >= thr
            bias = jnp.where(sel, 0.0, -inf).astype(BF16)
            bias_r = jnp.concatenate([bias] * R, axis=1)
            kc = k_ref[0, rows, :]
            logits = [_dot_nt(kc[:, g * HEAD_DIM:(g + 1) * HEAD_DIM], q_stack[g]) for g in range(G)]
            probs, alphas = [], []
            for g in range(G):
                lm = logits[g].astype(BF16) + bias_r
                m_old = m_scr[g]
                m_part = jnp.max(lm.reshape(KC // 16, 16, R * QB), axis=0).astype(F32)
                m_new = jnp.maximum(m_old, jnp.max(m_part, axis=0, keepdims=True))
                m_safe = jnp.where(m_new == -inf, 0.0, m_new)
                alphas.append(jnp.exp2(m_old - m_safe))
                probs.append(jnp.exp2(lm - m_safe.astype(BF16)))
                m_scr[g] = m_new
            for g in range(G):
                vt_ext = jnp.concatenate([vt_ref[0, g * HEAD_DIM:(g + 1) * HEAD_DIM, rows], ones_rows], axis=0)
                acc_scr[g] = acc_scr[g] * alphas[g] + _dot(vt_ext, probs[g])
            return tie_carry
        return attend

    @pl.when(any_ties)
    def _():
        lax.fori_loop(0, nch, make_attend(True), full(0.0))

    @pl.when(jnp.logical_not(any_ties))
    def _():
        lax.fori_loop(0, nch, make_attend(False), full(0.0))

    out_t = jnp.concatenate([acc_scr[g][:HEAD_DIM] / acc_scr[g][HEAD_DIM:HEAD_DIM + 1] for g in range(G)], axis=0)
    per_r = [out_t[:, r * QB:(r + 1) * QB].T for r in range(R)]
    att = jnp.concatenate([per_r[r][:, g * HEAD_DIM:(g + 1) * HEAD_DIM] for g in range(G) for r in range(R)], axis=1)
    o_ref[0] = (att * sg_ref[0]).astype(BF16)


def _attn_prompt(q, qi, wit, kbf, vt, kibf, sg, page_table, dqi3, dwi3, dkidx_t):
    nb, t, _ = q.shape
    nj = t // Q_BLOCK
    nd, n_pages = page_table.shape
    spp = nd // (nb * nj)
    assert spp * nb * nj == nd and SUBLANES % spp == 0
    topk = min(TOPK_MAX, t // 4)
    kc = KEY_CHUNK
    tri = jnp.where(jnp.arange(kc)[None, :] < jnp.arange(kc)[:, None], 1.0, 0.0).astype(BF16)
    qblk = lambda width: pl.BlockSpec((1, Q_BLOCK, width), lambda b, j, pt: (b, j, 0))
    whole = lambda r, c: pl.BlockSpec((1, r, c), lambda b, j, pt: (b, 0, 0))
    per_step = lambda r, c: pl.BlockSpec((spp, r, c), lambda b, j, pt: (b * nj + j, 0, 0))
    grid_spec = pltpu.PrefetchScalarGridSpec(
        num_scalar_prefetch=1,
        grid=(nb, nj),
        in_specs=[qblk(ATT_WIDTH), qblk(N_IDX_HEADS * IDX_DIM),
                  pl.BlockSpec((1, N_IDX_HEADS, Q_BLOCK), lambda b, j, pt: (b, 0, j)),
                  whole(t, KV_WIDTH), whole(KV_WIDTH, t), whole(t, IDX_DIM), qblk(ATT_WIDTH),
                  pl.BlockSpec((kc, kc), lambda b, j, pt: (0, 0)),
                  per_step(N_IDX_HEADS, IDX_DIM), per_step(N_IDX_HEADS, 1), pl.BlockSpec(memory_space=pl.ANY)],
        out_specs=[qblk(ATT_WIDTH),
                   pl.BlockSpec((SUBLANES, n_pages * PAGE_SIZE), lambda b, j, pt: ((b * nj + j) // (SUBLANES // spp), 0))],
        scratch_shapes=[pltpu.VMEM((t, Q_BLOCK), F32),
                        pltpu.VMEM((N_KV_HEADS, 1, N_Q_HEADS // N_KV_HEADS * Q_BLOCK), F32),
                        pltpu.VMEM((N_KV_HEADS, 2 * HEAD_DIM, N_Q_HEADS // N_KV_HEADS * Q_BLOCK), F32)]
                       + [pltpu.VMEM((2, n_pages, IDX_DIM, PAGE_SIZE), F32)] * spp
                       + [pltpu.SemaphoreType.DMA((2,))] * spp,
    )
    return pl.pallas_call(
        functools.partial(_attn_prompt_kernel, topk=topk),
        grid_spec=grid_spec,
        out_shape=[jax.ShapeDtypeStruct((nb, t, ATT_WIDTH), BF16),
                   jax.ShapeDtypeStruct((nd, n_pages * PAGE_SIZE), F32)],
        compiler_params=_cparams(("arbitrary", "arbitrary")),
        name="attn_prompt",
    )(page_table, q, qi, wit, kbf, vt, kibf, sg, tri, dqi3, dwi3, dkidx_t)


def _outproj_kernel(att_ref, rw_ref, w_ref, x_ref, g_ref, o_ref):
    mix = _dot(att_ref[0], w_ref[:ATT_WIDTH, :]) + _dot(rw_ref[0], w_ref[ATT_WIDTH:, :])
    o_ref[0] = x_ref[0] + g_ref[0] * mix


def _out_proj(att, rw, w_out_bf, x3, gate3, tm):
    nb, t, d = x3.shape
    if gate3.shape[1] == 1:
        gspec = pl.BlockSpec((1, 1, d), lambda b, i: (b, 0, 0))
    else:
        gspec = pl.BlockSpec((1, tm, d), lambda b, i: (b, i, 0))
    blk = lambda width: pl.BlockSpec((1, tm, width), lambda b, i: (b, i, 0))
    return pl.pallas_call(
        _outproj_kernel,
        grid=(nb, t // tm),
        in_specs=[blk(ATT_WIDTH), blk(RWKV_WIDTH), pl.BlockSpec(w_out_bf.shape, lambda b, i: (0, 0)), blk(d), gspec],
        out_specs=blk(d),
        out_shape=jax.ShapeDtypeStruct((nb, t, d), F32),
        compiler_params=_cparams(("arbitrary", "arbitrary")),
        name="out_proj",
    )(att, rw, w_out_bf, x3, gate3)


class _PageFetcher:
    def __init__(self, pt_ref, hbm, buf, sem, step=None, nsteps=None, stride=1, offset=0):
        self.pt, self.hbm, self.buf, self.sem = pt_ref, hbm, buf, sem
        self.n_pages = buf.shape[1]
        b = pl.program_id(0) if step is None else step
        nb = pl.num_programs(0) if nsteps is None else nsteps
        self.first, self.last = b == 0, b == nb - 1
        self.slot = b % 2
        self.first_seq = offset
        self.next_seq = jnp.minimum(b + 1, nb - 1) * stride + offset

    def _copy(self, seq, j, slot):
        return pltpu.make_async_copy(self.hbm.at[self.pt[seq, j]], self.buf.at[slot, j], self.sem.at[slot])

    def _wait(self, slot):
        pltpu.make_async_copy(self.hbm.at[pl.ds(0, self.n_pages)], self.buf.at[slot], self.sem.at[slot]).wait()

    def begin(self):
        @pl.when(self.first)
        def _():
            for j in range(self.n_pages):
                self._copy(self.first_seq, j, 0).start()
        self._wait(self.slot)

    def prefetch(self, j):
        self._copy(self.next_seq, j, 1 - self.slot).start(priority=j % 2)

    def page(self, j):
        return self.buf[self.slot, j]

    def end(self):
        @pl.when(self.last)
        def _():
            self._wait(1 - self.slot)


def _score_pages(pages, qi_ref, wi_ref, o_ref, u, out_row):
    pages.begin()
    qi = qi_ref[u]
    qi16 = jnp.concatenate([qi, jnp.zeros_like(qi)], axis=0).astype(BF16)
    wi = wi_ref[u]
    parts = []
    for j in range(pages.n_pages):
        pages.prefetch(j)
        s = _dot(qi16, pages.page(j).astype(BF16))[:N_IDX_HEADS]
        parts.append(jnp.sum(jnp.maximum(s, 0.0) * wi, axis=0, keepdims=True))
    o_ref[pl.ds(out_row, 1), :] = jnp.concatenate(parts, axis=1)
    pages.end()


def _dec_select_kernel(s_ref, qi_ref, ki_ref, wi_ref, sel_ref, sx, *, topk):
    nb, past = s_ref.shape
    inf = jnp.inf
    qi = qi_ref[...].astype(F32)
    ki = ki_ref[...].astype(F32)
    prod = qi * jnp.concatenate([ki] * N_IDX_HEADS, axis=1)
    nq = N_IDX_HEADS * IDX_DIM
    hsel = jnp.where(_iota((nq, LANES), 0) // IDX_DIM == _iota((nq, LANES), 1), 1.0, 0.0).astype(BF16)
    sh = _sum_dot(prod, hsel)[:, :N_IDX_HEADS]
    s_new = jnp.sum(jnp.maximum(sh, 0.0) * wi_ref[...], axis=1, keepdims=True)
    sx[:, :past] = s_ref[...]
    sx[:, past:] = jnp.where(_iota((nb, LANES), 1) == 0, s_new, -inf)

    rsum = lambda a: jnp.sum(a, axis=1, keepdims=True)
    k_q = float(topk)
    s = sx[...]
    lo = jnp.min(jnp.where(s > -inf, s, inf), axis=1, keepdims=True)
    hi = jnp.max(s, axis=1, keepdims=True)

    def bisect(_, lh):
        lo, hi = lh
        mid = 0.5 * lo + 0.5 * hi
        ge = rsum(jnp.where(sx[...] >= mid, 1.0, 0.0)) >= k_q
        return jnp.where(ge, mid, lo), jnp.where(ge, hi, mid)

    lo, hi = lax.fori_loop(0, N_BISECT, bisect, (lo, hi))
    s = sx[...]
    v0 = jnp.min(jnp.where(s >= lo, s, inf), axis=1, keepdims=True)

    def refine(state):
        v, _, _ = state
        s = sx[...]
        g = s > v
        cnt = rsum(jnp.where(g, 1.0, 0.0))
        v2 = jnp.min(jnp.where(g, s, inf), axis=1, keepdims=True)
        ok = cnt < k_q
        return jnp.where(ok, v, v2), cnt, jnp.sum(jnp.where(ok, 0.0, 1.0))

    thr, c_gt, _ = lax.while_loop(lambda st: st[2] > 0.0, refine,
                                  (v0, jnp.zeros((nb, 1), F32), jnp.float32(1.0)))
    need = k_q - c_gt
    upper = jnp.where(_iota((LANES, LANES), 0) < _iota((LANES, LANES), 1), 1.0, 0.0).astype(BF16)
    carry = jnp.zeros((nb, 1), F32)
    for c in range((past + LANES) // LANES):
        sc = sx[:, c * LANES:(c + 1) * LANES]
        eq = sc == thr
        eqf = jnp.where(eq, 1.0, 0.0)
        rank = carry + _dot(eqf.astype(BF16), upper)
        sel_ref[:, c * LANES:(c + 1) * LANES] = jnp.where((sc > thr) | (eq & (rank < need)), 1.0, 0.0)
        carry = carry + rsum(eqf)


def _dec_select(scores, qi, ki, wi, topk):
    nb, past = scores.shape
    args = [scores, qi, ki, wi]
    return pl.pallas_call(
        functools.partial(_dec_select_kernel, topk=topk),
        in_specs=[pl.BlockSpec(a.shape, lambda: (0, 0)) for a in args],
        out_specs=pl.BlockSpec((nb, past + LANES), lambda: (0, 0)),
        out_shape=jax.ShapeDtypeStruct((nb, past + LANES), F32),
        scratch_shapes=[pltpu.VMEM((nb, past + LANES), F32)],
        compiler_params=pltpu.CompilerParams(vmem_limit_bytes=VMEM_LIMIT),
        name="dec_select",
    )(*args)


def _dec_attn_kernel(pt_ref, q_ref, sel_ref, knew_ref, vnew_ref, sg_ref, k_hbm, v_hbm, o_ref,
                     kbuf, vbuf, ksem, vsem):
    k_pages = _PageFetcher(pt_ref, k_hbm, kbuf, ksem)
    v_pages = _PageFetcher(pt_ref, v_hbm, vbuf, vsem)
    n_pages = k_pages.n_pages
    past = n_pages * PAGE_SIZE
    inf = jnp.inf
    nh = N_Q_HEADS
    R = N_Q_HEADS // N_KV_HEADS

    q8 = q_ref[0].astype(F32)
    own = (_iota((nh, KV_WIDTH), 0) // R) == (_iota((nh, KV_WIDTH), 1) // HEAD_DIM)
    q_ext = jnp.where(own, jnp.concatenate([q8, q8], axis=1), 0.0)
    q16 = jnp.concatenate([q_ext, jnp.zeros_like(q_ext)], axis=0).astype(BF16)
    sel = sel_ref[pl.ds(pl.program_id(0) % SUBLANES, 1), :]

    k_pages.begin()
    parts = []
    for j in range(n_pages):
        k_pages.prefetch(j)
        parts.append(_dot(q16, k_pages.page(j).astype(BF16))[:nh])
    k_pages.end()
    k_new = knew_ref[0].astype(BF16).astype(F32)
    v_new = vnew_ref[0].astype(BF16).astype(F32)
    lg_new = jnp.sum(q16[:nh].astype(F32) * k_new, axis=1, keepdims=True)
    lm = jnp.where(sel[:, :past] > 0.5, jnp.concatenate(parts, axis=1), -inf)
    lm_new = jnp.where(sel[:, past:past + 1] > 0.5, lg_new, -inf)
    m = jnp.maximum(jnp.max(lm, axis=1, keepdims=True), lm_new)
    p = jnp.exp2(lm - m)
    p_new = jnp.exp2(lm_new - m)
    l = jnp.sum(p, axis=1, keepdims=True) + p_new
    p16 = jnp.concatenate([p, jnp.zeros_like(p)], axis=0).astype(BF16)

    v_pages.begin()
    acc = jnp.zeros((2 * nh, KV_WIDTH), F32)
    for j in range(n_pages):
        v_pages.prefetch(j)
        acc = acc + _dot_nt(p16[:, j * PAGE_SIZE:(j + 1) * PAGE_SIZE], v_pages.page(j).astype(BF16))
    v_pages.end()
    out = (acc[:nh] + p_new.astype(BF16).astype(F32) * v_new) / l
    res = jnp.where(_iota((nh, HEAD_DIM), 0) < R, out[:, :HEAD_DIM], out[:, HEAD_DIM:])
    o_ref[0] = (res * sg_ref[0]).astype(BF16)


def _dec_attn(page_table, q3, sel, k_new3, v_new3, sg3, cache_k2, cache_v2):
    nb, n_pages = page_table.shape
    per_seq = lambda r, c: pl.BlockSpec((1, r, c), lambda b, pt: (b, 0, 0))
    page_buf = pltpu.VMEM((2, n_pages, KV_WIDTH, PAGE_SIZE), F32)
    grid_spec = pltpu.PrefetchScalarGridSpec(
        num_scalar_prefetch=1,
        grid=(nb,),
        in_specs=[per_seq(N_Q_HEADS, HEAD_DIM),
                  pl.BlockSpec((SUBLANES, sel.shape[1]), lambda b, pt: (b // SUBLANES, 0)),
                  per_seq(1, KV_WIDTH), per_seq(1, KV_WIDTH), per_seq(N_Q_HEADS, HEAD_DIM),
                  pl.BlockSpec(memory_space=pl.ANY), pl.BlockSpec(memory_space=pl.ANY)],
        out_specs=per_seq(N_Q_HEADS, HEAD_DIM),
        scratch_shapes=[page_buf, page_buf, pltpu.SemaphoreType.DMA((2,)), pltpu.SemaphoreType.DMA((2,))],
    )
    return pl.pallas_call(
        _dec_attn_kernel,
        grid_spec=grid_spec,
        out_shape=jax.ShapeDtypeStruct((nb, N_Q_HEADS, HEAD_DIM), BF16),
        compiler_params=_cparams(("arbitrary",)),
        name="dec_attn",
    )(page_table, q3, sel, k_new3, v_new3, sg3, cache_k2, cache_v2)


def kernel(x_prompt, x_sample, cache_k, cache_v, cache_kidx, state_wkv, state_shift, page_table,
           c_prompt, c_sample, norm_w, w_ada, b_ada, w_in, q_norm_w, k_norm_w, mu_shift, w0, w_up,
           a0, a_up, k_k, k_a, r_k, ln_x_w, ln_x_b, w_out):
    nb, t, d = x_prompt.shape
    nd = x_sample.shape[0]
    assert x_sample.shape[1] == 1
    n_pages = page_table.shape[1]
    past = n_pages * PAGE_SIZE
    w = RWKV_WIDTH

    offs = [0]
    for sz in (ATT_WIDTH, KV_WIDTH, KV_WIDTH, N_IDX_HEADS * IDX_DIM, N_IDX_HEADS, IDX_DIM, ATT_WIDTH,
               w, w, w, LORA, LORA, w):
        offs.append(offs[-1] + sz)
    w_in_t = w_in.T
    row = lambda i: w_in_t[offs[i]:offs[i + 1]]
    w_main = jnp.concatenate([row(0), row(1), row(2), row(3), row(6), row(7), row(8), row(9), row(12),
                              row(10), row(11), row(5), jnp.zeros((LANES - IDX_DIM, d), w_in.dtype)], axis=0).astype(BF16)
    zl = jnp.zeros((LORA, w), F32)
    wts = {
        "norm_w": norm_w.reshape(1, d), "w_main": w_main,
        "w_wit": row(4).astype(BF16), "w_vt": row(2).astype(BF16),
        "qnw": jnp.concatenate([q_norm_w, q_norm_w]).reshape(1, LANES),
        "knw": jnp.concatenate([k_norm_w, k_norm_w]).reshape(1, LANES),
    }
    rw_params = {
        "mu_rkv": mu_shift[:3 * w].reshape(1, 3 * w), "mu_lora": mu_shift[3 * w:].reshape(1, 2 * LORA),
        "w0": w0.reshape(1, w), "a0": a0.reshape(1, w),
        "w_lora": jnp.concatenate([jnp.concatenate([w_up, zl], axis=1),
                                   jnp.concatenate([zl, a_up], axis=1)], axis=0).astype(BF16),
        "k_k": k_k.reshape(1, w), "k_a": k_a.reshape(1, w), "r_k": r_k.reshape(1, w),
        "ln_w": ln_x_w.reshape(1, w), "ln_b": ln_x_b.reshape(1, w),
    }
    w_out_bf = w_out.astype(BF16)

    n_c = nb + nd
    pad = (-n_c) % 8
    c_all = jnp.concatenate([c_prompt, c_sample, jnp.zeros((pad, d), F32)], axis=0)
    mod = _adaln_mod(c_all, w_ada, b_ada)
    shift_p, scale_p, gate_p = (mod[:nb, s * d:(s + 1) * d].reshape(nb, 1, d) for s in range(3))
    shift_s, scale_s, gate_s = (mod[nb:n_c, s * d:(s + 1) * d].reshape(1, nd, d) for s in range(3))

    tm = min(512, t)
    pr = _in_proj(x_prompt, shift_p, scale_p, _rope_tables(jnp.arange(t)), wts, tm)
    xs3 = x_sample.reshape(1, nd, d)
    sm = _in_proj(xs3, shift_s, scale_s, _rope_tables(jnp.full((1,), past)), wts, nd)
    qi_s, ki_s = sm["qi"][0], sm["ki"][0]
    wi_s = sm["wit"][0].T
    n_phys = cache_k.shape[0]
    k_pages = jnp.transpose(cache_k, (0, 2, 3, 1)).reshape(n_phys, KV_WIDTH, PAGE_SIZE)
    v_pages = jnp.transpose(cache_v, (0, 2, 3, 1)).reshape(n_phys, KV_WIDTH, PAGE_SIZE)

    att_p, scores = _attn_prompt(pr["q"], pr["qi"], pr["wit"], pr["kbf"], pr["vt"], pr["kibf"], pr["sg"],
                                 page_table, qi_s.reshape(nd, N_IDX_HEADS, IDX_DIM),
                                 wi_s.reshape(nd, N_IDX_HEADS, 1), jnp.transpose(cache_kidx, (0, 2, 1)))
    rw_p, s_p = _rwkv_prompt(pr["rkv"], pr["lora"], pr["gr"], rw_params)
    y_prompt = _out_proj(att_p, rw_p, w_out_bf, x_prompt, gate_p, tm)
    shift_prompt = jnp.concatenate([pr["rkv"][:, t - 1:], pr["lora"][:, t - 1:]], axis=-1)

    topk = min(TOPK_MAX, (past + 1) // 4)
    sel = _dec_select(scores, qi_s, sm["kibf"][0], wi_s, topk)
    att_s = _dec_attn(page_table, sm["q"][0].reshape(nd, N_Q_HEADS, HEAD_DIM), sel,
                      sm["k"][0].reshape(nd, 1, KV_WIDTH), sm["v"][0].reshape(nd, 1, KV_WIDTH),
                      sm["sg"][0].reshape(nd, N_Q_HEADS, HEAD_DIM),
                      k_pages, v_pages)
    prev = state_shift.reshape(nd, -1)
    vecs = _rwkv_dec_prep(sm["rkv"][0], sm["lora"][0], prev[:, :3 * w], prev[:, 3 * w:], rw_params)
    s_t, y_t = _rwkv_dec_state(vecs[:6], jnp.transpose(state_wkv, (1, 2, 3, 0)))
    s_s = jnp.transpose(s_t, (3, 0, 1, 2))
    rw_s = _rwkv_dec_epi(y_t.reshape(w, nd), vecs[6], sm["gr"][0], rw_params)
    y_sample = _out_proj(att_s.reshape(1, nd, ATT_WIDTH), rw_s.reshape(1, nd, w), w_out_bf, xs3, gate_s, nd)
    shift_sample = jnp.concatenate([sm["rkv"][0], sm["lora"][0]], axis=-1).reshape(nd, 1, 3 * w + 2 * LORA)

    return (y_prompt, y_sample.reshape(nd, 1, d),
            pr["k"].reshape(nb, t, N_KV_HEADS, HEAD_DIM), pr["v"].reshape(nb, t, N_KV_HEADS, HEAD_DIM), pr["ki"],
            s_p, shift_prompt,
            sm["k"][0].reshape(nd, 1, N_KV_HEADS, HEAD_DIM), sm["v"][0].reshape(nd, 1, N_KV_HEADS, HEAD_DIM),
            ki_s.reshape(nd, 1, IDX_DIM), s_s, shift_sample)
```

```python
import functools

import jax
import jax.numpy as jnp
from jax import lax
from jax.experimental import pallas as pl
from jax.experimental.pallas import tpu as pltpu

F32, BF16 = jnp.float32, jnp.bfloat16

HEAD_DIM = 64
N_Q_HEADS = 8
N_KV_HEADS = 2
N_IDX_HEADS = 8
IDX_DIM = 64
N_RWKV_HEADS = 8
ATT_WIDTH = N_Q_HEADS * HEAD_DIM
RWKV_WIDTH = N_RWKV_HEADS * HEAD_DIM
KV_WIDTH = N_KV_HEADS * HEAD_DIM
LORA = 64
TOPK_MAX = 256
ROPE_THETA = 500000.0
ROPE_DIMS = HEAD_DIM // 4
NORM_EPS = 1e-6
GN_EPS = 64e-5
PAGE_SIZE = 128
LANES = 128
SUBLANES = 8
Q_SCALE = HEAD_DIM ** -0.5 * 1.4426950408889634
VMEM_LIMIT = 56 * 1024 * 1024

Q_BLOCK = 256
KEY_CHUNK = 512
N_BISECT = 16
RW_CHUNK = 64
RW_BLOCK = 128


def _dot(a, b):
    return jnp.dot(a, b, preferred_element_type=F32)


def _dot_nt(a, b):
    return lax.dot_general(a, b, (((1,), (1,)), ((), ())), preferred_element_type=F32)


def _split(x):
    hi = x.astype(BF16)
    return hi, (x - hi.astype(F32)).astype(BF16)


def _sum_dot(x, sel01):
    hi, lo = _split(x)
    return _dot(hi, sel01) + _dot(lo, sel01)


def _sum_dot_left(sel01, x):
    hi, lo = _split(x)
    return _dot(sel01, hi) + _dot(sel01, lo)


def _dot_tn(a, b):
    return lax.dot_general(a, b, (((0,), (0,)), ((), ())), preferred_element_type=F32)


def _iota(shape, dim):
    return lax.broadcasted_iota(jnp.int32, shape, dim)


def _head_block_ones(n):
    return jnp.where(_iota((n, n), 0) // HEAD_DIM == _iota((n, n), 1) // HEAD_DIM, 1.0, 0.0).astype(BF16)


def _silu(x):
    return x * jax.nn.sigmoid(x)


def _cparams(sem):
    return pltpu.CompilerParams(dimension_semantics=sem, vmem_limit_bytes=VMEM_LIMIT)


def _mod_kernel(c_ref, w_ref, b_ref, o_ref):
    s = _silu(c_ref[...])
    o_ref[...] = _dot(s.astype(BF16), w_ref[...].astype(BF16)) + b_ref[...]


def _adaln_mod(c, w_ada, b_ada):
    rows, d = c.shape
    n = w_ada.shape[1]
    tn = 1024
    return pl.pallas_call(
        _mod_kernel,
        grid=(n // tn,),
        in_specs=[pl.BlockSpec((rows, d), lambda j: (0, 0)),
                  pl.BlockSpec((d, tn), lambda j: (0, j)),
                  pl.BlockSpec((1, tn), lambda j: (0, j))],
        out_specs=pl.BlockSpec((rows, tn), lambda j: (0, j)),
        out_shape=jax.ShapeDtypeStruct((rows, n), F32),
        compiler_params=_cparams(("arbitrary",)),
        name="adaln_mod",
    )(c, w_ada, b_ada.reshape(1, n))


_W_Q, _W_KV, _W_QI, _W_GA, _W_RKV, _W_GR, _W_TAIL, _W_END = 0, 512, 768, 1280, 1792, 3328, 3840, 4096


def _inproj_kernel(x_ref, shift_ref, scale_ref, nw_ref, w_ref, wwi_ref, wvt_ref, cos_ref, s1_ref, s2_ref,
                   qnw_ref, knw_ref,
                   q_o, k_o, kbf_o, v_o, vt_o, qi_o, ki_o, kibf_o, wit_o, sg_o, rkv_o, lora_o, gr_o):
    x = x_ref[0]
    ms = jnp.mean(x * x, axis=-1, keepdims=True)
    h = x * lax.rsqrt(ms + NORM_EPS) * nw_ref[...]
    h = h * (1.0 + scale_ref[0]) + shift_ref[0]
    hb = h.astype(BF16)
    cos, s1, s2 = cos_ref[...], s1_ref[...], s2_ref[...]
    bsum = _head_block_ones(LANES)

    def seg(a, b):
        return _dot_nt(hb, w_ref[a:b, :])

    def rope(t):
        return t * cos + pltpu.roll(t, LANES - ROPE_DIMS // 2, 1) * s1 + pltpu.roll(t, ROPE_DIMS // 2, 1) * s2

    def head_norm(t, w):
        return t * lax.rsqrt(_sum_dot(t * t, bsum) * (1.0 / HEAD_DIM) + NORM_EPS) * w

    yq = seg(_W_Q, _W_KV)
    for s in range(ATT_WIDTH // LANES):
        t = rope(head_norm(yq[:, s * LANES:(s + 1) * LANES], qnw_ref[...]))
        q_o[0, :, s * LANES:(s + 1) * LANES] = (t * Q_SCALE).astype(BF16)

    ykv = seg(_W_KV, _W_QI)
    k = rope(head_norm(ykv[:, :KV_WIDTH], knw_ref[...]))
    k_o[0] = k
    kbf_o[0] = k.astype(BF16)
    v_o[0] = ykv[:, KV_WIDTH:]
    vt_o[0] = _dot_nt(wvt_ref[...], hb).astype(BF16)

    yqi = seg(_W_QI, _W_GA)
    for s in range(N_IDX_HEADS * IDX_DIM // LANES):
        qi_o[0, :, s * LANES:(s + 1) * LANES] = rope(yqi[:, s * LANES:(s + 1) * LANES]).astype(BF16)

    sg_o[0] = _silu(seg(_W_GA, _W_RKV))
    for s in range(3):
        rkv_o[0, :, s * RWKV_WIDTH:(s + 1) * RWKV_WIDTH] = seg(_W_RKV + s * RWKV_WIDTH, _W_RKV + (s + 1) * RWKV_WIDTH)
    gr_o[0] = seg(_W_GR, _W_TAIL)

    yt = seg(_W_TAIL, _W_END)
    lora_o[0] = yt[:, :2 * LORA]
    ki = rope(yt[:, 2 * LORA:])[:, :IDX_DIM]
    ki_o[0] = ki
    kibf_o[0] = ki.astype(BF16)
    wit_o[0] = _dot_nt(wwi_ref[...], hb) * (N_IDX_HEADS ** -0.5 * IDX_DIM ** -0.5)


def _in_proj(x3, shift3, scale3, tabs, wts, tm):
    nb, t, d = x3.shape
    mod_rows = shift3.shape[1]
    tab_rows = tabs[0].shape[0]
    grid = (nb, t // tm)
    if mod_rows == 1:
        mod_spec = pl.BlockSpec((1, 1, d), lambda b, i: (b, 0, 0))
    else:
        mod_spec = pl.BlockSpec((1, tm, d), lambda b, i: (b, i, 0))
    if tab_rows == 1:
        tab_spec = pl.BlockSpec((1, LANES), lambda b, i: (0, 0))
    else:
        tab_spec = pl.BlockSpec((tm, LANES), lambda b, i: (i, 0))
    const = lambda shape: pl.BlockSpec(shape, lambda b, i: tuple(0 for _ in shape))
    row = lambda width: pl.BlockSpec((1, tm, width), lambda b, i: (b, i, 0))
    col = lambda rows: pl.BlockSpec((1, rows, tm), lambda b, i: (b, 0, i))
    out_defs = [
        ("q", (t, ATT_WIDTH), BF16, row(ATT_WIDTH)),
        ("k", (t, KV_WIDTH), F32, row(KV_WIDTH)),
        ("kbf", (t, KV_WIDTH), BF16, row(KV_WIDTH)),
        ("v", (t, KV_WIDTH), F32, row(KV_WIDTH)),
        ("vt", (KV_WIDTH, t), BF16, col(KV_WIDTH)),
        ("qi", (t, N_IDX_HEADS * IDX_DIM), BF16, row(N_IDX_HEADS * IDX_DIM)),
        ("ki", (t, IDX_DIM), F32, row(IDX_DIM)),
        ("kibf", (t, IDX_DIM), BF16, row(IDX_DIM)),
        ("wit", (N_IDX_HEADS, t), F32, col(N_IDX_HEADS)),
        ("sg", (t, ATT_WIDTH), F32, row(ATT_WIDTH)),
        ("rkv", (t, 3 * RWKV_WIDTH), F32, row(3 * RWKV_WIDTH)),
        ("lora", (t, 2 * LORA), F32, row(2 * LORA)),
        ("gr", (t, RWKV_WIDTH), F32, row(RWKV_WIDTH)),
    ]
    outs = pl.pallas_call(
        _inproj_kernel,
        grid=grid,
        in_specs=[pl.BlockSpec((1, tm, d), lambda b, i: (b, i, 0)), mod_spec, mod_spec, const((1, d)),
                  const(wts["w_main"].shape), const(wts["w_wit"].shape), const(wts["w_vt"].shape),
                  tab_spec, tab_spec, tab_spec, const((1, LANES)), const((1, LANES))],
        out_specs=[o[3] for o in out_defs],
        out_shape=[jax.ShapeDtypeStruct((nb,) + o[1], o[2]) for o in out_defs],
        compiler_params=_cparams(("arbitrary", "arbitrary")),
        name="in_proj",
    )(x3, shift3, scale3, wts["norm_w"], wts["w_main"], wts["w_wit"], wts["w_vt"], tabs[0], tabs[1], tabs[2],
      wts["qnw"], wts["knw"])
    return {o[0]: arr for o, arr in zip(out_defs, outs)}


def _rope_tables(pos):
    half = ROPE_DIMS // 2
    inv = jnp.power(ROPE_THETA, -jnp.arange(half, dtype=F32) / half)
    ang = pos.astype(F32)[:, None] * inv[None, :]
    cos, sin = jnp.cos(ang), jnp.sin(ang)
    n = pos.shape[0]
    pad = jnp.zeros((n, HEAD_DIM - ROPE_DIMS), F32)
    zero = jnp.zeros((n, half), F32)
    c_head = jnp.concatenate([cos, cos, pad + 1.0], axis=1)
    s1_head = jnp.concatenate([-sin, zero, pad], axis=1)
    s2_head = jnp.concatenate([zero, sin, pad], axis=1)
    two = lambda a: jnp.concatenate([a, a], axis=1)
    return two(c_head), two(s1_head), two(s2_head)


def _head_sum(t):
    ones = _head_block_ones(LANES)
    parts = [_sum_dot(t[:, s * LANES:(s + 1) * LANES], ones) for s in range(t.shape[1] // LANES)]
    return parts[0] if len(parts) == 1 else jnp.concatenate(parts, axis=1)


def _softplus(z):
    return jnp.maximum(z, 0.0) + jnp.log(1.0 + jnp.exp(-jnp.abs(z)))


def _rwkv_prep(xs, xl, prev, prevl, p):
    xm = xs + p["mu_rkv"] * (prev - xs)
    xml = xl + p["mu_lora"] * (prevl - xl)
    w = RWKV_WIDTH
    r, k, v = xm[:, :w], xm[:, w:2 * w], xm[:, 2 * w:]
    lane = _iota(xml.shape, 1)
    lhs = jnp.where(lane < LORA, jnp.tanh(xml), xml)
    lo = _dot(lhs.astype(BF16), p["w_lora"])
    wlog = -_softplus(-(p["w0"] + lo[:, :w])) - 0.5
    logw = -jnp.exp(wlog)
    a = jax.nn.sigmoid(p["a0"] + lo[:, w:])
    kkr = k * p["k_k"]
    kk = kkr / jnp.maximum(jnp.sqrt(_head_sum(kkr * kkr)), 1e-12)
    k_mod = k * (1.0 + (a - 1.0) * p["k_a"])
    bonus = _head_sum(r * k_mod * p["r_k"]) * v
    return r, logw, k_mod, v, -kk, kk * a, bonus


def _rwkv_epilogue(y, bonus, gate, p):
    mean = _head_sum(y) * (1.0 / HEAD_DIM)
    d = y - mean
    var = _head_sum(d * d) * (1.0 / HEAD_DIM)
    yn = d * lax.rsqrt(var + GN_EPS) * p["ln_w"] + p["ln_b"]
    return (yn + bonus) * _silu(gate)


_RW_PARAM_NAMES = ("mu_rkv", "mu_lora", "w0", "a0", "w_lora", "k_k", "k_a", "r_k", "ln_w", "ln_b")


def _load_params(refs):
    return {n: r[...] for n, r in zip(_RW_PARAM_NAMES, refs)}


def _rwkv_prompt_kernel(pt_ref, rkv_ref, lora_ref, gr_ref, *rest):
    nparam = len(_RW_PARAM_NAMES)
    p = _load_params(rest[:nparam])
    dq_ref, dsel_ref, dknew_ref, dvnew_ref, dsg_ref, dk_hbm, dv_hbm = rest[nparam:nparam + 7]
    out_ref, sfin_ref, dout_ref = rest[nparam + 7:nparam + 10]
    (st_s, c_rkv, c_lora, r_s, lw_s, k_s, v_s, al_s, be_s, y_s, bon_s,
     at_s, rt_s, bk_s, ec_s, wl_s, wh_s, zv_s, inv_s, dkbuf, dvbuf, dksem, dvsem) = rest[nparam + 10:]
    i = pl.program_id(0)
    nb, tb = rkv_ref.shape[0], rkv_ref.shape[1]
    C = RW_CHUNK
    W = 2 * C

    spd = dq_ref.shape[0]
    n_seq = spd * pl.num_programs(0)

    def sample_attention(u):
        s = i * spd + u
        k_pages = _PageFetcher(pt_ref, dk_hbm, dkbuf, dksem, s, n_seq)
        v_pages = _PageFetcher(pt_ref, dv_hbm, dvbuf, dvsem, s, n_seq)
        sel = dsel_ref[pl.ds(s % SUBLANES, 1), :]
        res = _dec_attend(k_pages, v_pages, dq_ref[u], sel, dknew_ref[u], dvnew_ref[u])
        dout_ref[u] = (res * dsg_ref[u]).astype(BF16)

    def side_stream(point):
        for u in range(spd * point // 4, spd * (point + 1) // 4):
            sample_attention(u)

    side_stream(0)

    @pl.when(i == 0)
    def _():
        st_s[...] = jnp.zeros_like(st_s)
        c_rkv[...] = jnp.zeros_like(c_rkv)
        c_lora[...] = jnp.zeros_like(c_lora)

    row = _iota((tb, 1), 0)
    for b in range(nb):
        xs, xl = rkv_ref[b], lora_ref[b]
        prev = jnp.where(row == 0, c_rkv[b], pltpu.roll(xs, 1, 0))
        prevl = jnp.where(row == 0, c_lora[b], pltpu.roll(xl, 1, 0))
        c_rkv[b] = xs[tb - 1:tb, :]
        c_lora[b] = xl[tb - 1:tb, :]
        r_s[b], lw_s[b], k_s[b], v_s[b], al_s[b], be_s[b], bon_s[b] = _rwkv_prep(xs, xl, prev, prevl, p)

    side_stream(1)

    rm, cn = _iota((W, W), 0), _iota((W, W), 1)
    tt, ss = rm % C, cn % C
    top, left = rm < C, cn < C
    strict, incl = ss < tt, ss <= tt
    mk_l_lo, mk_l_hi = top & left & strict, (~top) & (~left) & strict
    mk_z_lo, mk_z_hi = top & (~left) & strict, (~top) & left & strict
    mk_w = (_iota((C, W), 1) % C) <= _iota((C, W), 0)
    bd = (rm // HEAD_DIM) == (cn // HEAD_DIM)
    eye = jnp.where(rm == cn, 1.0, 0.0).astype(F32)
    tril = jnp.where(_iota((C, C), 1) <= _iota((C, C), 0), 1.0, 0.0).astype(BF16)
    lane_lo = _iota((C, LANES), 1) < HEAD_DIM
    m_lo = jnp.where(lane_lo, 1.0, 0.0).astype(F32)
    m_hi = 1.0 - m_lo

    bf = lambda a: a.astype(BF16)
    pairs = range(N_RWKV_HEADS // 2)
    lanes_of = lambda pr: slice(pr * LANES, (pr + 1) * LANES)
    rows_of = lambda c: pl.ds(pl.multiple_of(c * C, C), C)
    items = [(b, pr) for b in range(nb) for pr in pairs]

    def phase_a(c, carry):
        rows = rows_of(c)
        cums = []
        for b in range(nb):
            cum_all = _sum_dot_left(tril, lw_s[b, rows, :])
            cums += [cum_all[:, lanes_of(pr)] for pr in pairs]
        g_lo, g_hi = [], []
        for n, (b, pr) in enumerate(items):
            sl = lanes_of(pr)
            cum = cums[n]
            cum_c = cum[C - 1:C, :]
            e_neg, e_rel = jnp.exp(-cum), jnp.exp(cum_c - cum)
            rr, kk, aa, bb = r_s[b, rows, sl], k_s[b, rows, sl], al_s[b, rows, sl], be_s[b, rows, sl]
            a_t, r_t = aa * jnp.exp(cum - lw_s[b, rows, sl]), rr * jnp.exp(cum)
            b_t, k_t = bb * e_neg, kk * e_neg
            at_s[b, c, pr], rt_s[b, c, pr] = bf(a_t), bf(r_t)
            bk_s[b, c, pr] = jnp.concatenate([bb * e_rel, kk * e_rel], axis=0)
            ec_s[b, c, pr] = jnp.exp(cum_c)
            xk = bf(jnp.concatenate([b_t, k_t], axis=0))
            kx = bf(jnp.concatenate([k_t, b_t], axis=0))
            g_lo.append(_dot_nt(bf(jnp.concatenate([a_t * m_lo, r_t * m_lo], axis=0)), xk))
            g_hi.append(_dot_nt(bf(jnp.concatenate([r_t * m_hi, a_t * m_hi], axis=0)), kx))
        l_mat = []
        for n, (b, pr) in enumerate(items):
            l_mat.append(jnp.where(mk_l_lo, g_lo[n], 0.0) + jnp.where(mk_l_hi, g_hi[n], 0.0))
            z_mat = jnp.where(mk_z_lo, g_lo[n], 0.0) + jnp.where(mk_z_hi, g_hi[n], 0.0)
            wl_s[b, c, pr] = bf(jnp.where(mk_w, g_lo[n][C:, :], 0.0))
            wh_s[b, c, pr] = bf(jnp.where(mk_w, g_hi[n][:C, :], 0.0))
            vv_b = bf(v_s[b, rows, lanes_of(pr)])
            zv_s[b, c, pr] = _dot(bf(z_mat), jnp.concatenate([vv_b, vv_b], axis=0))
        inv = [eye + m for m in l_mat]
        pw = l_mat
        for _ in range(5):
            pw = [_dot(bf(m), bf(m)) for m in pw]
            inv = [x + _dot(bf(x), bf(m)) for x, m in zip(inv, pw)]
        for n, (b, pr) in enumerate(items):
            inv_s[b, c, pr] = bf(inv[n])
        return carry

    lax.fori_loop(0, tb // C, phase_a, 0)
    side_stream(2)

    def phase_b(c, carry):
        rows = rows_of(c)
        nit = range(len(items))
        st = [st_s[b, pr] for b, pr in items]
        st_b = [bf(s) for s in st]
        vv = [v_s[b, rows, lanes_of(pr)] for b, pr in items]
        a_s = [_dot_nt(at_s[b, c, pr], st_b[n]) for n, (b, pr) in enumerate(items)]
        u_st = [_dot(inv_s[b, c, pr], bf(zv_s[b, c, pr] + jnp.concatenate([a_s[n], a_s[n]], axis=0)))
                for n, (b, pr) in enumerate(items)]
        u = [jnp.where(lane_lo, x[:C], x[C:]) for x in u_st]
        uv = [jnp.concatenate([u[n], vv[n]], axis=0) for n in nit]
        upd = [_dot_tn(uv[n], bk_s[b, c, pr]) for n, (b, pr) in enumerate(items)]
        for n, (b, pr) in enumerate(items):
            st_s[b, pr] = st[n] * ec_s[b, c, pr] + jnp.where(bd, upd[n], 0.0)
        for n, (b, pr) in enumerate(items):
            uv_b = bf(uv[n])
            vu_b = jnp.concatenate([uv_b[C:], uv_b[:C]], axis=0)
            y_in = jnp.where(lane_lo, _dot(wl_s[b, c, pr], uv_b), _dot(wh_s[b, c, pr], vu_b))
            y_s[b, rows, lanes_of(pr)] = _dot_nt(rt_s[b, c, pr], st_b[n]) + y_in
        return carry

    lax.fori_loop(0, tb // C, phase_b, 0)
    side_stream(3)
    for b in range(nb):
        out_ref[b] = _rwkv_epilogue(y_s[b], bon_s[b], gr_ref[b], p).astype(BF16)

    @pl.when(i == pl.num_programs(0) - 1)
    def _():
        for b in range(nb):
            for hd in range(N_RWKV_HEADS):
                o = (hd % 2) * HEAD_DIM
                sfin_ref[b, hd] = st_s[b, hd // 2][o:o + HEAD_DIM, o:o + HEAD_DIM]


def _rwkv_prompt(rkv, lora, gr, params, page_table, dq3, dsel, dk_new3, dv_new3, dsg3, k_pages, v_pages):
    nb, t, _ = rkv.shape
    tb = min(RW_BLOCK, t)
    nsteps = t // tb
    nd, n_pages = page_table.shape
    spd = nd // nsteps
    assert spd * nsteps == nd and SUBLANES % spd == 0
    w = RWKV_WIDTH
    blk = lambda width: pl.BlockSpec((nb, tb, width), lambda i, pt: (0, i, 0))
    pspecs = [pl.BlockSpec(params[n].shape, lambda i, pt: (0, 0)) for n in _RW_PARAM_NAMES]
    per_step = lambda r, c: pl.BlockSpec((spd, r, c), lambda i, pt: (i, 0, 0))
    vec = lambda: pltpu.VMEM((nb, tb, w), F32)
    nch, npair, c = tb // RW_CHUNK, N_RWKV_HEADS // 2, RW_CHUNK
    per_chunk = lambda rows, dt: pltpu.VMEM((nb, nch, npair, rows, LANES), dt)
    state_shape = (nb, N_RWKV_HEADS, HEAD_DIM, HEAD_DIM)
    page_buf = pltpu.VMEM((2, n_pages, KV_WIDTH, PAGE_SIZE), F32)
    grid_spec = pltpu.PrefetchScalarGridSpec(
        num_scalar_prefetch=1,
        grid=(nsteps,),
        in_specs=[blk(3 * w), blk(2 * LORA), blk(w)] + pspecs
                 + [per_step(N_Q_HEADS, HEAD_DIM),
                    pl.BlockSpec((SUBLANES, dsel.shape[1]), lambda i, pt: (i * spd // SUBLANES, 0)),
                    per_step(1, KV_WIDTH), per_step(1, KV_WIDTH), per_step(N_Q_HEADS, HEAD_DIM),
                    pl.BlockSpec(memory_space=pl.ANY), pl.BlockSpec(memory_space=pl.ANY)],
        out_specs=[blk(w), pl.BlockSpec(state_shape, lambda i, pt: (0, 0, 0, 0)), per_step(N_Q_HEADS, HEAD_DIM)],
        scratch_shapes=[pltpu.VMEM((nb, npair, LANES, LANES), F32),
                        pltpu.VMEM((nb, 1, 3 * w), F32), pltpu.VMEM((nb, 1, 2 * LORA), F32),
                        vec(), vec(), vec(), vec(), vec(), vec(), vec(), vec(),
                        per_chunk(c, BF16), per_chunk(c, BF16), per_chunk(2 * c, F32), per_chunk(1, F32),
                        per_chunk(c, BF16), per_chunk(c, BF16), per_chunk(2 * c, F32), per_chunk(2 * c, BF16),
                        page_buf, page_buf, pltpu.SemaphoreType.DMA((2,)), pltpu.SemaphoreType.DMA((2,))],
    )
    return pl.pallas_call(
        _rwkv_prompt_kernel,
        grid_spec=grid_spec,
        out_shape=[jax.ShapeDtypeStruct((nb, t, w), BF16), jax.ShapeDtypeStruct(state_shape, F32),
                   jax.ShapeDtypeStruct((nd, N_Q_HEADS, HEAD_DIM), BF16)],
        compiler_params=_cparams(("arbitrary",)),
        name="rwkv_prompt",
    )(page_table, rkv, lora, gr, *[params[n] for n in _RW_PARAM_NAMES],
      dq3, dsel, dk_new3, dv_new3, dsg3, k_pages, v_pages)


def _rwkv_dec_prep_kernel(rkv_ref, lora_ref, prev_ref, prevl_ref, *rest):
    nparam = len(_RW_PARAM_NAMES)
    p = _load_params(rest[:nparam])
    outs = rest[nparam:]
    r, logw, k_mod, v, al, be, bonus = _rwkv_prep(rkv_ref[...], lora_ref[...], prev_ref[...], prevl_ref[...], p)
    for o, val in zip(outs[:6], (r, jnp.exp(logw), k_mod, v, al, be)):
        o[...] = val.T
    outs[6][...] = bonus


def _rwkv_dec_prep(rkv, lora, prev, prevl, params):
    n = rkv.shape[0]
    full = lambda a: pl.BlockSpec(a.shape, lambda: tuple(0 for _ in a.shape))
    args = [rkv, lora, prev, prevl] + [params[k] for k in _RW_PARAM_NAMES]
    t_shape, shape = (RWKV_WIDTH, n), (n, RWKV_WIDTH)
    return pl.pallas_call(
        _rwkv_dec_prep_kernel,
        in_specs=[full(a) for a in args],
        out_specs=[pl.BlockSpec(t_shape, lambda: (0, 0))] * 6 + [pl.BlockSpec(shape, lambda: (0, 0))],
        out_shape=[jax.ShapeDtypeStruct(t_shape, F32)] * 6 + [jax.ShapeDtypeStruct(shape, F32)],
        compiler_params=pltpu.CompilerParams(vmem_limit_bytes=VMEM_LIMIT),
        name="rwkv_dec_prep",
    )(*args)


def _rwkv_dec_state_kernel(r_ref, w_ref, k_ref, v_ref, a_ref, b_ref, s_ref, so_ref, y_ref):
    r, w, k, a, b = r_ref[0], w_ref[0], k_ref[0], a_ref[0], b_ref[0]

    def one(i, carry):
        st = s_ref[0, i]
        sa = jnp.sum(st * a, axis=0, keepdims=True)
        sn = st * w + sa * b + v_ref[0, pl.ds(i, 1), :] * k
        so_ref[0, i] = sn
        y_ref[0, pl.ds(i, 1), :] = jnp.sum(sn * r, axis=0, keepdims=True)
        return carry

    lax.fori_loop(0, HEAD_DIM, one, 0)


def _rwkv_dec_state(vecs_t, state_t):
    nh, n, _, nb = state_t.shape
    vspec = pl.BlockSpec((1, n, nb), lambda h: (h, 0, 0))
    sspec = pl.BlockSpec((1, n, n, nb), lambda h: (h, 0, 0, 0))
    return pl.pallas_call(
        _rwkv_dec_state_kernel,
        grid=(nh,),
        in_specs=[vspec] * 6 + [sspec],
        out_specs=[sspec, vspec],
        out_shape=[jax.ShapeDtypeStruct(state_t.shape, F32), jax.ShapeDtypeStruct((nh, n, nb), F32)],
        compiler_params=_cparams(("arbitrary",)),
        name="rwkv_dec_state",
    )(*[a.reshape(nh, n, nb) for a in vecs_t], state_t)


def _rwkv_dec_epi_kernel(yt_ref, bonus_ref, gr_ref, lnw_ref, lnb_ref, o_ref):
    o_ref[...] = _rwkv_epilogue(yt_ref[...].T, bonus_ref[...], gr_ref[...],
                                {"ln_w": lnw_ref[...], "ln_b": lnb_ref[...]}).astype(BF16)


def _rwkv_dec_epi(y_t, bonus, gr, params):
    n = bonus.shape[0]
    args = [y_t, bonus, gr, params["ln_w"], params["ln_b"]]
    return pl.pallas_call(
        _rwkv_dec_epi_kernel,
        in_specs=[pl.BlockSpec(a.shape, lambda: (0, 0)) for a in args],
        out_specs=pl.BlockSpec((n, RWKV_WIDTH), lambda: (0, 0)),
        out_shape=jax.ShapeDtypeStruct((n, RWKV_WIDTH), BF16),
        name="rwkv_dec_epi",
    )(*args)


def _attn_prompt_kernel(pt_ref, q_ref, qi_ref, wit_ref, k_ref, vt_ref, kidx_ref, sg_ref, tri_ref,
                        dqi_ref, dwi_ref, dkidx_hbm, o_ref, dsc_ref,
                        s_scr, m_scr, acc_scr, *dscratch, topk):
    j = pl.program_id(1)
    step = pl.program_id(0) * pl.num_programs(1) + j
    nsteps = pl.num_programs(0) * pl.num_programs(1)
    spp = dqi_ref.shape[0]
    for u in range(spp):
        pages = _PageFetcher(pt_ref, dkidx_hbm, dscratch[u], dscratch[spp + u], step, nsteps, stride=spp, offset=u)
        _score_pages(pages, dqi_ref, dwi_ref, dsc_ref, u, (step % (SUBLANES // spp)) * spp + u)
    QB, KC = Q_BLOCK, KEY_CHUNK
    nch = (j * QB) // KC + 1
    inf = jnp.inf
    qpos_i = j * QB + _iota((1, QB), 1)
    k_q = jnp.minimum(float(topk), qpos_i.astype(F32) + 1.0)
    wit = wit_ref[0]
    qi = qi_ref[0]
    qi_stack = jnp.concatenate([qi[:, h * IDX_DIM:(h + 1) * IDX_DIM] for h in range(N_IDX_HEADS)], axis=0)

    def chunk_rows(c):
        return pl.ds(pl.multiple_of(c * KC, KC), KC)

    SUB = 32
    part = lambda a: a.reshape(KC // SUB, SUB, QB)
    psum = lambda a: jnp.sum(part(a), axis=0)
    pmin = lambda a: jnp.min(part(a), axis=0)
    pmax = lambda a: jnp.max(part(a), axis=0)
    full8 = lambda val: jnp.full((SUB, QB), val, F32)
    fold_sum = lambda a: jnp.sum(a, axis=0, keepdims=True)
    fold_min = lambda a: jnp.min(a, axis=0, keepdims=True)
    fold_max = lambda a: jnp.max(a, axis=0, keepdims=True)

    def score_chunk(c, lohi):
        rows = chunk_rows(c)
        st = _dot_nt(kidx_ref[0, rows, :], qi_stack)
        acc = jnp.zeros((KC, QB), F32)
        for h in range(N_IDX_HEADS):
            acc = acc + jnp.maximum(st[:, h * QB:(h + 1) * QB], 0.0) * wit[h:h + 1, :]
        ok = (c * KC + _iota((KC, QB), 0)) <= qpos_i
        s_scr[rows, :] = jnp.where(ok, acc, -inf)
        return (jnp.minimum(lohi[0], pmin(jnp.where(ok, acc, inf))),
                jnp.maximum(lohi[1], pmax(jnp.where(ok, acc, -inf))))

    lo8, hi8 = lax.fori_loop(0, nch, score_chunk, (full8(inf), full8(-inf)))
    lo, hi = fold_min(lo8), fold_max(hi8)

    def sweep(fn, init):
        return lax.fori_loop(0, nch, lambda c, car: fn(car, s_scr[chunk_rows(c), :]), init)

    def bisect(_, lh):
        lo, hi = lh
        mid = 0.5 * lo + 0.5 * hi
        cnt = fold_sum(sweep(lambda c, s: c + psum(jnp.where(s >= mid, 1.0, 0.0)), full8(0.0)))
        ge = cnt >= k_q
        return jnp.where(ge, mid, lo), jnp.where(ge, hi, mid)

    lo, hi = lax.fori_loop(0, N_BISECT, bisect, (lo, hi))
    v0 = fold_min(sweep(lambda c, s: jnp.minimum(c, pmin(jnp.where(s >= lo, s, inf))), full8(inf)))

    def refine(state):
        v, _, _ = state

        def f(car, s):
            g = s > v
            return car[0] + psum(jnp.where(g, 1.0, 0.0)), jnp.minimum(car[1], pmin(jnp.where(g, s, inf)))

        cnt8, v28 = sweep(f, (full8(0.0), full8(inf)))
        cnt, v2 = fold_sum(cnt8), fold_min(v28)
        ok = cnt < k_q
        return jnp.where(ok, v, v2), cnt, jnp.sum(jnp.where(ok, 0.0, 1.0))

    full = lambda val: jnp.full((1, QB), val, F32)
    thr, c_gt, _ = lax.while_loop(lambda st: st[2] > 0.0, refine, (v0, full(0.0), jnp.float32(1.0)))
    need = k_q - c_gt

    q = q_ref[0]
    G, R = N_KV_HEADS, N_Q_HEADS // N_KV_HEADS
    q_stack = [jnp.concatenate([q[:, (g * R + r) * HEAD_DIM:(g * R + r + 1) * HEAD_DIM] for r in range(R)], axis=0)
               for g in range(G)]
    ones_rows = jnp.where(_iota((HEAD_DIM, KC), 0) == 0, 1.0, 0.0).astype(BF16)
    m_scr[...] = jnp.full(m_scr.shape, -inf, F32)
    acc_scr[...] = jnp.zeros_like(acc_scr)

    n_ge = fold_sum(sweep(lambda c, s: c + psum(jnp.where(s >= thr, 1.0, 0.0)), full8(0.0)))
    any_ties = jnp.sum(jnp.where(n_ge == k_q, 0.0, 1.0)) > 0.0

    def make_attend(with_ties):
        def attend(c, tie_carry):
            rows = chunk_rows(c)
            s = s_scr[rows, :]
            if with_ties:
                eq = s == thr
                rank = tie_carry + _dot(tri_ref[...], jnp.where(eq, 1.0, 0.0).astype(BF16))
                sel = (s > thr) | (eq & (rank < need))
                tie_carry = tie_carry + fold_sum(psum(jnp.where(eq, 1.0, 0.0)))
            else:
                sel = s >= thr
            bias = jnp.where(sel, 0.0, -inf).astype(BF16)
            bias_r = jnp.concatenate([bias] * R, axis=1)
            kc = k_ref[0, rows, :]
            logits = [_dot_nt(kc[:, g * HEAD_DIM:(g + 1) * HEAD_DIM], q_stack[g]) for g in range(G)]
            probs, alphas = [], []
            for g in range(G):
                lm = logits[g].astype(BF16) + bias_r
                m_old = m_scr[g]
                m_part = jnp.max(lm.reshape(KC // 16, 16, R * QB), axis=0).astype(F32)
                m_new = jnp.maximum(m_old, jnp.max(m_part, axis=0, keepdims=True))
                m_safe = jnp.where(m_new == -inf, 0.0, m_new)
                alphas.append(jnp.exp2(m_old - m_safe))
                probs.append(jnp.exp2(lm - m_safe.astype(BF16)))
                m_scr[g] = m_new
            for g in range(G):
                vt_ext = jnp.concatenate([vt_ref[0, g * HEAD_DIM:(g + 1) * HEAD_DIM, rows], ones_rows], axis=0)
                acc_scr[g] = acc_scr[g] * alphas[g] + _dot(vt_ext, probs[g])
            return tie_carry
        return attend

    @pl.when(any_ties)
    def _():
        lax.fori_loop(0, nch, make_attend(True), full(0.0))

    @pl.when(jnp.logical_not(any_ties))
    def _():
        lax.fori_loop(0, nch, make_attend(False), full(0.0))

    out_t = jnp.concatenate([acc_scr[g][:HEAD_DIM] / acc_scr[g][HEAD_DIM:HEAD_DIM + 1] for g in range(G)], axis=0)
    per_r = [out_t[:, r * QB:(r + 1) * QB].T for r in range(R)]
    att = jnp.concatenate([per_r[r][:, g * HEAD_DIM:(g + 1) * HEAD_DIM] for g in range(G) for r in range(R)], axis=1)
    o_ref[0] = (att * sg_ref[0]).astype(BF16)


def _attn_prompt(q, qi, wit, kbf, vt, kibf, sg, page_table, dqi3, dwi3, dkidx_t):
    nb, t, _ = q.shape
    nj = t // Q_BLOCK
    nd, n_pages = page_table.shape
    spp = nd // (nb * nj)
    assert spp * nb * nj == nd and SUBLANES % spp == 0
    topk = min(TOPK_MAX, t // 4)
    kc = KEY_CHUNK
    tri = jnp.where(jnp.arange(kc)[None, :] < jnp.arange(kc)[:, None], 1.0, 0.0).astype(BF16)
    qblk = lambda width: pl.BlockSpec((1, Q_BLOCK, width), lambda b, j, pt: (b, j, 0))
    whole = lambda r, c: pl.BlockSpec((1, r, c), lambda b, j, pt: (b, 0, 0))
    per_step = lambda r, c: pl.BlockSpec((spp, r, c), lambda b, j, pt: (b * nj + j, 0, 0))
    grid_spec = pltpu.PrefetchScalarGridSpec(
        num_scalar_prefetch=1,
        grid=(nb, nj),
        in_specs=[qblk(ATT_WIDTH), qblk(N_IDX_HEADS * IDX_DIM),
                  pl.BlockSpec((1, N_IDX_HEADS, Q_BLOCK), lambda b, j, pt: (b, 0, j)),
                  whole(t, KV_WIDTH), whole(KV_WIDTH, t), whole(t, IDX_DIM), qblk(ATT_WIDTH),
                  pl.BlockSpec((kc, kc), lambda b, j, pt: (0, 0)),
                  per_step(N_IDX_HEADS, IDX_DIM), per_step(N_IDX_HEADS, 1), pl.BlockSpec(memory_space=pl.ANY)],
        out_specs=[qblk(ATT_WIDTH),
                   pl.BlockSpec((SUBLANES, n_pages * PAGE_SIZE), lambda b, j, pt: ((b * nj + j) // (SUBLANES // spp), 0))],
        scratch_shapes=[pltpu.VMEM((t, Q_BLOCK), F32),
                        pltpu.VMEM((N_KV_HEADS, 1, N_Q_HEADS // N_KV_HEADS * Q_BLOCK), F32),
                        pltpu.VMEM((N_KV_HEADS, 2 * HEAD_DIM, N_Q_HEADS // N_KV_HEADS * Q_BLOCK), F32)]
                       + [pltpu.VMEM((2, n_pages, IDX_DIM, PAGE_SIZE), F32)] * spp
                       + [pltpu.SemaphoreType.DMA((2,))] * spp,
    )
    return pl.pallas_call(
        functools.partial(_attn_prompt_kernel, topk=topk),
        grid_spec=grid_spec,
        out_shape=[jax.ShapeDtypeStruct((nb, t, ATT_WIDTH), BF16),
                   jax.ShapeDtypeStruct((nd, n_pages * PAGE_SIZE), F32)],
        compiler_params=_cparams(("arbitrary", "arbitrary")),
        name="attn_prompt",
    )(page_table, q, qi, wit, kbf, vt, kibf, sg, tri, dqi3, dwi3, dkidx_t)


def _outproj_kernel(att_ref, rw_ref, w_ref, x_ref, g_ref, o_ref):
    mix = _dot(att_ref[0], w_ref[:ATT_WIDTH, :]) + _dot(rw_ref[0], w_ref[ATT_WIDTH:, :])
    o_ref[0] = x_ref[0] + g_ref[0] * mix


def _out_proj(att, rw, w_out_bf, x3, gate3, tm):
    nb, t, d = x3.shape
    if gate3.shape[1] == 1:
        gspec = pl.BlockSpec((1, 1, d), lambda b, i: (b, 0, 0))
    else:
        gspec = pl.BlockSpec((1, tm, d), lambda b, i: (b, i, 0))
    blk = lambda width: pl.BlockSpec((1, tm, width), lambda b, i: (b, i, 0))
    return pl.pallas_call(
        _outproj_kernel,
        grid=(nb, t // tm),
        in_specs=[blk(ATT_WIDTH), blk(RWKV_WIDTH), pl.BlockSpec(w_out_bf.shape, lambda b, i: (0, 0)), blk(d), gspec],
        out_specs=blk(d),
        out_shape=jax.ShapeDtypeStruct((nb, t, d), F32),
        compiler_params=_cparams(("arbitrary", "arbitrary")),
        name="out_proj",
    )(att, rw, w_out_bf, x3, gate3)


class _PageFetcher:
    def __init__(self, pt_ref, hbm, buf, sem, step=None, nsteps=None, stride=1, offset=0):
        self.pt, self.hbm, self.buf, self.sem = pt_ref, hbm, buf, sem
        self.n_pages = buf.shape[1]
        b = pl.program_id(0) if step is None else step
        nb = pl.num_programs(0) if nsteps is None else nsteps
        self.first, self.last = b == 0, b == nb - 1
        self.slot = b % 2
        self.first_seq = offset
        self.next_seq = jnp.minimum(b + 1, nb - 1) * stride + offset

    def _copy(self, seq, j, slot):
        return pltpu.make_async_copy(self.hbm.at[self.pt[seq, j]], self.buf.at[slot, j], self.sem.at[slot])

    def _wait(self, slot):
        pltpu.make_async_copy(self.hbm.at[pl.ds(0, self.n_pages)], self.buf.at[slot], self.sem.at[slot]).wait()

    def begin(self):
        @pl.when(self.first)
        def _():
            for j in range(self.n_pages):
                self._copy(self.first_seq, j, 0).start()
        self._wait(self.slot)

    def prefetch(self, j):
        self._copy(self.next_seq, j, 1 - self.slot).start(priority=j % 2)

    def page(self, j):
        return self.buf[self.slot, j]

    def end(self):
        @pl.when(self.last)
        def _():
            self._wait(1 - self.slot)


def _score_pages(pages, qi_ref, wi_ref, o_ref, u, out_row):
    pages.begin()
    qi = qi_ref[u]
    qi16 = jnp.concatenate([qi, jnp.zeros_like(qi)], axis=0).astype(BF16)
    wi = wi_ref[u]
    parts = []
    for j in range(pages.n_pages):
        pages.prefetch(j)
        s = _dot(qi16, pages.page(j).astype(BF16))[:N_IDX_HEADS]
        parts.append(jnp.sum(jnp.maximum(s, 0.0) * wi, axis=0, keepdims=True))
    o_ref[pl.ds(out_row, 1), :] = jnp.concatenate(parts, axis=1)
    pages.end()


def _dec_select_kernel(s_ref, qi_ref, ki_ref, wi_ref, sel_ref, sx, *, topk):
    nb, past = s_ref.shape
    inf = jnp.inf
    qi = qi_ref[...].astype(F32)
    ki = ki_ref[...].astype(F32)
    prod = qi * jnp.concatenate([ki] * N_IDX_HEADS, axis=1)
    nq = N_IDX_HEADS * IDX_DIM
    hsel = jnp.where(_iota((nq, LANES), 0) // IDX_DIM == _iota((nq, LANES), 1), 1.0, 0.0).astype(BF16)
    sh = _sum_dot(prod, hsel)[:, :N_IDX_HEADS]
    s_new = jnp.sum(jnp.maximum(sh, 0.0) * wi_ref[...], axis=1, keepdims=True)
    sx[:, :past] = s_ref[...]
    sx[:, past:] = jnp.where(_iota((nb, LANES), 1) == 0, s_new, -inf)

    rsum = lambda a: jnp.sum(a, axis=1, keepdims=True)
    k_q = float(topk)
    s = sx[...]
    lo = jnp.min(jnp.where(s > -inf, s, inf), axis=1, keepdims=True)
    hi = jnp.max(s, axis=1, keepdims=True)

    def bisect(_, lh):
        lo, hi = lh
        mid = 0.5 * lo + 0.5 * hi
        ge = rsum(jnp.where(sx[...] >= mid, 1.0, 0.0)) >= k_q
        return jnp.where(ge, mid, lo), jnp.where(ge, hi, mid)

    lo, hi = lax.fori_loop(0, N_BISECT, bisect, (lo, hi))
    s = sx[...]
    v0 = jnp.min(jnp.where(s >= lo, s, inf), axis=1, keepdims=True)

    def refine(state):
        v, _, _ = state
        s = sx[...]
        g = s > v
        cnt = rsum(jnp.where(g, 1.0, 0.0))
        v2 = jnp.min(jnp.where(g, s, inf), axis=1, keepdims=True)
        ok = cnt < k_q
        return jnp.where(ok, v, v2), cnt, jnp.sum(jnp.where(ok, 0.0, 1.0))

    thr, c_gt, _ = lax.while_loop(lambda st: st[2] > 0.0, refine,
                                  (v0, jnp.zeros((nb, 1), F32), jnp.float32(1.0)))
    need = k_q - c_gt
    upper = jnp.where(_iota((LANES, LANES), 0) < _iota((LANES, LANES), 1), 1.0, 0.0).astype(BF16)
    carry = jnp.zeros((nb, 1), F32)
    for c in range((past + LANES) // LANES):
        sc = sx[:, c * LANES:(c + 1) * LANES]
        eq = sc == thr
        eqf = jnp.where(eq, 1.0, 0.0)
        rank = carry + _dot(eqf.astype(BF16), upper)
        sel_ref[:, c * LANES:(c + 1) * LANES] = jnp.where((sc > thr) | (eq & (rank < need)), 1.0, 0.0)
        carry = carry + rsum(eqf)


def _dec_select(scores, qi, ki, wi, topk):
    nb, past = scores.shape
    args = [scores, qi, ki, wi]
    return pl.pallas_call(
        functools.partial(_dec_select_kernel, topk=topk),
        in_specs=[pl.BlockSpec(a.shape, lambda: (0, 0)) for a in args],
        out_specs=pl.BlockSpec((nb, past + LANES), lambda: (0, 0)),
        out_shape=jax.ShapeDtypeStruct((nb, past + LANES), F32),
        scratch_shapes=[pltpu.VMEM((nb, past + LANES), F32)],
        compiler_params=pltpu.CompilerParams(vmem_limit_bytes=VMEM_LIMIT),
        name="dec_select",
    )(*args)


def _dec_attend(k_pages, v_pages, q8, sel, k_new, v_new):
    n_pages = k_pages.n_pages
    past = n_pages * PAGE_SIZE
    inf = jnp.inf
    nh = N_Q_HEADS
    R = N_Q_HEADS // N_KV_HEADS
    q8 = q8.astype(F32)
    own = (_iota((nh, KV_WIDTH), 0) // R) == (_iota((nh, KV_WIDTH), 1) // HEAD_DIM)
    q_ext = jnp.where(own, jnp.concatenate([q8, q8], axis=1), 0.0)
    q16 = jnp.concatenate([q_ext, jnp.zeros_like(q_ext)], axis=0).astype(BF16)

    k_pages.begin()
    parts = []
    for j in range(n_pages):
        k_pages.prefetch(j)
        parts.append(_dot(q16, k_pages.page(j).astype(BF16))[:nh])
    k_pages.end()
    k_new = k_new.astype(BF16).astype(F32)
    v_new = v_new.astype(BF16).astype(F32)
    lg_new = jnp.sum(q16[:nh].astype(F32) * k_new, axis=1, keepdims=True)
    lm = jnp.where(sel[:, :past] > 0.5, jnp.concatenate(parts, axis=1), -inf)
    lm_new = jnp.where(sel[:, past:past + 1] > 0.5, lg_new, -inf)
    m = jnp.maximum(jnp.max(lm, axis=1, keepdims=True), lm_new)
    p = jnp.exp2(lm - m)
    p_new = jnp.exp2(lm_new - m)
    l = jnp.sum(p, axis=1, keepdims=True) + p_new
    p16 = jnp.concatenate([p, jnp.zeros_like(p)], axis=0).astype(BF16)

    v_pages.begin()
    acc = jnp.zeros((2 * nh, KV_WIDTH), F32)
    for j in range(n_pages):
        v_pages.prefetch(j)
        acc = acc + _dot_nt(p16[:, j * PAGE_SIZE:(j + 1) * PAGE_SIZE], v_pages.page(j).astype(BF16))
    v_pages.end()
    out = (acc[:nh] + p_new.astype(BF16).astype(F32) * v_new) / l
    return jnp.where(_iota((nh, HEAD_DIM), 0) < R, out[:, :HEAD_DIM], out[:, HEAD_DIM:])


def kernel(x_prompt, x_sample, cache_k, cache_v, cache_kidx, state_wkv, state_shift, page_table,
           c_prompt, c_sample, norm_w, w_ada, b_ada, w_in, q_norm_w, k_norm_w, mu_shift, w0, w_up,
           a0, a_up, k_k, k_a, r_k, ln_x_w, ln_x_b, w_out):
    nb, t, d = x_prompt.shape
    nd = x_sample.shape[0]
    assert x_sample.shape[1] == 1
    n_pages = page_table.shape[1]
    past = n_pages * PAGE_SIZE
    w = RWKV_WIDTH

    offs = [0]
    for sz in (ATT_WIDTH, KV_WIDTH, KV_WIDTH, N_IDX_HEADS * IDX_DIM, N_IDX_HEADS, IDX_DIM, ATT_WIDTH,
               w, w, w, LORA, LORA, w):
        offs.append(offs[-1] + sz)
    w_in_t = w_in.T
    row = lambda i: w_in_t[offs[i]:offs[i + 1]]
    w_main = jnp.concatenate([row(0), row(1), row(2), row(3), row(6), row(7), row(8), row(9), row(12),
                              row(10), row(11), row(5), jnp.zeros((LANES - IDX_DIM, d), w_in.dtype)], axis=0).astype(BF16)
    zl = jnp.zeros((LORA, w), F32)
    wts = {
        "norm_w": norm_w.reshape(1, d), "w_main": w_main,
        "w_wit": row(4).astype(BF16), "w_vt": row(2).astype(BF16),
        "qnw": jnp.concatenate([q_norm_w, q_norm_w]).reshape(1, LANES),
        "knw": jnp.concatenate([k_norm_w, k_norm_w]).reshape(1, LANES),
    }
    rw_params = {
        "mu_rkv": mu_shift[:3 * w].reshape(1, 3 * w), "mu_lora": mu_shift[3 * w:].reshape(1, 2 * LORA),
        "w0": w0.reshape(1, w), "a0": a0.reshape(1, w),
        "w_lora": jnp.concatenate([jnp.concatenate([w_up, zl], axis=1),
                                   jnp.concatenate([zl, a_up], axis=1)], axis=0).astype(BF16),
        "k_k": k_k.reshape(1, w), "k_a": k_a.reshape(1, w), "r_k": r_k.reshape(1, w),
        "ln_w": ln_x_w.reshape(1, w), "ln_b": ln_x_b.reshape(1, w),
    }
    w_out_bf = w_out.astype(BF16)

    n_c = nb + nd
    pad = (-n_c) % 8
    c_all = jnp.concatenate([c_prompt, c_sample, jnp.zeros((pad, d), F32)], axis=0)
    mod = _adaln_mod(c_all, w_ada, b_ada)
    shift_p, scale_p, gate_p = (mod[:nb, s * d:(s + 1) * d].reshape(nb, 1, d) for s in range(3))
    shift_s, scale_s, gate_s = (mod[nb:n_c, s * d:(s + 1) * d].reshape(1, nd, d) for s in range(3))

    tm = min(512, t)
    pr = _in_proj(x_prompt, shift_p, scale_p, _rope_tables(jnp.arange(t)), wts, tm)
    xs3 = x_sample.reshape(1, nd, d)
    sm = _in_proj(xs3, shift_s, scale_s, _rope_tables(jnp.full((1,), past)), wts, nd)
    qi_s, ki_s = sm["qi"][0], sm["ki"][0]
    wi_s = sm["wit"][0].T
    n_phys = cache_k.shape[0]
    k_pages = jnp.transpose(cache_k, (0, 2, 3, 1)).reshape(n_phys, KV_WIDTH, PAGE_SIZE)
    v_pages = jnp.transpose(cache_v, (0, 2, 3, 1)).reshape(n_phys, KV_WIDTH, PAGE_SIZE)

    att_p, scores = _attn_prompt(pr["q"], pr["qi"], pr["wit"], pr["kbf"], pr["vt"], pr["kibf"], pr["sg"],
                                 page_table, qi_s.reshape(nd, N_IDX_HEADS, IDX_DIM),
                                 wi_s.reshape(nd, N_IDX_HEADS, 1), jnp.transpose(cache_kidx, (0, 2, 1)))
    topk = min(TOPK_MAX, (past + 1) // 4)
    sel = _dec_select(scores, qi_s, sm["kibf"][0], wi_s, topk)
    rw_p, s_p, att_s = _rwkv_prompt(pr["rkv"], pr["lora"], pr["gr"], rw_params, page_table,
                                    sm["q"][0].reshape(nd, N_Q_HEADS, HEAD_DIM), sel,
                                    sm["k"][0].reshape(nd, 1, KV_WIDTH), sm["v"][0].reshape(nd, 1, KV_WIDTH),
                                    sm["sg"][0].reshape(nd, N_Q_HEADS, HEAD_DIM), k_pages, v_pages)
    y_prompt = _out_proj(att_p, rw_p, w_out_bf, x_prompt, gate_p, tm)
    shift_prompt = jnp.concatenate([pr["rkv"][:, t - 1:], pr["lora"][:, t - 1:]], axis=-1)

    prev = state_shift.reshape(nd, -1)
    vecs = _rwkv_dec_prep(sm["rkv"][0], sm["lora"][0], prev[:, :3 * w], prev[:, 3 * w:], rw_params)
    s_t, y_t = _rwkv_dec_state(vecs[:6], jnp.transpose(state_wkv, (1, 2, 3, 0)))
    s_s = jnp.transpose(s_t, (3, 0, 1, 2))
    rw_s = _rwkv_dec_epi(y_t.reshape(w, nd), vecs[6], sm["gr"][0], rw_params)
    y_sample = _out_proj(att_s.reshape(1, nd, ATT_WIDTH), rw_s.reshape(1, nd, w), w_out_bf, xs3, gate_s, nd)
    shift_sample = jnp.concatenate([sm["rkv"][0], sm["lora"][0]], axis=-1).reshape(nd, 1, 3 * w + 2 * LORA)

    return (y_prompt, y_sample.reshape(nd, 1, d),
            pr["k"].reshape(nb, t, N_KV_HEADS, HEAD_DIM), pr["v"].reshape(nb, t, N_KV_HEADS, HEAD_DIM), pr["ki"],
            s_p, shift_prompt,
            sm["k"][0].reshape(nd, 1, N_KV_HEADS, HEAD_DIM), sm["v"][0].reshape(nd, 1, N_KV_HEADS, HEAD_DIM),
            ki_s.reshape(nd, 1, IDX_DIM), s_s, shift_sample)
```

```python
import functools

import jax
import jax.numpy as jnp
from jax import lax
from jax.experimental import pallas as pl
from jax.experimental.pallas import tpu as pltpu

F32, BF16 = jnp.float32, jnp.bfloat16

HEAD_DIM = 64
N_Q_HEADS = 8
N_KV_HEADS = 2
N_IDX_HEADS = 8
IDX_DIM = 64
N_RWKV_HEADS = 8
ATT_WIDTH = N_Q_HEADS * HEAD_DIM
RWKV_WIDTH = N_RWKV_HEADS * HEAD_DIM
KV_WIDTH = N_KV_HEADS * HEAD_DIM
LORA = 64
TOPK_MAX = 256
ROPE_THETA = 500000.0
ROPE_DIMS = HEAD_DIM // 4
NORM_EPS = 1e-6
GN_EPS = 64e-5
PAGE_SIZE = 128
LANES = 128
SUBLANES = 8
Q_SCALE = HEAD_DIM ** -0.5 * 1.4426950408889634
VMEM_LIMIT = 56 * 1024 * 1024

Q_BLOCK = 256
KEY_CHUNK = 512
N_BISECT = 16
RW_CHUNK = 64
RW_BLOCK = 128


def _dot(a, b):
    return jnp.dot(a, b, preferred_element_type=F32)


def _dot_nt(a, b):
    return lax.dot_general(a, b, (((1,), (1,)), ((), ())), preferred_element_type=F32)


def _split(x):
    hi = x.astype(BF16)
    return hi, (x - hi.astype(F32)).astype(BF16)


def _sum_dot(x, sel01):
    hi, lo = _split(x)
    return _dot(hi, sel01) + _dot(lo, sel01)


def _sum_dot_left(sel01, x):
    hi, lo = _split(x)
    return _dot(sel01, hi) + _dot(sel01, lo)


def _dot_tn(a, b):
    return lax.dot_general(a, b, (((0,), (0,)), ((), ())), preferred_element_type=F32)


def _iota(shape, dim):
    return lax.broadcasted_iota(jnp.int32, shape, dim)


def _head_block_ones(n):
    return jnp.where(_iota((n, n), 0) // HEAD_DIM == _iota((n, n), 1) // HEAD_DIM, 1.0, 0.0).astype(BF16)


def _silu(x):
    return x * jax.nn.sigmoid(x)


def _cparams(sem):
    return pltpu.CompilerParams(dimension_semantics=sem, vmem_limit_bytes=VMEM_LIMIT)


def _mod_kernel(c_ref, w_ref, b_ref, o_ref):
    s = _silu(c_ref[...])
    o_ref[...] = _dot(s.astype(BF16), w_ref[...].astype(BF16)) + b_ref[...]


def _adaln_mod(c, w_ada, b_ada):
    rows, d = c.shape
    n = w_ada.shape[1]
    tn = 1024
    return pl.pallas_call(
        _mod_kernel,
        grid=(n // tn,),
        in_specs=[pl.BlockSpec((rows, d), lambda j: (0, 0)),
                  pl.BlockSpec((d, tn), lambda j: (0, j)),
                  pl.BlockSpec((1, tn), lambda j: (0, j))],
        out_specs=pl.BlockSpec((rows, tn), lambda j: (0, j)),
        out_shape=jax.ShapeDtypeStruct((rows, n), F32),
        compiler_params=_cparams(("arbitrary",)),
        name="adaln_mod",
    )(c, w_ada, b_ada.reshape(1, n))


_W_Q, _W_KV, _W_QI, _W_GA, _W_RKV, _W_GR, _W_TAIL, _W_END = 0, 512, 768, 1280, 1792, 3328, 3840, 4096


def _inproj_kernel(x_ref, shift_ref, scale_ref, nw_ref, w_ref, wwi_ref, wvt_ref, cos_ref, s1_ref, s2_ref,
                   qnw_ref, knw_ref,
                   q_o, k_o, kbf_o, v_o, vt_o, qi_o, ki_o, kibf_o, wit_o, sg_o, rkv_o, lora_o, gr_o):
    x = x_ref[0]
    ms = jnp.mean(x * x, axis=-1, keepdims=True)
    h = x * lax.rsqrt(ms + NORM_EPS) * nw_ref[...]
    h = h * (1.0 + scale_ref[0]) + shift_ref[0]
    hb = h.astype(BF16)
    cos, s1, s2 = cos_ref[...], s1_ref[...], s2_ref[...]
    bsum = _head_block_ones(LANES)

    def seg(a, b):
        return _dot_nt(hb, w_ref[a:b, :])

    def rope(t):
        return t * cos + pltpu.roll(t, LANES - ROPE_DIMS // 2, 1) * s1 + pltpu.roll(t, ROPE_DIMS // 2, 1) * s2

    def head_norm(t, w):
        return t * lax.rsqrt(_sum_dot(t * t, bsum) * (1.0 / HEAD_DIM) + NORM_EPS) * w

    yq = seg(_W_Q, _W_KV)
    for s in range(ATT_WIDTH // LANES):
        t = rope(head_norm(yq[:, s * LANES:(s + 1) * LANES], qnw_ref[...]))
        q_o[0, :, s * LANES:(s + 1) * LANES] = (t * Q_SCALE).astype(BF16)

    ykv = seg(_W_KV, _W_QI)
    k = rope(head_norm(ykv[:, :KV_WIDTH], knw_ref[...]))
    k_o[0] = k
    kbf_o[0] = k.astype(BF16)
    v_o[0] = ykv[:, KV_WIDTH:]
    vt_o[0] = _dot_nt(wvt_ref[...], hb).astype(BF16)

    yqi = seg(_W_QI, _W_GA)
    for s in range(N_IDX_HEADS * IDX_DIM // LANES):
        qi_o[0, :, s * LANES:(s + 1) * LANES] = rope(yqi[:, s * LANES:(s + 1) * LANES]).astype(BF16)

    sg_o[0] = _silu(seg(_W_GA, _W_RKV))
    for s in range(3):
        rkv_o[0, :, s * RWKV_WIDTH:(s + 1) * RWKV_WIDTH] = seg(_W_RKV + s * RWKV_WIDTH, _W_RKV + (s + 1) * RWKV_WIDTH)
    gr_o[0] = seg(_W_GR, _W_TAIL)

    yt = seg(_W_TAIL, _W_END)
    lora_o[0] = yt[:, :2 * LORA]
    ki = rope(yt[:, 2 * LORA:])[:, :IDX_DIM]
    ki_o[0] = ki
    kibf_o[0] = ki.astype(BF16)
    wit_o[0] = _dot_nt(wwi_ref[...], hb) * (N_IDX_HEADS ** -0.5 * IDX_DIM ** -0.5)


def _in_proj(x3, shift3, scale3, tabs, wts, tm):
    nb, t, d = x3.shape
    mod_rows = shift3.shape[1]
    tab_rows = tabs[0].shape[0]
    grid = (nb, t // tm)
    if mod_rows == 1:
        mod_spec = pl.BlockSpec((1, 1, d), lambda b, i: (b, 0, 0))
    else:
        mod_spec = pl.BlockSpec((1, tm, d), lambda b, i: (b, i, 0))
    if tab_rows == 1:
        tab_spec = pl.BlockSpec((1, LANES), lambda b, i: (0, 0))
    else:
        tab_spec = pl.BlockSpec((tm, LANES), lambda b, i: (i, 0))
    const = lambda shape: pl.BlockSpec(shape, lambda b, i: tuple(0 for _ in shape))
    row = lambda width: pl.BlockSpec((1, tm, width), lambda b, i: (b, i, 0))
    col = lambda rows: pl.BlockSpec((1, rows, tm), lambda b, i: (b, 0, i))
    out_defs = [
        ("q", (t, ATT_WIDTH), BF16, row(ATT_WIDTH)),
        ("k", (t, KV_WIDTH), F32, row(KV_WIDTH)),
        ("kbf", (t, KV_WIDTH), BF16, row(KV_WIDTH)),
        ("v", (t, KV_WIDTH), F32, row(KV_WIDTH)),
        ("vt", (KV_WIDTH, t), BF16, col(KV_WIDTH)),
        ("qi", (t, N_IDX_HEADS * IDX_DIM), BF16, row(N_IDX_HEADS * IDX_DIM)),
        ("ki", (t, IDX_DIM), F32, row(IDX_DIM)),
        ("kibf", (t, IDX_DIM), BF16, row(IDX_DIM)),
        ("wit", (N_IDX_HEADS, t), F32, col(N_IDX_HEADS)),
        ("sg", (t, ATT_WIDTH), F32, row(ATT_WIDTH)),
        ("rkv", (t, 3 * RWKV_WIDTH), F32, row(3 * RWKV_WIDTH)),
        ("lora", (t, 2 * LORA), F32, row(2 * LORA)),
        ("gr", (t, RWKV_WIDTH), F32, row(RWKV_WIDTH)),
    ]
    outs = pl.pallas_call(
        _inproj_kernel,
        grid=grid,
        in_specs=[pl.BlockSpec((1, tm, d), lambda b, i: (b, i, 0)), mod_spec, mod_spec, const((1, d)),
                  const(wts["w_main"].shape), const(wts["w_wit"].shape), const(wts["w_vt"].shape),
                  tab_spec, tab_spec, tab_spec, const((1, LANES)), const((1, LANES))],
        out_specs=[o[3] for o in out_defs],
        out_shape=[jax.ShapeDtypeStruct((nb,) + o[1], o[2]) for o in out_defs],
        compiler_params=_cparams(("arbitrary", "arbitrary")),
        name="in_proj",
    )(x3, shift3, scale3, wts["norm_w"], wts["w_main"], wts["w_wit"], wts["w_vt"], tabs[0], tabs[1], tabs[2],
      wts["qnw"], wts["knw"])
    return {o[0]: arr for o, arr in zip(out_defs, outs)}


def _rope_tables(pos):
    half = ROPE_DIMS // 2
    inv = jnp.power(ROPE_THETA, -jnp.arange(half, dtype=F32) / half)
    ang = pos.astype(F32)[:, None] * inv[None, :]
    cos, sin = jnp.cos(ang), jnp.sin(ang)
    n = pos.shape[0]
    pad = jnp.zeros((n, HEAD_DIM - ROPE_DIMS), F32)
    zero = jnp.zeros((n, half), F32)
    c_head = jnp.concatenate([cos, cos, pad + 1.0], axis=1)
    s1_head = jnp.concatenate([-sin, zero, pad], axis=1)
    s2_head = jnp.concatenate([zero, sin, pad], axis=1)
    two = lambda a: jnp.concatenate([a, a], axis=1)
    return two(c_head), two(s1_head), two(s2_head)


def _head_sum(t):
    ones = _head_block_ones(LANES)
    parts = [_sum_dot(t[:, s * LANES:(s + 1) * LANES], ones) for s in range(t.shape[1] // LANES)]
    return parts[0] if len(parts) == 1 else jnp.concatenate(parts, axis=1)


def _softplus(z):
    return jnp.maximum(z, 0.0) + jnp.log(1.0 + jnp.exp(-jnp.abs(z)))


def _rwkv_prep(xs, xl, prev, prevl, p):
    xm = xs + p["mu_rkv"] * (prev - xs)
    xml = xl + p["mu_lora"] * (prevl - xl)
    w = RWKV_WIDTH
    r, k, v = xm[:, :w], xm[:, w:2 * w], xm[:, 2 * w:]
    lane = _iota(xml.shape, 1)
    lhs = jnp.where(lane < LORA, jnp.tanh(xml), xml)
    lo = _dot(lhs.astype(BF16), p["w_lora"])
    wlog = -_softplus(-(p["w0"] + lo[:, :w])) - 0.5
    logw = -jnp.exp(wlog)
    a = jax.nn.sigmoid(p["a0"] + lo[:, w:])
    kkr = k * p["k_k"]
    kk = kkr / jnp.maximum(jnp.sqrt(_head_sum(kkr * kkr)), 1e-12)
    k_mod = k * (1.0 + (a - 1.0) * p["k_a"])
    bonus = _head_sum(r * k_mod * p["r_k"]) * v
    return r, logw, k_mod, v, -kk, kk * a, bonus


def _rwkv_epilogue(y, bonus, gate, p):
    mean = _head_sum(y) * (1.0 / HEAD_DIM)
    d = y - mean
    var = _head_sum(d * d) * (1.0 / HEAD_DIM)
    yn = d * lax.rsqrt(var + GN_EPS) * p["ln_w"] + p["ln_b"]
    return (yn + bonus) * _silu(gate)


_RW_PARAM_NAMES = ("mu_rkv", "mu_lora", "w0", "a0", "w_lora", "k_k", "k_a", "r_k", "ln_w", "ln_b")


def _load_params(refs):
    return {n: r[...] for n, r in zip(_RW_PARAM_NAMES, refs)}


def _rwkv_prompt_kernel(pt_ref, rkv_ref, lora_ref, gr_ref, *rest):
    nparam = len(_RW_PARAM_NAMES)
    p = _load_params(rest[:nparam])
    dq_ref, dsel_ref, dknew_ref, dvnew_ref, dsg_ref, dk_hbm, dv_hbm = rest[nparam:nparam + 7]
    out_ref, sfin_ref, dout_ref = rest[nparam + 7:nparam + 10]
    (st_s, c_rkv, c_lora, r_s, lw_s, k_s, v_s, al_s, be_s, y_s, bon_s,
     at_s, rt_s, bk_s, ec_s, wl_s, wh_s, zv_s, inv_s, dkbuf, dvbuf, dksem, dvsem) = rest[nparam + 10:]
    i = pl.program_id(0)
    nb, tb = rkv_ref.shape[0], rkv_ref.shape[1]
    C = RW_CHUNK
    W = 2 * C

    spd = dq_ref.shape[0]
    n_seq = spd * pl.num_programs(0)

    def sample_attention(u):
        s = i * spd + u
        k_pages = _PageFetcher(pt_ref, dk_hbm, dkbuf, dksem, s, n_seq)
        v_pages = _PageFetcher(pt_ref, dv_hbm, dvbuf, dvsem, s, n_seq)
        sel = dsel_ref[pl.ds(s % SUBLANES, 1), :]
        res = _dec_attend(k_pages, v_pages, dq_ref[u], sel, dknew_ref[u], dvnew_ref[u])
        dout_ref[u] = (res * dsg_ref[u]).astype(BF16)

    def side_stream(point):
        for u in range(spd * point // 4, spd * (point + 1) // 4):
            sample_attention(u)

    side_stream(0)

    @pl.when(i == 0)
    def _():
        st_s[...] = jnp.zeros_like(st_s)
        c_rkv[...] = jnp.zeros_like(c_rkv)
        c_lora[...] = jnp.zeros_like(c_lora)

    row = _iota((tb, 1), 0)
    for b in range(nb):
        xs, xl = rkv_ref[b], lora_ref[b]
        prev = jnp.where(row == 0, c_rkv[b], pltpu.roll(xs, 1, 0))
        prevl = jnp.where(row == 0, c_lora[b], pltpu.roll(xl, 1, 0))
        c_rkv[b] = xs[tb - 1:tb, :]
        c_lora[b] = xl[tb - 1:tb, :]
        r_s[b], lw_s[b], k_s[b], v_s[b], al_s[b], be_s[b], bon_s[b] = _rwkv_prep(xs, xl, prev, prevl, p)

    side_stream(1)

    rm, cn = _iota((W, W), 0), _iota((W, W), 1)
    tt, ss = rm % C, cn % C
    top, left = rm < C, cn < C
    strict, incl = ss < tt, ss <= tt
    mk_l_lo, mk_l_hi = top & left & strict, (~top) & (~left) & strict
    mk_z_lo, mk_z_hi = top & (~left) & strict, (~top) & left & strict
    mk_w = (_iota((C, W), 1) % C) <= _iota((C, W), 0)
    bd = (rm // HEAD_DIM) == (cn // HEAD_DIM)
    eye = jnp.where(rm == cn, 1.0, 0.0).astype(F32)
    tril = jnp.where(_iota((C, C), 1) <= _iota((C, C), 0), 1.0, 0.0).astype(BF16)
    lane_lo = _iota((C, LANES), 1) < HEAD_DIM
    m_lo = jnp.where(lane_lo, 1.0, 0.0).astype(F32)
    m_hi = 1.0 - m_lo

    bf = lambda a: a.astype(BF16)
    pairs = range(N_RWKV_HEADS // 2)
    lanes_of = lambda pr: slice(pr * LANES, (pr + 1) * LANES)
    rows_of = lambda c: pl.ds(pl.multiple_of(c * C, C), C)
    items = [(b, pr) for b in range(nb) for pr in pairs]

    def phase_a(c, carry):
        rows = rows_of(c)
        cums = []
        for b in range(nb):
            cum_all = _sum_dot_left(tril, lw_s[b, rows, :])
            cums += [cum_all[:, lanes_of(pr)] for pr in pairs]
        g_lo, g_hi = [], []
        for n, (b, pr) in enumerate(items):
            sl = lanes_of(pr)
            cum = cums[n]
            cum_c = cum[C - 1:C, :]
            e_neg, e_rel = jnp.exp(-cum), jnp.exp(cum_c - cum)
            rr, kk, aa, bb = r_s[b, rows, sl], k_s[b, rows, sl], al_s[b, rows, sl], be_s[b, rows, sl]
            a_t, r_t = aa * jnp.exp(cum - lw_s[b, rows, sl]), rr * jnp.exp(cum)
            b_t, k_t = bb * e_neg, kk * e_neg
            at_s[b, c, pr], rt_s[b, c, pr] = bf(a_t), bf(r_t)
            bk_s[b, c, pr] = jnp.concatenate([bb * e_rel, kk * e_rel], axis=0)
            ec_s[b, c, pr] = jnp.exp(cum_c)
            xk = bf(jnp.concatenate([b_t, k_t], axis=0))
            kx = bf(jnp.concatenate([k_t, b_t], axis=0))
            g_lo.append(_dot_nt(bf(jnp.concatenate([a_t * m_lo, r_t * m_lo], axis=0)), xk))
            g_hi.append(_dot_nt(bf(jnp.concatenate([r_t * m_hi, a_t * m_hi], axis=0)), kx))
        l_mat = []
        for n, (b, pr) in enumerate(items):
            l_mat.append(jnp.where(mk_l_lo, g_lo[n], 0.0) + jnp.where(mk_l_hi, g_hi[n], 0.0))
            z_mat = jnp.where(mk_z_lo, g_lo[n], 0.0) + jnp.where(mk_z_hi, g_hi[n], 0.0)
            wl_s[b, c, pr] = bf(jnp.where(mk_w, g_lo[n][C:, :], 0.0))
            wh_s[b, c, pr] = bf(jnp.where(mk_w, g_hi[n][:C, :], 0.0))
            vv_b = bf(v_s[b, rows, lanes_of(pr)])
            zv_s[b, c, pr] = _dot(bf(z_mat), jnp.concatenate([vv_b, vv_b], axis=0))
        inv = [eye + m for m in l_mat]
        pw = l_mat
        for _ in range(5):
            pw = [_dot(bf(m), bf(m)) for m in pw]
            inv = [x + _dot(bf(x), bf(m)) for x, m in zip(inv, pw)]
        for n, (b, pr) in enumerate(items):
            inv_s[b, c, pr] = bf(inv[n])
        return carry

    lax.fori_loop(0, tb // C, phase_a, 0)
    side_stream(2)

    def phase_b(c, carry):
        rows = rows_of(c)
        nit = range(len(items))
        st = [st_s[b, pr] for b, pr in items]
        st_b = [bf(s) for s in st]
        vv = [v_s[b, rows, lanes_of(pr)] for b, pr in items]
        a_s = [_dot_nt(at_s[b, c, pr], st_b[n]) for n, (b, pr) in enumerate(items)]
        u_st = [_dot(inv_s[b, c, pr], bf(zv_s[b, c, pr] + jnp.concatenate([a_s[n], a_s[n]], axis=0)))
                for n, (b, pr) in enumerate(items)]
        u = [jnp.where(lane_lo, x[:C], x[C:]) for x in u_st]
        uv = [jnp.concatenate([u[n], vv[n]], axis=0) for n in nit]
        upd = [_dot_tn(uv[n], bk_s[b, c, pr]) for n, (b, pr) in enumerate(items)]
        for n, (b, pr) in enumerate(items):
            st_s[b, pr] = st[n] * ec_s[b, c, pr] + jnp.where(bd, upd[n], 0.0)
        for n, (b, pr) in enumerate(items):
            uv_b = bf(uv[n])
            vu_b = jnp.concatenate([uv_b[C:], uv_b[:C]], axis=0)
            y_in = jnp.where(lane_lo, _dot(wl_s[b, c, pr], uv_b), _dot(wh_s[b, c, pr], vu_b))
            y_s[b, rows, lanes_of(pr)] = _dot_nt(rt_s[b, c, pr], st_b[n]) + y_in
        return carry

    lax.fori_loop(0, tb // C, phase_b, 0)
    side_stream(3)
    for b in range(nb):
        out_ref[b] = _rwkv_epilogue(y_s[b], bon_s[b], gr_ref[b], p).astype(BF16)

    @pl.when(i == pl.num_programs(0) - 1)
    def _():
        for b in range(nb):
            for hd in range(N_RWKV_HEADS):
                o = (hd % 2) * HEAD_DIM
                sfin_ref[b, hd] = st_s[b, hd // 2][o:o + HEAD_DIM, o:o + HEAD_DIM]


def _rwkv_prompt(rkv, lora, gr, params, page_table, dq3, dsel, dk_new3, dv_new3, dsg3, k_pages, v_pages):
    nb, t, _ = rkv.shape
    tb = min(RW_BLOCK, t)
    nsteps = t // tb
    nd, n_pages = page_table.shape
    spd = nd // nsteps
    assert spd * nsteps == nd and SUBLANES % spd == 0
    w = RWKV_WIDTH
    blk = lambda width: pl.BlockSpec((nb, tb, width), lambda i, pt: (0, i, 0))
    pspecs = [pl.BlockSpec(params[n].shape, lambda i, pt: (0, 0)) for n in _RW_PARAM_NAMES]
    per_step = lambda r, c: pl.BlockSpec((spd, r, c), lambda i, pt: (i, 0, 0))
    vec = lambda: pltpu.VMEM((nb, tb, w), F32)
    nch, npair, c = tb // RW_CHUNK, N_RWKV_HEADS // 2, RW_CHUNK
    per_chunk = lambda rows, dt: pltpu.VMEM((nb, nch, npair, rows, LANES), dt)
    state_shape = (nb, N_RWKV_HEADS, HEAD_DIM, HEAD_DIM)
    page_buf = pltpu.VMEM((2, n_pages, KV_WIDTH, PAGE_SIZE), F32)
    grid_spec = pltpu.PrefetchScalarGridSpec(
        num_scalar_prefetch=1,
        grid=(nsteps,),
        in_specs=[blk(3 * w), blk(2 * LORA), blk(w)] + pspecs
                 + [per_step(N_Q_HEADS, HEAD_DIM),
                    pl.BlockSpec((SUBLANES, dsel.shape[1]), lambda i, pt: (i * spd // SUBLANES, 0)),
                    per_step(1, KV_WIDTH), per_step(1, KV_WIDTH), per_step(N_Q_HEADS, HEAD_DIM),
                    pl.BlockSpec(memory_space=pl.ANY), pl.BlockSpec(memory_space=pl.ANY)],
        out_specs=[blk(w), pl.BlockSpec(state_shape, lambda i, pt: (0, 0, 0, 0)), per_step(N_Q_HEADS, HEAD_DIM)],
        scratch_shapes=[pltpu.VMEM((nb, npair, LANES, LANES), F32),
                        pltpu.VMEM((nb, 1, 3 * w), F32), pltpu.VMEM((nb, 1, 2 * LORA), F32),
                        vec(), vec(), vec(), vec(), vec(), vec(), vec(), vec(),
                        per_chunk(c, BF16), per_chunk(c, BF16), per_chunk(2 * c, F32), per_chunk(1, F32),
                        per_chunk(c, BF16), per_chunk(c, BF16), per_chunk(2 * c, F32), per_chunk(2 * c, BF16),
                        page_buf, page_buf, pltpu.SemaphoreType.DMA((2,)), pltpu.SemaphoreType.DMA((2,))],
    )
    return pl.pallas_call(
        _rwkv_prompt_kernel,
        grid_spec=grid_spec,
        out_shape=[jax.ShapeDtypeStruct((nb, t, w), BF16), jax.ShapeDtypeStruct(state_shape, F32),
                   jax.ShapeDtypeStruct((nd, N_Q_HEADS, HEAD_DIM), BF16)],
        compiler_params=_cparams(("arbitrary",)),
        name="rwkv_prompt",
    )(page_table, rkv, lora, gr, *[params[n] for n in _RW_PARAM_NAMES],
      dq3, dsel, dk_new3, dv_new3, dsg3, k_pages, v_pages)


def _rwkv_dec_prep_kernel(rkv_ref, lora_ref, prev_ref, prevl_ref, *rest):
    nparam = len(_RW_PARAM_NAMES)
    p = _load_params(rest[:nparam])
    outs = rest[nparam:]
    r, logw, k_mod, v, al, be, bonus = _rwkv_prep(rkv_ref[...], lora_ref[...], prev_ref[...], prevl_ref[...], p)
    for o, val in zip(outs[:6], (r, jnp.exp(logw), k_mod, v, al, be)):
        o[...] = val.T
    outs[6][...] = bonus


def _rwkv_dec_prep(rkv, lora, prev, prevl, params):
    n = rkv.shape[0]
    full = lambda a: pl.BlockSpec(a.shape, lambda: tuple(0 for _ in a.shape))
    args = [rkv, lora, prev, prevl] + [params[k] for k in _RW_PARAM_NAMES]
    t_shape, shape = (RWKV_WIDTH, n), (n, RWKV_WIDTH)
    return pl.pallas_call(
        _rwkv_dec_prep_kernel,
        in_specs=[full(a) for a in args],
        out_specs=[pl.BlockSpec(t_shape, lambda: (0, 0))] * 6 + [pl.BlockSpec(shape, lambda: (0, 0))],
        out_shape=[jax.ShapeDtypeStruct(t_shape, F32)] * 6 + [jax.ShapeDtypeStruct(shape, F32)],
        compiler_params=pltpu.CompilerParams(vmem_limit_bytes=VMEM_LIMIT),
        name="rwkv_dec_prep",
    )(*args)


def _rwkv_dec_state_kernel(r_ref, w_ref, k_ref, v_ref, a_ref, b_ref, s_ref, so_ref, y_ref):
    r, w, k, a, b = r_ref[0], w_ref[0], k_ref[0], a_ref[0], b_ref[0]

    def one(i, carry):
        st = s_ref[0, i]
        sa = jnp.sum(st * a, axis=0, keepdims=True)
        sn = st * w + sa * b + v_ref[0, pl.ds(i, 1), :] * k
        so_ref[0, i] = sn
        y_ref[0, pl.ds(i, 1), :] = jnp.sum(sn * r, axis=0, keepdims=True)
        return carry

    lax.fori_loop(0, HEAD_DIM, one, 0)


def _rwkv_dec_state(vecs_t, state_t):
    nh, n, _, nb = state_t.shape
    vspec = pl.BlockSpec((1, n, nb), lambda h: (h, 0, 0))
    sspec = pl.BlockSpec((1, n, n, nb), lambda h: (h, 0, 0, 0))
    return pl.pallas_call(
        _rwkv_dec_state_kernel,
        grid=(nh,),
        in_specs=[vspec] * 6 + [sspec],
        out_specs=[sspec, vspec],
        out_shape=[jax.ShapeDtypeStruct(state_t.shape, F32), jax.ShapeDtypeStruct((nh, n, nb), F32)],
        compiler_params=_cparams(("arbitrary",)),
        name="rwkv_dec_state",
    )(*[a.reshape(nh, n, nb) for a in vecs_t], state_t)


def _rwkv_dec_epi_kernel(yt_ref, bonus_ref, gr_ref, lnw_ref, lnb_ref, o_ref):
    o_ref[...] = _rwkv_epilogue(yt_ref[...].T, bonus_ref[...], gr_ref[...],
                                {"ln_w": lnw_ref[...], "ln_b": lnb_ref[...]}).astype(BF16)


def _rwkv_dec_epi(y_t, bonus, gr, params):
    n = bonus.shape[0]
    args = [y_t, bonus, gr, params["ln_w"], params["ln_b"]]
    return pl.pallas_call(
        _rwkv_dec_epi_kernel,
        in_specs=[pl.BlockSpec(a.shape, lambda: (0, 0)) for a in args],
        out_specs=pl.BlockSpec((n, RWKV_WIDTH), lambda: (0, 0)),
        out_shape=jax.ShapeDtypeStruct((n, RWKV_WIDTH), BF16),
        name="rwkv_dec_epi",
    )(*args)


def _attn_prompt_kernel(pt_ref, q_ref, qi_ref, wit_ref, k_ref, vt_ref, kidx_ref, sg_ref, tri_ref,
                        dqi_ref, dwi_ref, dkidx_hbm, o_ref, dsc_ref,
                        s_scr, m_scr, acc_scr, *dscratch, topk):
    j = pl.program_id(1)
    step = pl.program_id(0) * pl.num_programs(1) + j
    nsteps = pl.num_programs(0) * pl.num_programs(1)
    spp = dqi_ref.shape[0]
    for u in range(spp):
        pages = _PageFetcher(pt_ref, dkidx_hbm, dscratch[u], dscratch[spp + u], step, nsteps, stride=spp, offset=u)
        _score_pages(pages, dqi_ref, dwi_ref, dsc_ref, u, (step % (SUBLANES // spp)) * spp + u)
    QB, KC = Q_BLOCK, KEY_CHUNK
    nch = (j * QB) // KC + 1
    inf = jnp.inf
    qpos_i = j * QB + _iota((1, QB), 1)
    k_q = jnp.minimum(float(topk), qpos_i.astype(F32) + 1.0)
    wit = wit_ref[0]
    qi = qi_ref[0]
    qi_stack = jnp.concatenate([qi[:, h * IDX_DIM:(h + 1) * IDX_DIM] for h in range(N_IDX_HEADS)], axis=0)

    def chunk_rows(c):
        return pl.ds(pl.multiple_of(c * KC, KC), KC)

    SUB = 32
    part = lambda a: a.reshape(KC // SUB, SUB, QB)
    psum = lambda a: jnp.sum(part(a), axis=0)
    pmin = lambda a: jnp.min(part(a), axis=0)
    pmax = lambda a: jnp.max(part(a), axis=0)
    full8 = lambda val: jnp.full((SUB, QB), val, F32)
    fold_sum = lambda a: jnp.sum(a, axis=0, keepdims=True)
    fold_min = lambda a: jnp.min(a, axis=0, keepdims=True)
    fold_max = lambda a: jnp.max(a, axis=0, keepdims=True)

    def score_chunk(c, lohi):
        rows = chunk_rows(c)
        st = _dot_nt(kidx_ref[0, rows, :], qi_stack)
        acc = jnp.zeros((KC, QB), F32)
        for h in range(N_IDX_HEADS):
            acc = acc + jnp.maximum(st[:, h * QB:(h + 1) * QB], 0.0) * wit[h:h + 1, :]
        ok = (c * KC + _iota((KC, QB), 0)) <= qpos_i
        s_scr[rows, :] = jnp.where(ok, acc, -inf)
        return (jnp.minimum(lohi[0], pmin(jnp.where(ok, acc, inf))),
                jnp.maximum(lohi[1], pmax(jnp.where(ok, acc, -inf))))

    lo8, hi8 = lax.fori_loop(0, nch, score_chunk, (full8(inf), full8(-inf)))
    lo, hi = fold_min(lo8), fold_max(hi8)

    def sweep(fn, init):
        return lax.fori_loop(0, nch, lambda c, car: fn(car, s_scr[chunk_rows(c), :]), init)

    def bisect(_, lh):
        lo, hi = lh
        mid = 0.5 * lo + 0.5 * hi
        cnt = fold_sum(sweep(lambda c, s: c + psum(jnp.where(s >= mid, 1.0, 0.0)), full8(0.0)))
        ge = cnt >= k_q
        return jnp.where(ge, mid, lo), jnp.where(ge, hi, mid)

    lo, hi = lax.fori_loop(0, N_BISECT, bisect, (lo, hi))
    v0 = fold_min(sweep(lambda c, s: jnp.minimum(c, pmin(jnp.where(s >= lo, s, inf))), full8(inf)))

    def refine(state):
        v, _, _ = state

        def f(car, s):
            g = s > v
            return car[0] + psum(jnp.where(g, 1.0, 0.0)), jnp.minimum(car[1], pmin(jnp.where(g, s, inf)))

        cnt8, v28 = sweep(f, (full8(0.0), full8(inf)))
        cnt, v2 = fold_sum(cnt8), fold_min(v28)
        ok = cnt < k_q
        return jnp.where(ok, v, v2), cnt, jnp.sum(jnp.where(ok, 0.0, 1.0))

    full = lambda val: jnp.full((1, QB), val, F32)
    thr, c_gt, _ = lax.while_loop(lambda st: st[2] > 0.0, refine, (v0, full(0.0), jnp.float32(1.0)))
    need = k_q - c_gt

    q = q_ref[0]
    G, R = N_KV_HEADS, N_Q_HEADS // N_KV_HEADS
    q_stack = [jnp.concatenate([q[:, (g * R + r) * HEAD_DIM:(g * R + r + 1) * HEAD_DIM] for r in range(R)], axis=0)
               for g in range(G)]
    ones_rows = jnp.where(_iota((HEAD_DIM, KC), 0) == 0, 1.0, 0.0).astype(BF16)
    m_scr[...] = jnp.full(m_scr.shape, -inf, F32)
    acc_scr[...] = jnp.zeros_like(acc_scr)

    n_ge = fold_sum(sweep(lambda c, s: c + psum(jnp.where(s >= thr, 1.0, 0.0)), full8(0.0)))
    any_ties = jnp.sum(jnp.where(n_ge == k_q, 0.0, 1.0)) > 0.0

    def make_attend(with_ties):
        def attend(c, tie_carry):
            rows = chunk_rows(c)
            s = s_scr[rows, :]
            if with_ties:
                eq = s == thr
                rank = tie_carry + _dot(tri_ref[...], jnp.where(eq, 1.0, 0.0).astype(BF16))
                sel = (s > thr) | (eq & (rank < need))
                tie_carry = tie_carry + fold_sum(psum(jnp.where(eq, 1.0, 0.0)))
            else:
                sel = s >= thr
            bias = jnp.where(sel, 0.0, -inf).astype(BF16)
            bias_r = jnp.concatenate([bias] * R, axis=1)
            kc = k_ref[0, rows, :]
            logits = [_dot_nt(kc[:, g * HEAD_DIM:(g + 1) * HEAD_DIM], q_stack[g]) for g in range(G)]
            probs, alphas = [], []
            for g in range(G):
                lm = logits[g].astype(BF16) + bias_r
                m_old = m_scr[g]
                m_part = jnp.max(lm.reshape(KC // 16, 16, R * QB), axis=0).astype(F32)
                m_new = jnp.maximum(m_old, jnp.max(m_part, axis=0, keepdims=True))
                m_safe = jnp.where(m_new == -inf, 0.0, m_new)
                alphas.append(jnp.exp2(m_old - m_safe))
                probs.append(jnp.exp2(lm - m_safe.astype(BF16)))
                m_scr[g] = m_new
            for g in range(G):
                vt_ext = jnp.concatenate([vt_ref[0, g * HEAD_DIM:(g + 1) * HEAD_DIM, rows], ones_rows], axis=0)
                acc_scr[g] = acc_scr[g] * alphas[g] + _dot(vt_ext, probs[g])
            return tie_carry
        return attend

    @pl.when(any_ties)
    def _():
        lax.fori_loop(0, nch, make_attend(True), full(0.0))

    @pl.when(jnp.logical_not(any_ties))
    def _():
        lax.fori_loop(0, nch, make_attend(False), full(0.0))

    out_t = jnp.concatenate([acc_scr[g][:HEAD_DIM] / acc_scr[g][HEAD_DIM:HEAD_DIM + 1] for g in range(G)], axis=0)
    per_r = [out_t[:, r * QB:(r + 1) * QB].T for r in range(R)]
    att = jnp.concatenate([per_r[r][:, g * HEAD_DIM:(g + 1) * HEAD_DIM] for g in range(G) for r in range(R)], axis=1)
    o_ref[0] = (att * sg_ref[0]).astype(BF16)


def _attn_prompt(q, qi, wit, kbf, vt, kibf, sg, page_table, dqi3, dwi3, dkidx_t):
    nb, t, _ = q.shape
    nj = t // Q_BLOCK
    nd, n_pages = page_table.shape
    spp = nd // (nb * nj)
    assert spp * nb * nj == nd and SUBLANES % spp == 0
    topk = min(TOPK_MAX, t // 4)
    kc = KEY_CHUNK
    tri = jnp.where(jnp.arange(kc)[None, :] < jnp.arange(kc)[:, None], 1.0, 0.0).astype(BF16)
    qblk = lambda width: pl.BlockSpec((1, Q_BLOCK, width), lambda b, j, pt: (b, j, 0))
    whole = lambda r, c: pl.BlockSpec((1, r, c), lambda b, j, pt: (b, 0, 0))
    per_step = lambda r, c: pl.BlockSpec((spp, r, c), lambda b, j, pt: (b * nj + j, 0, 0))
    grid_spec = pltpu.PrefetchScalarGridSpec(
        num_scalar_prefetch=1,
        grid=(nb, nj),
        in_specs=[qblk(ATT_WIDTH), qblk(N_IDX_HEADS * IDX_DIM),
                  pl.BlockSpec((1, N_IDX_HEADS, Q_BLOCK), lambda b, j, pt: (b, 0, j)),
                  whole(t, KV_WIDTH), whole(KV_WIDTH, t), whole(t, IDX_DIM), qblk(ATT_WIDTH),
                  pl.BlockSpec((kc, kc), lambda b, j, pt: (0, 0)),
                  per_step(N_IDX_HEADS, IDX_DIM), per_step(N_IDX_HEADS, 1), pl.BlockSpec(memory_space=pl.ANY)],
        out_specs=[qblk(ATT_WIDTH),
                   pl.BlockSpec((SUBLANES, n_pages * PAGE_SIZE), lambda b, j, pt: ((b * nj + j) // (SUBLANES // spp), 0))],
        scratch_shapes=[pltpu.VMEM((t, Q_BLOCK), F32),
                        pltpu.VMEM((N_KV_HEADS, 1, N_Q_HEADS // N_KV_HEADS * Q_BLOCK), F32),
                        pltpu.VMEM((N_KV_HEADS, 2 * HEAD_DIM, N_Q_HEADS // N_KV_HEADS * Q_BLOCK), F32)]
                       + [pltpu.VMEM((2, n_pages, IDX_DIM, PAGE_SIZE), F32)] * spp
                       + [pltpu.SemaphoreType.DMA((2,))] * spp,
    )
    return pl.pallas_call(
        functools.partial(_attn_prompt_kernel, topk=topk),
        grid_spec=grid_spec,
        out_shape=[jax.ShapeDtypeStruct((nb, t, ATT_WIDTH), BF16),
                   jax.ShapeDtypeStruct((nd, n_pages * PAGE_SIZE), F32)],
        compiler_params=_cparams(("arbitrary", "arbitrary")),
        name="attn_prompt",
    )(page_table, q, qi, wit, kbf, vt, kibf, sg, tri, dqi3, dwi3, dkidx_t)


def _outproj_kernel(att_ref, rw_ref, w_ref, x_ref, g_ref, o_ref):
    mix = _dot(att_ref[0], w_ref[:ATT_WIDTH, :]) + _dot(rw_ref[0], w_ref[ATT_WIDTH:, :])
    o_ref[0] = x_ref[0] + g_ref[0] * mix


def _out_proj(att, rw, w_out_bf, x3, gate3, tm):
    nb, t, d = x3.shape
    if gate3.shape[1] == 1:
        gspec = pl.BlockSpec((1, 1, d), lambda b, i: (b, 0, 0))
    else:
        gspec = pl.BlockSpec((1, tm, d), lambda b, i: (b, i, 0))
    blk = lambda width: pl.BlockSpec((1, tm, width), lambda b, i: (b, i, 0))
    return pl.pallas_call(
        _outproj_kernel,
        grid=(nb, t // tm),
        in_specs=[blk(ATT_WIDTH), blk(RWKV_WIDTH), pl.BlockSpec(w_out_bf.shape, lambda b, i: (0, 0)), blk(d), gspec],
        out_specs=blk(d),
        out_shape=jax.ShapeDtypeStruct((nb, t, d), F32),
        compiler_params=_cparams(("arbitrary", "arbitrary")),
        name="out_proj",
    )(att, rw, w_out_bf, x3, gate3)


class _PageFetcher:
    def __init__(self, pt_ref, hbm, buf, sem, step=None, nsteps=None, stride=1, offset=0):
        self.pt, self.hbm, self.buf, self.sem = pt_ref, hbm, buf, sem
        self.n_pages = buf.shape[1]
        assert self.n_pages % 2 == 0
        b = pl.program_id(0) if step is None else step
        nb = pl.num_programs(0) if nsteps is None else nsteps
        self.first, self.last = b == 0, b == nb - 1
        self.slot = b % 2
        self.first_seq = offset
        self.next_seq = jnp.minimum(b + 1, nb - 1) * stride + offset

    def _copy(self, seq, j, slot):
        return pltpu.make_async_copy(self.hbm.at[self.pt[seq, j]], self.buf.at[slot, j], self.sem.at[slot])

    def _wait(self, slot):
        pltpu.make_async_copy(self.hbm.at[pl.ds(0, self.n_pages)], self.buf.at[slot], self.sem.at[slot]).wait()

    def begin(self):
        @pl.when(self.first)
        def _():
            for j in range(self.n_pages):
                self._copy(self.first_seq, j, 0).start()
        self._wait(self.slot)

    def prefetch(self, j):
        self._copy(self.next_seq, j, 1 - self.slot).start(priority=j % 2)

    def page(self, j):
        return self.buf[self.slot, j]

    def end(self):
        @pl.when(self.last)
        def _():
            self._wait(1 - self.slot)


def _score_pages(pages, qi_ref, wi_ref, o_ref, u, out_row):
    pages.begin()
    qi = qi_ref[u]
    qi16 = jnp.concatenate([qi, jnp.zeros_like(qi)], axis=0).astype(BF16)
    wi = wi_ref[u]
    parts = []
    for j in range(0, pages.n_pages, 2):
        pages.prefetch(j)
        pages.prefetch(j + 1)
        pair = jnp.concatenate([pages.page(j), pages.page(j + 1)], axis=1).astype(BF16)
        s = _dot(qi16, pair)[:N_IDX_HEADS]
        parts.append(jnp.sum(jnp.maximum(s, 0.0) * wi, axis=0, keepdims=True))
    o_ref[pl.ds(out_row, 1), :] = jnp.concatenate(parts, axis=1)
    pages.end()


def _dec_select_kernel(s_ref, qi_ref, ki_ref, wi_ref, sel_ref, sx, *, topk):
    nb, past = s_ref.shape
    inf = jnp.inf
    qi = qi_ref[...].astype(F32)
    ki = ki_ref[...].astype(F32)
    prod = qi * jnp.concatenate([ki] * N_IDX_HEADS, axis=1)
    nq = N_IDX_HEADS * IDX_DIM
    hsel = jnp.where(_iota((nq, LANES), 0) // IDX_DIM == _iota((nq, LANES), 1), 1.0, 0.0).astype(BF16)
    sh = _sum_dot(prod, hsel)[:, :N_IDX_HEADS]
    s_new = jnp.sum(jnp.maximum(sh, 0.0) * wi_ref[...], axis=1, keepdims=True)
    sx[:, :past] = s_ref[...]
    sx[:, past:] = jnp.where(_iota((nb, LANES), 1) == 0, s_new, -inf)

    rsum = lambda a: jnp.sum(a, axis=1, keepdims=True)
    k_q = float(topk)
    s = sx[...]
    lo = jnp.min(jnp.where(s > -inf, s, inf), axis=1, keepdims=True)
    hi = jnp.max(s, axis=1, keepdims=True)

    def bisect(_, lh):
        lo, hi = lh
        mid = 0.5 * lo + 0.5 * hi
        ge = rsum(jnp.where(sx[...] >= mid, 1.0, 0.0)) >= k_q
        return jnp.where(ge, mid, lo), jnp.where(ge, hi, mid)

    lo, hi = lax.fori_loop(0, N_BISECT, bisect, (lo, hi))
    s = sx[...]
    v0 = jnp.min(jnp.where(s >= lo, s, inf), axis=1, keepdims=True)

    def refine(state):
        v, _, _ = state
        s = sx[...]
        g = s > v
        cnt = rsum(jnp.where(g, 1.0, 0.0))
        v2 = jnp.min(jnp.where(g, s, inf), axis=1, keepdims=True)
        ok = cnt < k_q
        return jnp.where(ok, v, v2), cnt, jnp.sum(jnp.where(ok, 0.0, 1.0))

    thr, c_gt, _ = lax.while_loop(lambda st: st[2] > 0.0, refine,
                                  (v0, jnp.zeros((nb, 1), F32), jnp.float32(1.0)))
    need = k_q - c_gt
    upper = jnp.where(_iota((LANES, LANES), 0) < _iota((LANES, LANES), 1), 1.0, 0.0).astype(BF16)
    carry = jnp.zeros((nb, 1), F32)
    for c in range((past + LANES) // LANES):
        sc = sx[:, c * LANES:(c + 1) * LANES]
        eq = sc == thr
        eqf = jnp.where(eq, 1.0, 0.0)
        rank = carry + _dot(eqf.astype(BF16), upper)
        sel_ref[:, c * LANES:(c + 1) * LANES] = jnp.where((sc > thr) | (eq & (rank < need)), 1.0, 0.0)
        carry = carry + rsum(eqf)


def _dec_select(scores, qi, ki, wi, topk):
    nb, past = scores.shape
    args = [scores, qi, ki, wi]
    return pl.pallas_call(
        functools.partial(_dec_select_kernel, topk=topk),
        in_specs=[pl.BlockSpec(a.shape, lambda: (0, 0)) for a in args],
        out_specs=pl.BlockSpec((nb, past + LANES), lambda: (0, 0)),
        out_shape=jax.ShapeDtypeStruct((nb, past + LANES), F32),
        scratch_shapes=[pltpu.VMEM((nb, past + LANES), F32)],
        compiler_params=pltpu.CompilerParams(vmem_limit_bytes=VMEM_LIMIT),
        name="dec_select",
    )(*args)


def _dec_attend(k_pages, v_pages, q8, sel, k_new, v_new):
    n_pages = k_pages.n_pages
    past = n_pages * PAGE_SIZE
    inf = jnp.inf
    nh = N_Q_HEADS
    R = N_Q_HEADS // N_KV_HEADS
    q8 = q8.astype(F32)
    own = (_iota((nh, KV_WIDTH), 0) // R) == (_iota((nh, KV_WIDTH), 1) // HEAD_DIM)
    q_ext = jnp.where(own, jnp.concatenate([q8, q8], axis=1), 0.0)
    q16 = jnp.concatenate([q_ext, jnp.zeros_like(q_ext)], axis=0).astype(BF16)

    k_pages.begin()
    parts = []
    for j in range(0, n_pages, 2):
        k_pages.prefetch(j)
        k_pages.prefetch(j + 1)
        pair = jnp.concatenate([k_pages.page(j), k_pages.page(j + 1)], axis=1).astype(BF16)
        parts.append(_dot(q16, pair)[:nh])
    k_pages.end()
    k_new = k_new.astype(BF16).astype(F32)
    v_new = v_new.astype(BF16).astype(F32)
    lg_new = jnp.sum(q16[:nh].astype(F32) * k_new, axis=1, keepdims=True)
    lm = jnp.where(sel[:, :past] > 0.5, jnp.concatenate(parts, axis=1), -inf)
    lm_new = jnp.where(sel[:, past:past + 1] > 0.5, lg_new, -inf)
    m = jnp.maximum(jnp.max(lm, axis=1, keepdims=True), lm_new)
    p = jnp.exp2(lm - m)
    p_new = jnp.exp2(lm_new - m)
    l = jnp.sum(p, axis=1, keepdims=True) + p_new
    p16 = jnp.concatenate([p, jnp.zeros_like(p)], axis=0).astype(BF16)

    v_pages.begin()
    acc = jnp.zeros((2 * nh, KV_WIDTH), F32)
    for j in range(0, n_pages, 2):
        v_pages.prefetch(j)
        v_pages.prefetch(j + 1)
        pair = jnp.concatenate([v_pages.page(j), v_pages.page(j + 1)], axis=1).astype(BF16)
        acc = acc + _dot_nt(p16[:, j * PAGE_SIZE:(j + 2) * PAGE_SIZE], pair)
    v_pages.end()
    out = (acc[:nh] + p_new.astype(BF16).astype(F32) * v_new) / l
    return jnp.where(_iota((nh, HEAD_DIM), 0) < R, out[:, :HEAD_DIM], out[:, HEAD_DIM:])


def kernel(x_prompt, x_sample, cache_k, cache_v, cache_kidx, state_wkv, state_shift, page_table,
           c_prompt, c_sample, norm_w, w_ada, b_ada, w_in, q_norm_w, k_norm_w, mu_shift, w0, w_up,
           a0, a_up, k_k, k_a, r_k, ln_x_w, ln_x_b, w_out):
    nb, t, d = x_prompt.shape
    nd = x_sample.shape[0]
    assert x_sample.shape[1] == 1
    n_pages = page_table.shape[1]
    past = n_pages * PAGE_SIZE
    w = RWKV_WIDTH

    offs = [0]
    for sz in (ATT_WIDTH, KV_WIDTH, KV_WIDTH, N_IDX_HEADS * IDX_DIM, N_IDX_HEADS, IDX_DIM, ATT_WIDTH,
               w, w, w, LORA, LORA, w):
        offs.append(offs[-1] + sz)
    w_in_t = w_in.T
    row = lambda i: w_in_t[offs[i]:offs[i + 1]]
    w_main = jnp.concatenate([row(0), row(1), row(2), row(3), row(6), row(7), row(8), row(9), row(12),
                              row(10), row(11), row(5), jnp.zeros((LANES - IDX_DIM, d), w_in.dtype)], axis=0).astype(BF16)
    zl = jnp.zeros((LORA, w), F32)
    wts = {
        "norm_w": norm_w.reshape(1, d), "w_main": w_main,
        "w_wit": row(4).astype(BF16), "w_vt": row(2).astype(BF16),
        "qnw": jnp.concatenate([q_norm_w, q_norm_w]).reshape(1, LANES),
        "knw": jnp.concatenate([k_norm_w, k_norm_w]).reshape(1, LANES),
    }
    rw_params = {
        "mu_rkv": mu_shift[:3 * w].reshape(1, 3 * w), "mu_lora": mu_shift[3 * w:].reshape(1, 2 * LORA),
        "w0": w0.reshape(1, w), "a0": a0.reshape(1, w),
        "w_lora": jnp.concatenate([jnp.concatenate([w_up, zl], axis=1),
                                   jnp.concatenate([zl, a_up], axis=1)], axis=0).astype(BF16),
        "k_k": k_k.reshape(1, w), "k_a": k_a.reshape(1, w), "r_k": r_k.reshape(1, w),
        "ln_w": ln_x_w.reshape(1, w), "ln_b": ln_x_b.reshape(1, w),
    }
    w_out_bf = w_out.astype(BF16)

    n_c = nb + nd
    pad = (-n_c) % 8
    c_all = jnp.concatenate([c_prompt, c_sample, jnp.zeros((pad, d), F32)], axis=0)
    mod = _adaln_mod(c_all, w_ada, b_ada)
    shift_p, scale_p, gate_p = (mod[:nb, s * d:(s + 1) * d].reshape(nb, 1, d) for s in range(3))
    shift_s, scale_s, gate_s = (mod[nb:n_c, s * d:(s + 1) * d].reshape(1, nd, d) for s in range(3))

    tm = min(512, t)
    pr = _in_proj(x_prompt, shift_p, scale_p, _rope_tables(jnp.arange(t)), wts, tm)
    xs3 = x_sample.reshape(1, nd, d)
    sm = _in_proj(xs3, shift_s, scale_s, _rope_tables(jnp.full((1,), past)), wts, nd)
    qi_s, ki_s = sm["qi"][0], sm["ki"][0]
    wi_s = sm["wit"][0].T
    n_phys = cache_k.shape[0]
    k_pages = jnp.transpose(cache_k, (0, 2, 3, 1)).reshape(n_phys, KV_WIDTH, PAGE_SIZE)
    v_pages = jnp.transpose(cache_v, (0, 2, 3, 1)).reshape(n_phys, KV_WIDTH, PAGE_SIZE)

    att_p, scores = _attn_prompt(pr["q"], pr["qi"], pr["wit"], pr["kbf"], pr["vt"], pr["kibf"], pr["sg"],
                                 page_table, qi_s.reshape(nd, N_IDX_HEADS, IDX_DIM),
                                 wi_s.reshape(nd, N_IDX_HEADS, 1), jnp.transpose(cache_kidx, (0, 2, 1)))
    topk = min(TOPK_MAX, (past + 1) // 4)
    sel = _dec_select(scores, qi_s, sm["kibf"][0], wi_s, topk)
    rw_p, s_p, att_s = _rwkv_prompt(pr["rkv"], pr["lora"], pr["gr"], rw_params, page_table,
                                    sm["q"][0].reshape(nd, N_Q_HEADS, HEAD_DIM), sel,
                                    sm["k"][0].reshape(nd, 1, KV_WIDTH), sm["v"][0].reshape(nd, 1, KV_WIDTH),
                                    sm["sg"][0].reshape(nd, N_Q_HEADS, HEAD_DIM), k_pages, v_pages)
    y_prompt = _out_proj(att_p, rw_p, w_out_bf, x_prompt, gate_p, tm)
    shift_prompt = jnp.concatenate([pr["rkv"][:, t - 1:], pr["lora"][:, t - 1:]], axis=-1)

    prev = state_shift.reshape(nd, -1)
    vecs = _rwkv_dec_prep(sm["rkv"][0], sm["lora"][0], prev[:, :3 * w], prev[:, 3 * w:], rw_params)
    s_t, y_t = _rwkv_dec_state(vecs[:6], jnp.transpose(state_wkv, (1, 2, 3, 0)))
    s_s = jnp.transpose(s_t, (3, 0, 1, 2))
    rw_s = _rwkv_dec_epi(y_t.reshape(w, nd), vecs[6], sm["gr"][0], rw_params)
    y_sample = _out_proj(att_s.reshape(1, nd, ATT_WIDTH), rw_s.reshape(1, nd, w), w_out_bf, xs3, gate_s, nd)
    shift_sample = jnp.concatenate([sm["rkv"][0], sm["lora"][0]], axis=-1).reshape(nd, 1, 3 * w + 2 * LORA)

    return (y_prompt, y_sample.reshape(nd, 1, d),
            pr["k"].reshape(nb, t, N_KV_HEADS, HEAD_DIM), pr["v"].reshape(nb, t, N_KV_HEADS, HEAD_DIM), pr["ki"],
            s_p, shift_prompt,
            sm["k"][0].reshape(nd, 1, N_KV_HEADS, HEAD_DIM), sm["v"][0].reshape(nd, 1, N_KV_HEADS, HEAD_DIM),
            ki_s.reshape(nd, 1, IDX_DIM), s_s, shift_sample)
```

```python
import functools

import jax
import jax.numpy as jnp
from jax import lax
from jax.experimental import pallas as pl
from jax.experimental.pallas import tpu as pltpu

F32, BF16 = jnp.float32, jnp.bfloat16

HEAD_DIM = 64
N_Q_HEADS = 8
N_KV_HEADS = 2
N_IDX_HEADS = 8
IDX_DIM = 64
N_RWKV_HEADS = 8
ATT_WIDTH = N_Q_HEADS * HEAD_DIM
RWKV_WIDTH = N_RWKV_HEADS * HEAD_DIM
KV_WIDTH = N_KV_HEADS * HEAD_DIM
LORA = 64
TOPK_MAX = 256
ROPE_THETA = 500000.0
ROPE_DIMS = HEAD_DIM // 4
NORM_EPS = 1e-6
GN_EPS = 64e-5
PAGE_SIZE = 128
LANES = 128
SUBLANES = 8
Q_SCALE = HEAD_DIM ** -0.5 * 1.4426950408889634
VMEM_LIMIT = 56 * 1024 * 1024

Q_BLOCK = 512
KEY_CHUNK = 512
N_BISECT = 16
RW_CHUNK = 64
RW_BLOCK = 128


def _dot(a, b):
    return jnp.dot(a, b, preferred_element_type=F32)


def _dot_nt(a, b):
    return lax.dot_general(a, b, (((1,), (1,)), ((), ())), preferred_element_type=F32)


def _split(x):
    hi = x.astype(BF16)
    return hi, (x - hi.astype(F32)).astype(BF16)


def _sum_dot(x, sel01):
    hi, lo = _split(x)
    return _dot(hi, sel01) + _dot(lo, sel01)


def _sum_dot_left(sel01, x):
    hi, lo = _split(x)
    return _dot(sel01, hi) + _dot(sel01, lo)


def _dot_tn(a, b):
    return lax.dot_general(a, b, (((0,), (0,)), ((), ())), preferred_element_type=F32)


def _iota(shape, dim):
    return lax.broadcasted_iota(jnp.int32, shape, dim)


def _head_block_ones(n):
    return jnp.where(_iota((n, n), 0) // HEAD_DIM == _iota((n, n), 1) // HEAD_DIM, 1.0, 0.0).astype(BF16)


def _silu(x):
    return x * jax.nn.sigmoid(x)


def _cparams(sem):
    return pltpu.CompilerParams(dimension_semantics=sem, vmem_limit_bytes=VMEM_LIMIT)


def _mod_kernel(c_ref, w_ref, b_ref, o_ref):
    s = _silu(c_ref[...])
    o_ref[...] = _dot(s.astype(BF16), w_ref[...].astype(BF16)) + b_ref[...]


def _adaln_mod(c, w_ada, b_ada):
    rows, d = c.shape
    n = w_ada.shape[1]
    tn = 1024
    return pl.pallas_call(
        _mod_kernel,
        grid=(n // tn,),
        in_specs=[pl.BlockSpec((rows, d), lambda j: (0, 0)),
                  pl.BlockSpec((d, tn), lambda j: (0, j)),
                  pl.BlockSpec((1, tn), lambda j: (0, j))],
        out_specs=pl.BlockSpec((rows, tn), lambda j: (0, j)),
        out_shape=jax.ShapeDtypeStruct((rows, n), F32),
        compiler_params=_cparams(("arbitrary",)),
        name="adaln_mod",
    )(c, w_ada, b_ada.reshape(1, n))


_W_Q, _W_KV, _W_QI, _W_GA, _W_RKV, _W_GR, _W_TAIL, _W_END = 0, 512, 768, 1280, 1792, 3328, 3840, 4096


def _inproj_kernel(x_ref, shift_ref, scale_ref, nw_ref, w_ref, wwi_ref, wvt_ref, cos_ref, s1_ref, s2_ref,
                   qnw_ref, knw_ref,
                   q_o, k_o, kbf_o, v_o, vt_o, qi_o, ki_o, kibf_o, wit_o, sg_o, rkv_o, lora_o, gr_o):
    x = x_ref[0]
    ms = jnp.mean(x * x, axis=-1, keepdims=True)
    h = x * lax.rsqrt(ms + NORM_EPS) * nw_ref[...]
    h = h * (1.0 + scale_ref[0]) + shift_ref[0]
    hb = h.astype(BF16)
    cos, s1, s2 = cos_ref[...], s1_ref[...], s2_ref[...]
    bsum = _head_block_ones(LANES)

    def seg(a, b):
        return _dot_nt(hb, w_ref[a:b, :])

    def rope(t):
        return t * cos + pltpu.roll(t, LANES - ROPE_DIMS // 2, 1) * s1 + pltpu.roll(t, ROPE_DIMS // 2, 1) * s2

    def head_norm(t, w):
        return t * lax.rsqrt(_sum_dot(t * t, bsum) * (1.0 / HEAD_DIM) + NORM_EPS) * w

    yq = seg(_W_Q, _W_KV)
    for s in range(ATT_WIDTH // LANES):
        t = rope(head_norm(yq[:, s * LANES:(s + 1) * LANES], qnw_ref[...]))
        q_o[0, :, s * LANES:(s + 1) * LANES] = (t * Q_SCALE).astype(BF16)

    ykv = seg(_W_KV, _W_QI)
    k = rope(head_norm(ykv[:, :KV_WIDTH], knw_ref[...]))
    k_o[0] = k
    kbf_o[0] = k.astype(BF16)
    v_o[0] = ykv[:, KV_WIDTH:]
    vt_o[0] = _dot_nt(wvt_ref[...], hb).astype(BF16)

    yqi = seg(_W_QI, _W_GA)
    for s in range(N_IDX_HEADS * IDX_DIM // LANES):
        qi_o[0, :, s * LANES:(s + 1) * LANES] = rope(yqi[:, s * LANES:(s + 1) * LANES]).astype(BF16)

    sg_o[0] = _silu(seg(_W_GA, _W_RKV))
    for s in range(3):
        rkv_o[0, :, s * RWKV_WIDTH:(s + 1) * RWKV_WIDTH] = seg(_W_RKV + s * RWKV_WIDTH, _W_RKV + (s + 1) * RWKV_WIDTH)
    gr_o[0] = seg(_W_GR, _W_TAIL)

    yt = seg(_W_TAIL, _W_END)
    lora_o[0] = yt[:, :2 * LORA]
    ki = rope(yt[:, 2 * LORA:])[:, :IDX_DIM]
    ki_o[0] = ki
    kibf_o[0] = ki.astype(BF16)
    wit_o[0] = _dot_nt(wwi_ref[...], hb) * (N_IDX_HEADS ** -0.5 * IDX_DIM ** -0.5)


def _in_proj(x3, shift3, scale3, tabs, wts, tm):
    nb, t, d = x3.shape
    mod_rows = shift3.shape[1]
    tab_rows = tabs[0].shape[0]
    grid = (nb, t // tm)
    if mod_rows == 1:
        mod_spec = pl.BlockSpec((1, 1, d), lambda b, i: (b, 0, 0))
    else:
        mod_spec = pl.BlockSpec((1, tm, d), lambda b, i: (b, i, 0))
    if tab_rows == 1:
        tab_spec = pl.BlockSpec((1, LANES), lambda b, i: (0, 0))
    else:
        tab_spec = pl.BlockSpec((tm, LANES), lambda b, i: (i, 0))
    const = lambda shape: pl.BlockSpec(shape, lambda b, i: tuple(0 for _ in shape))
    row = lambda width: pl.BlockSpec((1, tm, width), lambda b, i: (b, i, 0))
    col = lambda rows: pl.BlockSpec((1, rows, tm), lambda b, i: (b, 0, i))
    out_defs = [
        ("q", (t, ATT_WIDTH), BF16, row(ATT_WIDTH)),
        ("k", (t, KV_WIDTH), F32, row(KV_WIDTH)),
        ("kbf", (t, KV_WIDTH), BF16, row(KV_WIDTH)),
        ("v", (t, KV_WIDTH), F32, row(KV_WIDTH)),
        ("vt", (KV_WIDTH, t), BF16, col(KV_WIDTH)),
        ("qi", (t, N_IDX_HEADS * IDX_DIM), BF16, row(N_IDX_HEADS * IDX_DIM)),
        ("ki", (t, IDX_DIM), F32, row(IDX_DIM)),
        ("kibf", (t, IDX_DIM), BF16, row(IDX_DIM)),
        ("wit", (N_IDX_HEADS, t), F32, col(N_IDX_HEADS)),
        ("sg", (t, ATT_WIDTH), F32, row(ATT_WIDTH)),
        ("rkv", (t, 3 * RWKV_WIDTH), F32, row(3 * RWKV_WIDTH)),
        ("lora", (t, 2 * LORA), F32, row(2 * LORA)),
        ("gr", (t, RWKV_WIDTH), F32, row(RWKV_WIDTH)),
    ]
    outs = pl.pallas_call(
        _inproj_kernel,
        grid=grid,
        in_specs=[pl.BlockSpec((1, tm, d), lambda b, i: (b, i, 0)), mod_spec, mod_spec, const((1, d)),
                  const(wts["w_main"].shape), const(wts["w_wit"].shape), const(wts["w_vt"].shape),
                  tab_spec, tab_spec, tab_spec, const((1, LANES)), const((1, LANES))],
        out_specs=[o[3] for o in out_defs],
        out_shape=[jax.ShapeDtypeStruct((nb,) + o[1], o[2]) for o in out_defs],
        compiler_params=_cparams(("arbitrary", "arbitrary")),
        name="in_proj",
    )(x3, shift3, scale3, wts["norm_w"], wts["w_main"], wts["w_wit"], wts["w_vt"], tabs[0], tabs[1], tabs[2],
      wts["qnw"], wts["knw"])
    return {o[0]: arr for o, arr in zip(out_defs, outs)}


def _rope_tables(pos):
    half = ROPE_DIMS // 2
    inv = jnp.power(ROPE_THETA, -jnp.arange(half, dtype=F32) / half)
    ang = pos.astype(F32)[:, None] * inv[None, :]
    cos, sin = jnp.cos(ang), jnp.sin(ang)
    n = pos.shape[0]
    pad = jnp.zeros((n, HEAD_DIM - ROPE_DIMS), F32)
    zero = jnp.zeros((n, half), F32)
    c_head = jnp.concatenate([cos, cos, pad + 1.0], axis=1)
    s1_head = jnp.concatenate([-sin, zero, pad], axis=1)
    s2_head = jnp.concatenate([zero, sin, pad], axis=1)
    two = lambda a: jnp.concatenate([a, a], axis=1)
    return two(c_head), two(s1_head), two(s2_head)


def _head_sum(t):
    ones = _head_block_ones(LANES)
    parts = [_sum_dot(t[:, s * LANES:(s + 1) * LANES], ones) for s in range(t.shape[1] // LANES)]
    return parts[0] if len(parts) == 1 else jnp.concatenate(parts, axis=1)


def _softplus(z):
    return jnp.maximum(z, 0.0) + jnp.log(1.0 + jnp.exp(-jnp.abs(z)))


def _rwkv_prep(xs, xl, prev, prevl, p):
    xm = xs + p["mu_rkv"] * (prev - xs)
    xml = xl + p["mu_lora"] * (prevl - xl)
    w = RWKV_WIDTH
    r, k, v = xm[:, :w], xm[:, w:2 * w], xm[:, 2 * w:]
    lane = _iota(xml.shape, 1)
    lhs = jnp.where(lane < LORA, jnp.tanh(xml), xml)
    lo = _dot(lhs.astype(BF16), p["w_lora"])
    wlog = -_softplus(-(p["w0"] + lo[:, :w])) - 0.5
    logw = -jnp.exp(wlog)
    a = jax.nn.sigmoid(p["a0"] + lo[:, w:])
    kkr = k * p["k_k"]
    kk = kkr / jnp.maximum(jnp.sqrt(_head_sum(kkr * kkr)), 1e-12)
    k_mod = k * (1.0 + (a - 1.0) * p["k_a"])
    bonus = _head_sum(r * k_mod * p["r_k"]) * v
    return r, logw, k_mod, v, -kk, kk * a, bonus


def _rwkv_epilogue(y, bonus, gate, p):
    mean = _head_sum(y) * (1.0 / HEAD_DIM)
    d = y - mean
    var = _head_sum(d * d) * (1.0 / HEAD_DIM)
    yn = d * lax.rsqrt(var + GN_EPS) * p["ln_w"] + p["ln_b"]
    return (yn + bonus) * _silu(gate)


_RW_PARAM_NAMES = ("mu_rkv", "mu_lora", "w0", "a0", "w_lora", "k_k", "k_a", "r_k", "ln_w", "ln_b")


def _load_params(refs):
    return {n: r[...] for n, r in zip(_RW_PARAM_NAMES, refs)}


def _rwkv_prompt_kernel(pt_ref, rkv_ref, lora_ref, gr_ref, *rest):
    nparam = len(_RW_PARAM_NAMES)
    p = _load_params(rest[:nparam])
    dq_ref, dsel_ref, dknew_ref, dvnew_ref, dsg_ref, dk_hbm, dv_hbm = rest[nparam:nparam + 7]
    out_ref, sfin_ref, dout_ref = rest[nparam + 7:nparam + 10]
    (st_s, c_rkv, c_lora, r_s, lw_s, k_s, v_s, al_s, be_s, y_s, bon_s,
     at_s, rt_s, bk_s, ec_s, wl_s, wh_s, zv_s, inv_s, dkbuf, dvbuf, dksem, dvsem) = rest[nparam + 10:]
    i = pl.program_id(0)
    nb, tb = rkv_ref.shape[0], rkv_ref.shape[1]
    C = RW_CHUNK
    W = 2 * C

    spd = dq_ref.shape[0]
    n_seq = spd * pl.num_programs(0)

    def sample_attention(u):
        s = i * spd + u
        k_pages = _PageFetcher(pt_ref, dk_hbm, dkbuf, dksem, s, n_seq)
        v_pages = _PageFetcher(pt_ref, dv_hbm, dvbuf, dvsem, s, n_seq)
        sel = dsel_ref[pl.ds(s % SUBLANES, 1), :]
        res = _dec_attend(k_pages, v_pages, dq_ref[u], sel, dknew_ref[u], dvnew_ref[u])
        dout_ref[u] = (res * dsg_ref[u]).astype(BF16)

    def side_stream(point):
        for u in range(spd * point // 4, spd * (point + 1) // 4):
            sample_attention(u)

    side_stream(0)

    @pl.when(i == 0)
    def _():
        st_s[...] = jnp.zeros_like(st_s)
        c_rkv[...] = jnp.zeros_like(c_rkv)
        c_lora[...] = jnp.zeros_like(c_lora)

    row = _iota((tb, 1), 0)
    for b in range(nb):
        xs, xl = rkv_ref[b], lora_ref[b]
        prev = jnp.where(row == 0, c_rkv[b], pltpu.roll(xs, 1, 0))
        prevl = jnp.where(row == 0, c_lora[b], pltpu.roll(xl, 1, 0))
        c_rkv[b] = xs[tb - 1:tb, :]
        c_lora[b] = xl[tb - 1:tb, :]
        r_s[b], lw_s[b], k_s[b], v_s[b], al_s[b], be_s[b], bon_s[b] = _rwkv_prep(xs, xl, prev, prevl, p)

    side_stream(1)

    rm, cn = _iota((W, W), 0), _iota((W, W), 1)
    tt, ss = rm % C, cn % C
    top, left = rm < C, cn < C
    strict, incl = ss < tt, ss <= tt
    mk_l_lo, mk_l_hi = top & left & strict, (~top) & (~left) & strict
    mk_z_lo, mk_z_hi = top & (~left) & strict, (~top) & left & strict
    mk_w = (_iota((C, W), 1) % C) <= _iota((C, W), 0)
    bd = (rm // HEAD_DIM) == (cn // HEAD_DIM)
    eye = jnp.where(rm == cn, 1.0, 0.0).astype(F32)
    tril = jnp.where(_iota((C, C), 1) <= _iota((C, C), 0), 1.0, 0.0).astype(BF16)
    lane_lo = _iota((C, LANES), 1) < HEAD_DIM
    m_lo = jnp.where(lane_lo, 1.0, 0.0).astype(F32)
    m_hi = 1.0 - m_lo

    bf = lambda a: a.astype(BF16)
    pairs = range(N_RWKV_HEADS // 2)
    lanes_of = lambda pr: slice(pr * LANES, (pr + 1) * LANES)
    rows_of = lambda c: pl.ds(pl.multiple_of(c * C, C), C)
    items = [(b, pr) for b in range(nb) for pr in pairs]

    def phase_a(c, carry):
        rows = rows_of(c)
        cums = []
        for b in range(nb):
            cum_all = _sum_dot_left(tril, lw_s[b, rows, :])
            cums += [cum_all[:, lanes_of(pr)] for pr in pairs]
        g_lo, g_hi = [], []
        for n, (b, pr) in enumerate(items):
            sl = lanes_of(pr)
            cum = cums[n]
            cum_c = cum[C - 1:C, :]
            e_neg, e_rel = jnp.exp(-cum), jnp.exp(cum_c - cum)
            rr, kk, aa, bb = r_s[b, rows, sl], k_s[b, rows, sl], al_s[b, rows, sl], be_s[b, rows, sl]
            a_t, r_t = aa * jnp.exp(cum - lw_s[b, rows, sl]), rr * jnp.exp(cum)
            b_t, k_t = bb * e_neg, kk * e_neg
            at_s[b, c, pr], rt_s[b, c, pr] = bf(a_t), bf(r_t)
            bk_s[b, c, pr] = jnp.concatenate([bb * e_rel, kk * e_rel], axis=0)
            ec_s[b, c, pr] = jnp.exp(cum_c)
            xk = bf(jnp.concatenate([b_t, k_t], axis=0))
            kx = bf(jnp.concatenate([k_t, b_t], axis=0))
            g_lo.append(_dot_nt(bf(jnp.concatenate([a_t * m_lo, r_t * m_lo], axis=0)), xk))
            g_hi.append(_dot_nt(bf(jnp.concatenate([r_t * m_hi, a_t * m_hi], axis=0)), kx))
        l_mat = []
        for n, (b, pr) in enumerate(items):
            l_mat.append(jnp.where(mk_l_lo, g_lo[n], 0.0) + jnp.where(mk_l_hi, g_hi[n], 0.0))
            z_mat = jnp.where(mk_z_lo, g_lo[n], 0.0) + jnp.where(mk_z_hi, g_hi[n], 0.0)
            wl_s[b, c, pr] = bf(jnp.where(mk_w, g_lo[n][C:, :], 0.0))
            wh_s[b, c, pr] = bf(jnp.where(mk_w, g_hi[n][:C, :], 0.0))
            vv_b = bf(v_s[b, rows, lanes_of(pr)])
            zv_s[b, c, pr] = _dot(bf(z_mat), jnp.concatenate([vv_b, vv_b], axis=0))
        inv = [eye + m for m in l_mat]
        pw = l_mat
        for _ in range(5):
            pw = [_dot(bf(m), bf(m)) for m in pw]
            inv = [x + _dot(bf(x), bf(m)) for x, m in zip(inv, pw)]
        for n, (b, pr) in enumerate(items):
            inv_s[b, c, pr] = bf(inv[n])
        return carry

    lax.fori_loop(0, tb // C, phase_a, 0)
    side_stream(2)

    def phase_b(c, carry):
        rows = rows_of(c)
        nit = range(len(items))
        st = [st_s[b, pr] for b, pr in items]
        st_b = [bf(s) for s in st]
        vv = [v_s[b, rows, lanes_of(pr)] for b, pr in items]
        a_s = [_dot_nt(at_s[b, c, pr], st_b[n]) for n, (b, pr) in enumerate(items)]
        u_st = [_dot(inv_s[b, c, pr], bf(zv_s[b, c, pr] + jnp.concatenate([a_s[n], a_s[n]], axis=0)))
                for n, (b, pr) in enumerate(items)]
        u = [jnp.where(lane_lo, x[:C], x[C:]) for x in u_st]
        uv = [jnp.concatenate([u[n], vv[n]], axis=0) for n in nit]
        upd = [_dot_tn(uv[n], bk_s[b, c, pr]) for n, (b, pr) in enumerate(items)]
        for n, (b, pr) in enumerate(items):
            st_s[b, pr] = st[n] * ec_s[b, c, pr] + jnp.where(bd, upd[n], 0.0)
        for n, (b, pr) in enumerate(items):
            uv_b = bf(uv[n])
            vu_b = jnp.concatenate([uv_b[C:], uv_b[:C]], axis=0)
            y_in = jnp.where(lane_lo, _dot(wl_s[b, c, pr], uv_b), _dot(wh_s[b, c, pr], vu_b))
            y_s[b, rows, lanes_of(pr)] = _dot_nt(rt_s[b, c, pr], st_b[n]) + y_in
        return carry

    lax.fori_loop(0, tb // C, phase_b, 0)
    side_stream(3)
    for b in range(nb):
        out_ref[b] = _rwkv_epilogue(y_s[b], bon_s[b], gr_ref[b], p).astype(BF16)

    @pl.when(i == pl.num_programs(0) - 1)
    def _():
        for b in range(nb):
            for hd in range(N_RWKV_HEADS):
                o = (hd % 2) * HEAD_DIM
                sfin_ref[b, hd] = st_s[b, hd // 2][o:o + HEAD_DIM, o:o + HEAD_DIM]


def _rwkv_prompt(rkv, lora, gr, params, page_table, dq3, dsel, dk_new3, dv_new3, dsg3, k_pages, v_pages):
    nb, t, _ = rkv.shape
    tb = min(RW_BLOCK, t)
    nsteps = t // tb
    nd, n_pages = page_table.shape
    spd = nd // nsteps
    assert spd * nsteps == nd and SUBLANES % spd == 0
    w = RWKV_WIDTH
    blk = lambda width: pl.BlockSpec((nb, tb, width), lambda i, pt: (0, i, 0))
    pspecs = [pl.BlockSpec(params[n].shape, lambda i, pt: (0, 0)) for n in _RW_PARAM_NAMES]
    per_step = lambda r, c: pl.BlockSpec((spd, r, c), lambda i, pt: (i, 0, 0))
    vec = lambda: pltpu.VMEM((nb, tb, w), F32)
    nch, npair, c = tb // RW_CHUNK, N_RWKV_HEADS // 2, RW_CHUNK
    per_chunk = lambda rows, dt: pltpu.VMEM((nb, nch, npair, rows, LANES), dt)
    state_shape = (nb, N_RWKV_HEADS, HEAD_DIM, HEAD_DIM)
    page_buf = pltpu.VMEM((2, n_pages, KV_WIDTH, PAGE_SIZE), F32)
    grid_spec = pltpu.PrefetchScalarGridSpec(
        num_scalar_prefetch=1,
        grid=(nsteps,),
        in_specs=[blk(3 * w), blk(2 * LORA), blk(w)] + pspecs
                 + [per_step(N_Q_HEADS, HEAD_DIM),
                    pl.BlockSpec((SUBLANES, dsel.shape[1]), lambda i, pt: (i * spd // SUBLANES, 0)),
                    per_step(1, KV_WIDTH), per_step(1, KV_WIDTH), per_step(N_Q_HEADS, HEAD_DIM),
                    pl.BlockSpec(memory_space=pl.ANY), pl.BlockSpec(memory_space=pl.ANY)],
        out_specs=[blk(w), pl.BlockSpec(state_shape, lambda i, pt: (0, 0, 0, 0)), per_step(N_Q_HEADS, HEAD_DIM)],
        scratch_shapes=[pltpu.VMEM((nb, npair, LANES, LANES), F32),
                        pltpu.VMEM((nb, 1, 3 * w), F32), pltpu.VMEM((nb, 1, 2 * LORA), F32),
                        vec(), vec(), vec(), vec(), vec(), vec(), vec(), vec(),
                        per_chunk(c, BF16), per_chunk(c, BF16), per_chunk(2 * c, F32), per_chunk(1, F32),
                        per_chunk(c, BF16), per_chunk(c, BF16), per_chunk(2 * c, F32), per_chunk(2 * c, BF16),
                        page_buf, page_buf, pltpu.SemaphoreType.DMA((2,)), pltpu.SemaphoreType.DMA((2,))],
    )
    return pl.pallas_call(
        _rwkv_prompt_kernel,
        grid_spec=grid_spec,
        out_shape=[jax.ShapeDtypeStruct((nb, t, w), BF16), jax.ShapeDtypeStruct(state_shape, F32),
                   jax.ShapeDtypeStruct((nd, N_Q_HEADS, HEAD_DIM), BF16)],
        compiler_params=_cparams(("arbitrary",)),
        name="rwkv_prompt",
    )(page_table, rkv, lora, gr, *[params[n] for n in _RW_PARAM_NAMES],
      dq3, dsel, dk_new3, dv_new3, dsg3, k_pages, v_pages)


def _rwkv_dec_prep_kernel(rkv_ref, lora_ref, prev_ref, prevl_ref, *rest):
    nparam = len(_RW_PARAM_NAMES)
    p = _load_params(rest[:nparam])
    outs = rest[nparam:]
    r, logw, k_mod, v, al, be, bonus = _rwkv_prep(rkv_ref[...], lora_ref[...], prev_ref[...], prevl_ref[...], p)
    for o, val in zip(outs[:6], (r, jnp.exp(logw), k_mod, v, al, be)):
        o[...] = val.T
    outs[6][...] = bonus


def _rwkv_dec_prep(rkv, lora, prev, prevl, params):
    n = rkv.shape[0]
    full = lambda a: pl.BlockSpec(a.shape, lambda: tuple(0 for _ in a.shape))
    args = [rkv, lora, prev, prevl] + [params[k] for k in _RW_PARAM_NAMES]
    t_shape, shape = (RWKV_WIDTH, n), (n, RWKV_WIDTH)
    return pl.pallas_call(
        _rwkv_dec_prep_kernel,
        in_specs=[full(a) for a in args],
        out_specs=[pl.BlockSpec(t_shape, lambda: (0, 0))] * 6 + [pl.BlockSpec(shape, lambda: (0, 0))],
        out_shape=[jax.ShapeDtypeStruct(t_shape, F32)] * 6 + [jax.ShapeDtypeStruct(shape, F32)],
        compiler_params=pltpu.CompilerParams(vmem_limit_bytes=VMEM_LIMIT),
        name="rwkv_dec_prep",
    )(*args)


def _rwkv_dec_state_kernel(r_ref, w_ref, k_ref, v_ref, a_ref, b_ref, s_ref, so_ref, y_ref):
    r, w, k, a, b = r_ref[0], w_ref[0], k_ref[0], a_ref[0], b_ref[0]

    def one(i, carry):
        st = s_ref[0, i]
        sa = jnp.sum(st * a, axis=0, keepdims=True)
        sn = st * w + sa * b + v_ref[0, pl.ds(i, 1), :] * k
        so_ref[0, i] = sn
        y_ref[0, pl.ds(i, 1), :] = jnp.sum(sn * r, axis=0, keepdims=True)
        return carry

    lax.fori_loop(0, HEAD_DIM, one, 0)


def _rwkv_dec_state(vecs_t, state_t):
    nh, n, _, nb = state_t.shape
    vspec = pl.BlockSpec((1, n, nb), lambda h: (h, 0, 0))
    sspec = pl.BlockSpec((1, n, n, nb), lambda h: (h, 0, 0, 0))
    return pl.pallas_call(
        _rwkv_dec_state_kernel,
        grid=(nh,),
        in_specs=[vspec] * 6 + [sspec],
        out_specs=[sspec, vspec],
        out_shape=[jax.ShapeDtypeStruct(state_t.shape, F32), jax.ShapeDtypeStruct((nh, n, nb), F32)],
        compiler_params=_cparams(("arbitrary",)),
        name="rwkv_dec_state",
    )(*[a.reshape(nh, n, nb) for a in vecs_t], state_t)


def _rwkv_dec_epi_kernel(yt_ref, bonus_ref, gr_ref, lnw_ref, lnb_ref, o_ref):
    o_ref[...] = _rwkv_epilogue(yt_ref[...].T, bonus_ref[...], gr_ref[...],
                                {"ln_w": lnw_ref[...], "ln_b": lnb_ref[...]}).astype(BF16)


def _rwkv_dec_epi(y_t, bonus, gr, params):
    n = bonus.shape[0]
    args = [y_t, bonus, gr, params["ln_w"], params["ln_b"]]
    return pl.pallas_call(
        _rwkv_dec_epi_kernel,
        in_specs=[pl.BlockSpec(a.shape, lambda: (0, 0)) for a in args],
        out_specs=pl.BlockSpec((n, RWKV_WIDTH), lambda: (0, 0)),
        out_shape=jax.ShapeDtypeStruct((n, RWKV_WIDTH), BF16),
        name="rwkv_dec_epi",
    )(*args)


def _attn_prompt_kernel(pt_ref, q_ref, qi_ref, wit_ref, k_ref, vt_ref, kidx_ref, sg_ref, tri_ref,
                        dqi_ref, dwi_ref, dkidx_hbm, o_ref, dsc_ref,
                        s_scr, m_scr, acc_scr, *dscratch, topk):
    j = pl.program_id(1)
    step = pl.program_id(0) * pl.num_programs(1) + j
    nsteps = pl.num_programs(0) * pl.num_programs(1)
    spp = dqi_ref.shape[0]
    for u in range(spp):
        pages = _PageFetcher(pt_ref, dkidx_hbm, dscratch[u], dscratch[spp + u], step, nsteps, stride=spp, offset=u)
        _score_pages(pages, dqi_ref, dwi_ref, dsc_ref, u, (step % (SUBLANES // spp)) * spp + u)
    QB, KC = Q_BLOCK, KEY_CHUNK
    nch = (j * QB) // KC + 1
    inf = jnp.inf
    qpos_i = j * QB + _iota((1, QB), 1)
    k_q = jnp.minimum(float(topk), qpos_i.astype(F32) + 1.0)
    wit = wit_ref[0]
    qi = qi_ref[0]
    qi_stack = jnp.concatenate([qi[:, h * IDX_DIM:(h + 1) * IDX_DIM] for h in range(N_IDX_HEADS)], axis=0)

    def chunk_rows(c):
        return pl.ds(pl.multiple_of(c * KC, KC), KC)

    SUB = 32
    part = lambda a: a.reshape(KC // SUB, SUB, QB)
    psum = lambda a: jnp.sum(part(a), axis=0)
    pmin = lambda a: jnp.min(part(a), axis=0)
    pmax = lambda a: jnp.max(part(a), axis=0)
    full8 = lambda val: jnp.full((SUB, QB), val, F32)
    fold_sum = lambda a: jnp.sum(a, axis=0, keepdims=True)
    fold_min = lambda a: jnp.min(a, axis=0, keepdims=True)
    fold_max = lambda a: jnp.max(a, axis=0, keepdims=True)

    def score_chunk(c, lohi):
        rows = chunk_rows(c)
        st = _dot_nt(kidx_ref[0, rows, :], qi_stack)
        acc = jnp.zeros((KC, QB), F32)
        for h in range(N_IDX_HEADS):
            acc = acc + jnp.maximum(st[:, h * QB:(h + 1) * QB], 0.0) * wit[h:h + 1, :]
        ok = (c * KC + _iota((KC, QB), 0)) <= qpos_i
        s_scr[rows, :] = jnp.where(ok, acc, -inf)
        return (jnp.minimum(lohi[0], pmin(jnp.where(ok, acc, inf))),
                jnp.maximum(lohi[1], pmax(jnp.where(ok, acc, -inf))))

    lo8, hi8 = lax.fori_loop(0, nch, score_chunk, (full8(inf), full8(-inf)))
    lo, hi = fold_min(lo8), fold_max(hi8)

    def sweep(fn, init):
        return lax.fori_loop(0, nch, lambda c, car: fn(car, s_scr[chunk_rows(c), :]), init)

    def bisect(_, lh):
        lo, hi = lh
        mid = 0.5 * lo + 0.5 * hi
        cnt = fold_sum(sweep(lambda c, s: c + psum(jnp.where(s >= mid, 1.0, 0.0)), full8(0.0)))
        ge = cnt >= k_q
        return jnp.where(ge, mid, lo), jnp.where(ge, hi, mid)

    lo, hi = lax.fori_loop(0, N_BISECT, bisect, (lo, hi))
    v0 = fold_min(sweep(lambda c, s: jnp.minimum(c, pmin(jnp.where(s >= lo, s, inf))), full8(inf)))

    def refine(state):
        v, _, _ = state

        def f(car, s):
            g = s > v
            return car[0] + psum(jnp.where(g, 1.0, 0.0)), jnp.minimum(car[1], pmin(jnp.where(g, s, inf)))

        cnt8, v28 = sweep(f, (full8(0.0), full8(inf)))
        cnt, v2 = fold_sum(cnt8), fold_min(v28)
        ok = cnt < k_q
        return jnp.where(ok, v, v2), cnt, jnp.sum(jnp.where(ok, 0.0, 1.0))

    full = lambda val: jnp.full((1, QB), val, F32)
    thr, c_gt, _ = lax.while_loop(lambda st: st[2] > 0.0, refine, (v0, full(0.0), jnp.float32(1.0)))
    need = k_q - c_gt

    q = q_ref[0]
    G, R = N_KV_HEADS, N_Q_HEADS // N_KV_HEADS
    q_stack = [jnp.concatenate([q[:, (g * R + r) * HEAD_DIM:(g * R + r + 1) * HEAD_DIM] for r in range(R)], axis=0)
               for g in range(G)]
    ones_rows = jnp.where(_iota((HEAD_DIM, KC), 0) == 0, 1.0, 0.0).astype(BF16)
    m_scr[...] = jnp.full(m_scr.shape, -inf, F32)
    acc_scr[...] = jnp.zeros_like(acc_scr)

    n_ge = fold_sum(sweep(lambda c, s: c + psum(jnp.where(s >= thr, 1.0, 0.0)), full8(0.0)))
    any_ties = jnp.sum(jnp.where(n_ge == k_q, 0.0, 1.0)) > 0.0

    def make_attend(with_ties):
        def attend(c, tie_carry):
            rows = chunk_rows(c)
            s = s_scr[rows, :]
            if with_ties:
                eq = s == thr
                rank = tie_carry + _dot(tri_ref[...], jnp.where(eq, 1.0, 0.0).astype(BF16))
                sel = (s > thr) | (eq & (rank < need))
                tie_carry = tie_carry + fold_sum(psum(jnp.where(eq, 1.0, 0.0)))
            else:
                sel = s >= thr
            bias = jnp.where(sel, 0.0, -inf).astype(BF16)
            bias_r = jnp.concatenate([bias] * R, axis=1)
            kc = k_ref[0, rows, :]
            logits = [_dot_nt(kc[:, g * HEAD_DIM:(g + 1) * HEAD_DIM], q_stack[g]) for g in range(G)]
            probs, alphas = [], []
            for g in range(G):
                lm = logits[g].astype(BF16) + bias_r
                m_old = m_scr[g]
                m_part = jnp.max(lm.reshape(KC // 16, 16, R * QB), axis=0).astype(F32)
                m_new = jnp.maximum(m_old, jnp.max(m_part, axis=0, keepdims=True))
                m_safe = jnp.where(m_new == -inf, 0.0, m_new)
                alphas.append(jnp.exp2(m_old - m_safe))
                probs.append(jnp.exp2(lm - m_safe.astype(BF16)))
                m_scr[g] = m_new
            for g in range(G):
                vt_ext = jnp.concatenate([vt_ref[0, g * HEAD_DIM:(g + 1) * HEAD_DIM, rows], ones_rows], axis=0)
                acc_scr[g] = acc_scr[g] * alphas[g] + _dot(vt_ext, probs[g])
            return tie_carry
        return attend

    @pl.when(any_ties)
    def _():
        lax.fori_loop(0, nch, make_attend(True), full(0.0))

    @pl.when(jnp.logical_not(any_ties))
    def _():
        lax.fori_loop(0, nch, make_attend(False), full(0.0))

    out_t = jnp.concatenate([acc_scr[g][:HEAD_DIM] / acc_scr[g][HEAD_DIM:HEAD_DIM + 1] for g in range(G)], axis=0)
    per_r = [out_t[:, r * QB:(r + 1) * QB].T for r in range(R)]
    att = jnp.concatenate([per_r[r][:, g * HEAD_DIM:(g + 1) * HEAD_DIM] for g in range(G) for r in range(R)], axis=1)
    o_ref[0] = (att * sg_ref[0]).astype(BF16)


def _attn_prompt(q, qi, wit, kbf, vt, kibf, sg, page_table, dqi3, dwi3, dkidx_t):
    nb, t, _ = q.shape
    nj = t // Q_BLOCK
    nd, n_pages = page_table.shape
    spp = nd // (nb * nj)
    assert spp * nb * nj == nd and SUBLANES % spp == 0
    topk = min(TOPK_MAX, t // 4)
    kc = KEY_CHUNK
    tri = jnp.where(jnp.arange(kc)[None, :] < jnp.arange(kc)[:, None], 1.0, 0.0).astype(BF16)
    qblk = lambda width: pl.BlockSpec((1, Q_BLOCK, width), lambda b, j, pt: (b, j, 0))
    whole = lambda r, c: pl.BlockSpec((1, r, c), lambda b, j, pt: (b, 0, 0))
    per_step = lambda r, c: pl.BlockSpec((spp, r, c), lambda b, j, pt: (b * nj + j, 0, 0))
    grid_spec = pltpu.PrefetchScalarGridSpec(
        num_scalar_prefetch=1,
        grid=(nb, nj),
        in_specs=[qblk(ATT_WIDTH), qblk(N_IDX_HEADS * IDX_DIM),
                  pl.BlockSpec((1, N_IDX_HEADS, Q_BLOCK), lambda b, j, pt: (b, 0, j)),
                  whole(t, KV_WIDTH), whole(KV_WIDTH, t), whole(t, IDX_DIM), qblk(ATT_WIDTH),
                  pl.BlockSpec((kc, kc), lambda b, j, pt: (0, 0)),
                  per_step(N_IDX_HEADS, IDX_DIM), per_step(N_IDX_HEADS, 1), pl.BlockSpec(memory_space=pl.ANY)],
        out_specs=[qblk(ATT_WIDTH),
                   pl.BlockSpec((SUBLANES, n_pages * PAGE_SIZE), lambda b, j, pt: ((b * nj + j) // (SUBLANES // spp), 0))],
        scratch_shapes=[pltpu.VMEM((t, Q_BLOCK), F32),
                        pltpu.VMEM((N_KV_HEADS, 1, N_Q_HEADS // N_KV_HEADS * Q_BLOCK), F32),
                        pltpu.VMEM((N_KV_HEADS, 2 * HEAD_DIM, N_Q_HEADS // N_KV_HEADS * Q_BLOCK), F32)]
                       + [pltpu.VMEM((2, n_pages, IDX_DIM, PAGE_SIZE), F32)] * spp
                       + [pltpu.SemaphoreType.DMA((2,))] * spp,
    )
    return pl.pallas_call(
        functools.partial(_attn_prompt_kernel, topk=topk),
        grid_spec=grid_spec,
        out_shape=[jax.ShapeDtypeStruct((nb, t, ATT_WIDTH), BF16),
                   jax.ShapeDtypeStruct((nd, n_pages * PAGE_SIZE), F32)],
        compiler_params=_cparams(("arbitrary", "arbitrary")),
        name="attn_prompt",
    )(page_table, q, qi, wit, kbf, vt, kibf, sg, tri, dqi3, dwi3, dkidx_t)


def _outproj_kernel(att_ref, rw_ref, w_ref, x_ref, g_ref, o_ref):
    mix = _dot(att_ref[0], w_ref[:ATT_WIDTH, :]) + _dot(rw_ref[0], w_ref[ATT_WIDTH:, :])
    o_ref[0] = x_ref[0] + g_ref[0] * mix


def _out_proj(att, rw, w_out_bf, x3, gate3, tm):
    nb, t, d = x3.shape
    if gate3.shape[1] == 1:
        gspec = pl.BlockSpec((1, 1, d), lambda b, i: (b, 0, 0))
    else:
        gspec = pl.BlockSpec((1, tm, d), lambda b, i: (b, i, 0))
    blk = lambda width: pl.BlockSpec((1, tm, width), lambda b, i: (b, i, 0))
    return pl.pallas_call(
        _outproj_kernel,
        grid=(nb, t // tm),
        in_specs=[blk(ATT_WIDTH), blk(RWKV_WIDTH), pl.BlockSpec(w_out_bf.shape, lambda b, i: (0, 0)), blk(d), gspec],
        out_specs=blk(d),
        out_shape=jax.ShapeDtypeStruct((nb, t, d), F32),
        compiler_params=_cparams(("arbitrary", "arbitrary")),
        name="out_proj",
    )(att, rw, w_out_bf, x3, gate3)


class _PageFetcher:
    def __init__(self, pt_ref, hbm, buf, sem, step=None, nsteps=None, stride=1, offset=0):
        self.pt, self.hbm, self.buf, self.sem = pt_ref, hbm, buf, sem
        self.n_pages = buf.shape[1]
        assert self.n_pages % 2 == 0
        b = pl.program_id(0) if step is None else step
        nb = pl.num_programs(0) if nsteps is None else nsteps
        self.first, self.last = b == 0, b == nb - 1
        self.slot = b % 2
        self.first_seq = offset
        self.next_seq = jnp.minimum(b + 1, nb - 1) * stride + offset

    def _copy(self, seq, j, slot):
        return pltpu.make_async_copy(self.hbm.at[self.pt[seq, j]], self.buf.at[slot, j], self.sem.at[slot])

    def _wait(self, slot):
        pltpu.make_async_copy(self.hbm.at[pl.ds(0, self.n_pages)], self.buf.at[slot], self.sem.at[slot]).wait()

    def begin(self):
        @pl.when(self.first)
        def _():
            for j in range(self.n_pages):
                self._copy(self.first_seq, j, 0).start()
        self._wait(self.slot)

    def prefetch(self, j):
        self._copy(self.next_seq, j, 1 - self.slot).start(priority=j % 2)

    def page(self, j):
        return self.buf[self.slot, j]

    def end(self):
        @pl.when(self.last)
        def _():
            self._wait(1 - self.slot)


def _score_pages(pages, qi_ref, wi_ref, o_ref, u, out_row):
    pages.begin()
    qi = qi_ref[u]
    qi16 = jnp.concatenate([qi, jnp.zeros_like(qi)], axis=0).astype(BF16)
    wi = wi_ref[u]
    parts = []
    for j in range(0, pages.n_pages, 2):
        pages.prefetch(j)
        pages.prefetch(j + 1)
        pair = jnp.concatenate([pages.page(j), pages.page(j + 1)], axis=1).astype(BF16)
        s = _dot(qi16, pair)[:N_IDX_HEADS]
        parts.append(jnp.sum(jnp.maximum(s, 0.0) * wi, axis=0, keepdims=True))
    o_ref[pl.ds(out_row, 1), :] = jnp.concatenate(parts, axis=1)
    pages.end()


def _dec_select_kernel(s_ref, qi_ref, ki_ref, wi_ref, sel_ref, sx, *, topk):
    nb, past = s_ref.shape
    inf = jnp.inf
    qi = qi_ref[...].astype(F32)
    ki = ki_ref[...].astype(F32)
    prod = qi * jnp.concatenate([ki] * N_IDX_HEADS, axis=1)
    nq = N_IDX_HEADS * IDX_DIM
    hsel = jnp.where(_iota((nq, LANES), 0) // IDX_DIM == _iota((nq, LANES), 1), 1.0, 0.0).astype(BF16)
    sh = _sum_dot(prod, hsel)[:, :N_IDX_HEADS]
    s_new = jnp.sum(jnp.maximum(sh, 0.0) * wi_ref[...], axis=1, keepdims=True)
    sx[:, :past] = s_ref[...]
    sx[:, past:] = jnp.where(_iota((nb, LANES), 1) == 0, s_new, -inf)

    rsum = lambda a: jnp.sum(a, axis=1, keepdims=True)
    k_q = float(topk)
    s = sx[...]
    lo = jnp.min(jnp.where(s > -inf, s, inf), axis=1, keepdims=True)
    hi = jnp.max(s, axis=1, keepdims=True)

    def bisect(_, lh):
        lo, hi = lh
        mid = 0.5 * lo + 0.5 * hi
        ge = rsum(jnp.where(sx[...] >= mid, 1.0, 0.0)) >= k_q
        return jnp.where(ge, mid, lo), jnp.where(ge, hi, mid)

    lo, hi = lax.fori_loop(0, N_BISECT, bisect, (lo, hi))
    s = sx[...]
    v0 = jnp.min(jnp.where(s >= lo, s, inf), axis=1, keepdims=True)

    def refine(state):
        v, _, _ = state
        s = sx[...]
        g = s > v
        cnt = rsum(jnp.where(g, 1.0, 0.0))
        v2 = jnp.min(jnp.where(g, s, inf), axis=1, keepdims=True)
        ok = cnt < k_q
        return jnp.where(ok, v, v2), cnt, jnp.sum(jnp.where(ok, 0.0, 1.0))

    thr, c_gt, _ = lax.while_loop(lambda st: st[2] > 0.0, refine,
                                  (v0, jnp.zeros((nb, 1), F32), jnp.float32(1.0)))
    need = k_q - c_gt
    upper = jnp.where(_iota((LANES, LANES), 0) < _iota((LANES, LANES), 1), 1.0, 0.0).astype(BF16)
    carry = jnp.zeros((nb, 1), F32)
    for c in range((past + LANES) // LANES):
        sc = sx[:, c * LANES:(c + 1) * LANES]
        eq = sc == thr
        eqf = jnp.where(eq, 1.0, 0.0)
        rank = carry + _dot(eqf.astype(BF16), upper)
        sel_ref[:, c * LANES:(c + 1) * LANES] = jnp.where((sc > thr) | (eq & (rank < need)), 1.0, 0.0)
        carry = carry + rsum(eqf)


def _dec_select(scores, qi, ki, wi, topk):
    nb, past = scores.shape
    args = [scores, qi, ki, wi]
    return pl.pallas_call(
        functools.partial(_dec_select_kernel, topk=topk),
        in_specs=[pl.BlockSpec(a.shape, lambda: (0, 0)) for a in args],
        out_specs=pl.BlockSpec((nb, past + LANES), lambda: (0, 0)),
        out_shape=jax.ShapeDtypeStruct((nb, past + LANES), F32),
        scratch_shapes=[pltpu.VMEM((nb, past + LANES), F32)],
        compiler_params=pltpu.CompilerParams(vmem_limit_bytes=VMEM_LIMIT),
        name="dec_select",
    )(*args)


def _dec_attend(k_pages, v_pages, q8, sel, k_new, v_new):
    n_pages = k_pages.n_pages
    past = n_pages * PAGE_SIZE
    inf = jnp.inf
    nh = N_Q_HEADS
    R = N_Q_HEADS // N_KV_HEADS
    q8 = q8.astype(F32)
    own = (_iota((nh, KV_WIDTH), 0) // R) == (_iota((nh, KV_WIDTH), 1) // HEAD_DIM)
    q_ext = jnp.where(own, jnp.concatenate([q8, q8], axis=1), 0.0)
    q16 = jnp.concatenate([q_ext, jnp.zeros_like(q_ext)], axis=0).astype(BF16)

    k_pages.begin()
    parts = []
    for j in range(0, n_pages, 2):
        k_pages.prefetch(j)
        k_pages.prefetch(j + 1)
        pair = jnp.concatenate([k_pages.page(j), k_pages.page(j + 1)], axis=1).astype(BF16)
        parts.append(_dot(q16, pair)[:nh])
    k_pages.end()
    k_new = k_new.astype(BF16).astype(F32)
    v_new = v_new.astype(BF16).astype(F32)
    lg_new = jnp.sum(q16[:nh].astype(F32) * k_new, axis=1, keepdims=True)
    lm = jnp.where(sel[:, :past] > 0.5, jnp.concatenate(parts, axis=1), -inf)
    lm_new = jnp.where(sel[:, past:past + 1] > 0.5, lg_new, -inf)
    m = jnp.maximum(jnp.max(lm, axis=1, keepdims=True), lm_new)
    p = jnp.exp2(lm - m)
    p_new = jnp.exp2(lm_new - m)
    l = jnp.sum(p, axis=1, keepdims=True) + p_new
    p16 = jnp.concatenate([p, jnp.zeros_like(p)], axis=0).astype(BF16)

    v_pages.begin()
    acc = jnp.zeros((2 * nh, KV_WIDTH), F32)
    for j in range(0, n_pages, 2):
        v_pages.prefetch(j)
        v_pages.prefetch(j + 1)
        pair = jnp.concatenate([v_pages.page(j), v_pages.page(j + 1)], axis=1).astype(BF16)
        acc = acc + _dot_nt(p16[:, j * PAGE_SIZE:(j + 2) * PAGE_SIZE], pair)
    v_pages.end()
    out = (acc[:nh] + p_new.astype(BF16).astype(F32) * v_new) / l
    return jnp.where(_iota((nh, HEAD_DIM), 0) < R, out[:, :HEAD_DIM], out[:, HEAD_DIM:])


def kernel(x_prompt, x_sample, cache_k, cache_v, cache_kidx, state_wkv, state_shift, page_table,
           c_prompt, c_sample, norm_w, w_ada, b_ada, w_in, q_norm_w, k_norm_w, mu_shift, w0, w_up,
           a0, a_up, k_k, k_a, r_k, ln_x_w, ln_x_b, w_out):
    nb, t, d = x_prompt.shape
    nd = x_sample.shape[0]
    assert x_sample.shape[1] == 1
    n_pages = page_table.shape[1]
    past = n_pages * PAGE_SIZE
    w = RWKV_WIDTH

    offs = [0]
    for sz in (ATT_WIDTH, KV_WIDTH, KV_WIDTH, N_IDX_HEADS * IDX_DIM, N_IDX_HEADS, IDX_DIM, ATT_WIDTH,
               w, w, w, LORA, LORA, w):
        offs.append(offs[-1] + sz)
    w_in_t = w_in.T
    row = lambda i: w_in_t[offs[i]:offs[i + 1]]
    w_main = jnp.concatenate([row(0), row(1), row(2), row(3), row(6), row(7), row(8), row(9), row(12),
                              row(10), row(11), row(5), jnp.zeros((LANES - IDX_DIM, d), w_in.dtype)], axis=0).astype(BF16)
    zl = jnp.zeros((LORA, w), F32)
    wts = {
        "norm_w": norm_w.reshape(1, d), "w_main": w_main,
        "w_wit": row(4).astype(BF16), "w_vt": row(2).astype(BF16),
        "qnw": jnp.concatenate([q_norm_w, q_norm_w]).reshape(1, LANES),
        "knw": jnp.concatenate([k_norm_w, k_norm_w]).reshape(1, LANES),
    }
    rw_params = {
        "mu_rkv": mu_shift[:3 * w].reshape(1, 3 * w), "mu_lora": mu_shift[3 * w:].reshape(1, 2 * LORA),
        "w0": w0.reshape(1, w), "a0": a0.reshape(1, w),
        "w_lora": jnp.concatenate([jnp.concatenate([w_up, zl], axis=1),
                                   jnp.concatenate([zl, a_up], axis=1)], axis=0).astype(BF16),
        "k_k": k_k.reshape(1, w), "k_a": k_a.reshape(1, w), "r_k": r_k.reshape(1, w),
        "ln_w": ln_x_w.reshape(1, w), "ln_b": ln_x_b.reshape(1, w),
    }
    w_out_bf = w_out.astype(BF16)

    n_c = nb + nd
    pad = (-n_c) % 8
    c_all = jnp.concatenate([c_prompt, c_sample, jnp.zeros((pad, d), F32)], axis=0)
    mod = _adaln_mod(c_all, w_ada, b_ada)
    shift_p, scale_p, gate_p = (mod[:nb, s * d:(s + 1) * d].reshape(nb, 1, d) for s in range(3))
    shift_s, scale_s, gate_s = (mod[nb:n_c, s * d:(s + 1) * d].reshape(1, nd, d) for s in range(3))

    tm = min(512, t)
    pr = _in_proj(x_prompt, shift_p, scale_p, _rope_tables(jnp.arange(t)), wts, tm)
    xs3 = x_sample.reshape(1, nd, d)
    sm = _in_proj(xs3, shift_s, scale_s, _rope_tables(jnp.full((1,), past)), wts, nd)
    qi_s, ki_s = sm["qi"][0], sm["ki"][0]
    wi_s = sm["wit"][0].T
    n_phys = cache_k.shape[0]
    k_pages = jnp.transpose(cache_k, (0, 2, 3, 1)).reshape(n_phys, KV_WIDTH, PAGE_SIZE)
    v_pages = jnp.transpose(cache_v, (0, 2, 3, 1)).reshape(n_phys, KV_WIDTH, PAGE_SIZE)

    att_p, scores = _attn_prompt(pr["q"], pr["qi"], pr["wit"], pr["kbf"], pr["vt"], pr["kibf"], pr["sg"],
                                 page_table, qi_s.reshape(nd, N_IDX_HEADS, IDX_DIM),
                                 wi_s.reshape(nd, N_IDX_HEADS, 1), jnp.transpose(cache_kidx, (0, 2, 1)))
    topk = min(TOPK_MAX, (past + 1) // 4)
    sel = _dec_select(scores, qi_s, sm["kibf"][0], wi_s, topk)
    rw_p, s_p, att_s = _rwkv_prompt(pr["rkv"], pr["lora"], pr["gr"], rw_params, page_table,
                                    sm["q"][0].reshape(nd, N_Q_HEADS, HEAD_DIM), sel,
                                    sm["k"][0].reshape(nd, 1, KV_WIDTH), sm["v"][0].reshape(nd, 1, KV_WIDTH),
                                    sm["sg"][0].reshape(nd, N_Q_HEADS, HEAD_DIM), k_pages, v_pages)
    y_prompt = _out_proj(att_p, rw_p, w_out_bf, x_prompt, gate_p, tm)
    shift_prompt = jnp.concatenate([pr["rkv"][:, t - 1:], pr["lora"][:, t - 1:]], axis=-1)

    prev = state_shift.reshape(nd, -1)
    vecs = _rwkv_dec_prep(sm["rkv"][0], sm["lora"][0], prev[:, :3 * w], prev[:, 3 * w:], rw_params)
    s_t, y_t = _rwkv_dec_state(vecs[:6], jnp.transpose(state_wkv, (1, 2, 3, 0)))
    s_s = jnp.transpose(s_t, (3, 0, 1, 2))
    rw_s = _rwkv_dec_epi(y_t.reshape(w, nd), vecs[6], sm["gr"][0], rw_params)
    y_sample = _out_proj(att_s.reshape(1, nd, ATT_WIDTH), rw_s.reshape(1, nd, w), w_out_bf, xs3, gate_s, nd)
    shift_sample = jnp.concatenate([sm["rkv"][0], sm["lora"][0]], axis=-1).reshape(nd, 1, 3 * w + 2 * LORA)

    return (y_prompt, y_sample.reshape(nd, 1, d),
            pr["k"].reshape(nb, t, N_KV_HEADS, HEAD_DIM), pr["v"].reshape(nb, t, N_KV_HEADS, HEAD_DIM), pr["ki"],
            s_p, shift_prompt,
            sm["k"][0].reshape(nd, 1, N_KV_HEADS, HEAD_DIM), sm["v"][0].reshape(nd, 1, N_KV_HEADS, HEAD_DIM),
            ki_s.reshape(nd, 1, IDX_DIM), s_s, shift_sample)
```
